```python
import jax, jax.numpy as jnp
from jax import lax
import numpy as np

D_MODEL = 1024
BATCH = 32
SEQ = 256
DEPTH = 2
DEC_BATCH = 2
DEC_SEQ = 2048
PAST_LEN = 512

GRID_W = 64
EPS = 1e-6
N_BRANCH = 3
D_RNN = 1024
LRU_HEADS = 8
LRU_BLOCK = D_RNN // LRU_HEADS
LRU_CONV_W = 4
LRU_C = 8.0
N_HEADS = 8
N_KV_HEADS = 2
KV_GROUPS = N_HEADS // N_KV_HEADS
HEAD_DIM = 128
D_ATTN = N_HEADS * HEAD_DIM
D_KV = N_KV_HEADS * HEAD_DIM
WINDOW = 128
BLOCK_Q = 128
ROPE_BASE = 10000.0
NEG_INF = -1e30
D_POOL = 1024
POOL_WINDOWS = (2, 4, 8, 16)
POOL_GROUP = D_POOL // len(POOL_WINDOWS)
D_IN = D_RNN + D_ATTN + 2 * D_KV + D_POOL + N_BRANCH * D_MODEL
D_FF = 2816
FFN_CONV_W = 3

kernel_name = 'hybrid_flow_prefix_trunk_step'


def rmsnorm(x, g):
    xf = x.astype(jnp.float32)
    xf = xf * lax.rsqrt(jnp.mean(xf * xf, axis=-1, keepdims=True) + EPS)
    return xf.astype(x.dtype) * g


def dwconv(x, w, b, pad_left):
    T = x.shape[1]
    width = w.shape[0]
    xp = jnp.pad(x, ((0, 0), (pad_left, width - 1 - pad_left), (0, 0)))
    y = xp[:, 0:T] * w[0]
    for k in range(1, width):
        y = y + xp[:, k:k + T] * w[k]
    return y + b


def linear_scan(a, b, h0):
    b = b.at[:, 0].add(a[:, 0] * h0)

    def combine(left, right):
        a_l, b_l = left
        a_r, b_r = right
        return a_l * a_r, a_r * b_l + b_r

    _, h = lax.associative_scan(combine, (a, b), axis=1)
    return h


def rglru_direction(x, wa, ba, wx, bx, lam, h0):
    B, T, _ = x.shape
    xb = x.reshape(B, T, LRU_HEADS, LRU_BLOCK)
    r = jax.nn.sigmoid(jnp.einsum('bthi,hij->bthj', xb, wa).reshape(B, T, D_RNN) + ba)
    i = jax.nn.sigmoid(jnp.einsum('bthi,hij->bthj', xb, wx).reshape(B, T, D_RNN) + bx)
    log_a = -LRU_C * r * jax.nn.softplus(-lam)
    a = jnp.exp(log_a)
    b = jnp.sqrt(-jnp.expm1(2.0 * log_a)) * (i * x)
    return linear_scan(a, b, h0)


def rglru_bidir(x, p, h0):
    xf = x.astype(jnp.float32)
    h0 = h0.astype(jnp.float32)
    hf = rglru_direction(xf, p['lru_wa'][0], p['lru_ba'][0], p['lru_wx'][0], p['lru_bx'][0],
                         p['lru_lambda'][0], h0[:, 0])
    hb = jnp.flip(rglru_direction(jnp.flip(xf, 1), p['lru_wa'][1], p['lru_ba'][1], p['lru_wx'][1],
                                  p['lru_bx'][1], p['lru_lambda'][1], h0[:, 1]), 1)
    final = jnp.stack([hf[:, -1], hb[:, 0]], axis=1)
    return (hf + hb).astype(x.dtype), final.astype(x.dtype)


def rope_half(x, pos):
    nf = x.shape[-1] // 2
    freqs = ROPE_BASE ** (-jnp.arange(nf, dtype=jnp.float32) / nf)
    ang = pos.astype(jnp.float32)[:, None] * freqs[None, :]
    cos = jnp.cos(ang)[None, :, None, :]
    sin = jnp.sin(ang)[None, :, None, :]
    xf = x.astype(jnp.float32)
    x1, x2 = xf[..., :nf], xf[..., nf:]
    return jnp.concatenate([x1 * cos - x2 * sin, x1 * sin + x2 * cos], axis=-1)


def axial_rope(x):
    T = x.shape[1]
    rows = T // GRID_W
    row, col = jnp.meshgrid(jnp.arange(rows), jnp.arange(GRID_W), indexing='ij')
    half = x.shape[-1] // 2
    out = jnp.concatenate([rope_half(x[..., :half], row.reshape(-1)),
                           rope_half(x[..., half:], col.reshape(-1))], axis=-1)
    return out.astype(x.dtype)


def sink_logits(sink, B):
    return jnp.broadcast_to(sink.astype(jnp.float32).reshape(1, N_KV_HEADS, KV_GROUPS, 1, 1),
                            (B, N_KV_HEADS, KV_GROUPS, BLOCK_Q, 1))


def attn_context(q, k, v, sink):
    B, S = q.shape[:2]
    scale = HEAD_DIM ** -0.5
    sink_b = sink_logits(sink, B)

    def block(j):
        qj = lax.dynamic_slice_in_dim(q, j * BLOCK_Q, BLOCK_Q, axis=1)
        qj = qj.reshape(B, BLOCK_Q, N_KV_HEADS, KV_GROUPS, HEAD_DIM)
        s = jnp.einsum('bqkgd,bckd->bkgqc', qj, k).astype(jnp.float32) * scale
        pr = jax.nn.softmax(jnp.concatenate([s, sink_b], axis=-1), axis=-1)[..., :S]
        o = jnp.einsum('bkgqc,bckd->bqkgd', pr.astype(v.dtype), v)
        return o.reshape(B, BLOCK_Q, D_ATTN)

    out = lax.map(block, jnp.arange(S // BLOCK_Q))
    return jnp.transpose(out, (1, 0, 2, 3)).reshape(B, S, D_ATTN)


def attn_latent(q, k, v, kc, vc, sink):
    B, T = q.shape[:2]
    Lc = kc.shape[1]
    span = BLOCK_Q + 2 * WINDOW
    scale = HEAD_DIM ** -0.5
    kp = jnp.pad(k, ((0, 0), (WINDOW, WINDOW), (0, 0), (0, 0)))
    vp = jnp.pad(v, ((0, 0), (WINDOW, WINDOW), (0, 0), (0, 0)))
    sink_b = sink_logits(sink, B)

    def block(j):
        start = j * BLOCK_Q
        qj = lax.dynamic_slice_in_dim(q, start, BLOCK_Q, axis=1)
        qj = qj.reshape(B, BLOCK_Q, N_KV_HEADS, KV_GROUPS, HEAD_DIM)
        kj = lax.dynamic_slice_in_dim(kp, start, span, axis=1)
        vj = lax.dynamic_slice_in_dim(vp, start, span, axis=1)
        qpos = start + jnp.arange(BLOCK_Q)
        kpos = start - WINDOW + jnp.arange(span)
        valid = ((jnp.abs(qpos[:, None] - kpos[None, :]) <= WINDOW)
                 & (kpos >= 0)[None, :] & (kpos < T)[None, :])
        s_w = jnp.einsum('bqkgd,bwkd->bkgqw', qj, kj).astype(jnp.float32) * scale
        s_w = jnp.where(valid, s_w, NEG_INF)
        s_c = jnp.einsum('bqkgd,bckd->bkgqc', qj, kc).astype(jnp.float32) * scale
        pr = jax.nn.softmax(jnp.concatenate([s_w, s_c, sink_b], axis=-1), axis=-1)
        o = (jnp.einsum('bkgqw,bwkd->bqkgd', pr[..., :span].astype(vj.dtype), vj)
             + jnp.einsum('bkgqc,bckd->bqkgd', pr[..., span:span + Lc].astype(vc.dtype), vc))
        return o.reshape(B, BLOCK_Q, D_ATTN)

    out = lax.map(block, jnp.arange(T // BLOCK_Q))
    return jnp.transpose(out, (1, 0, 2, 3)).reshape(B, T, D_ATTN)


def multiscale_pool(x, w, scale):
    B, T, _ = x.shape
    xf = x.astype(jnp.float32)
    csum = jnp.concatenate([jnp.zeros((B, 1, D_POOL), jnp.float32), jnp.cumsum(xf, axis=1)], axis=1)
    t = jnp.arange(T)
    parts = []
    for gi, win in enumerate(POOL_WINDOWS):
        c0, c1 = gi * POOL_GROUP, (gi + 1) * POOL_GROUP
        lo = jnp.clip(t - win // 2, 0, T)
        hi = jnp.clip(t + win // 2, 0, T)
        cnt = (hi - lo).astype(jnp.float32)[None, :, None]
        mean = (csum[:, hi, c0:c1] - csum[:, lo, c0:c1]) / cnt
        parts.append(mean - xf[..., c0:c1])
    pooled = jnp.stack(parts, axis=2).astype(x.dtype)
    y = jnp.einsum('btgi,gij->btgj', pooled, w).reshape(B, T, D_POOL)
    return y * scale


def trunk_layer(x, mod, p, ctx_k, ctx_v, ctx_h):
    latent = ctx_k is not None
    B, T, _ = x.shape
    shift1, scale1, gate1, shift2, scale2, gate2 = jnp.split(mod, 6, axis=-1)
    h = rmsnorm(x, p['norm1']) * (1.0 + scale1) + shift1
    proj = h @ p['w_in']
    s0 = D_RNN
    s1 = s0 + D_ATTN
    s2 = s1 + D_KV
    s3 = s2 + D_KV
    s4 = s3 + D_POOL
    xa, q, k, v, xc, g = (proj[..., :s0], proj[..., s0:s1], proj[..., s1:s2],
                          proj[..., s2:s3], proj[..., s3:s4], proj[..., s4:])
    xa = dwconv(xa, p['lru_conv'], p['lru_conv_b'], LRU_CONV_W // 2)
    h0 = ctx_h if latent else jnp.zeros((B, 2, D_RNN), x.dtype)
    ya, h_final = rglru_bidir(xa, p, h0)
    q = q.reshape(B, T, N_HEADS, HEAD_DIM)
    k = k.reshape(B, T, N_KV_HEADS, HEAD_DIM)
    v = v.reshape(B, T, N_KV_HEADS, HEAD_DIM)
    if latent:
        yb = attn_latent(axial_rope(q), axial_rope(k), v, ctx_k, ctx_v, p['attn_sink'])
    else:
        yb = attn_context(q, k, v, p['attn_sink'])
    yc = multiscale_pool(xc, p['pool_w'], p['pool_scale'])
    gates = jax.nn.sigmoid(g + p['b_gate']).reshape(B, T, N_BRANCH, D_MODEL)
    merged = (gates[:, :, 0] * (ya @ p['w_branch'][0])
              + gates[:, :, 1] * (yb @ p['w_branch'][1])
              + gates[:, :, 2] * (yc @ p['w_branch'][2]))
    x = x + gate1 * (merged @ p['w_out'])
    h2 = rmsnorm(x, p['norm2']) * (1.0 + scale2) + shift2
    up = h2 @ p['ffn_up']
    gff = dwconv(up[..., :D_FF], p['ffn_conv'], p['ffn_conv_b'], FFN_CONV_W // 2)
    x = x + gate2 * ((jax.nn.gelu(gff) * up[..., D_FF:]) @ p['ffn_down'])
    return x, k, v, h_final


def setup_inputs(seed: int = 0) -> dict:
    key = jax.random.key(seed)
    ks = iter(jax.random.split(key, 32))

    def nrm(shape, s):
        return jax.random.normal(next(ks), shape, jnp.float32) * s

    u = jax.random.uniform(next(ks), (DEPTH, 2, D_RNN), jnp.float32, minval=0.9, maxval=0.999)
    a0 = u ** (1.0 / LRU_C)
    lru_lambda = jnp.log(a0) - jnp.log1p(-a0)
    return {
        'x_prompt': nrm((BATCH, SEQ, D_MODEL), 1.0),
        'x_sample': nrm((DEC_BATCH, DEC_SEQ, D_MODEL), 1.0),
        'cache_k': nrm((DEC_BATCH, DEPTH, PAST_LEN, N_KV_HEADS, HEAD_DIM), 1.0),
        'cache_v': nrm((DEC_BATCH, DEPTH, PAST_LEN, N_KV_HEADS, HEAD_DIM), 1.0),
        'state_lru': nrm((DEC_BATCH, DEPTH, 2, D_RNN), 0.5),
        'c': nrm((DEC_BATCH, D_MODEL), 1.0),
        'c_ctx': nrm((D_MODEL,), 1.0),
        'w_ada': nrm((DEPTH, D_MODEL, 6 * D_MODEL), 0.5 * D_MODEL ** -0.5),
        'b_ada': nrm((DEPTH, 6 * D_MODEL), 0.02),
        'norm1': 1.0 + nrm((DEPTH, D_MODEL), 0.05),
        'norm2': 1.0 + nrm((DEPTH, D_MODEL), 0.05),
        'w_in': nrm((DEPTH, D_MODEL, D_IN), D_MODEL ** -0.5),
        'b_gate': nrm((DEPTH, N_BRANCH * D_MODEL), 0.1),
        'lru_conv': nrm((DEPTH, LRU_CONV_W, D_RNN), LRU_CONV_W ** -0.5),
        'lru_conv_b': nrm((DEPTH, D_RNN), 0.02),
        'lru_wa': nrm((DEPTH, 2, LRU_HEADS, LRU_BLOCK, LRU_BLOCK), LRU_BLOCK ** -0.5),
        'lru_ba': nrm((DEPTH, 2, D_RNN), 0.1),
        'lru_wx': nrm((DEPTH, 2, LRU_HEADS, LRU_BLOCK, LRU_BLOCK), LRU_BLOCK ** -0.5),
        'lru_bx': nrm((DEPTH, 2, D_RNN), 0.1),
        'lru_lambda': lru_lambda,
        'attn_sink': nrm((DEPTH, N_HEADS), 0.5),
        'pool_w': nrm((DEPTH, len(POOL_WINDOWS), POOL_GROUP, POOL_GROUP), POOL_GROUP ** -0.5),
        'pool_scale': 1.0 + nrm((DEPTH, D_POOL), 0.05),
        'w_branch': nrm((DEPTH, N_BRANCH, D_MODEL, D_MODEL), D_MODEL ** -0.5),
        'w_out': nrm((DEPTH, D_MODEL, D_MODEL), D_MODEL ** -0.5),
        'ffn_up': nrm((DEPTH, D_MODEL, 2 * D_FF), D_MODEL ** -0.5),
        'ffn_conv': nrm((DEPTH, FFN_CONV_W, D_FF), FFN_CONV_W ** -0.5),
        'ffn_conv_b': nrm((DEPTH, D_FF), 0.02),
        'ffn_down': nrm((DEPTH, D_FF, D_MODEL), D_FF ** -0.5),
        'final_norm': 1.0 + nrm((D_MODEL,), 0.05),
    }


def reference(x_prompt, x_sample, cache_k, cache_v, state_lru, c, c_ctx, w_ada, b_ada, norm1, norm2,
              w_in, b_gate, lru_conv, lru_conv_b, lru_wa, lru_ba, lru_wx, lru_bx, lru_lambda, attn_sink,
              pool_w, pool_scale, w_branch, w_out, ffn_up, ffn_conv, ffn_conv_b, ffn_down, final_norm):
    xp = x_prompt
    xs = x_sample
    ks, vs, hs = [], [], []
    for l in range(DEPTH):
        p = {
            'norm1': norm1[l], 'norm2': norm2[l], 'w_in': w_in[l], 'b_gate': b_gate[l],
            'lru_conv': lru_conv[l], 'lru_conv_b': lru_conv_b[l], 'lru_wa': lru_wa[l], 'lru_ba': lru_ba[l],
            'lru_wx': lru_wx[l], 'lru_bx': lru_bx[l], 'lru_lambda': lru_lambda[l], 'attn_sink': attn_sink[l],
            'pool_w': pool_w[l], 'pool_scale': pool_scale[l], 'w_branch': w_branch[l], 'w_out': w_out[l],
            'ffn_up': ffn_up[l], 'ffn_conv': ffn_conv[l], 'ffn_conv_b': ffn_conv_b[l], 'ffn_down': ffn_down[l],
        }
        mod_ctx = jax.nn.silu(c_ctx) @ w_ada[l] + b_ada[l]
        mod_lat = (jax.nn.silu(c) @ w_ada[l] + b_ada[l])[:, None, :]
        xp, k_l, v_l, h_l = trunk_layer(xp, mod_ctx, p, None, None, None)
        xs, _, _, _ = trunk_layer(xs, mod_lat, p, cache_k[:, l], cache_v[:, l], state_lru[:, l])
        ks.append(k_l)
        vs.append(v_l)
        hs.append(h_l)
    y_prompt = rmsnorm(xp, final_norm)
    y_sample = rmsnorm(xs, final_norm)
    new_cache_k = jnp.stack(ks, axis=1)
    new_cache_v = jnp.stack(vs, axis=1)
    new_state_lru = jnp.stack(hs, axis=1)
    return (y_prompt, y_sample, new_cache_k, new_cache_v, new_state_lru)
```

```python
import functools

import numpy as np
import jax
import jax.numpy as jnp
from jax import lax
from jax.experimental import pallas as pl
from jax.experimental.pallas import tpu as pltpu

F32 = jnp.float32
BF16 = jnp.bfloat16

D_MODEL = 1024
DEPTH = 2
GRID_W = 64
EPS = 1e-6
N_BRANCH = 3
D_RNN = 1024
LRU_HEADS = 8
LRU_BLOCK = D_RNN // LRU_HEADS
LRU_C = 8.0
N_HEADS = 8
N_KV_HEADS = 2
KV_GROUPS = N_HEADS // N_KV_HEADS
HEAD_DIM = 128
D_KV = N_KV_HEADS * HEAD_DIM
WINDOW = 128
BLOCK_Q = 128
ROPE_BASE = 10000.0
NEG_INF = -1e30
D_POOL = 1024
POOL_WINDOWS = (2, 4, 8, 16)
POOL_GROUP = D_POOL // len(POOL_WINDOWS)
D_FF = 2816
D_IN = D_RNN + N_HEADS * HEAD_DIM + 2 * D_KV + D_POOL + N_BRANCH * D_MODEL

COL_XA = 0
COL_Q = 1024
COL_XC = 2048
COL_G = 3072
COL_K = 6144
COL_V = 6400

VMEM_LIMIT_BYTES = 52 * 1024 * 1024
SUBLANES = 8
LANES = 128
BF16_ROWS = 16

SCAN_CHUNK = 256
SCAN_PITCH = 260
SCAN_ROWS = 2048
POOL_PAD = 8


def _params(*sem):
    return pltpu.CompilerParams(dimension_semantics=sem, vmem_limit_bytes=VMEM_LIMIT_BYTES)


def _dot(a, b):
    return jnp.dot(a, b, preferred_element_type=F32)


def _dot_nt(a, b):
    return lax.dot_general(a, b, (((1,), (1,)), ((), ())), preferred_element_type=F32)


def _sigmoid(z):
    return 0.5 * (1.0 + jnp.tanh(0.5 * z))


def _rms(x):
    return x * lax.rsqrt(jnp.mean(x * x, axis=-1, keepdims=True) + EPS)


def _ada_kernel(c_ref, w_ref, b_ref, o_ref):
    c = c_ref[...]
    s = c * _sigmoid(c)
    o_ref[0] = _dot(s.astype(BF16), w_ref[0].astype(BF16)) + b_ref[0]


def _ada(c_rows, w_ada, b_ada):
    tn = 1536
    return pl.pallas_call(
        _ada_kernel,
        grid=(DEPTH, 6 * D_MODEL // tn),
        in_specs=[
            pl.BlockSpec((SUBLANES, D_MODEL), lambda l, j: (0, 0)),
            pl.BlockSpec((1, D_MODEL, tn), lambda l, j: (l, 0, j)),
            pl.BlockSpec((1, 1, tn), lambda l, j: (l, 0, j)),
        ],
        out_specs=pl.BlockSpec((1, SUBLANES, tn), lambda l, j: (l, 0, j)),
        out_shape=jax.ShapeDtypeStruct((DEPTH, SUBLANES, 6 * D_MODEL), F32),
        compiler_params=_params("parallel", "parallel"),
        name="ada_mod",
    )(c_rows, w_ada, b_ada.reshape(DEPTH, 1, 6 * D_MODEL))


def _inproj_kernel(x_ref, mod_ref, g_ref, w_ref, o_ref, h_ref):
    @pl.when(pl.program_id(1) == 0)
    def _():
        mod = mod_ref[0]
        shift = mod[:, 0:D_MODEL]
        scale = mod[:, D_MODEL:2 * D_MODEL]
        h = _rms(x_ref[...]) * g_ref[...]
        h_ref[...] = (h * (1.0 + scale) + shift).astype(BF16)

    o_ref[...] = _dot(h_ref[...], w_ref[...])


def _inproj(x, mod, rows_per_mod, g, w):
    m = x.shape[0]
    tm, tn = 1024, 512
    per = rows_per_mod // tm
    return pl.pallas_call(
        _inproj_kernel,
        grid=(m // tm, D_IN // tn),
        in_specs=[
            pl.BlockSpec((tm, D_MODEL), lambda i, j: (i, 0)),
            pl.BlockSpec((1, 1, 6 * D_MODEL), lambda i, j: (i // per, 0, 0)),
            pl.BlockSpec((1, D_MODEL), lambda i, j: (0, 0)),
            pl.BlockSpec((D_MODEL, tn), lambda i, j: (0, j)),
        ],
        out_specs=pl.BlockSpec((tm, tn), lambda i, j: (i, j)),
        out_shape=jax.ShapeDtypeStruct((m, D_IN), F32),
        scratch_shapes=[pltpu.VMEM((tm, D_MODEL), BF16)],
        compiler_params=_params("parallel", "arbitrary"),
        name="in_proj",
    )(x, mod, g, w)


def _lru_kernel(*refs, seq_len, latent):
    if latent:
        (x_ref, cw_ref, cb_ref, wg_ref, bg_ref, lam_ref, h0_ref, y_ref,
         af_ref, bf_ref, ab_ref, bb_ref, hf_ref, hb_ref) = refs
    else:
        (x_ref, cw_ref, cb_ref, wg_ref, bg_ref, lam_ref, y_ref, fin_ref,
         af_ref, bf_ref, ab_ref, bb_ref, hf_ref, hb_ref) = refs
    rows = SCAN_ROWS
    nchunk = rows // SCAN_CHUNK
    x = x_ref[...]
    pos = lax.broadcasted_iota(jnp.int32, (rows, LRU_BLOCK), 0) & (seq_len - 1)
    xm2 = jnp.where(pos >= 2, pltpu.roll(x, 2, 0), 0.0)
    xm1 = jnp.where(pos >= 1, pltpu.roll(x, 1, 0), 0.0)
    xp1 = jnp.where(pos <= seq_len - 2, pltpu.roll(x, rows - 1, 0), 0.0)
    cw = cw_ref[...]
    xc = xm2 * cw[0:1] + xm1 * cw[1:2] + x * cw[2:3] + xp1 * cw[3:4] + cb_ref[...]
    z = _dot(xc.astype(BF16), wg_ref[0]) + bg_ref[0]
    lam = lam_ref[0]
    for d, (a_ref, b_ref) in enumerate(((af_ref, bf_ref), (ab_ref, bb_ref))):
        zr = z[:, 2 * d * LRU_BLOCK:(2 * d + 1) * LRU_BLOCK]
        zi = z[:, (2 * d + 1) * LRU_BLOCK:(2 * d + 2) * LRU_BLOCK]
        nl = -lam[:, d * LRU_BLOCK:(d + 1) * LRU_BLOCK]
        softplus = jnp.maximum(nl, 0.0) + jnp.log(1.0 + jnp.exp(-jnp.abs(nl)))
        log_a = (-LRU_C * softplus) * _sigmoid(zr)
        a = jnp.exp(log_a)
        b = jnp.sqrt(1.0 - a * a) * (_sigmoid(zi) * xc)
        for c in range(nchunk):
            a_ref[pl.ds(c * SCAN_PITCH, SCAN_CHUNK), :] = a[c * SCAN_CHUNK:(c + 1) * SCAN_CHUNK]
            b_ref[pl.ds(c * SCAN_PITCH, SCAN_CHUNK), :] = b[c * SCAN_CHUNK:(c + 1) * SCAN_CHUNK]

    def step(t, carry):
        hf, hb, pf, pb = carry
        rf = pl.ds(t, nchunk, stride=SCAN_PITCH)
        rb = pl.ds(SCAN_CHUNK - 1 - t, nchunk, stride=SCAN_PITCH)
        a_f = af_ref[rf, :]
        a_b = ab_ref[rb, :]
        hf = a_f * hf + bf_ref[rf, :]
        hb = a_b * hb + bb_ref[rb, :]
        hf_ref[rf, :] = hf
        hb_ref[rb, :] = hb
        if latent:
            pf = pf * a_f
            pb = pb * a_b
            af_ref[rf, :] = pf
            ab_ref[rb, :] = pb
        return hf, hb, pf, pb

    zero = jnp.zeros((nchunk, LRU_BLOCK), F32)
    one = jnp.ones((nchunk, LRU_BLOCK), F32)
    lax.fori_loop(0, SCAN_CHUNK, step, (zero, zero, one, one), unroll=8)

    def chunk(ref, c):
        return ref[pl.ds(c * SCAN_PITCH, SCAN_CHUNK), :]

    if latent:
        carry = h0_ref[0, 0]
        fwd = []
        for c in range(nchunk):
            h = chunk(hf_ref, c) + chunk(af_ref, c) * carry
            carry = h[SCAN_CHUNK - 1:SCAN_CHUNK]
            fwd.append(h)
        carry = h0_ref[1, 0]
        for c in reversed(range(nchunk)):
            h = chunk(hb_ref, c) + chunk(ab_ref, c) * carry
            carry = h[0:1]
            y_ref[pl.ds(c * SCAN_CHUNK, SCAN_CHUNK), :] = (fwd[c] + h).astype(y_ref.dtype)
    else:
        for c in range(nchunk):
            y_ref[pl.ds(c * SCAN_CHUNK, SCAN_CHUNK), :] = (
                chunk(hf_ref, c) + chunk(hb_ref, c)).astype(y_ref.dtype)
        fin_ref[0] = hf_ref[pl.ds(SCAN_CHUNK - 1, nchunk, stride=SCAN_PITCH), :]
        fin_ref[1] = hb_ref[pl.ds(0, nchunk, stride=SCAN_PITCH), :]


def _lru(proj, lp, seq_len, h0):
    m = proj.shape[0]
    latent = h0 is not None
    rows = SCAN_ROWS
    in_specs = [
        pl.BlockSpec((rows, LRU_BLOCK), lambda r, h: (r, COL_XA // LRU_BLOCK + h)),
        pl.BlockSpec((4, LRU_BLOCK), lambda r, h: (0, h)),
        pl.BlockSpec((1, LRU_BLOCK), lambda r, h: (0, h)),
        pl.BlockSpec((1, LRU_BLOCK, 4 * LRU_BLOCK), lambda r, h: (h, 0, 0)),
        pl.BlockSpec((1, 1, 4 * LRU_BLOCK), lambda r, h: (h, 0, 0)),
        pl.BlockSpec((1, 1, 2 * LRU_BLOCK), lambda r, h: (h, 0, 0)),
    ]
    args = [proj, lp["conv_w"], lp["conv_b"], lp["wg"], lp["bg"], lp["lam"]]
    y_spec = pl.BlockSpec((rows, LRU_BLOCK), lambda r, h: (r, h))
    y_shape = jax.ShapeDtypeStruct((m, D_RNN), BF16)
    if latent:
        in_specs.append(pl.BlockSpec((2, 1, 1, LRU_BLOCK), lambda r, h: (0, r, 0, h)))
        args.append(h0)
        out_specs, out_shape = y_spec, y_shape
    else:
        nseq = m // seq_len
        out_specs = (y_spec, pl.BlockSpec((2, rows // seq_len, LRU_BLOCK), lambda r, h: (0, r, h)))
        out_shape = (y_shape, jax.ShapeDtypeStruct((2, nseq, D_RNN), F32))
    scan_buf = pltpu.VMEM((rows // SCAN_CHUNK * SCAN_PITCH, LRU_BLOCK), F32)
    return pl.pallas_call(
        functools.partial(_lru_kernel, seq_len=seq_len, latent=latent),
        grid=(m // rows, LRU_HEADS),
        in_specs=in_specs,
        out_specs=out_specs,
        out_shape=out_shape,
        scratch_shapes=[scan_buf] * 6,
        compiler_params=_params("parallel", "parallel"),
        name="rglru_latent" if latent else "rglru_context",
    )(*args)


def _stack_heads(q_ref, kh):
    parts = [q_ref[:, (kh * KV_GROUPS + g) * HEAD_DIM:(kh * KV_GROUPS + g + 1) * HEAD_DIM]
             for g in range(KV_GROUPS)]
    return jnp.concatenate(parts, axis=0).astype(BF16)


def _sink_column(sink_ref, kh, rows):
    parts = [jnp.full((rows, 1), sink_ref[kh * KV_GROUPS + g], F32) for g in range(KV_GROUPS)]
    return jnp.concatenate(parts, axis=0)


def _attn_ctx_kernel(sink_ref, q_ref, k_ref, v_ref, y_ref):
    seq = q_ref.shape[0]
    scale = HEAD_DIM ** -0.5
    for kh in range(N_KV_HEADS):
        q = _stack_heads(q_ref, kh)
        k = k_ref[:, kh * HEAD_DIM:(kh + 1) * HEAD_DIM].astype(BF16)
        v = v_ref[:, kh * HEAD_DIM:(kh + 1) * HEAD_DIM].astype(BF16)
        s = _dot_nt(q, k) * scale
        sink = _sink_column(sink_ref, kh, seq)
        m = jnp.maximum(jnp.max(s, axis=-1, keepdims=True), sink)
        e = jnp.exp(s - m)
        denom = jnp.sum(e, axis=-1, keepdims=True) + jnp.exp(sink - m)
        o = _dot(e.astype(BF16), v) * (1.0 / denom)
        for g in range(KV_GROUPS):
            h = kh * KV_GROUPS + g
            y_ref[:, h * HEAD_DIM:(h + 1) * HEAD_DIM] = o[g * seq:(g + 1) * seq].astype(y_ref.dtype)


def _attn_ctx(proj, sink, seq_len):
    m = proj.shape[0]
    return pl.pallas_call(
        _attn_ctx_kernel,
        grid=(m // seq_len,),
        in_specs=[
            pl.BlockSpec(memory_space=pltpu.SMEM),
            pl.BlockSpec((seq_len, N_HEADS * HEAD_DIM), lambda b: (b, COL_Q // (N_HEADS * HEAD_DIM))),
            pl.BlockSpec((seq_len, D_KV), lambda b: (b, COL_K // D_KV)),
            pl.BlockSpec((seq_len, D_KV), lambda b: (b, COL_V // D_KV)),
        ],
        out_specs=pl.BlockSpec((seq_len, N_HEADS * HEAD_DIM), lambda b: (b, 0)),
        out_shape=jax.ShapeDtypeStruct((m, N_HEADS * HEAD_DIM), BF16),
        compiler_params=_params("parallel"),
        name="attn_context",
    )(sink, proj, proj, proj)


def _rope_tables(seq_len):
    nf = HEAD_DIM // 4
    freqs = ROPE_BASE ** (-np.arange(nf, dtype=np.float64) / nf)
    t = np.arange(seq_len)
    ang_row = (t // GRID_W)[:, None] * freqs[None, :]
    ang_col = (t % GRID_W)[:, None] * freqs[None, :]
    ang = np.concatenate([ang_row, ang_row, ang_col, ang_col], axis=1)
    first = (np.arange(HEAD_DIM) % (2 * nf)) < nf
    cos = np.cos(ang)
    sin = np.sin(ang)
    sin_a = np.where(first[None, :], -sin, 0.0)
    sin_b = np.where(first[None, :], 0.0, sin)
    return tuple(jnp.asarray(a, F32) for a in (cos, sin_a, sin_b))


def _rope_kernel(q_ref, k_ref, v_ref, cos_ref, sa_ref, sb_ref, qo_ref, ko_ref, vo_ref):
    cos = cos_ref[...]
    sa = sa_ref[...]
    sb = sb_ref[...]
    nf = HEAD_DIM // 4

    def rope(x):
        return x * cos + pltpu.roll(x, HEAD_DIM - nf, 1) * sa + pltpu.roll(x, nf, 1) * sb

    for h in range(N_HEADS):
        sl = slice(h * HEAD_DIM, (h + 1) * HEAD_DIM)
        qo_ref[:, sl] = rope(q_ref[:, sl]).astype(BF16)
    for h in range(N_KV_HEADS):
        sl = slice(h * HEAD_DIM, (h + 1) * HEAD_DIM)
        ko_ref[:, sl] = rope(k_ref[:, sl]).astype(BF16)
    vo_ref[...] = v_ref[...].astype(BF16)


def _rope(proj, seq_len):
    m = proj.shape[0]
    tm = 512
    per = seq_len // tm
    cos, sa, sb = _rope_tables(seq_len)
    tab = pl.BlockSpec((tm, HEAD_DIM), lambda i: (i % per, 0))
    return pl.pallas_call(
        _rope_kernel,
        grid=(m // tm,),
        in_specs=[
            pl.BlockSpec((tm, N_HEADS * HEAD_DIM), lambda i: (i, COL_Q // (N_HEADS * HEAD_DIM))),
            pl.BlockSpec((tm, D_KV), lambda i: (i, COL_K // D_KV)),
            pl.BlockSpec((tm, D_KV), lambda i: (i, COL_V // D_KV)),
            tab, tab, tab,
        ],
        out_specs=(
            pl.BlockSpec((tm, N_HEADS * HEAD_DIM), lambda i: (i, 0)),
            pl.BlockSpec((tm, D_KV), lambda i: (i, 0)),
            pl.BlockSpec((tm, D_KV), lambda i: (i, 0)),
        ),
        out_shape=(
            jax.ShapeDtypeStruct((m, N_HEADS * HEAD_DIM), BF16),
            jax.ShapeDtypeStruct((m, D_KV), BF16),
            jax.ShapeDtypeStruct((m, D_KV), BF16),
        ),
        compiler_params=_params("parallel"),
        name="rope_qk",
    )(proj, proj, proj, cos, sa, sb)


def _attn_lat_kernel(sink_ref, q_ref, kp_ref, kc_ref, kn_ref, vp_ref, vc_ref, vn_ref,
                     ck_ref, cv_ref, y_ref, *, nblk):
    j = pl.program_id(1)
    scale = HEAD_DIM ** -0.5
    rows = KV_GROUPS * BLOCK_Q
    row = lax.broadcasted_iota(jnp.int32, (rows, BLOCK_Q), 0) & (BLOCK_Q - 1)
    col = lax.broadcasted_iota(jnp.int32, (rows, BLOCK_Q), 1)
    ok_prev = col >= row + jnp.where(j > 0, 0, BLOCK_Q)
    ok_next = col <= row - jnp.where(j < nblk - 1, 0, BLOCK_Q)
    for kh in range(N_KV_HEADS):
        sl = slice(kh * HEAD_DIM, (kh + 1) * HEAD_DIM)
        q = _stack_heads(q_ref, kh)
        s_p = jnp.where(ok_prev, _dot_nt(q, kp_ref[:, sl]) * scale, NEG_INF)
        s_c = _dot_nt(q, kc_ref[:, sl]) * scale
        s_n = jnp.where(ok_next, _dot_nt(q, kn_ref[:, sl]) * scale, NEG_INF)
        s_x = _dot_nt(q, ck_ref[0, :, sl]) * scale
        sink = _sink_column(sink_ref, kh, BLOCK_Q)
        m = sink
        for s in (s_p, s_c, s_n, s_x):
            m = jnp.maximum(m, jnp.max(s, axis=-1, keepdims=True))
        denom = jnp.exp(sink - m)
        o = jnp.zeros((rows, HEAD_DIM), F32)
        for s, v in ((s_p, vp_ref[:, sl]), (s_c, vc_ref[:, sl]), (s_n, vn_ref[:, sl]),
                     (s_x, cv_ref[0, :, sl])):
            e = jnp.exp(s - m)
            denom = denom + jnp.sum(e, axis=-1, keepdims=True)
            o = o + _dot(e.astype(BF16), v)
        o = o * (1.0 / denom)
        for g in range(KV_GROUPS):
            h = kh * KV_GROUPS + g
            y_ref[:, h * HEAD_DIM:(h + 1) * HEAD_DIM] = (
                o[g * BLOCK_Q:(g + 1) * BLOCK_Q].astype(y_ref.dtype))


def _attn_lat(qr, kr, vb, ck, cv, sink, seq_len):
    m = qr.shape[0]
    nblk = seq_len // BLOCK_Q
    nb = m // seq_len
    past = ck.shape[1]

    def cur(b, j):
        return (b * nblk + j, 0)

    def prev(b, j):
        return (b * nblk + jnp.maximum(j - 1, 0), 0)

    def nxt(b, j):
        return (b * nblk + jnp.minimum(j + 1, nblk - 1), 0)

    kv = lambda f: pl.BlockSpec((BLOCK_Q, D_KV), f)
    return pl.pallas_call(
        functools.partial(_attn_lat_kernel, nblk=nblk),
        grid=(nb, nblk),
        in_specs=[
            pl.BlockSpec(memory_space=pltpu.SMEM),
            pl.BlockSpec((BLOCK_Q, N_HEADS * HEAD_DIM), cur),
            kv(prev), kv(cur), kv(nxt), kv(prev), kv(cur), kv(nxt),
            pl.BlockSpec((1, past, D_KV), lambda b, j: (b, 0, 0)),
            pl.BlockSpec((1, past, D_KV), lambda b, j: (b, 0, 0)),
        ],
        out_specs=pl.BlockSpec((BLOCK_Q, N_HEADS * HEAD_DIM), cur),
        out_shape=jax.ShapeDtypeStruct((m, N_HEADS * HEAD_DIM), BF16),
        compiler_params=_params("parallel", "parallel"),
        name="attn_latent",
    )(sink, qr, kr, kr, kr, vb, vb, vb, ck, cv)


def _pool_kernel(x_ref, w_ref, s_ref, y_ref, pad_ref, *, seq_len):
    rows = x_ref.shape[0]
    nseq = rows // seq_len
    zeros = jnp.zeros((nseq, POOL_PAD, D_POOL), F32)
    pad_ref[:, 0:POOL_PAD, :] = zeros
    pad_ref[:, POOL_PAD + seq_len:, :] = zeros
    pad_ref[:, POOL_PAD:POOL_PAD + seq_len, :] = x_ref[...].reshape(nseq, seq_len, D_POOL)
    t = lax.broadcasted_iota(jnp.int32, (nseq, seq_len, POOL_GROUP), 1)
    for gi, win in enumerate(POOL_WINDOWS):
        cs = slice(gi * POOL_GROUP, (gi + 1) * POOL_GROUP)
        half = win // 2
        total = pad_ref[:, POOL_PAD - half:POOL_PAD - half + seq_len, cs]
        for d in range(1 - half, half):
            total = total + pad_ref[:, POOL_PAD + d:POOL_PAD + d + seq_len, cs]
        cnt = (jnp.minimum(t + half, seq_len) - jnp.maximum(t - half, 0)).astype(F32)
        pooled = total / cnt - pad_ref[:, POOL_PAD:POOL_PAD + seq_len, cs]
        pooled = pooled.reshape(rows, POOL_GROUP).astype(BF16)
        y_ref[:, cs] = (_dot(pooled, w_ref[gi]) * s_ref[:, cs]).astype(y_ref.dtype)


def _pool(proj, w, s, seq_len):
    m = proj.shape[0]
    rows = 2048
    nseq = rows // seq_len
    return pl.pallas_call(
        functools.partial(_pool_kernel, seq_len=seq_len),
        grid=(m // rows,),
        in_specs=[
            pl.BlockSpec((rows, D_POOL), lambda r: (r, COL_XC // D_POOL)),
            pl.BlockSpec((len(POOL_WINDOWS), POOL_GROUP, POOL_GROUP), lambda r: (0, 0, 0)),
            pl.BlockSpec((1, D_POOL), lambda r: (0, 0)),
        ],
        out_specs=pl.BlockSpec((rows, D_POOL), lambda r: (r, 0)),
        out_shape=jax.ShapeDtypeStruct((m, D_POOL), BF16),
        scratch_shapes=[pltpu.VMEM((nseq, seq_len + 2 * POOL_PAD, D_POOL), F32)],
        compiler_params=_params("parallel"),
        name="pool_mix",
    )(proj, w, s)


def _merge_kernel(ya_ref, yb_ref, yc_ref, g0_ref, g1_ref, g2_ref, x_ref, mod_ref, bg_ref,
                  wb_ref, wo_ref, n2_ref, x1_ref, h2_ref):
    mod = mod_ref[0]
    merged = None
    for k, (y_ref, g_ref) in enumerate(((ya_ref, g0_ref), (yb_ref, g1_ref), (yc_ref, g2_ref))):
        gate = _sigmoid(g_ref[...] + bg_ref[:, k * D_MODEL:(k + 1) * D_MODEL])
        term = gate * _dot(y_ref[...], wb_ref[k])
        merged = term if merged is None else merged + term
    gate1 = mod[:, 2 * D_MODEL:3 * D_MODEL]
    x1 = x_ref[...] + gate1 * _dot(merged.astype(BF16), wo_ref[...])
    x1_ref[...] = x1
    shift2 = mod[:, 3 * D_MODEL:4 * D_MODEL]
    scale2 = mod[:, 4 * D_MODEL:5 * D_MODEL]
    h2_ref[...] = (_rms(x1) * n2_ref[...] * (1.0 + scale2) + shift2).astype(BF16)


def _merge(ya, yb, yc, proj, x, mod, rows_per_mod, bg, wb, wo, n2):
    m = x.shape[0]
    tm = 256
    per = rows_per_mod // tm
    row = lambda c: pl.BlockSpec((tm, D_MODEL), lambda i: (i, c))
    const2 = lambda shape: pl.BlockSpec(shape, lambda i: (0,) * len(shape))
    return pl.pallas_call(
        _merge_kernel,
        grid=(m // tm,),
        in_specs=[
            row(0), row(0), row(0),
            row(COL_G // D_MODEL), row(COL_G // D_MODEL + 1), row(COL_G // D_MODEL + 2),
            row(0),
            pl.BlockSpec((1, 1, 6 * D_MODEL), lambda i: (i // per, 0, 0)),
            const2((1, N_BRANCH * D_MODEL)),
            const2((N_BRANCH, D_MODEL, D_MODEL)),
            const2((D_MODEL, D_MODEL)),
            const2((1, D_MODEL)),
        ],
        out_specs=(row(0), row(0)),
        out_shape=(jax.ShapeDtypeStruct((m, D_MODEL), F32), jax.ShapeDtypeStruct((m, D_MODEL), BF16)),
        compiler_params=_params("parallel"),
        name="merge_out",
    )(ya, yb, yc, proj, proj, proj, x, mod, bg, wb, wo, n2)


FFN_CK = 256


def _gelu_tanh(x):
    return 0.5 * x * (1.0 + jnp.tanh(np.sqrt(2.0 / np.pi) * (x + 0.044715 * (x * x * x))))


def _ffn_kernel(*refs, seq_len, tm, final):
    if final:
        (h_ref, hp_ref, hn_ref, x_ref, mod_ref, wg_ref, wv_ref, cw_ref, cb_ref, wd_ref, fn_ref,
         o_ref, hx_ref, u_ref, acc_ref) = refs
    else:
        (h_ref, hp_ref, hn_ref, x_ref, mod_ref, wg_ref, wv_ref, cw_ref, cb_ref, wd_ref,
         o_ref, hx_ref, u_ref, acc_ref) = refs
    i = pl.program_id(0)
    j = pl.program_id(1)
    halo = BF16_ROWS

    @pl.when(j == 0)
    def _():
        hx_ref[0:halo, :] = hp_ref[...]
        hx_ref[halo:halo + tm, :] = h_ref[...]
        hx_ref[halo + tm:, :] = hn_ref[...]
        acc_ref[...] = jnp.zeros_like(acc_ref)

    u_ref[...] = _dot(hx_ref[...], wg_ref[...])
    uv = _dot(h_ref[...], wv_ref[...])
    pos = (i * tm + lax.broadcasted_iota(jnp.int32, (tm, FFN_CK), 0)) & (seq_len - 1)
    um = jnp.where(pos >= 1, u_ref[halo - 1:halo - 1 + tm, :], 0.0)
    up = jnp.where(pos <= seq_len - 2, u_ref[halo + 1:halo + 1 + tm, :], 0.0)
    cw = cw_ref[...]
    gff = um * cw[0:1] + u_ref[halo:halo + tm, :] * cw[1:2] + up * cw[2:3] + cb_ref[...]
    act = (_gelu_tanh(gff) * uv).astype(BF16)
    acc_ref[...] += _dot(act, wd_ref[...])

    @pl.when(j == pl.num_programs(1) - 1)
    def _():
        gate2 = mod_ref[0][:, 5 * D_MODEL:6 * D_MODEL]
        out = x_ref[...] + gate2 * acc_ref[...]
        if final:
            out = _rms(out) * fn_ref[...]
        o_ref[...] = out


def _ffn(h2, x1, mod, rows_per_mod, wup, cw, cb, wd, seq_len, final_norm):
    m = x1.shape[0]
    tm = 1024
    per = rows_per_mod // tm
    nck = D_FF // FFN_CK
    hb = tm // BF16_ROWS
    last_halo = m // BF16_ROWS - 1
    final = final_norm is not None
    in_specs = [
        pl.BlockSpec((tm, D_MODEL), lambda i, j: (i, 0)),
        pl.BlockSpec((BF16_ROWS, D_MODEL), lambda i, j: (jnp.maximum(i * hb - 1, 0), 0)),
        pl.BlockSpec((BF16_ROWS, D_MODEL), lambda i, j: (jnp.minimum((i + 1) * hb, last_halo), 0)),
        pl.BlockSpec((tm, D_MODEL), lambda i, j: (i, 0)),
        pl.BlockSpec((1, 1, 6 * D_MODEL), lambda i, j: (i // per, 0, 0)),
        pl.BlockSpec((D_MODEL, FFN_CK), lambda i, j: (0, j)),
        pl.BlockSpec((D_MODEL, FFN_CK), lambda i, j: (0, nck + j)),
        pl.BlockSpec((3, FFN_CK), lambda i, j: (0, j)),
        pl.BlockSpec((1, FFN_CK), lambda i, j: (0, j)),
        pl.BlockSpec((FFN_CK, D_MODEL), lambda i, j: (j, 0)),
    ]
    args = [h2, h2, h2, x1, mod, wup, wup, cw, cb, wd]
    if final:
        in_specs.append(pl.BlockSpec((1, D_MODEL), lambda i, j: (0, 0)))
        args.append(final_norm)
    return pl.pallas_call(
        functools.partial(_ffn_kernel, seq_len=seq_len, tm=tm, final=final),
        grid=(m // tm, nck),
        in_specs=in_specs,
        out_specs=pl.BlockSpec((tm, D_MODEL), lambda i, j: (i, 0)),
        out_shape=jax.ShapeDtypeStruct((m, D_MODEL), F32),
        scratch_shapes=[
            pltpu.VMEM((tm + 2 * BF16_ROWS, D_MODEL), BF16),
            pltpu.VMEM((tm + 2 * BF16_ROWS, FFN_CK), F32),
            pltpu.VMEM((tm, D_MODEL), F32),
        ],
        compiler_params=_params("parallel", "arbitrary"),
        name="conv_glu_ffn",
    )(*args)


def _trunk_layer(x, mod, rows_per_mod, p, seq_len, ctx, final_norm):
    proj = _inproj(x, mod, rows_per_mod, p["norm1"], p["w_in"])
    if ctx is None:
        ya, h_fin = _lru(proj, p["lru"], seq_len, None)
        yb = _attn_ctx(proj, p["sink"], seq_len)
    else:
        ck, cv, h0 = ctx
        ya = _lru(proj, p["lru"], seq_len, h0)
        h_fin = None
        qr, kr, vb = _rope(proj, seq_len)
        yb = _attn_lat(qr, kr, vb, ck, cv, p["sink"], seq_len)
    yc = _pool(proj, p["pool_w"], p["pool_scale"], seq_len)
    x1, h2 = _merge(ya, yb, yc, proj, x, mod, rows_per_mod, p["b_gate"], p["w_branch"], p["w_out"],
                    p["norm2"])
    out = _ffn(h2, x1, mod, rows_per_mod, p["ffn_up"], p["ffn_conv"], p["ffn_conv_b"], p["ffn_down"],
               seq_len, final_norm)
    return out, proj, h_fin


def _layer_params(l, norm1, norm2, w_in, b_gate, lru_conv, lru_conv_b, lru_wa, lru_ba, lru_wx, lru_bx,
                  lru_lambda, attn_sink, pool_w, pool_scale, w_branch, w_out, ffn_up, ffn_conv,
                  ffn_conv_b, ffn_down):
    s0 = D_RNN
    s1 = s0 + N_HEADS * HEAD_DIM
    s2 = s1 + D_KV
    s3 = s2 + D_KV
    s4 = s3 + D_POOL
    w = w_in[l]
    w_perm = jnp.concatenate([w[:, :s1], w[:, s3:s4], w[:, s4:], w[:, s1:s3]], axis=1).astype(BF16)

    def per_head(v):
        return v.reshape(2, LRU_HEADS, LRU_BLOCK).transpose(1, 0, 2)

    wg = jnp.concatenate([lru_wa[l, 0], lru_wx[l, 0], lru_wa[l, 1], lru_wx[l, 1]], axis=-1).astype(BF16)
    ba = per_head(lru_ba[l])
    bx = per_head(lru_bx[l])
    bg = jnp.concatenate([ba[:, 0], bx[:, 0], ba[:, 1], bx[:, 1]], axis=-1)[:, None, :]
    lam = per_head(lru_lambda[l]).reshape(LRU_HEADS, 1, 2 * LRU_BLOCK)
    return {
        "norm1": norm1[l][None], "norm2": norm2[l][None], "w_in": w_perm, "b_gate": b_gate[l][None],
        "lru": {"conv_w": lru_conv[l], "conv_b": lru_conv_b[l][None], "wg": wg, "bg": bg, "lam": lam},
        "sink": attn_sink[l], "pool_w": pool_w[l].astype(BF16), "pool_scale": pool_scale[l][None],
        "w_branch": w_branch[l].astype(BF16), "w_out": w_out[l].astype(BF16),
        "ffn_up": ffn_up[l].astype(BF16), "ffn_conv": ffn_conv[l], "ffn_conv_b": ffn_conv_b[l][None],
        "ffn_down": ffn_down[l].astype(BF16),
    }


def kernel(x_prompt, x_sample, cache_k, cache_v, state_lru, c, c_ctx, w_ada, b_ada, norm1, norm2, w_in,
           b_gate, lru_conv, lru_conv_b, lru_wa, lru_ba, lru_wx, lru_bx, lru_lambda, attn_sink, pool_w,
           pool_scale, w_branch, w_out, ffn_up, ffn_conv, ffn_conv_b, ffn_down, final_norm):
    batch, seq, _ = x_prompt.shape
    dec_batch, dec_seq, _ = x_sample.shape
    past = cache_k.shape[2]
    assert seq == SCAN_CHUNK and dec_seq % SCAN_ROWS == 0 and (batch * seq) % SCAN_ROWS == 0

    c_rows = jnp.concatenate(
        [c_ctx[None], c, jnp.zeros((SUBLANES - 1 - dec_batch, D_MODEL), F32)], axis=0)
    mods = _ada(c_rows, w_ada, b_ada)

    xp = x_prompt.reshape(batch * seq, D_MODEL)
    xs = x_sample.reshape(dec_batch * dec_seq, D_MODEL)
    fn = final_norm[None]
    ks, vs, hs = [], [], []
    for l in range(DEPTH):
        p = _layer_params(l, norm1, norm2, w_in, b_gate, lru_conv, lru_conv_b, lru_wa, lru_ba, lru_wx,
                          lru_bx, lru_lambda, attn_sink, pool_w, pool_scale, w_branch, w_out, ffn_up,
                          ffn_conv, ffn_conv_b, ffn_down)
        last = fn if l == DEPTH - 1 else None
        mod_ctx = mods[l, 0:1][:, None, :]
        mod_lat = mods[l, 1:1 + dec_batch][:, None, :]
        xp, proj, h_fin = _trunk_layer(xp, mod_ctx, batch * seq, p, seq, None, last)
        ck = cache_k[:, l].reshape(dec_batch, past, D_KV).astype(BF16)
        cv = cache_v[:, l].reshape(dec_batch, past, D_KV).astype(BF16)
        h0 = state_lru[:, l].transpose(1, 0, 2)[:, :, None, :]
        xs, _, _ = _trunk_layer(xs, mod_lat, dec_seq, p, dec_seq, (ck, cv, h0), last)
        ks.append(proj[:, COL_K:COL_K + D_KV].reshape(batch, seq, N_KV_HEADS, HEAD_DIM))
        vs.append(proj[:, COL_V:COL_V + D_KV].reshape(batch, seq, N_KV_HEADS, HEAD_DIM))
        hs.append(h_fin.transpose(1, 0, 2))
    y_prompt = xp.reshape(batch, seq, D_MODEL)
    y_sample = xs.reshape(dec_batch, dec_seq, D_MODEL)
    return (y_prompt, y_sample, jnp.stack(ks, axis=1), jnp.stack(vs, axis=1), jnp.stack(hs, axis=1))
```

```python
import functools

import numpy as np
import jax
import jax.numpy as jnp
from jax import lax
from jax.experimental import pallas as pl
from jax.experimental.pallas import tpu as pltpu

F32 = jnp.float32
BF16 = jnp.bfloat16

D_MODEL = 1024
DEPTH = 2
GRID_W = 64
EPS = 1e-6
N_BRANCH = 3
D_RNN = 1024
LRU_HEADS = 8
LRU_BLOCK = D_RNN // LRU_HEADS
LRU_C = 8.0
N_HEADS = 8
N_KV_HEADS = 2
KV_GROUPS = N_HEADS // N_KV_HEADS
HEAD_DIM = 128
D_KV = N_KV_HEADS * HEAD_DIM
WINDOW = 128
BLOCK_Q = 128
ROPE_BASE = 10000.0
NEG_INF = -1e30
D_POOL = 1024
POOL_WINDOWS = (2, 4, 8, 16)
POOL_GROUP = D_POOL // len(POOL_WINDOWS)
D_FF = 2816
D_IN = D_RNN + N_HEADS * HEAD_DIM + 2 * D_KV + D_POOL + N_BRANCH * D_MODEL

COL_XA = 0
COL_Q = COL_XA + D_RNN
COL_K = COL_Q + N_HEADS * HEAD_DIM
COL_V = COL_K + D_KV
COL_XC = COL_V + D_KV
COL_G = COL_XC + D_POOL
LOG2E = float(np.log2(np.e))

VMEM_LIMIT_BYTES = 52 * 1024 * 1024
SUBLANES = 8
LANES = 128
BF16_ROWS = 16

SCAN_CHUNK = 256
SCAN_PITCH = 260
SCAN_ROWS = 2048
POOL_PAD = 8


def _params(*sem):
    return pltpu.CompilerParams(dimension_semantics=sem, vmem_limit_bytes=VMEM_LIMIT_BYTES)


def _dot(a, b):
    return jnp.dot(a, b, preferred_element_type=F32)


def _dot_nt(a, b):
    return lax.dot_general(a, b, (((1,), (1,)), ((), ())), preferred_element_type=F32)


def _sigmoid(z):
    return 0.5 * (1.0 + jnp.tanh(0.5 * z))


def _rms(x):
    return x * lax.rsqrt(jnp.mean(x * x, axis=-1, keepdims=True) + EPS)


def _ada_kernel(c_ref, w_ref, b_ref, o_ref):
    c = c_ref[...]
    s = c * _sigmoid(c)
    o_ref[0] = _dot(s.astype(BF16), w_ref[0].astype(BF16)) + b_ref[0]


def _ada(c_rows, w_ada, b_ada):
    tn = 1536
    return pl.pallas_call(
        _ada_kernel,
        grid=(DEPTH, 6 * D_MODEL // tn),
        in_specs=[
            pl.BlockSpec((SUBLANES, D_MODEL), lambda l, j: (0, 0)),
            pl.BlockSpec((1, D_MODEL, tn), lambda l, j: (l, 0, j)),
            pl.BlockSpec((1, 1, tn), lambda l, j: (l, 0, j)),
        ],
        out_specs=pl.BlockSpec((1, SUBLANES, tn), lambda l, j: (l, 0, j)),
        out_shape=jax.ShapeDtypeStruct((DEPTH, SUBLANES, 6 * D_MODEL), F32),
        compiler_params=_params("parallel", "parallel"),
        name="ada_mod",
    )(c_rows, w_ada, b_ada.reshape(DEPTH, 1, 6 * D_MODEL))


INPROJ_TN = 2 * D_KV


def _inproj_kernel(*refs, want_kv):
    if want_kv:
        x_ref, mod_ref, g_ref, w_ref, o_ref, kv_ref, h_ref = refs
    else:
        x_ref, mod_ref, g_ref, w_ref, o_ref, h_ref = refs
    j = pl.program_id(1)

    @pl.when(j == 0)
    def _():
        mod = mod_ref[0]
        shift = mod[:, 0:D_MODEL]
        scale = mod[:, D_MODEL:2 * D_MODEL]
        h = _rms(x_ref[...]) * g_ref[...]
        h_ref[...] = (h * (1.0 + scale) + shift).astype(BF16)

    acc = _dot(h_ref[...], w_ref[...])
    o_ref[...] = acc.astype(o_ref.dtype)
    if want_kv:
        @pl.when(j == COL_K // INPROJ_TN)
        def _():
            kv_ref[...] = acc


def _inproj(x, mod, rows_per_mod, g, w, want_kv):
    m = x.shape[0]
    tm, tn = 1024, INPROJ_TN
    per = rows_per_mod // tm
    proj_spec = pl.BlockSpec((tm, tn), lambda i, j: (i, j))
    proj_shape = jax.ShapeDtypeStruct((m, D_IN), BF16)
    if want_kv:
        out_specs = (proj_spec, pl.BlockSpec((tm, tn), lambda i, j: (i, 0)))
        out_shape = (proj_shape, jax.ShapeDtypeStruct((m, tn), F32))
    else:
        out_specs, out_shape = proj_spec, proj_shape
    out = pl.pallas_call(
        functools.partial(_inproj_kernel, want_kv=want_kv),
        grid=(m // tm, D_IN // tn),
        in_specs=[
            pl.BlockSpec((tm, D_MODEL), lambda i, j: (i, 0)),
            pl.BlockSpec((1, 1, 6 * D_MODEL), lambda i, j: (i // per, 0, 0)),
            pl.BlockSpec((1, D_MODEL), lambda i, j: (0, 0)),
            pl.BlockSpec((D_MODEL, tn), lambda i, j: (0, j)),
        ],
        out_specs=out_specs,
        out_shape=out_shape,
        scratch_shapes=[pltpu.VMEM((tm, D_MODEL), BF16)],
        compiler_params=_params("parallel", "arbitrary"),
        name="in_proj",
    )(x, mod, g, w)
    return out if want_kv else (out, None)


def _lru_kernel(*refs, seq_len, latent):
    if latent:
        (x_ref, cw_ref, cb_ref, wg_ref, bg_ref, lam_ref, h0_ref, y_ref,
         af_ref, bf_ref, ab_ref, bb_ref, hf_ref, hb_ref) = refs
    else:
        (x_ref, cw_ref, cb_ref, wg_ref, bg_ref, lam_ref, y_ref, fin_ref,
         af_ref, bf_ref, ab_ref, bb_ref, hf_ref, hb_ref) = refs
    rows = SCAN_ROWS
    nchunk = rows // SCAN_CHUNK
    x = x_ref[...].astype(F32)
    pos = lax.broadcasted_iota(jnp.int32, (rows, LRU_BLOCK), 0) & (seq_len - 1)
    xm2 = jnp.where(pos >= 2, pltpu.roll(x, 2, 0), 0.0)
    xm1 = jnp.where(pos >= 1, pltpu.roll(x, 1, 0), 0.0)
    xp1 = jnp.where(pos <= seq_len - 2, pltpu.roll(x, rows - 1, 0), 0.0)
    cw = cw_ref[...]
    xc = xm2 * cw[0:1] + xm1 * cw[1:2] + x * cw[2:3] + xp1 * cw[3:4] + cb_ref[...]
    th = jnp.tanh(_dot(xc.astype(BF16), wg_ref[0] * 0.5) + 0.5 * bg_ref[0])
    xh = 0.5 * xc
    lam = lam_ref[0]
    for d, (a_ref, b_ref) in enumerate(((af_ref, bf_ref), (ab_ref, bb_ref))):
        th_r = th[:, 2 * d * LRU_BLOCK:(2 * d + 1) * LRU_BLOCK]
        th_i = th[:, (2 * d + 1) * LRU_BLOCK:(2 * d + 2) * LRU_BLOCK]
        nl = -lam[:, d * LRU_BLOCK:(d + 1) * LRU_BLOCK]
        softplus = jnp.maximum(nl, 0.0) + jnp.log(1.0 + jnp.exp(-jnp.abs(nl)))
        ch = (-0.5 * LRU_C * LOG2E) * softplus
        a = jnp.exp2(ch + ch * th_r)
        y = 1.0 - a * a
        b = jnp.where(y > 0.0, y * lax.rsqrt(y), 0.0) * ((1.0 + th_i) * xh)
        for c in range(nchunk):
            a_ref[pl.ds(c * SCAN_PITCH, SCAN_CHUNK), :] = a[c * SCAN_CHUNK:(c + 1) * SCAN_CHUNK]
            b_ref[pl.ds(c * SCAN_PITCH, SCAN_CHUNK), :] = b[c * SCAN_CHUNK:(c + 1) * SCAN_CHUNK]

    def step(t, carry):
        hf, hb, pf, pb = carry
        rf = pl.ds(t, nchunk, stride=SCAN_PITCH)
        rb = pl.ds(SCAN_CHUNK - 1 - t, nchunk, stride=SCAN_PITCH)
        a_f = af_ref[rf, :]
        a_b = ab_ref[rb, :]
        hf = a_f * hf + bf_ref[rf, :]
        hb = a_b * hb + bb_ref[rb, :]
        hf_ref[rf, :] = hf
        hb_ref[rb, :] = hb
        if latent:
            pf = pf * a_f
            pb = pb * a_b
            af_ref[rf, :] = pf
            ab_ref[rb, :] = pb
        return hf, hb, pf, pb

    zero = jnp.zeros((nchunk, LRU_BLOCK), F32)
    one = jnp.ones((nchunk, LRU_BLOCK), F32)
    lax.fori_loop(0, SCAN_CHUNK, step, (zero, zero, one, one), unroll=8)

    def chunk(ref, c):
        return ref[pl.ds(c * SCAN_PITCH, SCAN_CHUNK), :]

    if latent:
        carry = h0_ref[0, 0]
        fwd = []
        for c in range(nchunk):
            h = chunk(hf_ref, c) + chunk(af_ref, c) * carry
            carry = h[SCAN_CHUNK - 1:SCAN_CHUNK]
            fwd.append(h)
        carry = h0_ref[1, 0]
        for c in reversed(range(nchunk)):
            h = chunk(hb_ref, c) + chunk(ab_ref, c) * carry
            carry = h[0:1]
            y_ref[pl.ds(c * SCAN_CHUNK, SCAN_CHUNK), :] = (fwd[c] + h).astype(y_ref.dtype)
    else:
        for c in range(nchunk):
            y_ref[pl.ds(c * SCAN_CHUNK, SCAN_CHUNK), :] = (
                chunk(hf_ref, c) + chunk(hb_ref, c)).astype(y_ref.dtype)
        fin_ref[0] = hf_ref[pl.ds(SCAN_CHUNK - 1, nchunk, stride=SCAN_PITCH), :]
        fin_ref[1] = hb_ref[pl.ds(0, nchunk, stride=SCAN_PITCH), :]


def _lru(proj, lp, seq_len, h0):
    m = proj.shape[0]
    latent = h0 is not None
    rows = SCAN_ROWS
    in_specs = [
        pl.BlockSpec((rows, LRU_BLOCK), lambda r, h: (r, COL_XA // LRU_BLOCK + h)),
        pl.BlockSpec((4, LRU_BLOCK), lambda r, h: (0, h)),
        pl.BlockSpec((1, LRU_BLOCK), lambda r, h: (0, h)),
        pl.BlockSpec((1, LRU_BLOCK, 4 * LRU_BLOCK), lambda r, h: (h, 0, 0)),
        pl.BlockSpec((1, 1, 4 * LRU_BLOCK), lambda r, h: (h, 0, 0)),
        pl.BlockSpec((1, 1, 2 * LRU_BLOCK), lambda r, h: (h, 0, 0)),
    ]
    args = [proj, lp["conv_w"], lp["conv_b"], lp["wg"], lp["bg"], lp["lam"]]
    y_spec = pl.BlockSpec((rows, LRU_BLOCK), lambda r, h: (r, h))
    y_shape = jax.ShapeDtypeStruct((m, D_RNN), BF16)
    if latent:
        in_specs.append(pl.BlockSpec((2, 1, 1, LRU_BLOCK), lambda r, h: (0, r, 0, h)))
        args.append(h0)
        out_specs, out_shape = y_spec, y_shape
    else:
        nseq = m // seq_len
        out_specs = (y_spec, pl.BlockSpec((2, rows // seq_len, LRU_BLOCK), lambda r, h: (0, r, h)))
        out_shape = (y_shape, jax.ShapeDtypeStruct((2, nseq, D_RNN), F32))
    scan_buf = pltpu.VMEM((rows // SCAN_CHUNK * SCAN_PITCH, LRU_BLOCK), F32)
    return pl.pallas_call(
        functools.partial(_lru_kernel, seq_len=seq_len, latent=latent),
        grid=(m // rows, LRU_HEADS),
        in_specs=in_specs,
        out_specs=out_specs,
        out_shape=out_shape,
        scratch_shapes=[scan_buf] * 6,
        compiler_params=_params("parallel", "parallel"),
        name="rglru_latent" if latent else "rglru_context",
    )(*args)


SOFTMAX_SCALE = HEAD_DIM ** -0.5 * LOG2E


def _stack_heads(q_ref, kh, r0, rows):
    parts = [q_ref[r0:r0 + rows, (kh * KV_GROUPS + g) * HEAD_DIM:(kh * KV_GROUPS + g + 1) * HEAD_DIM]
             for g in range(KV_GROUPS)]
    return jnp.concatenate(parts, axis=0)


def _sink_column(sink_ref, kh, rows):
    parts = [jnp.full((rows, 1), sink_ref[kh * KV_GROUPS + g] * LOG2E, F32) for g in range(KV_GROUPS)]
    return jnp.concatenate(parts, axis=0)


def _softmax_pv(t, sink, v):
    m = jnp.maximum(jnp.max(t, axis=-1, keepdims=True), sink)
    e = jnp.exp2(t - m)
    denom = jnp.sum(e, axis=-1, keepdims=True) + jnp.exp2(sink - m)
    return _dot(e.astype(BF16), v) * (1.0 / denom)


def _store_heads(y_ref, o, kh, r0, rows):
    for g in range(KV_GROUPS):
        h = kh * KV_GROUPS + g
        y_ref[r0:r0 + rows, h * HEAD_DIM:(h + 1) * HEAD_DIM] = o[g * rows:(g + 1) * rows].astype(y_ref.dtype)


def _attn_ctx_kernel(sink_ref, q_ref, k_ref, v_ref, y_ref, *, seq_len):
    for b in range(q_ref.shape[0] // seq_len):
        r0 = b * seq_len
        for kh in range(N_KV_HEADS):
            sl = slice(kh * HEAD_DIM, (kh + 1) * HEAD_DIM)
            q = _stack_heads(q_ref, kh, r0, seq_len)
            t = _dot_nt(q, k_ref[r0:r0 + seq_len, sl]) * SOFTMAX_SCALE
            o = _softmax_pv(t, _sink_column(sink_ref, kh, seq_len), v_ref[r0:r0 + seq_len, sl])
            _store_heads(y_ref, o, kh, r0, seq_len)


ATTN_CTX_SEQS = 4


def _attn_ctx(proj, sink, seq_len):
    m = proj.shape[0]
    rows = ATTN_CTX_SEQS * seq_len
    return pl.pallas_call(
        functools.partial(_attn_ctx_kernel, seq_len=seq_len),
        grid=(m // rows,),
        in_specs=[
            pl.BlockSpec(memory_space=pltpu.SMEM),
            pl.BlockSpec((rows, N_HEADS * HEAD_DIM), lambda b: (b, COL_Q // (N_HEADS * HEAD_DIM))),
            pl.BlockSpec((rows, D_KV), lambda b: (b, COL_K // D_KV)),
            pl.BlockSpec((rows, D_KV), lambda b: (b, COL_V // D_KV)),
        ],
        out_specs=pl.BlockSpec((rows, N_HEADS * HEAD_DIM), lambda b: (b, 0)),
        out_shape=jax.ShapeDtypeStruct((m, N_HEADS * HEAD_DIM), BF16),
        compiler_params=_params("parallel"),
        name="attn_context",
    )(sink, proj, proj, proj)


def _rope_tables(seq_len):
    nf = HEAD_DIM // 4
    freqs = ROPE_BASE ** (-np.arange(nf, dtype=np.float64) / nf)
    t = np.arange(seq_len)
    ang_row = (t // GRID_W)[:, None] * freqs[None, :]
    ang_col = (t % GRID_W)[:, None] * freqs[None, :]
    ang = np.concatenate([ang_row, ang_row, ang_col, ang_col], axis=1)
    first = (np.arange(HEAD_DIM) % (2 * nf)) < nf
    cos = np.cos(ang)
    sin = np.sin(ang)
    sin_a = np.where(first[None, :], -sin, 0.0)
    sin_b = np.where(first[None, :], 0.0, sin)
    return tuple(jnp.asarray(a, F32) for a in (cos, sin_a, sin_b))


def _rope_kernel(q_ref, k_ref, cos_ref, sa_ref, sb_ref, qo_ref, ko_ref):
    cos = cos_ref[...]
    sa = sa_ref[...]
    sb = sb_ref[...]
    nf = HEAD_DIM // 4

    def rope(x):
        x = x.astype(F32)
        return x * cos + pltpu.roll(x, HEAD_DIM - nf, 1) * sa + pltpu.roll(x, nf, 1) * sb

    for h in range(N_HEADS):
        sl = slice(h * HEAD_DIM, (h + 1) * HEAD_DIM)
        qo_ref[:, sl] = rope(q_ref[:, sl]).astype(BF16)
    for h in range(N_KV_HEADS):
        sl = slice(h * HEAD_DIM, (h + 1) * HEAD_DIM)
        ko_ref[:, sl] = rope(k_ref[:, sl]).astype(BF16)


def _rope(proj, seq_len):
    m = proj.shape[0]
    tm = 512
    per = seq_len // tm
    cos, sa, sb = _rope_tables(seq_len)
    tab = pl.BlockSpec((tm, HEAD_DIM), lambda i: (i % per, 0))
    return pl.pallas_call(
        _rope_kernel,
        grid=(m // tm,),
        in_specs=[
            pl.BlockSpec((tm, N_HEADS * HEAD_DIM), lambda i: (i, COL_Q // (N_HEADS * HEAD_DIM))),
            pl.BlockSpec((tm, D_KV), lambda i: (i, COL_K // D_KV)),
            tab, tab, tab,
        ],
        out_specs=(
            pl.BlockSpec((tm, N_HEADS * HEAD_DIM), lambda i: (i, 0)),
            pl.BlockSpec((tm, D_KV), lambda i: (i, 0)),
        ),
        out_shape=(
            jax.ShapeDtypeStruct((m, N_HEADS * HEAD_DIM), BF16),
            jax.ShapeDtypeStruct((m, D_KV), BF16),
        ),
        compiler_params=_params("parallel"),
        name="rope_qk",
    )(proj, proj, cos, sa, sb)


def _attn_lat_kernel(sink_ref, q_ref, kp_ref, kc_ref, kn_ref, vp_ref, vc_ref, vn_ref,
                     ck_ref, cv_ref, y_ref, *, nblk):
    j = pl.program_id(1)
    rows = KV_GROUPS * BLOCK_Q
    span = 3 * BLOCK_Q
    row = lax.broadcasted_iota(jnp.int32, (rows, span), 0) & (BLOCK_Q - 1)
    col = lax.broadcasted_iota(jnp.int32, (rows, span), 1)
    lo = jnp.where(j > 0, row, BLOCK_Q)
    hi = jnp.where(j < nblk - 1, row + 2 * BLOCK_Q, 2 * BLOCK_Q - 1)
    bias = jnp.where(jnp.logical_and(col >= lo, col <= hi), 0.0, NEG_INF)
    for kh in range(N_KV_HEADS):
        sl = slice(kh * HEAD_DIM, (kh + 1) * HEAD_DIM)
        q = _stack_heads(q_ref, kh, 0, BLOCK_Q)
        keys = jnp.concatenate([kp_ref[:, sl], kc_ref[:, sl], kn_ref[:, sl], ck_ref[0, :, sl]], axis=0)
        vals = jnp.concatenate([vp_ref[:, sl], vc_ref[:, sl], vn_ref[:, sl], cv_ref[0, :, sl]], axis=0)
        t = _dot_nt(q, keys) * SOFTMAX_SCALE
        t = jnp.concatenate([t[:, :span] + bias, t[:, span:]], axis=1)
        o = _softmax_pv(t, _sink_column(sink_ref, kh, BLOCK_Q), vals)
        _store_heads(y_ref, o, kh, 0, BLOCK_Q)


def _attn_lat(qr, kr, proj, ck, cv, sink, seq_len):
    m = qr.shape[0]
    nblk = seq_len // BLOCK_Q
    nb = m // seq_len
    past = ck.shape[1]

    def rows_at(shift, colblk):
        def index(b, j):
            return (b * nblk + jnp.clip(j + shift, 0, nblk - 1), colblk)
        return index

    k_spec = lambda shift: pl.BlockSpec((BLOCK_Q, D_KV), rows_at(shift, 0))
    v_spec = lambda shift: pl.BlockSpec((BLOCK_Q, D_KV), rows_at(shift, COL_V // D_KV))
    return pl.pallas_call(
        functools.partial(_attn_lat_kernel, nblk=nblk),
        grid=(nb, nblk),
        in_specs=[
            pl.BlockSpec(memory_space=pltpu.SMEM),
            pl.BlockSpec((BLOCK_Q, N_HEADS * HEAD_DIM), rows_at(0, 0)),
            k_spec(-1), k_spec(0), k_spec(1), v_spec(-1), v_spec(0), v_spec(1),
            pl.BlockSpec((1, past, D_KV), lambda b, j: (b, 0, 0)),
            pl.BlockSpec((1, past, D_KV), lambda b, j: (b, 0, 0)),
        ],
        out_specs=pl.BlockSpec((BLOCK_Q, N_HEADS * HEAD_DIM), rows_at(0, 0)),
        out_shape=jax.ShapeDtypeStruct((m, N_HEADS * HEAD_DIM), BF16),
        compiler_params=_params("parallel", "parallel"),
        name="attn_latent",
    )(sink, qr, kr, kr, kr, proj, proj, proj, ck, cv)


def _pool_kernel(x0_ref, x1_ref, x2_ref, x3_ref, w_ref, s_ref, y_ref, pad_ref, *, seq_len):
    rows = x0_ref.shape[0]
    nseq = rows // seq_len
    zeros = jnp.zeros((nseq, POOL_PAD, POOL_GROUP), F32)
    pad_ref[:, 0:POOL_PAD, :] = zeros
    pad_ref[:, POOL_PAD + seq_len:, :] = zeros
    t = lax.broadcasted_iota(jnp.int32, (nseq, seq_len, POOL_GROUP), 1)
    for gi, (win, x_ref) in enumerate(zip(POOL_WINDOWS, (x0_ref, x1_ref, x2_ref, x3_ref))):
        cs = slice(gi * POOL_GROUP, (gi + 1) * POOL_GROUP)
        half = win // 2
        x = x_ref[...].astype(F32).reshape(nseq, seq_len, POOL_GROUP)
        pad_ref[:, POOL_PAD:POOL_PAD + seq_len, :] = x
        total = pad_ref[:, POOL_PAD - half:POOL_PAD - half + seq_len, :]
        for d in range(1 - half, half):
            total = total + (x if d == 0 else pad_ref[:, POOL_PAD + d:POOL_PAD + d + seq_len, :])
        cnt = (jnp.minimum(t + half, seq_len) - jnp.maximum(t - half, 0)).astype(F32)
        pooled = (total / cnt - x).reshape(rows, POOL_GROUP).astype(BF16)
        y_ref[:, cs] = (_dot(pooled, w_ref[gi]) * s_ref[:, cs]).astype(y_ref.dtype)


def _pool(proj, w, s, seq_len):
    m = proj.shape[0]
    rows = 2048
    nseq = rows // seq_len
    group = lambda gi: pl.BlockSpec((rows, POOL_GROUP), lambda r: (r, COL_XC // POOL_GROUP + gi))
    return pl.pallas_call(
        functools.partial(_pool_kernel, seq_len=seq_len),
        grid=(m // rows,),
        in_specs=[
            group(0), group(1), group(2), group(3),
            pl.BlockSpec((len(POOL_WINDOWS), POOL_GROUP, POOL_GROUP), lambda r: (0, 0, 0)),
            pl.BlockSpec((1, D_POOL), lambda r: (0, 0)),
        ],
        out_specs=pl.BlockSpec((rows, D_POOL), lambda r: (r, 0)),
        out_shape=jax.ShapeDtypeStruct((m, D_POOL), BF16),
        scratch_shapes=[pltpu.VMEM((nseq, seq_len + 2 * POOL_PAD, POOL_GROUP), F32)],
        compiler_params=_params("parallel"),
        name="pool_mix",
    )(proj, proj, proj, proj, w, s)


def _merge_kernel(ya_ref, yb_ref, yc_ref, g0_ref, g1_ref, g2_ref, g3_ref, g4_ref, g5_ref, x_ref, mod_ref,
                  bg_ref, wb_ref, wo_ref, n2_ref, x1_ref, h2_ref):
    mod = mod_ref[0]
    g_refs = (g0_ref, g1_ref, g2_ref, g3_ref, g4_ref, g5_ref)
    half = D_MODEL // 2
    merged = None
    for k, y_ref in enumerate((ya_ref, yb_ref, yc_ref)):
        y = _dot(y_ref[...], wb_ref[k])
        parts = []
        for p in range(2):
            z = g_refs[2 * k + p][...].astype(F32) + bg_ref[:, k * D_MODEL + p * half:k * D_MODEL + (p + 1) * half]
            parts.append((1.0 + jnp.tanh(0.5 * z)) * y[:, p * half:(p + 1) * half])
        term = jnp.concatenate(parts, axis=1)
        merged = term if merged is None else merged + term
    merged = 0.5 * merged
    gate1 = mod[:, 2 * D_MODEL:3 * D_MODEL]
    x1 = x_ref[...] + gate1 * _dot(merged.astype(BF16), wo_ref[...])
    x1_ref[...] = x1
    shift2 = mod[:, 3 * D_MODEL:4 * D_MODEL]
    scale2 = mod[:, 4 * D_MODEL:5 * D_MODEL]
    h2_ref[...] = (_rms(x1) * n2_ref[...] * (1.0 + scale2) + shift2).astype(BF16)


def _merge(ya, yb, yc, proj, x, mod, rows_per_mod, bg, wb, wo, n2):
    m = x.shape[0]
    tm = 512
    per = rows_per_mod // tm
    half = D_MODEL // 2
    row = pl.BlockSpec((tm, D_MODEL), lambda i: (i, 0))
    gate = lambda c: pl.BlockSpec((tm, half), lambda i: (i, COL_G // half + c))
    const = lambda shape: pl.BlockSpec(shape, lambda i: (0,) * len(shape), pipeline_mode=pl.Buffered(1))
    return pl.pallas_call(
        _merge_kernel,
        grid=(m // tm,),
        in_specs=[
            row, row, row,
            gate(0), gate(1), gate(2), gate(3), gate(4), gate(5),
            row,
            pl.BlockSpec((1, 1, 6 * D_MODEL), lambda i: (i // per, 0, 0)),
            const((1, N_BRANCH * D_MODEL)),
            const((N_BRANCH, D_MODEL, D_MODEL)),
            const((D_MODEL, D_MODEL)),
            const((1, D_MODEL)),
        ],
        out_specs=(row, row),
        out_shape=(jax.ShapeDtypeStruct((m, D_MODEL), F32), jax.ShapeDtypeStruct((m, D_MODEL), BF16)),
        compiler_params=_params("parallel"),
        name="merge_out",
    )(ya, yb, yc, proj, proj, proj, proj, proj, proj, x, mod, bg, wb, wo, n2)


FFN_CK = 256
FFN_TM = 512
FFN_GAP = SUBLANES
GELU_C = float(np.sqrt(2.0 / np.pi))


def _ffn_kernel(*refs, seq_len, final):
    refs = list(refs)
    h_ref, hp_ref, hn_ref, x_ref, mod_ref, wup_ref, cw_ref, cb_ref, wd_ref = refs[:9]
    fn_ref = refs[9] if final else None
    o_ref, hx_ref, u_ref, act_ref = refs[-4:]
    tm = FFN_TM
    halo = seq_len > tm
    i = pl.program_id(0)

    if halo:
        per_seq = seq_len // tm
        at_start = i % per_seq == 0
        at_end = i % per_seq == per_seq - 1
        zeros = jnp.zeros((BF16_ROWS, D_MODEL), BF16)

        @pl.when(at_start)
        def _():
            hx_ref[0:BF16_ROWS, :] = zeros

        @pl.when(jnp.logical_not(at_start))
        def _():
            hx_ref[0:BF16_ROWS, :] = hp_ref[...]

        @pl.when(at_end)
        def _():
            hx_ref[BF16_ROWS + tm:, :] = zeros

        @pl.when(jnp.logical_not(at_end))
        def _():
            hx_ref[BF16_ROWS + tm:, :] = hn_ref[...]

        hx_ref[BF16_ROWS:BF16_ROWS + tm, :] = h_ref[...]
        bases = (BF16_ROWS,)
        seg = tm
    else:
        nseg = tm // seq_len
        seg = seq_len
        bases = tuple(FFN_GAP + s * (seg + FFN_GAP) for s in range(nseg))
        for s in range(nseg + 1):
            u_ref[s * (seg + FFN_GAP):s * (seg + FFN_GAP) + FFN_GAP, :] = jnp.zeros((FFN_GAP, FFN_CK), F32)

    def taps(offset):
        return jnp.concatenate([u_ref[b + offset:b + offset + seg, :] for b in bases], axis=0)

    for c in range(D_FF // FFN_CK):
        cs = slice(c * FFN_CK, (c + 1) * FFN_CK)
        vs = slice(D_FF + c * FFN_CK, D_FF + (c + 1) * FFN_CK)
        if halo:
            u_ext = _dot(hx_ref[...], wup_ref[:, cs])
            u_ref[...] = u_ext
            u0 = u_ext[BF16_ROWS:BF16_ROWS + tm]
        else:
            u0 = _dot(h_ref[...], wup_ref[:, cs])
            for s, b in enumerate(bases):
                u_ref[b:b + seg, :] = u0[s * seg:(s + 1) * seg]
        uv = _dot(h_ref[...], wup_ref[:, vs])
        gff = taps(-1) * cw_ref[0:1, cs] + u0 * cw_ref[1:2, cs] + taps(1) * cw_ref[2:3, cs] + cb_ref[:, cs]
        inner = gff * (GELU_C + (GELU_C * 0.044715) * (gff * gff))
        act_ref[:, cs] = (0.5 * (gff * uv) * (1.0 + jnp.tanh(inner))).astype(BF16)

    gate2 = mod_ref[0][:, 5 * D_MODEL:6 * D_MODEL]
    out = x_ref[...] + gate2 * _dot(act_ref[...], wd_ref[...])
    if final:
        out = _rms(out) * fn_ref[...]
    o_ref[...] = out


def _ffn(h2, x1, mod, rows_per_mod, wup, cw, cb, wd, seq_len, final_norm):
    m = x1.shape[0]
    tm = FFN_TM
    per = rows_per_mod // tm
    hb = tm // BF16_ROWS
    last_halo = m // BF16_ROWS - 1
    final = final_norm is not None
    halo = seq_len > tm
    const = lambda shape: pl.BlockSpec(shape, lambda i: (0,) * len(shape), pipeline_mode=pl.Buffered(1))
    in_specs = [
        pl.BlockSpec((tm, D_MODEL), lambda i: (i, 0)),
        pl.BlockSpec((BF16_ROWS, D_MODEL), lambda i: (jnp.maximum(i * hb - 1, 0), 0)),
        pl.BlockSpec((BF16_ROWS, D_MODEL), lambda i: (jnp.minimum((i + 1) * hb, last_halo), 0)),
        pl.BlockSpec((tm, D_MODEL), lambda i: (i, 0)),
        pl.BlockSpec((1, 1, 6 * D_MODEL), lambda i: (i // per, 0, 0)),
        const((D_MODEL, 2 * D_FF)),
        const((3, D_FF)),
        const((1, D_FF)),
        const((D_FF, D_MODEL)),
    ]
    args = [h2, h2, h2, x1, mod, wup, cw, cb, wd]
    if final:
        in_specs.append(const((1, D_MODEL)))
        args.append(final_norm)
    if halo:
        u_rows = tm + 2 * BF16_ROWS
    else:
        u_rows = FFN_GAP + (tm // seq_len) * (seq_len + FFN_GAP)
    return pl.pallas_call(
        functools.partial(_ffn_kernel, seq_len=seq_len, final=final),
        grid=(m // tm,),
        in_specs=in_specs,
        out_specs=pl.BlockSpec((tm, D_MODEL), lambda i: (i, 0)),
        out_shape=jax.ShapeDtypeStruct((m, D_MODEL), F32),
        scratch_shapes=[
            pltpu.VMEM((tm + 2 * BF16_ROWS, D_MODEL), BF16),
            pltpu.VMEM((u_rows, FFN_CK), F32),
            pltpu.VMEM((tm, D_FF), BF16),
        ],
        compiler_params=_params("parallel"),
        name="conv_glu_ffn",
    )(*args)


def _trunk_layer(x, mod, rows_per_mod, p, seq_len, ctx, final_norm):
    proj, kv = _inproj(x, mod, rows_per_mod, p["norm1"], p["w_in"], want_kv=ctx is None)
    if ctx is None:
        ya, h_fin = _lru(proj, p["lru"], seq_len, None)
        yb = _attn_ctx(proj, p["sink"], seq_len)
    else:
        ck, cv, h0 = ctx
        ya = _lru(proj, p["lru"], seq_len, h0)
        h_fin = None
        qr, kr = _rope(proj, seq_len)
        yb = _attn_lat(qr, kr, proj, ck, cv, p["sink"], seq_len)
    yc = _pool(proj, p["pool_w"], p["pool_scale"], seq_len)
    x1, h2 = _merge(ya, yb, yc, proj, x, mod, rows_per_mod, p["b_gate"], p["w_branch"], p["w_out"],
                    p["norm2"])
    out = _ffn(h2, x1, mod, rows_per_mod, p["ffn_up"], p["ffn_conv"], p["ffn_conv_b"], p["ffn_down"],
               seq_len, final_norm)
    return out, kv, h_fin


def _layer_params(l, norm1, norm2, w_in, b_gate, lru_conv, lru_conv_b, lru_wa, lru_ba, lru_wx, lru_bx,
                  lru_lambda, attn_sink, pool_w, pool_scale, w_branch, w_out, ffn_up, ffn_conv,
                  ffn_conv_b, ffn_down):
    def per_head(v):
        return v.reshape(2, LRU_HEADS, LRU_BLOCK).transpose(1, 0, 2)

    wg = jnp.concatenate([lru_wa[l, 0], lru_wx[l, 0], lru_wa[l, 1], lru_wx[l, 1]], axis=-1).astype(BF16)
    ba = per_head(lru_ba[l])
    bx = per_head(lru_bx[l])
    bg = jnp.concatenate([ba[:, 0], bx[:, 0], ba[:, 1], bx[:, 1]], axis=-1)[:, None, :]
    lam = per_head(lru_lambda[l]).reshape(LRU_HEADS, 1, 2 * LRU_BLOCK)
    return {
        "norm1": norm1[l][None], "norm2": norm2[l][None], "w_in": w_in[l].astype(BF16), "b_gate": b_gate[l][None],
        "lru": {"conv_w": lru_conv[l], "conv_b": lru_conv_b[l][None], "wg": wg, "bg": bg, "lam": lam},
        "sink": attn_sink[l], "pool_w": pool_w[l].astype(BF16), "pool_scale": pool_scale[l][None],
        "w_branch": w_branch[l].astype(BF16), "w_out": w_out[l].astype(BF16),
        "ffn_up": ffn_up[l].astype(BF16), "ffn_conv": ffn_conv[l], "ffn_conv_b": ffn_conv_b[l][None],
        "ffn_down": ffn_down[l].astype(BF16),
    }


def kernel(x_prompt, x_sample, cache_k, cache_v, state_lru, c, c_ctx, w_ada, b_ada, norm1, norm2, w_in,
           b_gate, lru_conv, lru_conv_b, lru_wa, lru_ba, lru_wx, lru_bx, lru_lambda, attn_sink, pool_w,
           pool_scale, w_branch, w_out, ffn_up, ffn_conv, ffn_conv_b, ffn_down, final_norm):
    batch, seq, _ = x_prompt.shape
    dec_batch, dec_seq, _ = x_sample.shape
    past = cache_k.shape[2]
    assert seq == SCAN_CHUNK and dec_seq % SCAN_ROWS == 0 and (batch * seq) % SCAN_ROWS == 0

    c_rows = jnp.concatenate(
        [c_ctx[None], c, jnp.zeros((SUBLANES - 1 - dec_batch, D_MODEL), F32)], axis=0)
    mods = _ada(c_rows, w_ada, b_ada)

    xp = x_prompt.reshape(batch * seq, D_MODEL)
    xs = x_sample.reshape(dec_batch * dec_seq, D_MODEL)
    fn = final_norm[None]
    ks, vs, hs = [], [], []
    for l in range(DEPTH):
        p = _layer_params(l, norm1, norm2, w_in, b_gate, lru_conv, lru_conv_b, lru_wa, lru_ba, lru_wx,
                          lru_bx, lru_lambda, attn_sink, pool_w, pool_scale, w_branch, w_out, ffn_up,
                          ffn_conv, ffn_conv_b, ffn_down)
        last = fn if l == DEPTH - 1 else None
        mod_ctx = mods[l, 0:1][:, None, :]
        mod_lat = mods[l, 1:1 + dec_batch][:, None, :]
        xp, kv, h_fin = _trunk_layer(xp, mod_ctx, batch * seq, p, seq, None, last)
        ck = cache_k[:, l].reshape(dec_batch, past, D_KV).astype(BF16)
        cv = cache_v[:, l].reshape(dec_batch, past, D_KV).astype(BF16)
        h0 = state_lru[:, l].transpose(1, 0, 2)[:, :, None, :]
        xs, _, _ = _trunk_layer(xs, mod_lat, dec_seq, p, dec_seq, (ck, cv, h0), last)
        ks.append(kv[:, :D_KV].reshape(batch, seq, N_KV_HEADS, HEAD_DIM))
        vs.append(kv[:, D_KV:].reshape(batch, seq, N_KV_HEADS, HEAD_DIM))
        hs.append(h_fin.transpose(1, 0, 2))
    y_prompt = xp.reshape(batch, seq, D_MODEL)
    y_sample = xs.reshape(dec_batch, dec_seq, D_MODEL)
    return (y_prompt, y_sample, jnp.stack(ks, axis=1), jnp.stack(vs, axis=1), jnp.stack(hs, axis=1))
```

```python
import functools

import numpy as np
import jax
import jax.numpy as jnp
from jax import lax
from jax.experimental import pallas as pl
from jax.experimental.pallas import tpu as pltpu

F32 = jnp.float32
BF16 = jnp.bfloat16

D_MODEL = 1024
DEPTH = 2
GRID_W = 64
EPS = 1e-6
N_BRANCH = 3
D_RNN = 1024
LRU_HEADS = 8
LRU_BLOCK = D_RNN // LRU_HEADS
LRU_C = 8.0
N_HEADS = 8
N_KV_HEADS = 2
KV_GROUPS = N_HEADS // N_KV_HEADS
HEAD_DIM = 128
D_KV = N_KV_HEADS * HEAD_DIM
WINDOW = 128
BLOCK_Q = 128
ROPE_BASE = 10000.0
NEG_INF = -1e30
D_POOL = 1024
POOL_WINDOWS = (2, 4, 8, 16)
POOL_GROUP = D_POOL // len(POOL_WINDOWS)
D_FF = 2816
D_IN = D_RNN + N_HEADS * HEAD_DIM + 2 * D_KV + D_POOL + N_BRANCH * D_MODEL

COL_XA = 0
COL_Q = COL_XA + D_RNN
COL_K = COL_Q + N_HEADS * HEAD_DIM
COL_V = COL_K + D_KV
COL_XC = COL_V + D_KV
COL_G = COL_XC + D_POOL
LOG2E = float(np.log2(np.e))

VMEM_LIMIT_BYTES = 52 * 1024 * 1024
SUBLANES = 8
LANES = 128
BF16_ROWS = 16

SCAN_CHUNK = 256
SCAN_PITCH = 260
SCAN_ROWS = 2048
POOL_PAD = 8


def _params(*sem):
    return pltpu.CompilerParams(dimension_semantics=sem, vmem_limit_bytes=VMEM_LIMIT_BYTES)


def _dot(a, b):
    return jnp.dot(a, b, preferred_element_type=F32)


def _dot_nt(a, b):
    return lax.dot_general(a, b, (((1,), (1,)), ((), ())), preferred_element_type=F32)


def _sigmoid(z):
    return 0.5 * (1.0 + jnp.tanh(0.5 * z))


def _rms(x):
    return x * lax.rsqrt(jnp.mean(x * x, axis=-1, keepdims=True) + EPS)


def _ada_kernel(c_ref, w_ref, b_ref, o_ref):
    c = c_ref[...]
    s = c * _sigmoid(c)
    o_ref[0] = _dot(s.astype(BF16), w_ref[0].astype(BF16)) + b_ref[0]


def _ada(c_rows, w_ada, b_ada):
    tn = 1536
    return pl.pallas_call(
        _ada_kernel,
        grid=(DEPTH, 6 * D_MODEL // tn),
        in_specs=[
            pl.BlockSpec((SUBLANES, D_MODEL), lambda l, j: (0, 0)),
            pl.BlockSpec((1, D_MODEL, tn), lambda l, j: (l, 0, j)),
            pl.BlockSpec((1, 1, tn), lambda l, j: (l, 0, j)),
        ],
        out_specs=pl.BlockSpec((1, SUBLANES, tn), lambda l, j: (l, 0, j)),
        out_shape=jax.ShapeDtypeStruct((DEPTH, SUBLANES, 6 * D_MODEL), F32),
        compiler_params=_params("parallel", "parallel"),
        name="ada_mod",
    )(c_rows, w_ada, b_ada.reshape(DEPTH, 1, 6 * D_MODEL))


INPROJ_TM = 512
INPROJ_CK = 2 * D_KV


def _resident(shape, layer):
    ndim = len(shape)
    return pl.BlockSpec((None,) + tuple(shape), lambda i: (layer,) + (0,) * ndim,
                        pipeline_mode=pl.Buffered(1))


def _inproj_kernel(*refs, seq_len, want_kv):
    if want_kv:
        x_ref, mod_ref, g_ref, w_ref, o_ref, k_ref, v_ref = refs
    else:
        x_ref, mod_ref, g_ref, w_ref, o_ref = refs
    mod = mod_ref[0]
    shift = mod[:, 0:D_MODEL]
    scale = mod[:, D_MODEL:2 * D_MODEL]
    h = (_rms(x_ref[...]) * g_ref[...] * (1.0 + scale) + shift).astype(BF16)
    for c in range(D_IN // INPROJ_CK):
        acc = _dot(h, w_ref[:, c * INPROJ_CK:(c + 1) * INPROJ_CK])
        o_ref[:, c * INPROJ_CK:(c + 1) * INPROJ_CK] = acc.astype(o_ref.dtype)
        if want_kv and c == COL_K // INPROJ_CK:
            for b in range(INPROJ_TM // seq_len):
                rows = slice(b * seq_len, (b + 1) * seq_len)
                for hd in range(N_KV_HEADS):
                    dst = pl.ds(hd, seq_len, stride=N_KV_HEADS)
                    k_ref[b, dst, :] = acc[rows, hd * HEAD_DIM:(hd + 1) * HEAD_DIM]
                    v_ref[b, dst, :] = acc[rows, D_KV + hd * HEAD_DIM:D_KV + (hd + 1) * HEAD_DIM]


def _inproj(x, mod, rows_per_mod, g, w, layer, seq_len, want_kv):
    m = x.shape[0]
    tm = INPROJ_TM
    per = rows_per_mod // tm
    proj_spec = pl.BlockSpec((tm, D_IN), lambda i: (i, 0))
    proj_shape = jax.ShapeDtypeStruct((m, D_IN), BF16)
    if want_kv:
        nb = tm // seq_len
        cache_spec = pl.BlockSpec((nb, seq_len * N_KV_HEADS, HEAD_DIM), lambda i: (i, 0, 0))
        cache_shape = jax.ShapeDtypeStruct((m // seq_len, seq_len * N_KV_HEADS, HEAD_DIM), F32)
        out_specs = (proj_spec, cache_spec, cache_spec)
        out_shape = (proj_shape, cache_shape, cache_shape)
    else:
        out_specs, out_shape = proj_spec, proj_shape
    out = pl.pallas_call(
        functools.partial(_inproj_kernel, seq_len=seq_len, want_kv=want_kv),
        grid=(m // tm,),
        in_specs=[
            pl.BlockSpec((tm, D_MODEL), lambda i: (i, 0)),
            pl.BlockSpec((1, 1, 6 * D_MODEL), lambda i: (i // per, 0, 0)),
            pl.BlockSpec((1, D_MODEL), lambda i: (0, 0)),
            _resident((D_MODEL, D_IN), layer),
        ],
        out_specs=out_specs,
        out_shape=out_shape,
        compiler_params=_params("parallel"),
        name="in_proj",
    )(x, mod, g, w)
    return out if want_kv else (out, None, None)


def _lru_kernel(*refs, seq_len, latent):
    if latent:
        (x_ref, cw_ref, cb_ref, wg_ref, bg_ref, lam_ref, h0_ref, y_ref,
         af_ref, bf_ref, ab_ref, bb_ref, hf_ref, hb_ref) = refs
    else:
        (x_ref, cw_ref, cb_ref, wg_ref, bg_ref, lam_ref, y_ref, fin_ref,
         af_ref, bf_ref, ab_ref, bb_ref, hf_ref, hb_ref) = refs
    rows = SCAN_ROWS
    nchunk = rows // SCAN_CHUNK
    x = x_ref[...].astype(F32)
    pos = lax.broadcasted_iota(jnp.int32, (rows, LRU_BLOCK), 0) & (seq_len - 1)
    xm2 = jnp.where(pos >= 2, pltpu.roll(x, 2, 0), 0.0)
    xm1 = jnp.where(pos >= 1, pltpu.roll(x, 1, 0), 0.0)
    xp1 = jnp.where(pos <= seq_len - 2, pltpu.roll(x, rows - 1, 0), 0.0)
    cw = cw_ref[...]
    xc = xm2 * cw[0:1] + xm1 * cw[1:2] + x * cw[2:3] + xp1 * cw[3:4] + cb_ref[...]
    th = jnp.tanh(_dot(xc.astype(BF16), wg_ref[0] * 0.5) + 0.5 * bg_ref[0])
    xh = 0.5 * xc
    lam = lam_ref[0]
    for d, (a_ref, b_ref) in enumerate(((af_ref, bf_ref), (ab_ref, bb_ref))):
        th_r = th[:, 2 * d * LRU_BLOCK:(2 * d + 1) * LRU_BLOCK]
        th_i = th[:, (2 * d + 1) * LRU_BLOCK:(2 * d + 2) * LRU_BLOCK]
        nl = -lam[:, d * LRU_BLOCK:(d + 1) * LRU_BLOCK]
        softplus = jnp.maximum(nl, 0.0) + jnp.log(1.0 + jnp.exp(-jnp.abs(nl)))
        ch = (-0.5 * LRU_C * LOG2E) * softplus
        a = jnp.exp2(ch + ch * th_r)
        y = 1.0 - a * a
        b = jnp.where(y > 0.0, y * lax.rsqrt(y), 0.0) * ((1.0 + th_i) * xh)
        for c in range(nchunk):
            a_ref[pl.ds(c * SCAN_PITCH, SCAN_CHUNK), :] = a[c * SCAN_CHUNK:(c + 1) * SCAN_CHUNK]
            b_ref[pl.ds(c * SCAN_PITCH, SCAN_CHUNK), :] = b[c * SCAN_CHUNK:(c + 1) * SCAN_CHUNK]

    def step(t, carry):
        hf, hb, pf, pb = carry
        rf = pl.ds(t, nchunk, stride=SCAN_PITCH)
        rb = pl.ds(SCAN_CHUNK - 1 - t, nchunk, stride=SCAN_PITCH)
        a_f = af_ref[rf, :]
        a_b = ab_ref[rb, :]
        hf = a_f * hf + bf_ref[rf, :]
        hb = a_b * hb + bb_ref[rb, :]
        hf_ref[rf, :] = hf
        hb_ref[rb, :] = hb
        if latent:
            pf = pf * a_f
            pb = pb * a_b
            af_ref[rf, :] = pf
            ab_ref[rb, :] = pb
        return hf, hb, pf, pb

    zero = jnp.zeros((nchunk, LRU_BLOCK), F32)
    one = jnp.ones((nchunk, LRU_BLOCK), F32)
    lax.fori_loop(0, SCAN_CHUNK, step, (zero, zero, one, one), unroll=8)

    def chunk(ref, c):
        return ref[pl.ds(c * SCAN_PITCH, SCAN_CHUNK), :]

    if latent:
        carry = h0_ref[0, 0]
        fwd = []
        for c in range(nchunk):
            h = chunk(hf_ref, c) + chunk(af_ref, c) * carry
            carry = h[SCAN_CHUNK - 1:SCAN_CHUNK]
            fwd.append(h)
        carry = h0_ref[1, 0]
        for c in reversed(range(nchunk)):
            h = chunk(hb_ref, c) + chunk(ab_ref, c) * carry
            carry = h[0:1]
            y_ref[pl.ds(c * SCAN_CHUNK, SCAN_CHUNK), :] = (fwd[c] + h).astype(y_ref.dtype)
    else:
        for c in range(nchunk):
            y_ref[pl.ds(c * SCAN_CHUNK, SCAN_CHUNK), :] = (
                chunk(hf_ref, c) + chunk(hb_ref, c)).astype(y_ref.dtype)
        fin_ref[0] = hf_ref[pl.ds(SCAN_CHUNK - 1, nchunk, stride=SCAN_PITCH), :]
        fin_ref[1] = hb_ref[pl.ds(0, nchunk, stride=SCAN_PITCH), :]


def _lru(proj, lp, seq_len, h0):
    m = proj.shape[0]
    latent = h0 is not None
    rows = SCAN_ROWS
    in_specs = [
        pl.BlockSpec((rows, LRU_BLOCK), lambda r, h: (r, COL_XA // LRU_BLOCK + h)),
        pl.BlockSpec((4, LRU_BLOCK), lambda r, h: (0, h)),
        pl.BlockSpec((1, LRU_BLOCK), lambda r, h: (0, h)),
        pl.BlockSpec((1, LRU_BLOCK, 4 * LRU_BLOCK), lambda r, h: (h, 0, 0)),
        pl.BlockSpec((1, 1, 4 * LRU_BLOCK), lambda r, h: (h, 0, 0)),
        pl.BlockSpec((1, 1, 2 * LRU_BLOCK), lambda r, h: (h, 0, 0)),
    ]
    args = [proj, lp["conv_w"], lp["conv_b"], lp["wg"], lp["bg"], lp["lam"]]
    y_spec = pl.BlockSpec((rows, LRU_BLOCK), lambda r, h: (r, h))
    y_shape = jax.ShapeDtypeStruct((m, D_RNN), BF16)
    if latent:
        in_specs.append(pl.BlockSpec((2, 1, 1, LRU_BLOCK), lambda r, h: (0, r, 0, h)))
        args.append(h0)
        out_specs, out_shape = y_spec, y_shape
    else:
        nseq = m // seq_len
        out_specs = (y_spec, pl.BlockSpec((2, rows // seq_len, LRU_BLOCK), lambda r, h: (0, r, h)))
        out_shape = (y_shape, jax.ShapeDtypeStruct((2, nseq, D_RNN), F32))
    scan_buf = pltpu.VMEM((rows // SCAN_CHUNK * SCAN_PITCH, LRU_BLOCK), F32)
    return pl.pallas_call(
        functools.partial(_lru_kernel, seq_len=seq_len, latent=latent),
        grid=(m // rows, LRU_HEADS),
        in_specs=in_specs,
        out_specs=out_specs,
        out_shape=out_shape,
        scratch_shapes=[scan_buf] * 6,
        compiler_params=_params("parallel", "parallel"),
        name="rglru_latent" if latent else "rglru_context",
    )(*args)


SOFTMAX_SCALE = HEAD_DIM ** -0.5 * LOG2E


def _stack_heads(q_ref, kh, r0, rows):
    parts = [q_ref[r0:r0 + rows, (kh * KV_GROUPS + g) * HEAD_DIM:(kh * KV_GROUPS + g + 1) * HEAD_DIM]
             for g in range(KV_GROUPS)]
    return jnp.concatenate(parts, axis=0)


def _dot_tn(a, b):
    return lax.dot_general(a, b, (((0,), (0,)), ((), ())), preferred_element_type=F32)


def _sink_row(sink_ref, kh, cols):
    parts = [jnp.full((1, cols), sink_ref[kh * KV_GROUPS + g] * LOG2E, F32) for g in range(KV_GROUPS)]
    return jnp.concatenate(parts, axis=1)


def _softmax_pv_t(t, sink, v):
    m = jnp.maximum(jnp.max(t, axis=0, keepdims=True), sink)
    e = jnp.exp2(t - m)
    denom = jnp.sum(e, axis=0, keepdims=True) + jnp.exp2(sink - m)
    return _dot_tn(v, e.astype(BF16)) * (1.0 / denom)


def _store_heads_t(y_ref, o_t, kh, r0, rows):
    for g in range(KV_GROUPS):
        h = kh * KV_GROUPS + g
        y_ref[r0:r0 + rows, h * HEAD_DIM:(h + 1) * HEAD_DIM] = (
            o_t[:, g * rows:(g + 1) * rows].T.astype(y_ref.dtype))


def _attn_ctx_kernel(sink_ref, q_ref, k_ref, v_ref, y_ref, *, seq_len):
    for b in range(q_ref.shape[0] // seq_len):
        r0 = b * seq_len
        for kh in range(N_KV_HEADS):
            sl = slice(kh * HEAD_DIM, (kh + 1) * HEAD_DIM)
            q = _stack_heads(q_ref, kh, r0, seq_len)
            t = _dot_nt(k_ref[r0:r0 + seq_len, sl], q) * SOFTMAX_SCALE
            o_t = _softmax_pv_t(t, _sink_row(sink_ref, kh, seq_len), v_ref[r0:r0 + seq_len, sl])
            _store_heads_t(y_ref, o_t, kh, r0, seq_len)


ATTN_CTX_SEQS = 4


def _attn_ctx(proj, sink, seq_len):
    m = proj.shape[0]
    rows = ATTN_CTX_SEQS * seq_len
    return pl.pallas_call(
        functools.partial(_attn_ctx_kernel, seq_len=seq_len),
        grid=(m // rows,),
        in_specs=[
            pl.BlockSpec(memory_space=pltpu.SMEM),
            pl.BlockSpec((rows, N_HEADS * HEAD_DIM), lambda b: (b, COL_Q // (N_HEADS * HEAD_DIM))),
            pl.BlockSpec((rows, D_KV), lambda b: (b, COL_K // D_KV)),
            pl.BlockSpec((rows, D_KV), lambda b: (b, COL_V // D_KV)),
        ],
        out_specs=pl.BlockSpec((rows, N_HEADS * HEAD_DIM), lambda b: (b, 0)),
        out_shape=jax.ShapeDtypeStruct((m, N_HEADS * HEAD_DIM), BF16),
        compiler_params=_params("parallel"),
        name="attn_context",
    )(sink, proj, proj, proj)


def _rope_tables(seq_len):
    nf = HEAD_DIM // 4
    freqs = ROPE_BASE ** (-np.arange(nf, dtype=np.float64) / nf)
    t = np.arange(seq_len)
    ang_row = (t // GRID_W)[:, None] * freqs[None, :]
    ang_col = (t % GRID_W)[:, None] * freqs[None, :]
    ang = np.concatenate([ang_row, ang_row, ang_col, ang_col], axis=1)
    first = (np.arange(HEAD_DIM) % (2 * nf)) < nf
    cos = np.cos(ang)
    sin = np.sin(ang)
    sin_a = np.where(first[None, :], -sin, 0.0)
    sin_b = np.where(first[None, :], 0.0, sin)
    return tuple(jnp.asarray(a, F32) for a in (cos, sin_a, sin_b))


def _rope_kernel(q_ref, k_ref, cos_ref, sa_ref, sb_ref, qo_ref, ko_ref):
    cos = cos_ref[...]
    sa = sa_ref[...]
    sb = sb_ref[...]
    nf = HEAD_DIM // 4

    def rope(x):
        x = x.astype(F32)
        return x * cos + pltpu.roll(x, HEAD_DIM - nf, 1) * sa + pltpu.roll(x, nf, 1) * sb

    for h in range(N_HEADS):
        sl = slice(h * HEAD_DIM, (h + 1) * HEAD_DIM)
        qo_ref[:, sl] = rope(q_ref[:, sl]).astype(BF16)
    for h in range(N_KV_HEADS):
        sl = slice(h * HEAD_DIM, (h + 1) * HEAD_DIM)
        ko_ref[:, sl] = rope(k_ref[:, sl]).astype(BF16)


def _rope(proj, seq_len):
    m = proj.shape[0]
    tm = 512
    per = seq_len // tm
    cos, sa, sb = _rope_tables(seq_len)
    tab = pl.BlockSpec((tm, HEAD_DIM), lambda i: (i % per, 0))
    return pl.pallas_call(
        _rope_kernel,
        grid=(m // tm,),
        in_specs=[
            pl.BlockSpec((tm, N_HEADS * HEAD_DIM), lambda i: (i, COL_Q // (N_HEADS * HEAD_DIM))),
            pl.BlockSpec((tm, D_KV), lambda i: (i, COL_K // D_KV)),
            tab, tab, tab,
        ],
        out_specs=(
            pl.BlockSpec((tm, N_HEADS * HEAD_DIM), lambda i: (i, 0)),
            pl.BlockSpec((tm, D_KV), lambda i: (i, 0)),
        ),
        out_shape=(
            jax.ShapeDtypeStruct((m, N_HEADS * HEAD_DIM), BF16),
            jax.ShapeDtypeStruct((m, D_KV), BF16),
        ),
        compiler_params=_params("parallel"),
        name="rope_qk",
    )(proj, proj, cos, sa, sb)


def _attn_lat_kernel(sink_ref, q_ref, kp_ref, kc_ref, kn_ref, vp_ref, vc_ref, vn_ref,
                     ck_ref, cv_ref, y_ref, *, nblk):
    j = pl.program_id(1)
    cols = KV_GROUPS * BLOCK_Q
    span = 3 * BLOCK_Q
    key = lax.broadcasted_iota(jnp.int32, (span, cols), 0)
    qry = lax.broadcasted_iota(jnp.int32, (span, cols), 1) & (BLOCK_Q - 1)
    lo = jnp.where(j > 0, qry, BLOCK_Q)
    hi = jnp.where(j < nblk - 1, qry + 2 * BLOCK_Q, 2 * BLOCK_Q - 1)
    bias = jnp.where(jnp.logical_and(key >= lo, key <= hi), 0.0, NEG_INF)
    for kh in range(N_KV_HEADS):
        sl = slice(kh * HEAD_DIM, (kh + 1) * HEAD_DIM)
        q = _stack_heads(q_ref, kh, 0, BLOCK_Q)
        keys = jnp.concatenate([kp_ref[:, sl], kc_ref[:, sl], kn_ref[:, sl], ck_ref[0, :, sl]], axis=0)
        vals = jnp.concatenate([vp_ref[:, sl], vc_ref[:, sl], vn_ref[:, sl], cv_ref[0, :, sl]], axis=0)
        t = _dot_nt(keys, q) * SOFTMAX_SCALE
        t = jnp.concatenate([t[:span] + bias, t[span:]], axis=0)
        o_t = _softmax_pv_t(t, _sink_row(sink_ref, kh, BLOCK_Q), vals)
        _store_heads_t(y_ref, o_t, kh, 0, BLOCK_Q)


def _attn_lat(qr, kr, proj, ck, cv, sink, seq_len):
    m = qr.shape[0]
    nblk = seq_len // BLOCK_Q
    nb = m // seq_len
    past = ck.shape[1]

    def rows_at(shift, colblk):
        def index(b, j):
            return (b * nblk + jnp.clip(j + shift, 0, nblk - 1), colblk)
        return index

    k_spec = lambda shift: pl.BlockSpec((BLOCK_Q, D_KV), rows_at(shift, 0))
    v_spec = lambda shift: pl.BlockSpec((BLOCK_Q, D_KV), rows_at(shift, COL_V // D_KV))
    return pl.pallas_call(
        functools.partial(_attn_lat_kernel, nblk=nblk),
        grid=(nb, nblk),
        in_specs=[
            pl.BlockSpec(memory_space=pltpu.SMEM),
            pl.BlockSpec((BLOCK_Q, N_HEADS * HEAD_DIM), rows_at(0, 0)),
            k_spec(-1), k_spec(0), k_spec(1), v_spec(-1), v_spec(0), v_spec(1),
            pl.BlockSpec((1, past, D_KV), lambda b, j: (b, 0, 0)),
            pl.BlockSpec((1, past, D_KV), lambda b, j: (b, 0, 0)),
        ],
        out_specs=pl.BlockSpec((BLOCK_Q, N_HEADS * HEAD_DIM), rows_at(0, 0)),
        out_shape=jax.ShapeDtypeStruct((m, N_HEADS * HEAD_DIM), BF16),
        compiler_params=_params("parallel", "parallel"),
        name="attn_latent",
    )(sink, qr, kr, kr, kr, proj, proj, proj, ck, cv)


def _pool_kernel(x0_ref, x1_ref, x2_ref, x3_ref, w_ref, s_ref, y_ref, pad_ref, *, seq_len):
    rows = x0_ref.shape[0]
    nseq = rows // seq_len
    zeros = jnp.zeros((nseq, POOL_PAD, POOL_GROUP), F32)
    pad_ref[:, 0:POOL_PAD, :] = zeros
    pad_ref[:, POOL_PAD + seq_len:, :] = zeros
    t = lax.broadcasted_iota(jnp.int32, (nseq, seq_len, POOL_GROUP), 1)
    for gi, (win, x_ref) in enumerate(zip(POOL_WINDOWS, (x0_ref, x1_ref, x2_ref, x3_ref))):
        cs = slice(gi * POOL_GROUP, (gi + 1) * POOL_GROUP)
        half = win // 2
        x = x_ref[...].astype(F32).reshape(nseq, seq_len, POOL_GROUP)
        pad_ref[:, POOL_PAD:POOL_PAD + seq_len, :] = x
        total = pad_ref[:, POOL_PAD - half:POOL_PAD - half + seq_len, :]
        for d in range(1 - half, half):
            total = total + (x if d == 0 else pad_ref[:, POOL_PAD + d:POOL_PAD + d + seq_len, :])
        cnt = (jnp.minimum(t + half, seq_len) - jnp.maximum(t - half, 0)).astype(F32)
        pooled = (total / cnt - x).reshape(rows, POOL_GROUP).astype(BF16)
        y_ref[:, cs] = (_dot(pooled, w_ref[gi]) * s_ref[:, cs]).astype(y_ref.dtype)


def _pool(proj, w, s, seq_len):
    m = proj.shape[0]
    rows = 2048
    nseq = rows // seq_len
    group = lambda gi: pl.BlockSpec((rows, POOL_GROUP), lambda r: (r, COL_XC // POOL_GROUP + gi))
    return pl.pallas_call(
        functools.partial(_pool_kernel, seq_len=seq_len),
        grid=(m // rows,),
        in_specs=[
            group(0), group(1), group(2), group(3),
            pl.BlockSpec((len(POOL_WINDOWS), POOL_GROUP, POOL_GROUP), lambda r: (0, 0, 0)),
            pl.BlockSpec((1, D_POOL), lambda r: (0, 0)),
        ],
        out_specs=pl.BlockSpec((rows, D_POOL), lambda r: (r, 0)),
        out_shape=jax.ShapeDtypeStruct((m, D_POOL), BF16),
        scratch_shapes=[pltpu.VMEM((nseq, seq_len + 2 * POOL_PAD, POOL_GROUP), F32)],
        compiler_params=_params("parallel"),
        name="pool_mix",
    )(proj, proj, proj, proj, w, s)


def _merge_kernel(ya_ref, yb_ref, yc_ref, g0_ref, g1_ref, g2_ref, g3_ref, g4_ref, g5_ref, x_ref, mod_ref,
                  bg_ref, wb_ref, wo_ref, n2_ref, x1_ref, h2_ref):
    mod = mod_ref[0]
    g_refs = (g0_ref, g1_ref, g2_ref, g3_ref, g4_ref, g5_ref)
    half = D_MODEL // 2
    merged = None
    for k, y_ref in enumerate((ya_ref, yb_ref, yc_ref)):
        y = _dot(y_ref[...], wb_ref[k])
        parts = []
        for p in range(2):
            z = g_refs[2 * k + p][...].astype(F32) + bg_ref[:, k * D_MODEL + p * half:k * D_MODEL + (p + 1) * half]
            parts.append((1.0 + jnp.tanh(0.5 * z)) * y[:, p * half:(p + 1) * half])
        term = jnp.concatenate(parts, axis=1)
        merged = term if merged is None else merged + term
    merged = 0.5 * merged
    gate1 = mod[:, 2 * D_MODEL:3 * D_MODEL]
    x1 = x_ref[...] + gate1 * _dot(merged.astype(BF16), wo_ref[...])
    x1_ref[...] = x1
    shift2 = mod[:, 3 * D_MODEL:4 * D_MODEL]
    scale2 = mod[:, 4 * D_MODEL:5 * D_MODEL]
    h2_ref[...] = (_rms(x1) * n2_ref[...] * (1.0 + scale2) + shift2).astype(BF16)


def _merge(ya, yb, yc, proj, x, mod, rows_per_mod, bg, wb, wo, n2, layer):
    m = x.shape[0]
    tm = 512
    per = rows_per_mod // tm
    half = D_MODEL // 2
    row = pl.BlockSpec((tm, D_MODEL), lambda i: (i, 0))
    gate = lambda c: pl.BlockSpec((tm, half), lambda i: (i, COL_G // half + c))
    const = lambda shape: pl.BlockSpec(shape, lambda i: (0,) * len(shape), pipeline_mode=pl.Buffered(1))
    return pl.pallas_call(
        _merge_kernel,
        grid=(m // tm,),
        in_specs=[
            row, row, row,
            gate(0), gate(1), gate(2), gate(3), gate(4), gate(5),
            row,
            pl.BlockSpec((1, 1, 6 * D_MODEL), lambda i: (i // per, 0, 0)),
            const((1, N_BRANCH * D_MODEL)),
            _resident((N_BRANCH, D_MODEL, D_MODEL), layer),
            _resident((D_MODEL, D_MODEL), layer),
            const((1, D_MODEL)),
        ],
        out_specs=(row, row),
        out_shape=(jax.ShapeDtypeStruct((m, D_MODEL), F32), jax.ShapeDtypeStruct((m, D_MODEL), BF16)),
        compiler_params=_params("parallel"),
        name="merge_out",
    )(ya, yb, yc, proj, proj, proj, proj, proj, proj, x, mod, bg, wb, wo, n2)


FFN_CK = 256
FFN_TM = 512
FFN_GAP = SUBLANES
GELU_C = float(np.sqrt(2.0 / np.pi))


def _ffn_kernel(*refs, seq_len, final):
    refs = list(refs)
    h_ref, hp_ref, hn_ref, x_ref, mod_ref, wup_ref, cw_ref, cb_ref, wd_ref = refs[:9]
    fn_ref = refs[9] if final else None
    o_ref, hx_ref, u_ref, act_ref = refs[-4:]
    tm = FFN_TM
    halo = seq_len > tm
    i = pl.program_id(0)

    if halo:
        per_seq = seq_len // tm
        at_start = i % per_seq == 0
        at_end = i % per_seq == per_seq - 1
        zeros = jnp.zeros((BF16_ROWS, D_MODEL), BF16)

        @pl.when(at_start)
        def _():
            hx_ref[0:BF16_ROWS, :] = zeros

        @pl.when(jnp.logical_not(at_start))
        def _():
            hx_ref[0:BF16_ROWS, :] = hp_ref[...]

        @pl.when(at_end)
        def _():
            hx_ref[BF16_ROWS + tm:, :] = zeros

        @pl.when(jnp.logical_not(at_end))
        def _():
            hx_ref[BF16_ROWS + tm:, :] = hn_ref[...]

        hx_ref[BF16_ROWS:BF16_ROWS + tm, :] = h_ref[...]
        bases = (BF16_ROWS,)
        seg = tm
    else:
        nseg = tm // seq_len
        seg = seq_len
        bases = tuple(FFN_GAP + s * (seg + FFN_GAP) for s in range(nseg))
        for s in range(nseg + 1):
            u_ref[s * (seg + FFN_GAP):s * (seg + FFN_GAP) + FFN_GAP, :] = jnp.zeros((FFN_GAP, FFN_CK), F32)

    def taps(offset):
        return jnp.concatenate([u_ref[b + offset:b + offset + seg, :] for b in bases], axis=0)

    for c in range(D_FF // FFN_CK):
        cs = slice(c * FFN_CK, (c + 1) * FFN_CK)
        vs = slice(D_FF + c * FFN_CK, D_FF + (c + 1) * FFN_CK)
        if halo:
            u_ext = _dot(hx_ref[...], wup_ref[:, cs])
            u_ref[...] = u_ext
            u0 = u_ext[BF16_ROWS:BF16_ROWS + tm]
        else:
            u0 = _dot(h_ref[...], wup_ref[:, cs])
            for s, b in enumerate(bases):
                u_ref[b:b + seg, :] = u0[s * seg:(s + 1) * seg]
        uv = _dot(h_ref[...], wup_ref[:, vs])
        gff = taps(-1) * cw_ref[0:1, cs] + u0 * cw_ref[1:2, cs] + taps(1) * cw_ref[2:3, cs] + cb_ref[:, cs]
        inner = gff * (GELU_C + (GELU_C * 0.044715) * (gff * gff))
        act_ref[:, cs] = (0.5 * (gff * uv) * (1.0 + jnp.tanh(inner))).astype(BF16)

    gate2 = mod_ref[0][:, 5 * D_MODEL:6 * D_MODEL]
    out = x_ref[...] + gate2 * _dot(act_ref[...], wd_ref[...])
    if final:
        out = _rms(out) * fn_ref[...]
    o_ref[...] = out


def _ffn(h2, x1, mod, rows_per_mod, wup, cw, cb, wd, layer, seq_len, final_norm):
    m = x1.shape[0]
    tm = FFN_TM
    per = rows_per_mod // tm
    hb = tm // BF16_ROWS
    last_halo = m // BF16_ROWS - 1
    final = final_norm is not None
    halo = seq_len > tm
    const = lambda shape: pl.BlockSpec(shape, lambda i: (0,) * len(shape), pipeline_mode=pl.Buffered(1))
    in_specs = [
        pl.BlockSpec((tm, D_MODEL), lambda i: (i, 0)),
        pl.BlockSpec((BF16_ROWS, D_MODEL), lambda i: (jnp.maximum(i * hb - 1, 0), 0)),
        pl.BlockSpec((BF16_ROWS, D_MODEL), lambda i: (jnp.minimum((i + 1) * hb, last_halo), 0)),
        pl.BlockSpec((tm, D_MODEL), lambda i: (i, 0)),
        pl.BlockSpec((1, 1, 6 * D_MODEL), lambda i: (i // per, 0, 0)),
        _resident((D_MODEL, 2 * D_FF), layer),
        const((3, D_FF)),
        const((1, D_FF)),
        _resident((D_FF, D_MODEL), layer),
    ]
    args = [h2, h2, h2, x1, mod, wup, cw, cb, wd]
    if final:
        in_specs.append(const((1, D_MODEL)))
        args.append(final_norm)
    if halo:
        u_rows = tm + 2 * BF16_ROWS
    else:
        u_rows = FFN_GAP + (tm // seq_len) * (seq_len + FFN_GAP)
    return pl.pallas_call(
        functools.partial(_ffn_kernel, seq_len=seq_len, final=final),
        grid=(m // tm,),
        in_specs=in_specs,
        out_specs=pl.BlockSpec((tm, D_MODEL), lambda i: (i, 0)),
        out_shape=jax.ShapeDtypeStruct((m, D_MODEL), F32),
        scratch_shapes=[
            pltpu.VMEM((tm + 2 * BF16_ROWS, D_MODEL), BF16),
            pltpu.VMEM((u_rows, FFN_CK), F32),
            pltpu.VMEM((tm, D_FF), BF16),
        ],
        compiler_params=_params("parallel"),
        name="conv_glu_ffn",
    )(*args)


def _trunk_layer(x, mod, rows_per_mod, p, big, layer, seq_len, ctx, final_norm):
    proj, k_new, v_new = _inproj(x, mod, rows_per_mod, p["norm1"], big["w_in"], layer, seq_len,
                                 want_kv=ctx is None)
    if ctx is None:
        ya, h_fin = _lru(proj, p["lru"], seq_len, None)
        yb = _attn_ctx(proj, p["sink"], seq_len)
    else:
        ck, cv, h0 = ctx
        ya = _lru(proj, p["lru"], seq_len, h0)
        h_fin = None
        qr, kr = _rope(proj, seq_len)
        yb = _attn_lat(qr, kr, proj, ck, cv, p["sink"], seq_len)
    yc = _pool(proj, p["pool_w"], p["pool_scale"], seq_len)
    x1, h2 = _merge(ya, yb, yc, proj, x, mod, rows_per_mod, p["b_gate"], big["w_branch"], big["w_out"],
                    p["norm2"], layer)
    out = _ffn(h2, x1, mod, rows_per_mod, big["ffn_up"], p["ffn_conv"], p["ffn_conv_b"], big["ffn_down"],
               layer, seq_len, final_norm)
    return out, k_new, v_new, h_fin


def _layer_params(l, norm1, norm2, b_gate, lru_conv, lru_conv_b, lru_wa, lru_ba, lru_wx, lru_bx,
                  lru_lambda, attn_sink, pool_w, pool_scale, ffn_conv, ffn_conv_b):
    def per_head(v):
        return v.reshape(2, LRU_HEADS, LRU_BLOCK).transpose(1, 0, 2)

    wg = jnp.concatenate([lru_wa[l, 0], lru_wx[l, 0], lru_wa[l, 1], lru_wx[l, 1]], axis=-1).astype(BF16)
    ba = per_head(lru_ba[l])
    bx = per_head(lru_bx[l])
    bg = jnp.concatenate([ba[:, 0], bx[:, 0], ba[:, 1], bx[:, 1]], axis=-1)[:, None, :]
    lam = per_head(lru_lambda[l]).reshape(LRU_HEADS, 1, 2 * LRU_BLOCK)
    return {
        "norm1": norm1[l][None], "norm2": norm2[l][None], "b_gate": b_gate[l][None],
        "lru": {"conv_w": lru_conv[l], "conv_b": lru_conv_b[l][None], "wg": wg, "bg": bg, "lam": lam},
        "sink": attn_sink[l], "pool_w": pool_w[l].astype(BF16), "pool_scale": pool_scale[l][None],
        "ffn_conv": ffn_conv[l], "ffn_conv_b": ffn_conv_b[l][None],
    }


def kernel(x_prompt, x_sample, cache_k, cache_v, state_lru, c, c_ctx, w_ada, b_ada, norm1, norm2, w_in,
           b_gate, lru_conv, lru_conv_b, lru_wa, lru_ba, lru_wx, lru_bx, lru_lambda, attn_sink, pool_w,
           pool_scale, w_branch, w_out, ffn_up, ffn_conv, ffn_conv_b, ffn_down, final_norm):
    batch, seq, _ = x_prompt.shape
    dec_batch, dec_seq, _ = x_sample.shape
    past = cache_k.shape[2]
    assert seq == SCAN_CHUNK and dec_seq % SCAN_ROWS == 0 and (batch * seq) % SCAN_ROWS == 0

    c_rows = jnp.concatenate(
        [c_ctx[None], c, jnp.zeros((SUBLANES - 1 - dec_batch, D_MODEL), F32)], axis=0)
    mods = _ada(c_rows, w_ada, b_ada)

    xp = x_prompt.reshape(batch * seq, D_MODEL)
    xs = x_sample.reshape(dec_batch * dec_seq, D_MODEL)
    fn = final_norm[None]
    big = {"w_in": w_in.astype(BF16), "w_branch": w_branch.astype(BF16), "w_out": w_out.astype(BF16),
           "ffn_up": ffn_up.astype(BF16), "ffn_down": ffn_down.astype(BF16)}
    ks, vs, hs = [], [], []
    for l in range(DEPTH):
        p = _layer_params(l, norm1, norm2, b_gate, lru_conv, lru_conv_b, lru_wa, lru_ba, lru_wx, lru_bx,
                          lru_lambda, attn_sink, pool_w, pool_scale, ffn_conv, ffn_conv_b)
        last = fn if l == DEPTH - 1 else None
        mod_ctx = mods[l, 0:1][:, None, :]
        mod_lat = mods[l, 1:1 + dec_batch][:, None, :]
        xp, k_new, v_new, h_fin = _trunk_layer(xp, mod_ctx, batch * seq, p, big, l, seq, None, last)
        ck = cache_k[:, l].reshape(dec_batch, past, D_KV).astype(BF16)
        cv = cache_v[:, l].reshape(dec_batch, past, D_KV).astype(BF16)
        h0 = state_lru[:, l].transpose(1, 0, 2)[:, :, None, :]
        xs, _, _, _ = _trunk_layer(xs, mod_lat, dec_seq, p, big, l, dec_seq, (ck, cv, h0), last)
        ks.append(k_new.reshape(batch, seq, N_KV_HEADS, HEAD_DIM))
        vs.append(v_new.reshape(batch, seq, N_KV_HEADS, HEAD_DIM))
        hs.append(h_fin.transpose(1, 0, 2))
    y_prompt = xp.reshape(batch, seq, D_MODEL)
    y_sample = xs.reshape(dec_batch, dec_seq, D_MODEL)
    return (y_prompt, y_sample, jnp.stack(ks, axis=1), jnp.stack(vs, axis=1), jnp.stack(hs, axis=1))
```

```python
import functools

import numpy as np
import jax
import jax.numpy as jnp
from jax import lax
from jax.experimental import pallas as pl
from jax.experimental.pallas import tpu as pltpu

F32 = jnp.float32
BF16 = jnp.bfloat16

D_MODEL = 1024
DEPTH = 2
GRID_W = 64
EPS = 1e-6
N_BRANCH = 3
D_RNN = 1024
LRU_HEADS = 8
LRU_BLOCK = D_RNN // LRU_HEADS
LRU_C = 8.0
N_HEADS = 8
N_KV_HEADS = 2
KV_GROUPS = N_HEADS // N_KV_HEADS
HEAD_DIM = 128
D_KV = N_KV_HEADS * HEAD_DIM
WINDOW = 128
BLOCK_Q = 128
ROPE_BASE = 10000.0
NEG_INF = -1e30
D_POOL = 1024
POOL_WINDOWS = (2, 4, 8, 16)
POOL_GROUP = D_POOL // len(POOL_WINDOWS)
D_FF = 2816
D_IN = D_RNN + N_HEADS * HEAD_DIM + 2 * D_KV + D_POOL + N_BRANCH * D_MODEL

COL_XA = 0
COL_Q = COL_XA + D_RNN
COL_K = COL_Q + N_HEADS * HEAD_DIM
COL_V = COL_K + D_KV
COL_XC = COL_V + D_KV
COL_G = COL_XC + D_POOL
LOG2E = float(np.log2(np.e))

VMEM_LIMIT_BYTES = 52 * 1024 * 1024
SUBLANES = 8
LANES = 128
BF16_ROWS = 16

SCAN_CHUNK = 256
SCAN_PITCH = 260
SCAN_ROWS = 2048
POOL_PAD = 8


def _params(*sem):
    return pltpu.CompilerParams(dimension_semantics=sem, vmem_limit_bytes=VMEM_LIMIT_BYTES)


def _dot(a, b):
    return jnp.dot(a, b, preferred_element_type=F32)


def _dot_nt(a, b):
    return lax.dot_general(a, b, (((1,), (1,)), ((), ())), preferred_element_type=F32)


def _sigmoid(z):
    return 0.5 * (1.0 + jnp.tanh(0.5 * z))


def _rms(x):
    return x * lax.rsqrt(jnp.mean(x * x, axis=-1, keepdims=True) + EPS)


def _ada_kernel(c_ref, w_ref, b_ref, o_ref):
    c = c_ref[...]
    s = c * _sigmoid(c)
    o_ref[0] = _dot(s.astype(BF16), w_ref[0].astype(BF16)) + b_ref[0]


def _ada(c_rows, w_ada, b_ada):
    tn = 1536
    return pl.pallas_call(
        _ada_kernel,
        grid=(DEPTH, 6 * D_MODEL // tn),
        in_specs=[
            pl.BlockSpec((SUBLANES, D_MODEL), lambda l, j: (0, 0)),
            pl.BlockSpec((1, D_MODEL, tn), lambda l, j: (l, 0, j)),
            pl.BlockSpec((1, 1, tn), lambda l, j: (l, 0, j)),
        ],
        out_specs=pl.BlockSpec((1, SUBLANES, tn), lambda l, j: (l, 0, j)),
        out_shape=jax.ShapeDtypeStruct((DEPTH, SUBLANES, 6 * D_MODEL), F32),
        compiler_params=_params("parallel", "parallel"),
        name="ada_mod",
    )(c_rows, w_ada, b_ada.reshape(DEPTH, 1, 6 * D_MODEL))


INPROJ_TM = 512
INPROJ_CK = 2 * D_KV


def _resident(shape, layer):
    ndim = len(shape)
    return pl.BlockSpec((None,) + tuple(shape), lambda i: (layer,) + (0,) * ndim,
                        pipeline_mode=pl.Buffered(1))


def _inproj_kernel(*refs, seq_len, want_kv):
    if want_kv:
        x_ref, mod_ref, g_ref, w_ref, o_ref, k_ref, v_ref = refs
    else:
        x_ref, mod_ref, g_ref, w_ref, cos_ref, sa_ref, sb_ref, o_ref = refs
    mod = mod_ref[0]
    shift = mod[:, 0:D_MODEL]
    scale = mod[:, D_MODEL:2 * D_MODEL]
    h = (_rms(x_ref[...]) * g_ref[...] * (1.0 + scale) + shift).astype(BF16)
    nf = HEAD_DIM // 4

    def rope(x):
        return (x * cos_ref[...] + pltpu.roll(x, HEAD_DIM - nf, 1) * sa_ref[...]
                + pltpu.roll(x, nf, 1) * sb_ref[...])

    for c in range(D_IN // INPROJ_CK):
        acc = _dot(h, w_ref[:, c * INPROJ_CK:(c + 1) * INPROJ_CK])
        if not want_kv:
            lo = c * INPROJ_CK
            heads = [acc[:, j * HEAD_DIM:(j + 1) * HEAD_DIM] for j in range(INPROJ_CK // HEAD_DIM)]
            heads = [rope(hd) if COL_Q <= lo + j * HEAD_DIM < COL_V else hd for j, hd in enumerate(heads)]
            acc = jnp.concatenate(heads, axis=1)
        o_ref[:, c * INPROJ_CK:(c + 1) * INPROJ_CK] = acc.astype(o_ref.dtype)
        if want_kv and c == COL_K // INPROJ_CK:
            for b in range(INPROJ_TM // seq_len):
                rows = slice(b * seq_len, (b + 1) * seq_len)
                for hd in range(N_KV_HEADS):
                    dst = pl.ds(hd, seq_len, stride=N_KV_HEADS)
                    k_ref[b, dst, :] = acc[rows, hd * HEAD_DIM:(hd + 1) * HEAD_DIM]
                    v_ref[b, dst, :] = acc[rows, D_KV + hd * HEAD_DIM:D_KV + (hd + 1) * HEAD_DIM]


def _inproj(x, mod, rows_per_mod, g, w, layer, seq_len, want_kv):
    m = x.shape[0]
    tm = INPROJ_TM
    per = rows_per_mod // tm
    proj_spec = pl.BlockSpec((tm, D_IN), lambda i: (i, 0))
    proj_shape = jax.ShapeDtypeStruct((m, D_IN), BF16)
    if want_kv:
        nb = tm // seq_len
        cache_spec = pl.BlockSpec((nb, seq_len * N_KV_HEADS, HEAD_DIM), lambda i: (i, 0, 0))
        cache_shape = jax.ShapeDtypeStruct((m // seq_len, seq_len * N_KV_HEADS, HEAD_DIM), F32)
        out_specs = (proj_spec, cache_spec, cache_spec)
        out_shape = (proj_shape, cache_shape, cache_shape)
        extra_specs, extra_args = [], []
    else:
        out_specs, out_shape = proj_spec, proj_shape
        tab = pl.BlockSpec((tm, HEAD_DIM), lambda i: (i % (seq_len // tm), 0))
        extra_specs, extra_args = [tab, tab, tab], list(_rope_tables(seq_len))
    out = pl.pallas_call(
        functools.partial(_inproj_kernel, seq_len=seq_len, want_kv=want_kv),
        grid=(m // tm,),
        in_specs=[
            pl.BlockSpec((tm, D_MODEL), lambda i: (i, 0)),
            pl.BlockSpec((1, 1, 6 * D_MODEL), lambda i: (i // per, 0, 0)),
            pl.BlockSpec((1, D_MODEL), lambda i: (0, 0)),
            _resident((D_MODEL, D_IN), layer),
        ] + extra_specs,
        out_specs=out_specs,
        out_shape=out_shape,
        compiler_params=_params("parallel"),
        name="in_proj",
    )(x, mod, g, w, *extra_args)
    return out if want_kv else (out, None, None)


LRU_HEADS_PER_STEP = 2


def _shift_rows(x, k, seq_len):
    rows, lanes = x.shape
    rolled = pltpu.roll(x, k % rows, 0)
    edge = lax.broadcasted_iota(jnp.int32, (SUBLANES, lanes), 0)
    pieces = []
    for base in range(0, rows, seq_len):
        if k > 0:
            head = jnp.where(edge >= k, rolled[base:base + SUBLANES], 0.0)
            pieces += [head, rolled[base + SUBLANES:base + seq_len]]
        else:
            end = base + seq_len
            tail = jnp.where(edge < SUBLANES + k, rolled[end - SUBLANES:end], 0.0)
            pieces += [rolled[base:end - SUBLANES], tail]
    return jnp.concatenate(pieces, axis=0)


def _lru_kernel(*refs, seq_len, latent):
    if latent:
        (x_ref, cw_ref, cb_ref, wg_ref, bg_ref, lam_ref, h0_ref, y_ref,
         af_ref, bf_ref, ab_ref, bb_ref, hf_ref, hb_ref) = refs
    else:
        (x_ref, cw_ref, cb_ref, wg_ref, bg_ref, lam_ref, y_ref, fin_ref,
         af_ref, bf_ref, ab_ref, bb_ref, hf_ref, hb_ref) = refs
    rows = SCAN_ROWS
    nchunk = rows // SCAN_CHUNK
    nhead = LRU_HEADS_PER_STEP
    for hd in range(nhead):
        lanes = slice(hd * LRU_BLOCK, (hd + 1) * LRU_BLOCK)
        x = x_ref[:, lanes].astype(F32)
        xc = (_shift_rows(x, 2, seq_len) * cw_ref[0:1, lanes] + _shift_rows(x, 1, seq_len) * cw_ref[1:2, lanes]
              + x * cw_ref[2:3, lanes] + _shift_rows(x, -1, seq_len) * cw_ref[3:4, lanes] + cb_ref[:, lanes])
        th = jnp.tanh(_dot(xc.astype(BF16), wg_ref[hd] * 0.5) + 0.5 * bg_ref[hd])
        xh = 0.5 * xc
        lam = lam_ref[hd]
        for d, (a_ref, b_ref) in enumerate(((af_ref, bf_ref), (ab_ref, bb_ref))):
            th_r = th[:, 2 * d * LRU_BLOCK:(2 * d + 1) * LRU_BLOCK]
            th_i = th[:, (2 * d + 1) * LRU_BLOCK:(2 * d + 2) * LRU_BLOCK]
            nl = -lam[:, d * LRU_BLOCK:(d + 1) * LRU_BLOCK]
            softplus = jnp.maximum(nl, 0.0) + jnp.log(1.0 + jnp.exp(-jnp.abs(nl)))
            ch = (-0.5 * LRU_C * LOG2E) * softplus
            a = jnp.exp2(ch + ch * th_r)
            y = 1.0 - a * a
            b = (y * lax.rsqrt(jnp.maximum(y, 1e-30))) * ((1.0 + th_i) * xh)
            for c in range(nchunk):
                a_ref[hd, pl.ds(c * SCAN_PITCH, SCAN_CHUNK), :] = a[c * SCAN_CHUNK:(c + 1) * SCAN_CHUNK]
                b_ref[hd, pl.ds(c * SCAN_PITCH, SCAN_CHUNK), :] = b[c * SCAN_CHUNK:(c + 1) * SCAN_CHUNK]

    def step(t, carry):
        rf = pl.ds(t, nchunk, stride=SCAN_PITCH)
        rb = pl.ds(SCAN_CHUNK - 1 - t, nchunk, stride=SCAN_PITCH)
        out = []
        for hd in range(nhead):
            hf, hb, pf, pb = carry[4 * hd:4 * hd + 4]
            a_f = af_ref[hd, rf, :]
            a_b = ab_ref[hd, rb, :]
            hf = a_f * hf + bf_ref[hd, rf, :]
            hb = a_b * hb + bb_ref[hd, rb, :]
            hf_ref[hd, rf, :] = hf
            hb_ref[hd, rb, :] = hb
            if latent:
                pf = pf * a_f
                pb = pb * a_b
                af_ref[hd, rf, :] = pf
                ab_ref[hd, rb, :] = pb
            out += [hf, hb, pf, pb]
        return tuple(out)

    zero = jnp.zeros((nchunk, LRU_BLOCK), F32)
    one = jnp.ones((nchunk, LRU_BLOCK), F32)
    lax.fori_loop(0, SCAN_CHUNK, step, (zero, zero, one, one) * nhead, unroll=8)

    for hd in range(nhead):
        lanes = slice(hd * LRU_BLOCK, (hd + 1) * LRU_BLOCK)

        def chunk(ref, c):
            return ref[hd, pl.ds(c * SCAN_PITCH, SCAN_CHUNK), :]

        if latent:
            carry = h0_ref[0, 0][:, lanes]
            fwd = []
            for c in range(nchunk):
                h = chunk(hf_ref, c) + chunk(af_ref, c) * carry
                carry = h[SCAN_CHUNK - 1:SCAN_CHUNK]
                fwd.append(h)
            carry = h0_ref[1, 0][:, lanes]
            for c in reversed(range(nchunk)):
                h = chunk(hb_ref, c) + chunk(ab_ref, c) * carry
                carry = h[0:1]
                y_ref[pl.ds(c * SCAN_CHUNK, SCAN_CHUNK), lanes] = (fwd[c] + h).astype(y_ref.dtype)
        else:
            for c in range(nchunk):
                y_ref[pl.ds(c * SCAN_CHUNK, SCAN_CHUNK), lanes] = (
                    chunk(hf_ref, c) + chunk(hb_ref, c)).astype(y_ref.dtype)
            fin_ref[0, :, lanes] = hf_ref[hd, pl.ds(SCAN_CHUNK - 1, nchunk, stride=SCAN_PITCH), :]
            fin_ref[1, :, lanes] = hb_ref[hd, pl.ds(0, nchunk, stride=SCAN_PITCH), :]


def _lru(proj, lp, seq_len, h0):
    m = proj.shape[0]
    latent = h0 is not None
    rows = SCAN_ROWS
    nhead = LRU_HEADS_PER_STEP
    width = nhead * LRU_BLOCK
    in_specs = [
        pl.BlockSpec((rows, width), lambda r, h: (r, COL_XA // width + h)),
        pl.BlockSpec((4, width), lambda r, h: (0, h)),
        pl.BlockSpec((1, width), lambda r, h: (0, h)),
        pl.BlockSpec((nhead, LRU_BLOCK, 4 * LRU_BLOCK), lambda r, h: (h, 0, 0)),
        pl.BlockSpec((nhead, 1, 4 * LRU_BLOCK), lambda r, h: (h, 0, 0)),
        pl.BlockSpec((nhead, 1, 2 * LRU_BLOCK), lambda r, h: (h, 0, 0)),
    ]
    args = [proj, lp["conv_w"], lp["conv_b"], lp["wg"], lp["bg"], lp["lam"]]
    y_spec = pl.BlockSpec((rows, width), lambda r, h: (r, h))
    y_shape = jax.ShapeDtypeStruct((m, D_RNN), BF16)
    if latent:
        in_specs.append(pl.BlockSpec((2, 1, 1, width), lambda r, h: (0, r, 0, h)))
        args.append(h0)
        out_specs, out_shape = y_spec, y_shape
    else:
        nseq = m // seq_len
        out_specs = (y_spec, pl.BlockSpec((2, rows // seq_len, width), lambda r, h: (0, r, h)))
        out_shape = (y_shape, jax.ShapeDtypeStruct((2, nseq, D_RNN), F32))
    scan_buf = pltpu.VMEM((nhead, rows // SCAN_CHUNK * SCAN_PITCH, LRU_BLOCK), F32)
    return pl.pallas_call(
        functools.partial(_lru_kernel, seq_len=seq_len, latent=latent),
        grid=(m // rows, LRU_HEADS // nhead),
        in_specs=in_specs,
        out_specs=out_specs,
        out_shape=out_shape,
        scratch_shapes=[scan_buf] * 6,
        compiler_params=_params("parallel", "parallel"),
        name="rglru_latent" if latent else "rglru_context",
    )(*args)


SOFTMAX_SCALE = HEAD_DIM ** -0.5 * LOG2E


def _stack_heads(q_ref, kh, r0, rows):
    parts = [q_ref[r0:r0 + rows, (kh * KV_GROUPS + g) * HEAD_DIM:(kh * KV_GROUPS + g + 1) * HEAD_DIM]
             for g in range(KV_GROUPS)]
    return jnp.concatenate(parts, axis=0)


def _dot_tn(a, b):
    return lax.dot_general(a, b, (((0,), (0,)), ((), ())), preferred_element_type=F32)


def _sink_row(sink_ref, kh, cols):
    parts = [jnp.full((1, cols), sink_ref[kh * KV_GROUPS + g] * LOG2E, F32) for g in range(KV_GROUPS)]
    return jnp.concatenate(parts, axis=1)


def _softmax_pv_t(t, sink, v):
    m = jnp.maximum(jnp.max(t, axis=0, keepdims=True), sink)
    e = jnp.exp2(t - m)
    denom = jnp.sum(e, axis=0, keepdims=True) + jnp.exp2(sink - m)
    return _dot_tn(v, e.astype(BF16)) * (1.0 / denom)


def _store_heads_t(y_ref, o_t, kh, r0, rows):
    for g in range(KV_GROUPS):
        h = kh * KV_GROUPS + g
        y_ref[r0:r0 + rows, h * HEAD_DIM:(h + 1) * HEAD_DIM] = (
            o_t[:, g * rows:(g + 1) * rows].T.astype(y_ref.dtype))


def _attn_ctx_kernel(sink_ref, q_ref, k_ref, v_ref, y_ref, *, seq_len):
    for b in range(q_ref.shape[0] // seq_len):
        r0 = b * seq_len
        for kh in range(N_KV_HEADS):
            sl = slice(kh * HEAD_DIM, (kh + 1) * HEAD_DIM)
            q = _stack_heads(q_ref, kh, r0, seq_len)
            t = _dot_nt(k_ref[r0:r0 + seq_len, sl], q) * SOFTMAX_SCALE
            o_t = _softmax_pv_t(t, _sink_row(sink_ref, kh, seq_len), v_ref[r0:r0 + seq_len, sl])
            _store_heads_t(y_ref, o_t, kh, r0, seq_len)


ATTN_CTX_SEQS = 4


def _attn_ctx(proj, sink, seq_len):
    m = proj.shape[0]
    rows = ATTN_CTX_SEQS * seq_len
    return pl.pallas_call(
        functools.partial(_attn_ctx_kernel, seq_len=seq_len),
        grid=(m // rows,),
        in_specs=[
            pl.BlockSpec(memory_space=pltpu.SMEM),
            pl.BlockSpec((rows, N_HEADS * HEAD_DIM), lambda b: (b, COL_Q // (N_HEADS * HEAD_DIM))),
            pl.BlockSpec((rows, D_KV), lambda b: (b, COL_K // D_KV)),
            pl.BlockSpec((rows, D_KV), lambda b: (b, COL_V // D_KV)),
        ],
        out_specs=pl.BlockSpec((rows, N_HEADS * HEAD_DIM), lambda b: (b, 0)),
        out_shape=jax.ShapeDtypeStruct((m, N_HEADS * HEAD_DIM), BF16),
        compiler_params=_params("parallel"),
        name="attn_context",
    )(sink, proj, proj, proj)


def _rope_tables(seq_len):
    nf = HEAD_DIM // 4
    freqs = ROPE_BASE ** (-np.arange(nf, dtype=np.float64) / nf)
    t = np.arange(seq_len)
    ang_row = (t // GRID_W)[:, None] * freqs[None, :]
    ang_col = (t % GRID_W)[:, None] * freqs[None, :]
    ang = np.concatenate([ang_row, ang_row, ang_col, ang_col], axis=1)
    first = (np.arange(HEAD_DIM) % (2 * nf)) < nf
    cos = np.cos(ang)
    sin = np.sin(ang)
    sin_a = np.where(first[None, :], -sin, 0.0)
    sin_b = np.where(first[None, :], 0.0, sin)
    return tuple(jnp.asarray(a, F32) for a in (cos, sin_a, sin_b))


def _attn_lat_kernel(sink_ref, q_ref, kp_ref, kc_ref, kn_ref, vp_ref, vc_ref, vn_ref,
                     ck_ref, cv_ref, y_ref, *, nblk):
    j = pl.program_id(1)
    cols = KV_GROUPS * BLOCK_Q
    span = 3 * BLOCK_Q
    key = lax.broadcasted_iota(jnp.int32, (span, cols), 0)
    qry = lax.broadcasted_iota(jnp.int32, (span, cols), 1) & (BLOCK_Q - 1)
    lo = jnp.where(j > 0, qry, BLOCK_Q)
    hi = jnp.where(j < nblk - 1, qry + 2 * BLOCK_Q, 2 * BLOCK_Q - 1)
    bias = jnp.where(jnp.logical_and(key >= lo, key <= hi), 0.0, NEG_INF)
    for kh in range(N_KV_HEADS):
        sl = slice(kh * HEAD_DIM, (kh + 1) * HEAD_DIM)
        q = _stack_heads(q_ref, kh, 0, BLOCK_Q)
        keys = jnp.concatenate([kp_ref[:, sl], kc_ref[:, sl], kn_ref[:, sl], ck_ref[0, :, sl]], axis=0)
        vals = jnp.concatenate([vp_ref[:, sl], vc_ref[:, sl], vn_ref[:, sl], cv_ref[0, :, sl]], axis=0)
        t = _dot_nt(keys, q) * SOFTMAX_SCALE
        t = jnp.concatenate([t[:span] + bias, t[span:]], axis=0)
        o_t = _softmax_pv_t(t, _sink_row(sink_ref, kh, BLOCK_Q), vals)
        _store_heads_t(y_ref, o_t, kh, 0, BLOCK_Q)


def _attn_lat(proj, ck, cv, sink, seq_len):
    m = proj.shape[0]
    nblk = seq_len // BLOCK_Q
    nb = m // seq_len
    past = ck.shape[1]

    def rows_at(shift, colblk):
        def index(b, j):
            return (b * nblk + jnp.clip(j + shift, 0, nblk - 1), colblk)
        return index

    k_spec = lambda shift: pl.BlockSpec((BLOCK_Q, D_KV), rows_at(shift, COL_K // D_KV))
    v_spec = lambda shift: pl.BlockSpec((BLOCK_Q, D_KV), rows_at(shift, COL_V // D_KV))
    return pl.pallas_call(
        functools.partial(_attn_lat_kernel, nblk=nblk),
        grid=(nb, nblk),
        in_specs=[
            pl.BlockSpec(memory_space=pltpu.SMEM),
            pl.BlockSpec((BLOCK_Q, N_HEADS * HEAD_DIM), rows_at(0, COL_Q // (N_HEADS * HEAD_DIM))),
            k_spec(-1), k_spec(0), k_spec(1), v_spec(-1), v_spec(0), v_spec(1),
            pl.BlockSpec((1, past, D_KV), lambda b, j: (b, 0, 0)),
            pl.BlockSpec((1, past, D_KV), lambda b, j: (b, 0, 0)),
        ],
        out_specs=pl.BlockSpec((BLOCK_Q, N_HEADS * HEAD_DIM), rows_at(0, 0)),
        out_shape=jax.ShapeDtypeStruct((m, N_HEADS * HEAD_DIM), BF16),
        compiler_params=_params("parallel", "parallel"),
        name="attn_latent",
    )(sink, proj, proj, proj, proj, proj, proj, proj, ck, cv)


POOL_TILE = 256
POOL_HALO = 128
POOL_ROWS = 2048


def _pool_bands():
    r = np.arange(POOL_TILE)[:, None]
    c = np.arange(POOL_TILE + 2 * POOL_HALO)[None, :] - POOL_HALO
    bands = [(c >= r - win // 2) & (c < r + win // 2) for win in POOL_WINDOWS]
    return jnp.asarray(np.stack(bands), BF16)


def _pool_kernel(x0_ref, x1_ref, x2_ref, x3_ref, band_ref, w_ref, s_ref, y_ref, pad_ref, *, seq_len):
    tiles_per_seq = seq_len // POOL_TILE
    edge = lax.broadcasted_iota(jnp.int32, (SUBLANES, POOL_GROUP), 0)
    if tiles_per_seq > 1:
        zeros = jnp.zeros((POOL_HALO, POOL_GROUP), BF16)
        pad_ref[0:POOL_HALO, :] = zeros
        pad_ref[POOL_HALO + seq_len:, :] = zeros
    for gi, (win, x_ref) in enumerate(zip(POOL_WINDOWS, (x0_ref, x1_ref, x2_ref, x3_ref))):
        cs = slice(gi * POOL_GROUP, (gi + 1) * POOL_GROUP)
        half = win // 2
        inv_head = 1.0 / ((edge + half) - jnp.maximum(edge - half, 0)).astype(F32)
        inv_tail = 1.0 / (jnp.minimum(SUBLANES - edge, half) + half).astype(F32)
        if tiles_per_seq > 1:
            pad_ref[POOL_HALO:POOL_HALO + seq_len, :] = x_ref[...]
        for t in range(POOL_ROWS // POOL_TILE):
            rows = slice(t * POOL_TILE, (t + 1) * POOL_TILE)
            x = x_ref[rows, :]
            if tiles_per_seq > 1:
                sums = _dot(band_ref[gi], pad_ref[t * POOL_TILE:(t + 1) * POOL_TILE + 2 * POOL_HALO, :])
            else:
                sums = _dot(band_ref[gi, :, POOL_HALO:POOL_HALO + POOL_TILE], x)
            head = sums[:SUBLANES] * (inv_head if t % tiles_per_seq == 0 else 1.0 / win)
            tail = sums[-SUBLANES:] * (inv_tail if t % tiles_per_seq == tiles_per_seq - 1 else 1.0 / win)
            mean = jnp.concatenate([head, sums[SUBLANES:-SUBLANES] * (1.0 / win), tail], axis=0)
            pooled = (mean - x.astype(F32)).astype(BF16)
            y_ref[rows, cs] = (_dot(pooled, w_ref[gi]) * s_ref[:, cs]).astype(y_ref.dtype)


def _pool(proj, w, s, seq_len):
    m = proj.shape[0]
    rows = POOL_ROWS
    assert seq_len in (POOL_TILE, rows) and max(POOL_WINDOWS) // 2 <= SUBLANES
    group = lambda gi: pl.BlockSpec((rows, POOL_GROUP), lambda r: (r, COL_XC // POOL_GROUP + gi))
    whole = lambda shape: pl.BlockSpec(shape, lambda r: (0,) * len(shape))
    nwin = len(POOL_WINDOWS)
    return pl.pallas_call(
        functools.partial(_pool_kernel, seq_len=seq_len),
        grid=(m // rows,),
        in_specs=[
            group(0), group(1), group(2), group(3),
            whole((nwin, POOL_TILE, POOL_TILE + 2 * POOL_HALO)),
            whole((nwin, POOL_GROUP, POOL_GROUP)),
            whole((1, D_POOL)),
        ],
        out_specs=pl.BlockSpec((rows, D_POOL), lambda r: (r, 0)),
        out_shape=jax.ShapeDtypeStruct((m, D_POOL), BF16),
        scratch_shapes=[pltpu.VMEM((rows + 2 * POOL_HALO, POOL_GROUP), BF16)],
        compiler_params=_params("parallel"),
        name="pool_mix",
    )(proj, proj, proj, proj, _pool_bands(), w, s)


def _merge_kernel(ya_ref, yb_ref, yc_ref, g0_ref, g1_ref, g2_ref, g3_ref, g4_ref, g5_ref, x_ref, mod_ref,
                  bg_ref, wb_ref, wo_ref, n2_ref, x1_ref, h2_ref):
    mod = mod_ref[0]
    g_refs = (g0_ref, g1_ref, g2_ref, g3_ref, g4_ref, g5_ref)
    half = D_MODEL // 2
    merged = None
    for k, y_ref in enumerate((ya_ref, yb_ref, yc_ref)):
        y = _dot(y_ref[...], wb_ref[k])
        parts = []
        for p in range(2):
            z = g_refs[2 * k + p][...].astype(F32) + bg_ref[:, k * D_MODEL + p * half:k * D_MODEL + (p + 1) * half]
            parts.append((1.0 + jnp.tanh(0.5 * z)) * y[:, p * half:(p + 1) * half])
        term = jnp.concatenate(parts, axis=1)
        merged = term if merged is None else merged + term
    merged = 0.5 * merged
    gate1 = mod[:, 2 * D_MODEL:3 * D_MODEL]
    x1 = x_ref[...] + gate1 * _dot(merged.astype(BF16), wo_ref[...])
    x1_ref[...] = x1
    shift2 = mod[:, 3 * D_MODEL:4 * D_MODEL]
    scale2 = mod[:, 4 * D_MODEL:5 * D_MODEL]
    h2_ref[...] = (_rms(x1) * n2_ref[...] * (1.0 + scale2) + shift2).astype(BF16)


def _merge(ya, yb, yc, proj, x, mod, rows_per_mod, bg, wb, wo, n2, layer):
    m = x.shape[0]
    tm = 512
    per = rows_per_mod // tm
    half = D_MODEL // 2
    row = pl.BlockSpec((tm, D_MODEL), lambda i: (i, 0))
    gate = lambda c: pl.BlockSpec((tm, half), lambda i: (i, COL_G // half + c))
    const = lambda shape: pl.BlockSpec(shape, lambda i: (0,) * len(shape), pipeline_mode=pl.Buffered(1))
    return pl.pallas_call(
        _merge_kernel,
        grid=(m // tm,),
        in_specs=[
            row, row, row,
            gate(0), gate(1), gate(2), gate(3), gate(4), gate(5),
            row,
            pl.BlockSpec((1, 1, 6 * D_MODEL), lambda i: (i // per, 0, 0)),
            const((1, N_BRANCH * D_MODEL)),
            _resident((N_BRANCH, D_MODEL, D_MODEL), layer),
            _resident((D_MODEL, D_MODEL), layer),
            const((1, D_MODEL)),
        ],
        out_specs=(row, row),
        out_shape=(jax.ShapeDtypeStruct((m, D_MODEL), F32), jax.ShapeDtypeStruct((m, D_MODEL), BF16)),
        compiler_params=_params("parallel"),
        name="merge_out",
    )(ya, yb, yc, proj, proj, proj, proj, proj, proj, x, mod, bg, wb, wo, n2)


FFN_CK = 256
FFN_TM = 512
FFN_GAP = SUBLANES
GELU_C = float(np.sqrt(2.0 / np.pi))


def _ffn_kernel(*refs, seq_len, final):
    refs = list(refs)
    h_ref, hp_ref, hn_ref, x_ref, mod_ref, wup_ref, cw_ref, cb_ref, wd_ref = refs[:9]
    fn_ref = refs[9] if final else None
    o_ref, hx_ref, u_ref, act_ref = refs[-4:]
    tm = FFN_TM
    halo = seq_len > tm
    i = pl.program_id(0)

    if halo:
        per_seq = seq_len // tm
        at_start = i % per_seq == 0
        at_end = i % per_seq == per_seq - 1
        zeros = jnp.zeros((BF16_ROWS, D_MODEL), BF16)

        @pl.when(at_start)
        def _():
            hx_ref[0:BF16_ROWS, :] = zeros

        @pl.when(jnp.logical_not(at_start))
        def _():
            hx_ref[0:BF16_ROWS, :] = hp_ref[...]

        @pl.when(at_end)
        def _():
            hx_ref[BF16_ROWS + tm:, :] = zeros

        @pl.when(jnp.logical_not(at_end))
        def _():
            hx_ref[BF16_ROWS + tm:, :] = hn_ref[...]

        hx_ref[BF16_ROWS:BF16_ROWS + tm, :] = h_ref[...]
        bases = (BF16_ROWS,)
        seg = tm
    else:
        nseg = tm // seq_len
        seg = seq_len
        bases = tuple(FFN_GAP + s * (seg + FFN_GAP) for s in range(nseg))
        for s in range(nseg + 1):
            u_ref[s * (seg + FFN_GAP):s * (seg + FFN_GAP) + FFN_GAP, :] = jnp.zeros((FFN_GAP, FFN_CK), F32)

    def taps(offset):
        return jnp.concatenate([u_ref[b + offset:b + offset + seg, :] for b in bases], axis=0)

    for c in range(D_FF // FFN_CK):
        cs = slice(c * FFN_CK, (c + 1) * FFN_CK)
        vs = slice(D_FF + c * FFN_CK, D_FF + (c + 1) * FFN_CK)
        if halo:
            u_ext = _dot(hx_ref[...], wup_ref[:, cs])
            u_ref[...] = u_ext
            u0 = u_ext[BF16_ROWS:BF16_ROWS + tm]
        else:
            u0 = _dot(h_ref[...], wup_ref[:, cs])
            for s, b in enumerate(bases):
                u_ref[b:b + seg, :] = u0[s * seg:(s + 1) * seg]
        uv = _dot(h_ref[...], wup_ref[:, vs])
        gff = taps(-1) * cw_ref[0:1, cs] + u0 * cw_ref[1:2, cs] + taps(1) * cw_ref[2:3, cs] + cb_ref[:, cs]
        inner = gff * (GELU_C + (GELU_C * 0.044715) * (gff * gff))
        act_ref[:, cs] = (0.5 * (gff * uv) * (1.0 + jnp.tanh(inner))).astype(BF16)

    gate2 = mod_ref[0][:, 5 * D_MODEL:6 * D_MODEL]
    out = x_ref[...] + gate2 * _dot(act_ref[...], wd_ref[...])
    if final:
        out = _rms(out) * fn_ref[...]
    o_ref[...] = out


def _ffn(h2, x1, mod, rows_per_mod, wup, cw, cb, wd, layer, seq_len, final_norm):
    m = x1.shape[0]
    tm = FFN_TM
    per = rows_per_mod // tm
    hb = tm // BF16_ROWS
    last_halo = m // BF16_ROWS - 1
    final = final_norm is not None
    halo = seq_len > tm
    const = lambda shape: pl.BlockSpec(shape, lambda i: (0,) * len(shape), pipeline_mode=pl.Buffered(1))
    in_specs = [
        pl.BlockSpec((tm, D_MODEL), lambda i: (i, 0)),
        pl.BlockSpec((BF16_ROWS, D_MODEL), lambda i: (jnp.maximum(i * hb - 1, 0), 0)),
        pl.BlockSpec((BF16_ROWS, D_MODEL), lambda i: (jnp.minimum((i + 1) * hb, last_halo), 0)),
        pl.BlockSpec((tm, D_MODEL), lambda i: (i, 0)),
        pl.BlockSpec((1, 1, 6 * D_MODEL), lambda i: (i // per, 0, 0)),
        _resident((D_MODEL, 2 * D_FF), layer),
        const((3, D_FF)),
        const((1, D_FF)),
        _resident((D_FF, D_MODEL), layer),
    ]
    args = [h2, h2, h2, x1, mod, wup, cw, cb, wd]
    if final:
        in_specs.append(const((1, D_MODEL)))
        args.append(final_norm)
    if halo:
        u_rows = tm + 2 * BF16_ROWS
    else:
        u_rows = FFN_GAP + (tm // seq_len) * (seq_len + FFN_GAP)
    return pl.pallas_call(
        functools.partial(_ffn_kernel, seq_len=seq_len, final=final),
        grid=(m // tm,),
        in_specs=in_specs,
        out_specs=pl.BlockSpec((tm, D_MODEL), lambda i: (i, 0)),
        out_shape=jax.ShapeDtypeStruct((m, D_MODEL), F32),
        scratch_shapes=[
            pltpu.VMEM((tm + 2 * BF16_ROWS, D_MODEL), BF16),
            pltpu.VMEM((u_rows, FFN_CK), F32),
            pltpu.VMEM((tm, D_FF), BF16),
        ],
        compiler_params=_params("parallel"),
        name="conv_glu_ffn",
    )(*args)


def _trunk_layer(x, mod, rows_per_mod, p, big, layer, seq_len, ctx, final_norm):
    proj, k_new, v_new = _inproj(x, mod, rows_per_mod, p["norm1"], big["w_in"], layer, seq_len,
                                 want_kv=ctx is None)
    if ctx is None:
        ya, h_fin = _lru(proj, p["lru"], seq_len, None)
        yb = _attn_ctx(proj, p["sink"], seq_len)
    else:
        ck, cv, h0 = ctx
        ya = _lru(proj, p["lru"], seq_len, h0)
        h_fin = None
        yb = _attn_lat(proj, ck, cv, p["sink"], seq_len)
    yc = _pool(proj, p["pool_w"], p["pool_scale"], seq_len)
    x1, h2 = _merge(ya, yb, yc, proj, x, mod, rows_per_mod, p["b_gate"], big["w_branch"], big["w_out"],
                    p["norm2"], layer)
    out = _ffn(h2, x1, mod, rows_per_mod, big["ffn_up"], p["ffn_conv"], p["ffn_conv_b"], big["ffn_down"],
               layer, seq_len, final_norm)
    return out, k_new, v_new, h_fin


def _layer_params(l, norm1, norm2, b_gate, lru_conv, lru_conv_b, lru_wa, lru_ba, lru_wx, lru_bx,
                  lru_lambda, attn_sink, pool_w, pool_scale, ffn_conv, ffn_conv_b):
    def per_head(v):
        return v.reshape(2, LRU_HEADS, LRU_BLOCK).transpose(1, 0, 2)

    wg = jnp.concatenate([lru_wa[l, 0], lru_wx[l, 0], lru_wa[l, 1], lru_wx[l, 1]], axis=-1).astype(BF16)
    ba = per_head(lru_ba[l])
    bx = per_head(lru_bx[l])
    bg = jnp.concatenate([ba[:, 0], bx[:, 0], ba[:, 1], bx[:, 1]], axis=-1)[:, None, :]
    lam = per_head(lru_lambda[l]).reshape(LRU_HEADS, 1, 2 * LRU_BLOCK)
    return {
        "norm1": norm1[l][None], "norm2": norm2[l][None], "b_gate": b_gate[l][None],
        "lru": {"conv_w": lru_conv[l], "conv_b": lru_conv_b[l][None], "wg": wg, "bg": bg, "lam": lam},
        "sink": attn_sink[l], "pool_w": pool_w[l].astype(BF16), "pool_scale": pool_scale[l][None],
        "ffn_conv": ffn_conv[l], "ffn_conv_b": ffn_conv_b[l][None],
    }


def kernel(x_prompt, x_sample, cache_k, cache_v, state_lru, c, c_ctx, w_ada, b_ada, norm1, norm2, w_in,
           b_gate, lru_conv, lru_conv_b, lru_wa, lru_ba, lru_wx, lru_bx, lru_lambda, attn_sink, pool_w,
           pool_scale, w_branch, w_out, ffn_up, ffn_conv, ffn_conv_b, ffn_down, final_norm):
    batch, seq, _ = x_prompt.shape
    dec_batch, dec_seq, _ = x_sample.shape
    past = cache_k.shape[2]
    assert seq == SCAN_CHUNK and dec_seq % SCAN_ROWS == 0 and (batch * seq) % SCAN_ROWS == 0

    c_rows = jnp.concatenate(
        [c_ctx[None], c, jnp.zeros((SUBLANES - 1 - dec_batch, D_MODEL), F32)], axis=0)
    mods = _ada(c_rows, w_ada, b_ada)

    xp = x_prompt.reshape(batch * seq, D_MODEL)
    xs = x_sample.reshape(dec_batch * dec_seq, D_MODEL)
    fn = final_norm[None]
    big = {"w_in": w_in.astype(BF16), "w_branch": w_branch.astype(BF16), "w_out": w_out.astype(BF16),
           "ffn_up": ffn_up.astype(BF16), "ffn_down": ffn_down.astype(BF16)}
    ks, vs, hs = [], [], []
    for l in range(DEPTH):
        p = _layer_params(l, norm1, norm2, b_gate, lru_conv, lru_conv_b, lru_wa, lru_ba, lru_wx, lru_bx,
                          lru_lambda, attn_sink, pool_w, pool_scale, ffn_conv, ffn_conv_b)
        last = fn if l == DEPTH - 1 else None
        mod_ctx = mods[l, 0:1][:, None, :]
        mod_lat = mods[l, 1:1 + dec_batch][:, None, :]
        xp, k_new, v_new, h_fin = _trunk_layer(xp, mod_ctx, batch * seq, p, big, l, seq, None, last)
        ck = cache_k[:, l].reshape(dec_batch, past, D_KV).astype(BF16)
        cv = cache_v[:, l].reshape(dec_batch, past, D_KV).astype(BF16)
        h0 = state_lru[:, l].transpose(1, 0, 2)[:, :, None, :]
        xs, _, _, _ = _trunk_layer(xs, mod_lat, dec_seq, p, big, l, dec_seq, (ck, cv, h0), last)
        ks.append(k_new.reshape(batch, seq, N_KV_HEADS, HEAD_DIM))
        vs.append(v_new.reshape(batch, seq, N_KV_HEADS, HEAD_DIM))
        hs.append(h_fin.transpose(1, 0, 2))
    y_prompt = xp.reshape(batch, seq, D_MODEL)
    y_sample = xs.reshape(dec_batch, dec_seq, D_MODEL)
    return (y_prompt, y_sample, jnp.stack(ks, axis=1), jnp.stack(vs, axis=1), jnp.stack(hs, axis=1))
```

```python
import functools

import numpy as np
import jax
import jax.numpy as jnp
from jax import lax
from jax.experimental import pallas as pl
from jax.experimental.pallas import tpu as pltpu

F32 = jnp.float32
BF16 = jnp.bfloat16

D_MODEL = 1024
DEPTH = 2
GRID_W = 64
EPS = 1e-6
N_BRANCH = 3
D_RNN = 1024
LRU_HEADS = 8
LRU_BLOCK = D_RNN // LRU_HEADS
LRU_C = 8.0
N_HEADS = 8
N_KV_HEADS = 2
KV_GROUPS = N_HEADS // N_KV_HEADS
HEAD_DIM = 128
D_KV = N_KV_HEADS * HEAD_DIM
WINDOW = 128
BLOCK_Q = 128
ROPE_BASE = 10000.0
NEG_INF = -1e30
D_POOL = 1024
POOL_WINDOWS = (2, 4, 8, 16)
POOL_GROUP = D_POOL // len(POOL_WINDOWS)
D_FF = 2816
D_IN = D_RNN + N_HEADS * HEAD_DIM + 2 * D_KV + D_POOL + N_BRANCH * D_MODEL

COL_XA = 0
COL_Q = COL_XA + D_RNN
COL_K = COL_Q + N_HEADS * HEAD_DIM
COL_V = COL_K + D_KV
COL_XC = COL_V + D_KV
COL_G = COL_XC + D_POOL
LOG2E = float(np.log2(np.e))

VMEM_LIMIT_BYTES = 52 * 1024 * 1024
SUBLANES = 8
LANES = 128
BF16_ROWS = 16

SCAN_CHUNK = 256
SCAN_PITCH = 260
SCAN_ROWS = 2048
POOL_PAD = 8


def _params(*sem):
    return pltpu.CompilerParams(dimension_semantics=sem, vmem_limit_bytes=VMEM_LIMIT_BYTES)


def _dot(a, b):
    return jnp.dot(a, b, preferred_element_type=F32)


def _dot_nt(a, b):
    return lax.dot_general(a, b, (((1,), (1,)), ((), ())), preferred_element_type=F32)


def _sigmoid(z):
    return 0.5 * (1.0 + jnp.tanh(0.5 * z))


def _rms(x):
    return x * lax.rsqrt(jnp.mean(x * x, axis=-1, keepdims=True) + EPS)


def _ada_kernel(c_ref, w_ref, b_ref, o_ref):
    c = c_ref[...]
    s = c * _sigmoid(c)
    o_ref[0] = _dot(s.astype(BF16), w_ref[0].astype(BF16)) + b_ref[0]


def _ada(c_rows, w_ada, b_ada):
    tn = 1536
    return pl.pallas_call(
        _ada_kernel,
        grid=(DEPTH, 6 * D_MODEL // tn),
        in_specs=[
            pl.BlockSpec((SUBLANES, D_MODEL), lambda l, j: (0, 0)),
            pl.BlockSpec((1, D_MODEL, tn), lambda l, j: (l, 0, j)),
            pl.BlockSpec((1, 1, tn), lambda l, j: (l, 0, j)),
        ],
        out_specs=pl.BlockSpec((1, SUBLANES, tn), lambda l, j: (l, 0, j)),
        out_shape=jax.ShapeDtypeStruct((DEPTH, SUBLANES, 6 * D_MODEL), F32),
        compiler_params=_params("parallel", "parallel"),
        name="ada_mod",
    )(c_rows, w_ada, b_ada.reshape(DEPTH, 1, 6 * D_MODEL))


INPROJ_TM = 512
INPROJ_CK = 2 * D_KV


def _resident(shape, layer):
    ndim = len(shape)
    return pl.BlockSpec((None,) + tuple(shape), lambda i: (layer,) + (0,) * ndim,
                        pipeline_mode=pl.Buffered(1))


def _inproj_kernel(*refs, seq_len, want_kv):
    if want_kv:
        x_ref, mod_ref, g_ref, w_ref, o_ref, k_ref, v_ref = refs
    else:
        x_ref, mod_ref, g_ref, w_ref, cos_ref, sa_ref, sb_ref, o_ref = refs
    mod = mod_ref[0]
    shift = mod[:, 0:D_MODEL]
    scale = mod[:, D_MODEL:2 * D_MODEL]
    h = (_rms(x_ref[...]) * g_ref[...] * (1.0 + scale) + shift).astype(BF16)
    nf = HEAD_DIM // 4

    def rope(x):
        return (x * cos_ref[...] + pltpu.roll(x, HEAD_DIM - nf, 1) * sa_ref[...]
                + pltpu.roll(x, nf, 1) * sb_ref[...])

    for c in range(D_IN // INPROJ_CK):
        acc = _dot(h, w_ref[:, c * INPROJ_CK:(c + 1) * INPROJ_CK])
        if not want_kv:
            lo = c * INPROJ_CK
            heads = [acc[:, j * HEAD_DIM:(j + 1) * HEAD_DIM] for j in range(INPROJ_CK // HEAD_DIM)]
            heads = [rope(hd) if COL_Q <= lo + j * HEAD_DIM < COL_V else hd for j, hd in enumerate(heads)]
            acc = jnp.concatenate(heads, axis=1)
        o_ref[:, c * INPROJ_CK:(c + 1) * INPROJ_CK] = acc.astype(o_ref.dtype)
        if want_kv and c == COL_K // INPROJ_CK:
            for b in range(INPROJ_TM // seq_len):
                rows = slice(b * seq_len, (b + 1) * seq_len)
                for hd in range(N_KV_HEADS):
                    dst = pl.ds(hd, seq_len, stride=N_KV_HEADS)
                    k_ref[b, dst, :] = acc[rows, hd * HEAD_DIM:(hd + 1) * HEAD_DIM]
                    v_ref[b, dst, :] = acc[rows, D_KV + hd * HEAD_DIM:D_KV + (hd + 1) * HEAD_DIM]


def _inproj(x, mod, rows_per_mod, g, w, layer, seq_len, want_kv):
    m = x.shape[0]
    tm = INPROJ_TM
    per = rows_per_mod // tm
    proj_spec = pl.BlockSpec((tm, D_IN), lambda i: (i, 0))
    proj_shape = jax.ShapeDtypeStruct((m, D_IN), BF16)
    if want_kv:
        nb = tm // seq_len
        cache_spec = pl.BlockSpec((nb, seq_len * N_KV_HEADS, HEAD_DIM), lambda i: (i, 0, 0))
        cache_shape = jax.ShapeDtypeStruct((m // seq_len, seq_len * N_KV_HEADS, HEAD_DIM), F32)
        out_specs = (proj_spec, cache_spec, cache_spec)
        out_shape = (proj_shape, cache_shape, cache_shape)
        extra_specs, extra_args = [], []
    else:
        out_specs, out_shape = proj_spec, proj_shape
        tab = pl.BlockSpec((tm, HEAD_DIM), lambda i: (i % (seq_len // tm), 0))
        extra_specs, extra_args = [tab, tab, tab], list(_rope_tables(seq_len))
    out = pl.pallas_call(
        functools.partial(_inproj_kernel, seq_len=seq_len, want_kv=want_kv),
        grid=(m // tm,),
        in_specs=[
            pl.BlockSpec((tm, D_MODEL), lambda i: (i, 0)),
            pl.BlockSpec((1, 1, 6 * D_MODEL), lambda i: (i // per, 0, 0)),
            pl.BlockSpec((1, D_MODEL), lambda i: (0, 0)),
            _resident((D_MODEL, D_IN), layer),
        ] + extra_specs,
        out_specs=out_specs,
        out_shape=out_shape,
        compiler_params=_params("parallel"),
        name="in_proj",
    )(x, mod, g, w, *extra_args)
    return out if want_kv else (out, None, None)


LRU_HEADS_PER_STEP = 2


def _shift_rows(x, k, seq_len):
    rows, lanes = x.shape
    rolled = pltpu.roll(x, k % rows, 0)
    edge = lax.broadcasted_iota(jnp.int32, (SUBLANES, lanes), 0)
    pieces = []
    for base in range(0, rows, seq_len):
        if k > 0:
            head = jnp.where(edge >= k, rolled[base:base + SUBLANES], 0.0)
            pieces += [head, rolled[base + SUBLANES:base + seq_len]]
        else:
            end = base + seq_len
            tail = jnp.where(edge < SUBLANES + k, rolled[end - SUBLANES:end], 0.0)
            pieces += [rolled[base:end - SUBLANES], tail]
    return jnp.concatenate(pieces, axis=0)


def _lru_kernel(*refs, seq_len, latent):
    if latent:
        (x_ref, cw_ref, cb_ref, wg_ref, bg_ref, lam_ref, h0_ref, y_ref,
         af_ref, bf_ref, ab_ref, bb_ref, hf_ref, hb_ref) = refs
    else:
        (x_ref, cw_ref, cb_ref, wg_ref, bg_ref, lam_ref, y_ref, fin_ref,
         af_ref, bf_ref, ab_ref, bb_ref, hf_ref, hb_ref) = refs
    rows = SCAN_ROWS
    nchunk = rows // SCAN_CHUNK
    nhead = LRU_HEADS_PER_STEP
    for hd in range(nhead):
        lanes = slice(hd * LRU_BLOCK, (hd + 1) * LRU_BLOCK)
        x = x_ref[:, lanes].astype(F32)
        xc = (_shift_rows(x, 2, seq_len) * cw_ref[0:1, lanes] + _shift_rows(x, 1, seq_len) * cw_ref[1:2, lanes]
              + x * cw_ref[2:3, lanes] + _shift_rows(x, -1, seq_len) * cw_ref[3:4, lanes] + cb_ref[:, lanes])
        th = jnp.tanh(_dot(xc.astype(BF16), wg_ref[hd] * 0.5) + 0.5 * bg_ref[hd])
        xh = 0.5 * xc
        lam = lam_ref[hd]
        for d, (a_ref, b_ref) in enumerate(((af_ref, bf_ref), (ab_ref, bb_ref))):
            th_r = th[:, 2 * d * LRU_BLOCK:(2 * d + 1) * LRU_BLOCK]
            th_i = th[:, (2 * d + 1) * LRU_BLOCK:(2 * d + 2) * LRU_BLOCK]
            nl = -lam[:, d * LRU_BLOCK:(d + 1) * LRU_BLOCK]
            softplus = jnp.maximum(nl, 0.0) + jnp.log(1.0 + jnp.exp(-jnp.abs(nl)))
            ch = (-0.5 * LRU_C * LOG2E) * softplus
            a = jnp.exp2(ch + ch * th_r)
            y = 1.0 - a * a
            b = (y * lax.rsqrt(jnp.maximum(y, 1e-30))) * ((1.0 + th_i) * xh)
            for c in range(nchunk):
                a_ref[hd, pl.ds(c * SCAN_PITCH, SCAN_CHUNK), :] = a[c * SCAN_CHUNK:(c + 1) * SCAN_CHUNK]
                b_ref[hd, pl.ds(c * SCAN_PITCH, SCAN_CHUNK), :] = b[c * SCAN_CHUNK:(c + 1) * SCAN_CHUNK]

    def step(t, carry):
        rf = pl.ds(t, nchunk, stride=SCAN_PITCH)
        rb = pl.ds(SCAN_CHUNK - 1 - t, nchunk, stride=SCAN_PITCH)
        out = []
        for hd in range(nhead):
            hf, hb, pf, pb = carry[4 * hd:4 * hd + 4]
            a_f = af_ref[hd, rf, :]
            a_b = ab_ref[hd, rb, :]
            hf = a_f * hf + bf_ref[hd, rf, :]
            hb = a_b * hb + bb_ref[hd, rb, :]
            hf_ref[hd, rf, :] = hf
            hb_ref[hd, rb, :] = hb
            if latent:
                pf = pf * a_f
                pb = pb * a_b
                af_ref[hd, rf, :] = pf
                ab_ref[hd, rb, :] = pb
            out += [hf, hb, pf, pb]
        return tuple(out)

    zero = jnp.zeros((nchunk, LRU_BLOCK), F32)
    one = jnp.ones((nchunk, LRU_BLOCK), F32)
    lax.fori_loop(0, SCAN_CHUNK, step, (zero, zero, one, one) * nhead, unroll=8)

    for hd in range(nhead):
        lanes = slice(hd * LRU_BLOCK, (hd + 1) * LRU_BLOCK)

        def chunk(ref, c):
            return ref[hd, pl.ds(c * SCAN_PITCH, SCAN_CHUNK), :]

        if latent:
            carry = h0_ref[0, 0][:, lanes]
            fwd = []
            for c in range(nchunk):
                h = chunk(hf_ref, c) + chunk(af_ref, c) * carry
                carry = h[SCAN_CHUNK - 1:SCAN_CHUNK]
                fwd.append(h)
            carry = h0_ref[1, 0][:, lanes]
            for c in reversed(range(nchunk)):
                h = chunk(hb_ref, c) + chunk(ab_ref, c) * carry
                carry = h[0:1]
                y_ref[pl.ds(c * SCAN_CHUNK, SCAN_CHUNK), lanes] = (fwd[c] + h).astype(y_ref.dtype)
        else:
            for c in range(nchunk):
                y_ref[pl.ds(c * SCAN_CHUNK, SCAN_CHUNK), lanes] = (
                    chunk(hf_ref, c) + chunk(hb_ref, c)).astype(y_ref.dtype)
            fin_ref[0, :, lanes] = hf_ref[hd, pl.ds(SCAN_CHUNK - 1, nchunk, stride=SCAN_PITCH), :]
            fin_ref[1, :, lanes] = hb_ref[hd, pl.ds(0, nchunk, stride=SCAN_PITCH), :]


def _lru(proj, lp, seq_len, h0):
    m = proj.shape[0]
    latent = h0 is not None
    rows = SCAN_ROWS
    nhead = LRU_HEADS_PER_STEP
    width = nhead * LRU_BLOCK
    in_specs = [
        pl.BlockSpec((rows, width), lambda r, h: (r, COL_XA // width + h)),
        pl.BlockSpec((4, width), lambda r, h: (0, h)),
        pl.BlockSpec((1, width), lambda r, h: (0, h)),
        pl.BlockSpec((nhead, LRU_BLOCK, 4 * LRU_BLOCK), lambda r, h: (h, 0, 0)),
        pl.BlockSpec((nhead, 1, 4 * LRU_BLOCK), lambda r, h: (h, 0, 0)),
        pl.BlockSpec((nhead, 1, 2 * LRU_BLOCK), lambda r, h: (h, 0, 0)),
    ]
    args = [proj, lp["conv_w"], lp["conv_b"], lp["wg"], lp["bg"], lp["lam"]]
    y_spec = pl.BlockSpec((rows, width), lambda r, h: (r, h))
    y_shape = jax.ShapeDtypeStruct((m, D_RNN), BF16)
    if latent:
        in_specs.append(pl.BlockSpec((2, 1, 1, width), lambda r, h: (0, r, 0, h)))
        args.append(h0)
        out_specs, out_shape = y_spec, y_shape
    else:
        nseq = m // seq_len
        out_specs = (y_spec, pl.BlockSpec((2, rows // seq_len, width), lambda r, h: (0, r, h)))
        out_shape = (y_shape, jax.ShapeDtypeStruct((2, nseq, D_RNN), F32))
    scan_buf = pltpu.VMEM((nhead, rows // SCAN_CHUNK * SCAN_PITCH, LRU_BLOCK), F32)
    return pl.pallas_call(
        functools.partial(_lru_kernel, seq_len=seq_len, latent=latent),
        grid=(m // rows, LRU_HEADS // nhead),
        in_specs=in_specs,
        out_specs=out_specs,
        out_shape=out_shape,
        scratch_shapes=[scan_buf] * 6,
        compiler_params=_params("parallel", "parallel"),
        name="rglru_latent" if latent else "rglru_context",
    )(*args)


SOFTMAX_SCALE = HEAD_DIM ** -0.5 * LOG2E


def _stack_heads(q_ref, kh, r0, rows):
    parts = [q_ref[r0:r0 + rows, (kh * KV_GROUPS + g) * HEAD_DIM:(kh * KV_GROUPS + g + 1) * HEAD_DIM]
             for g in range(KV_GROUPS)]
    return jnp.concatenate(parts, axis=0)


def _dot_tn(a, b):
    return lax.dot_general(a, b, (((0,), (0,)), ((), ())), preferred_element_type=F32)


def _sink_row(sink_ref, kh, cols):
    parts = [jnp.full((1, cols), sink_ref[kh * KV_GROUPS + g] * LOG2E, F32) for g in range(KV_GROUPS)]
    return jnp.concatenate(parts, axis=1)


def _softmax_pv_t(t, sink, v):
    m = jnp.maximum(jnp.max(t, axis=0, keepdims=True), sink)
    e = jnp.exp2(t - m)
    denom = jnp.sum(e, axis=0, keepdims=True) + jnp.exp2(sink - m)
    return _dot_tn(v, e.astype(BF16)) * (1.0 / denom)


def _store_heads_t(y_ref, o_t, kh, r0, rows):
    for g in range(KV_GROUPS):
        h = kh * KV_GROUPS + g
        y_ref[r0:r0 + rows, h * HEAD_DIM:(h + 1) * HEAD_DIM] = (
            o_t[:, g * rows:(g + 1) * rows].T.astype(y_ref.dtype))


def _attn_ctx_kernel(sink_ref, q_ref, kv_ref, y_ref, *, seq_len):
    for b in range(q_ref.shape[0] // seq_len):
        r0 = b * seq_len
        for kh in range(N_KV_HEADS):
            sl = slice(kh * HEAD_DIM, (kh + 1) * HEAD_DIM)
            vl = slice(D_KV + kh * HEAD_DIM, D_KV + (kh + 1) * HEAD_DIM)
            q = _stack_heads(q_ref, kh, r0, seq_len)
            t = _dot_nt(kv_ref[r0:r0 + seq_len, sl], q) * SOFTMAX_SCALE
            o_t = _softmax_pv_t(t, _sink_row(sink_ref, kh, seq_len), kv_ref[r0:r0 + seq_len, vl])
            _store_heads_t(y_ref, o_t, kh, r0, seq_len)


ATTN_CTX_SEQS = 4


def _attn_ctx(proj, sink, seq_len):
    m = proj.shape[0]
    rows = ATTN_CTX_SEQS * seq_len
    return pl.pallas_call(
        functools.partial(_attn_ctx_kernel, seq_len=seq_len),
        grid=(m // rows,),
        in_specs=[
            pl.BlockSpec(memory_space=pltpu.SMEM),
            pl.BlockSpec((rows, N_HEADS * HEAD_DIM), lambda b: (b, COL_Q // (N_HEADS * HEAD_DIM))),
            pl.BlockSpec((rows, 2 * D_KV), lambda b: (b, COL_K // (2 * D_KV))),
        ],
        out_specs=pl.BlockSpec((rows, N_HEADS * HEAD_DIM), lambda b: (b, 0)),
        out_shape=jax.ShapeDtypeStruct((m, N_HEADS * HEAD_DIM), BF16),
        compiler_params=_params("parallel"),
        name="attn_context",
    )(sink, proj, proj)


def _rope_tables(seq_len):
    nf = HEAD_DIM // 4
    freqs = ROPE_BASE ** (-np.arange(nf, dtype=np.float64) / nf)
    t = np.arange(seq_len)
    ang_row = (t // GRID_W)[:, None] * freqs[None, :]
    ang_col = (t % GRID_W)[:, None] * freqs[None, :]
    ang = np.concatenate([ang_row, ang_row, ang_col, ang_col], axis=1)
    first = (np.arange(HEAD_DIM) % (2 * nf)) < nf
    cos = np.cos(ang)
    sin = np.sin(ang)
    sin_a = np.where(first[None, :], -sin, 0.0)
    sin_b = np.where(first[None, :], 0.0, sin)
    return tuple(jnp.asarray(a, F32) for a in (cos, sin_a, sin_b))


ATTN_LAT_BLOCKS = 4


def _attn_lat_kernel(sink_ref, q_ref, kvp_ref, kvc_ref, kvn_ref, ck_ref, cv_ref, y_ref, *, nblk):
    step = pl.program_id(1)
    cols = KV_GROUPS * BLOCK_Q
    span = 3 * BLOCK_Q
    key = lax.broadcasted_iota(jnp.int32, (span, cols), 0)
    qry = lax.broadcasted_iota(jnp.int32, (span, cols), 1) & (BLOCK_Q - 1)
    kv = jnp.concatenate([kvp_ref[...], kvc_ref[...], kvn_ref[...]], axis=0)
    for i in range(ATTN_LAT_BLOCKS):
        j = step * ATTN_LAT_BLOCKS + i
        lo = jnp.where(j > 0, qry, BLOCK_Q)
        hi = jnp.where(j < nblk - 1, qry + 2 * BLOCK_Q, 2 * BLOCK_Q - 1)
        bias = jnp.where(jnp.logical_and(key >= lo, key <= hi), 0.0, NEG_INF)
        win = kv[i * BLOCK_Q:i * BLOCK_Q + span]
        for kh in range(N_KV_HEADS):
            sl = slice(kh * HEAD_DIM, (kh + 1) * HEAD_DIM)
            vl = slice(D_KV + kh * HEAD_DIM, D_KV + (kh + 1) * HEAD_DIM)
            q = _stack_heads(q_ref, kh, i * BLOCK_Q, BLOCK_Q)
            keys = jnp.concatenate([win[:, sl], ck_ref[0, :, sl]], axis=0)
            vals = jnp.concatenate([win[:, vl], cv_ref[0, :, sl]], axis=0)
            t = _dot_nt(keys, q) * SOFTMAX_SCALE
            t = jnp.concatenate([t[:span] + bias, t[span:]], axis=0)
            o_t = _softmax_pv_t(t, _sink_row(sink_ref, kh, BLOCK_Q), vals)
            _store_heads_t(y_ref, o_t, kh, i * BLOCK_Q, BLOCK_Q)


def _attn_lat(proj, ck, cv, sink, seq_len):
    m = proj.shape[0]
    nblk = seq_len // BLOCK_Q
    nstep = nblk // ATTN_LAT_BLOCKS
    rows = ATTN_LAT_BLOCKS * BLOCK_Q
    nb = m // seq_len
    past = ck.shape[1]
    kv_col = COL_K // (2 * D_KV)
    assert COL_V == COL_K + D_KV and COL_K % (2 * D_KV) == 0

    def halo(shift):
        def index(b, s):
            return (b * nblk + jnp.clip(s * ATTN_LAT_BLOCKS + shift, 0, nblk - 1), kv_col)
        return index

    return pl.pallas_call(
        functools.partial(_attn_lat_kernel, nblk=nblk),
        grid=(nb, nstep),
        in_specs=[
            pl.BlockSpec(memory_space=pltpu.SMEM),
            pl.BlockSpec((rows, N_HEADS * HEAD_DIM), lambda b, s: (b * nstep + s, COL_Q // (N_HEADS * HEAD_DIM))),
            pl.BlockSpec((BLOCK_Q, 2 * D_KV), halo(-1)),
            pl.BlockSpec((rows, 2 * D_KV), lambda b, s: (b * nstep + s, kv_col)),
            pl.BlockSpec((BLOCK_Q, 2 * D_KV), halo(ATTN_LAT_BLOCKS)),
            pl.BlockSpec((1, past, D_KV), lambda b, s: (b, 0, 0)),
            pl.BlockSpec((1, past, D_KV), lambda b, s: (b, 0, 0)),
        ],
        out_specs=pl.BlockSpec((rows, N_HEADS * HEAD_DIM), lambda b, s: (b * nstep + s, 0)),
        out_shape=jax.ShapeDtypeStruct((m, N_HEADS * HEAD_DIM), BF16),
        compiler_params=_params("parallel", "parallel"),
        name="attn_latent",
    )(sink, proj, proj, proj, proj, ck, cv)


POOL_TILE = 256
POOL_LEAD = BF16_ROWS
POOL_ROWS = 2048


def _pool_plan(seq_len):
    lead = 0 if seq_len == POOL_TILE else POOL_LEAD
    return lead, POOL_TILE - 2 * lead


def _pool_bands(seq_len):
    lead, nout = _pool_plan(seq_len)
    r = np.arange(nout)[:, None]
    c = np.arange(POOL_TILE)[None, :] - lead
    bands = [(c >= r - win // 2) & (c < r + win // 2) for win in POOL_WINDOWS]
    return jnp.asarray(np.stack(bands), BF16)


def _pool_kernel(x0_ref, x1_ref, x2_ref, x3_ref, band_ref, w_ref, s_ref, y_ref, pad_ref, *, seq_len):
    lead, nout = _pool_plan(seq_len)
    edge = lax.broadcasted_iota(jnp.int32, (SUBLANES, POOL_GROUP), 0)
    if lead:
        pad_ref[0:lead, :] = jnp.zeros((lead, POOL_GROUP), BF16)
        pad_ref[lead + seq_len:, :] = jnp.zeros((pad_ref.shape[0] - lead - seq_len, POOL_GROUP), BF16)
    for gi, (win, x_ref) in enumerate(zip(POOL_WINDOWS, (x0_ref, x1_ref, x2_ref, x3_ref))):
        cs = slice(gi * POOL_GROUP, (gi + 1) * POOL_GROUP)
        half = win // 2
        inv_head = 1.0 / ((edge + half) - jnp.maximum(edge - half, 0)).astype(F32)
        inv_tail = 1.0 / (jnp.minimum(SUBLANES - edge, half) + half).astype(F32)
        if lead:
            pad_ref[lead:lead + seq_len, :] = x_ref[...]
        for base in range(0, POOL_ROWS, seq_len):
            for p0 in range(0, seq_len, nout):
                n = min(nout, seq_len - p0)
                x = x_ref[base + p0:base + p0 + n, :]
                src = pad_ref[p0:p0 + POOL_TILE, :] if lead else x
                sums = _dot(band_ref[gi, :n, :], src)
                head = sums[:SUBLANES] * (inv_head if p0 == 0 else 1.0 / win)
                tail = sums[n - SUBLANES:] * (inv_tail if p0 + n == seq_len else 1.0 / win)
                mean = jnp.concatenate([head, sums[SUBLANES:n - SUBLANES] * (1.0 / win), tail], axis=0)
                pooled = (mean - x.astype(F32)).astype(BF16)
                y_ref[base + p0:base + p0 + n, cs] = (
                    _dot(pooled, w_ref[gi]) * s_ref[:, cs]).astype(y_ref.dtype)


def _pool(proj, w, s, seq_len):
    m = proj.shape[0]
    rows = POOL_ROWS
    lead, nout = _pool_plan(seq_len)
    assert seq_len in (POOL_TILE, rows) and max(POOL_WINDOWS) // 2 <= min(SUBLANES, lead or SUBLANES)
    pad_rows = (pl.cdiv(seq_len, nout) - 1) * nout + POOL_TILE
    group = lambda gi: pl.BlockSpec((rows, POOL_GROUP), lambda r: (r, COL_XC // POOL_GROUP + gi))
    whole = lambda shape: pl.BlockSpec(shape, lambda r: (0,) * len(shape))
    nwin = len(POOL_WINDOWS)
    return pl.pallas_call(
        functools.partial(_pool_kernel, seq_len=seq_len),
        grid=(m // rows,),
        in_specs=[
            group(0), group(1), group(2), group(3),
            whole((nwin, nout, POOL_TILE)),
            whole((nwin, POOL_GROUP, POOL_GROUP)),
            whole((1, D_POOL)),
        ],
        out_specs=pl.BlockSpec((rows, D_POOL), lambda r: (r, 0)),
        out_shape=jax.ShapeDtypeStruct((m, D_POOL), BF16),
        scratch_shapes=[pltpu.VMEM((pad_rows, POOL_GROUP), BF16)],
        compiler_params=_params("parallel"),
        name="pool_mix",
    )(proj, proj, proj, proj, _pool_bands(seq_len), w, s)


def _merge_kernel(ya_ref, yb_ref, yc_ref, g0_ref, g1_ref, g2_ref, g3_ref, g4_ref, g5_ref, x_ref, mod_ref,
                  bg_ref, wb_ref, wo_ref, n2_ref, x1_ref, h2_ref):
    mod = mod_ref[0]
    g_refs = (g0_ref, g1_ref, g2_ref, g3_ref, g4_ref, g5_ref)
    half = D_MODEL // 2
    merged = None
    for k, y_ref in enumerate((ya_ref, yb_ref, yc_ref)):
        y = _dot(y_ref[...], wb_ref[k])
        parts = []
        for p in range(2):
            z = g_refs[2 * k + p][...].astype(F32) + bg_ref[:, k * D_MODEL + p * half:k * D_MODEL + (p + 1) * half]
            parts.append((1.0 + jnp.tanh(0.5 * z)) * y[:, p * half:(p + 1) * half])
        term = jnp.concatenate(parts, axis=1)
        merged = term if merged is None else merged + term
    merged = 0.5 * merged
    gate1 = mod[:, 2 * D_MODEL:3 * D_MODEL]
    x1 = x_ref[...] + gate1 * _dot(merged.astype(BF16), wo_ref[...])
    x1_ref[...] = x1
    shift2 = mod[:, 3 * D_MODEL:4 * D_MODEL]
    scale2 = mod[:, 4 * D_MODEL:5 * D_MODEL]
    h2_ref[...] = (_rms(x1) * n2_ref[...] * (1.0 + scale2) + shift2).astype(BF16)


def _merge(ya, yb, yc, proj, x, mod, rows_per_mod, bg, wb, wo, n2, layer):
    m = x.shape[0]
    tm = 512
    per = rows_per_mod // tm
    half = D_MODEL // 2
    row = pl.BlockSpec((tm, D_MODEL), lambda i: (i, 0))
    gate = lambda c: pl.BlockSpec((tm, half), lambda i: (i, COL_G // half + c))
    const = lambda shape: pl.BlockSpec(shape, lambda i: (0,) * len(shape), pipeline_mode=pl.Buffered(1))
    return pl.pallas_call(
        _merge_kernel,
        grid=(m // tm,),
        in_specs=[
            row, row, row,
            gate(0), gate(1), gate(2), gate(3), gate(4), gate(5),
            row,
            pl.BlockSpec((1, 1, 6 * D_MODEL), lambda i: (i // per, 0, 0)),
            const((1, N_BRANCH * D_MODEL)),
            _resident((N_BRANCH, D_MODEL, D_MODEL), layer),
            _resident((D_MODEL, D_MODEL), layer),
            const((1, D_MODEL)),
        ],
        out_specs=(row, row),
        out_shape=(jax.ShapeDtypeStruct((m, D_MODEL), F32), jax.ShapeDtypeStruct((m, D_MODEL), BF16)),
        compiler_params=_params("parallel"),
        name="merge_out",
    )(ya, yb, yc, proj, proj, proj, proj, proj, proj, x, mod, bg, wb, wo, n2)


FFN_CK = 256
FFN_TM = 512
FFN_GAP = SUBLANES
GELU_C = float(np.sqrt(2.0 / np.pi))


def _ffn_kernel(*refs, seq_len, final):
    refs = list(refs)
    h_ref, hp_ref, hn_ref, x_ref, mod_ref, wup_ref, cw_ref, cb_ref, wd_ref = refs[:9]
    fn_ref = refs[9] if final else None
    o_ref, hx_ref, u_ref, act_ref = refs[-4:]
    tm = FFN_TM
    halo = seq_len > tm
    i = pl.program_id(0)

    if halo:
        per_seq = seq_len // tm
        at_start = i % per_seq == 0
        at_end = i % per_seq == per_seq - 1
        zeros = jnp.zeros((BF16_ROWS, D_MODEL), BF16)

        @pl.when(at_start)
        def _():
            hx_ref[0:BF16_ROWS, :] = zeros

        @pl.when(jnp.logical_not(at_start))
        def _():
            hx_ref[0:BF16_ROWS, :] = hp_ref[...]

        @pl.when(at_end)
        def _():
            hx_ref[BF16_ROWS + tm:, :] = zeros

        @pl.when(jnp.logical_not(at_end))
        def _():
            hx_ref[BF16_ROWS + tm:, :] = hn_ref[...]

        hx_ref[BF16_ROWS:BF16_ROWS + tm, :] = h_ref[...]
        bases = (BF16_ROWS,)
        seg = tm
    else:
        nseg = tm // seq_len
        seg = seq_len
        bases = tuple(FFN_GAP + s * (seg + FFN_GAP) for s in range(nseg))
        for s in range(nseg + 1):
            u_ref[s * (seg + FFN_GAP):s * (seg + FFN_GAP) + FFN_GAP, :] = jnp.zeros((FFN_GAP, FFN_CK), F32)

    def taps(offset):
        return jnp.concatenate([u_ref[b + offset:b + offset + seg, :] for b in bases], axis=0)

    for c in range(D_FF // FFN_CK):
        cs = slice(c * FFN_CK, (c + 1) * FFN_CK)
        vs = slice(D_FF + c * FFN_CK, D_FF + (c + 1) * FFN_CK)
        if halo:
            u_ext = _dot(hx_ref[...], wup_ref[:, cs])
            u_ref[...] = u_ext
            u0 = u_ext[BF16_ROWS:BF16_ROWS + tm]
        else:
            u0 = _dot(h_ref[...], wup_ref[:, cs])
            for s, b in enumerate(bases):
                u_ref[b:b + seg, :] = u0[s * seg:(s + 1) * seg]
        uv = _dot(h_ref[...], wup_ref[:, vs])
        gff = taps(-1) * cw_ref[0:1, cs] + u0 * cw_ref[1:2, cs] + taps(1) * cw_ref[2:3, cs] + cb_ref[:, cs]
        inner = gff * (GELU_C + (GELU_C * 0.044715) * (gff * gff))
        act_ref[:, cs] = (0.5 * (gff * uv) * (1.0 + jnp.tanh(inner))).astype(BF16)

    gate2 = mod_ref[0][:, 5 * D_MODEL:6 * D_MODEL]
    out = x_ref[...] + gate2 * _dot(act_ref[...], wd_ref[...])
    if final:
        out = _rms(out) * fn_ref[...]
    o_ref[...] = out


def _ffn(h2, x1, mod, rows_per_mod, wup, cw, cb, wd, layer, seq_len, final_norm):
    m = x1.shape[0]
    tm = FFN_TM
    per = rows_per_mod // tm
    hb = tm // BF16_ROWS
    last_halo = m // BF16_ROWS - 1
    final = final_norm is not None
    halo = seq_len > tm
    const = lambda shape: pl.BlockSpec(shape, lambda i: (0,) * len(shape), pipeline_mode=pl.Buffered(1))
    in_specs = [
        pl.BlockSpec((tm, D_MODEL), lambda i: (i, 0)),
        pl.BlockSpec((BF16_ROWS, D_MODEL), lambda i: (jnp.maximum(i * hb - 1, 0), 0)),
        pl.BlockSpec((BF16_ROWS, D_MODEL), lambda i: (jnp.minimum((i + 1) * hb, last_halo), 0)),
        pl.BlockSpec((tm, D_MODEL), lambda i: (i, 0)),
        pl.BlockSpec((1, 1, 6 * D_MODEL), lambda i: (i // per, 0, 0)),
        _resident((D_MODEL, 2 * D_FF), layer),
        const((3, D_FF)),
        const((1, D_FF)),
        _resident((D_FF, D_MODEL), layer),
    ]
    args = [h2, h2, h2, x1, mod, wup, cw, cb, wd]
    if final:
        in_specs.append(const((1, D_MODEL)))
        args.append(final_norm)
    if halo:
        u_rows = tm + 2 * BF16_ROWS
    else:
        u_rows = FFN_GAP + (tm // seq_len) * (seq_len + FFN_GAP)
    return pl.pallas_call(
        functools.partial(_ffn_kernel, seq_len=seq_len, final=final),
        grid=(m // tm,),
        in_specs=in_specs,
        out_specs=pl.BlockSpec((tm, D_MODEL), lambda i: (i, 0)),
        out_shape=jax.ShapeDtypeStruct((m, D_MODEL), F32),
        scratch_shapes=[
            pltpu.VMEM((tm + 2 * BF16_ROWS, D_MODEL), BF16),
            pltpu.VMEM((u_rows, FFN_CK), F32),
            pltpu.VMEM((tm, D_FF), BF16),
        ],
        compiler_params=_params("parallel"),
        name="conv_glu_ffn",
    )(*args)


def _trunk_layer(x, mod, rows_per_mod, p, big, layer, seq_len, ctx, final_norm):
    proj, k_new, v_new = _inproj(x, mod, rows_per_mod, p["norm1"], big["w_in"], layer, seq_len,
                                 want_kv=ctx is None)
    if ctx is None:
        ya, h_fin = _lru(proj, p["lru"], seq_len, None)
        yb = _attn_ctx(proj, p["sink"], seq_len)
    else:
        ck, cv, h0 = ctx
        ya = _lru(proj, p["lru"], seq_len, h0)
        h_fin = None
        yb = _attn_lat(proj, ck, cv, p["sink"], seq_len)
    yc = _pool(proj, p["pool_w"], p["pool_scale"], seq_len)
    x1, h2 = _merge(ya, yb, yc, proj, x, mod, rows_per_mod, p["b_gate"], big["w_branch"], big["w_out"],
                    p["norm2"], layer)
    out = _ffn(h2, x1, mod, rows_per_mod, big["ffn_up"], p["ffn_conv"], p["ffn_conv_b"], big["ffn_down"],
               layer, seq_len, final_norm)
    return out, k_new, v_new, h_fin


def _layer_params(l, norm1, norm2, b_gate, lru_conv, lru_conv_b, lru_wa, lru_ba, lru_wx, lru_bx,
                  lru_lambda, attn_sink, pool_w, pool_scale, ffn_conv, ffn_conv_b):
    def per_head(v):
        return v.reshape(2, LRU_HEADS, LRU_BLOCK).transpose(1, 0, 2)

    wg = jnp.concatenate([lru_wa[l, 0], lru_wx[l, 0], lru_wa[l, 1], lru_wx[l, 1]], axis=-1).astype(BF16)
    ba = per_head(lru_ba[l])
    bx = per_head(lru_bx[l])
    bg = jnp.concatenate([ba[:, 0], bx[:, 0], ba[:, 1], bx[:, 1]], axis=-1)[:, None, :]
    lam = per_head(lru_lambda[l]).reshape(LRU_HEADS, 1, 2 * LRU_BLOCK)
    return {
        "norm1": norm1[l][None], "norm2": norm2[l][None], "b_gate": b_gate[l][None],
        "lru": {"conv_w": lru_conv[l], "conv_b": lru_conv_b[l][None], "wg": wg, "bg": bg, "lam": lam},
        "sink": attn_sink[l], "pool_w": pool_w[l].astype(BF16), "pool_scale": pool_scale[l][None],
        "ffn_conv": ffn_conv[l], "ffn_conv_b": ffn_conv_b[l][None],
    }


def kernel(x_prompt, x_sample, cache_k, cache_v, state_lru, c, c_ctx, w_ada, b_ada, norm1, norm2, w_in,
           b_gate, lru_conv, lru_conv_b, lru_wa, lru_ba, lru_wx, lru_bx, lru_lambda, attn_sink, pool_w,
           pool_scale, w_branch, w_out, ffn_up, ffn_conv, ffn_conv_b, ffn_down, final_norm):
    batch, seq, _ = x_prompt.shape
    dec_batch, dec_seq, _ = x_sample.shape
    past = cache_k.shape[2]
    assert seq == SCAN_CHUNK and dec_seq % SCAN_ROWS == 0 and (batch * seq) % SCAN_ROWS == 0

    c_rows = jnp.concatenate(
        [c_ctx[None], c, jnp.zeros((SUBLANES - 1 - dec_batch, D_MODEL), F32)], axis=0)
    mods = _ada(c_rows, w_ada, b_ada)

    xp = x_prompt.reshape(batch * seq, D_MODEL)
    xs = x_sample.reshape(dec_batch * dec_seq, D_MODEL)
    fn = final_norm[None]
    big = {"w_in": w_in.astype(BF16), "w_branch": w_branch.astype(BF16), "w_out": w_out.astype(BF16),
           "ffn_up": ffn_up.astype(BF16), "ffn_down": ffn_down.astype(BF16)}
    ks, vs, hs = [], [], []
    for l in range(DEPTH):
        p = _layer_params(l, norm1, norm2, b_gate, lru_conv, lru_conv_b, lru_wa, lru_ba, lru_wx, lru_bx,
                          lru_lambda, attn_sink, pool_w, pool_scale, ffn_conv, ffn_conv_b)
        last = fn if l == DEPTH - 1 else None
        mod_ctx = mods[l, 0:1][:, None, :]
        mod_lat = mods[l, 1:1 + dec_batch][:, None, :]
        xp, k_new, v_new, h_fin = _trunk_layer(xp, mod_ctx, batch * seq, p, big, l, seq, None, last)
        ck = cache_k[:, l].reshape(dec_batch, past, D_KV).astype(BF16)
        cv = cache_v[:, l].reshape(dec_batch, past, D_KV).astype(BF16)
        h0 = state_lru[:, l].transpose(1, 0, 2)[:, :, None, :]
        xs, _, _, _ = _trunk_layer(xs, mod_lat, dec_seq, p, big, l, dec_seq, (ck, cv, h0), last)
        ks.append(k_new.reshape(batch, seq, N_KV_HEADS, HEAD_DIM))
        vs.append(v_new.reshape(batch, seq, N_KV_HEADS, HEAD_DIM))
        hs.append(h_fin.transpose(1, 0, 2))
    y_prompt = xp.reshape(batch, seq, D_MODEL)
    y_sample = xs.reshape(dec_batch, dec_seq, D_MODEL)
    return (y_prompt, y_sample, jnp.stack(ks, axis=1), jnp.stack(vs, axis=1), jnp.stack(hs, axis=1))
```

```python
import functools

import numpy as np
import jax
import jax.numpy as jnp
from jax import lax
from jax.experimental import pallas as pl
from jax.experimental.pallas import tpu as pltpu

F32 = jnp.float32
BF16 = jnp.bfloat16

D_MODEL = 1024
DEPTH = 2
GRID_W = 64
EPS = 1e-6
N_BRANCH = 3
D_RNN = 1024
LRU_HEADS = 8
LRU_BLOCK = D_RNN // LRU_HEADS
LRU_C = 8.0
N_HEADS = 8
N_KV_HEADS = 2
KV_GROUPS = N_HEADS // N_KV_HEADS
HEAD_DIM = 128
D_KV = N_KV_HEADS * HEAD_DIM
WINDOW = 128
BLOCK_Q = 128
ROPE_BASE = 10000.0
NEG_INF = -1e30
D_POOL = 1024
POOL_WINDOWS = (2, 4, 8, 16)
POOL_GROUP = D_POOL // len(POOL_WINDOWS)
D_FF = 2816
D_IN = D_RNN + N_HEADS * HEAD_DIM + 2 * D_KV + D_POOL + N_BRANCH * D_MODEL

COL_XA = 0
COL_Q = COL_XA + D_RNN
COL_K = COL_Q + N_HEADS * HEAD_DIM
COL_V = COL_K + D_KV
COL_XC = COL_V + D_KV
COL_G = COL_XC + D_POOL
LOG2E = float(np.log2(np.e))

VMEM_LIMIT_BYTES = 52 * 1024 * 1024
SUBLANES = 8
LANES = 128
BF16_ROWS = 16

SCAN_CHUNK = 256
SCAN_PITCH = 260
SCAN_ROWS = 2048
POOL_PAD = 8


def _params(*sem):
    return pltpu.CompilerParams(dimension_semantics=sem, vmem_limit_bytes=VMEM_LIMIT_BYTES)


def _dot(a, b):
    return jnp.dot(a, b, preferred_element_type=F32)


def _dot_nt(a, b):
    return lax.dot_general(a, b, (((1,), (1,)), ((), ())), preferred_element_type=F32)


def _sigmoid(z):
    return 0.5 * (1.0 + jnp.tanh(0.5 * z))


def _rms(x):
    return x * lax.rsqrt(jnp.mean(x * x, axis=-1, keepdims=True) + EPS)


def _ada_kernel(c_ref, w_ref, b_ref, o_ref):
    c = c_ref[...]
    s = c * _sigmoid(c)
    o_ref[0] = _dot(s.astype(BF16), w_ref[0].astype(BF16)) + b_ref[0]


def _ada(c_rows, w_ada, b_ada):
    tn = 1536
    return pl.pallas_call(
        _ada_kernel,
        grid=(DEPTH, 6 * D_MODEL // tn),
        in_specs=[
            pl.BlockSpec((SUBLANES, D_MODEL), lambda l, j: (0, 0)),
            pl.BlockSpec((1, D_MODEL, tn), lambda l, j: (l, 0, j)),
            pl.BlockSpec((1, 1, tn), lambda l, j: (l, 0, j)),
        ],
        out_specs=pl.BlockSpec((1, SUBLANES, tn), lambda l, j: (l, 0, j)),
        out_shape=jax.ShapeDtypeStruct((DEPTH, SUBLANES, 6 * D_MODEL), F32),
        compiler_params=_params("parallel", "parallel"),
        name="ada_mod",
    )(c_rows, w_ada, b_ada.reshape(DEPTH, 1, 6 * D_MODEL))


INPROJ_TM = 512
INPROJ_CK = 2 * D_KV


def _resident(shape, layer):
    ndim = len(shape)
    return pl.BlockSpec((None,) + tuple(shape), lambda i: (layer,) + (0,) * ndim,
                        pipeline_mode=pl.Buffered(1))


def _inproj_kernel(*refs, seq_len, want_kv):
    if want_kv:
        x_ref, mod_ref, g_ref, w_ref, o_ref, k_ref, v_ref = refs
    else:
        x_ref, mod_ref, g_ref, w_ref, cos_ref, sa_ref, sb_ref, o_ref = refs
    mod = mod_ref[0]
    shift = mod[:, 0:D_MODEL]
    scale = mod[:, D_MODEL:2 * D_MODEL]
    h = (_rms(x_ref[...]) * g_ref[...] * (1.0 + scale) + shift).astype(BF16)
    nf = HEAD_DIM // 4

    def rope(x):
        return (x * cos_ref[...] + pltpu.roll(x, HEAD_DIM - nf, 1) * sa_ref[...]
                + pltpu.roll(x, nf, 1) * sb_ref[...])

    for c in range(D_IN // INPROJ_CK):
        acc = _dot(h, w_ref[:, c * INPROJ_CK:(c + 1) * INPROJ_CK])
        if not want_kv:
            lo = c * INPROJ_CK
            heads = [acc[:, j * HEAD_DIM:(j + 1) * HEAD_DIM] for j in range(INPROJ_CK // HEAD_DIM)]
            heads = [rope(hd) if COL_Q <= lo + j * HEAD_DIM < COL_V else hd for j, hd in enumerate(heads)]
            acc = jnp.concatenate(heads, axis=1)
        o_ref[:, c * INPROJ_CK:(c + 1) * INPROJ_CK] = acc.astype(o_ref.dtype)
        if want_kv and c == COL_K // INPROJ_CK:
            for b in range(INPROJ_TM // seq_len):
                rows = slice(b * seq_len, (b + 1) * seq_len)
                for hd in range(N_KV_HEADS):
                    dst = pl.ds(hd, seq_len, stride=N_KV_HEADS)
                    k_ref[b, dst, :] = acc[rows, hd * HEAD_DIM:(hd + 1) * HEAD_DIM]
                    v_ref[b, dst, :] = acc[rows, D_KV + hd * HEAD_DIM:D_KV + (hd + 1) * HEAD_DIM]


def _inproj(x, mod, rows_per_mod, g, w, layer, seq_len, want_kv):
    m = x.shape[0]
    tm = INPROJ_TM
    per = rows_per_mod // tm
    proj_spec = pl.BlockSpec((tm, D_IN), lambda i: (i, 0))
    proj_shape = jax.ShapeDtypeStruct((m, D_IN), BF16)
    if want_kv:
        nb = tm // seq_len
        cache_spec = pl.BlockSpec((nb, seq_len * N_KV_HEADS, HEAD_DIM), lambda i: (i, 0, 0))
        cache_shape = jax.ShapeDtypeStruct((m // seq_len, seq_len * N_KV_HEADS, HEAD_DIM), F32)
        out_specs = (proj_spec, cache_spec, cache_spec)
        out_shape = (proj_shape, cache_shape, cache_shape)
        extra_specs, extra_args = [], []
    else:
        out_specs, out_shape = proj_spec, proj_shape
        tab = pl.BlockSpec((tm, HEAD_DIM), lambda i: (i % (seq_len // tm), 0))
        extra_specs, extra_args = [tab, tab, tab], list(_rope_tables(seq_len))
    out = pl.pallas_call(
        functools.partial(_inproj_kernel, seq_len=seq_len, want_kv=want_kv),
        grid=(m // tm,),
        in_specs=[
            pl.BlockSpec((tm, D_MODEL), lambda i: (i, 0)),
            pl.BlockSpec((1, 1, 6 * D_MODEL), lambda i: (i // per, 0, 0)),
            pl.BlockSpec((1, D_MODEL), lambda i: (0, 0)),
            _resident((D_MODEL, D_IN), layer),
        ] + extra_specs,
        out_specs=out_specs,
        out_shape=out_shape,
        compiler_params=_params("parallel"),
        name="in_proj",
    )(x, mod, g, w, *extra_args)
    return out if want_kv else (out, None, None)


LRU_HEADS_PER_STEP = 2


def _chunk_neighbour(v, towards_later):
    sub = lax.broadcasted_iota(jnp.int32, v.shape, 0)
    if towards_later:
        return jnp.where(sub >= 1, pltpu.roll(v, 1, 0), 0.0)
    return jnp.where(sub <= SUBLANES - 2, pltpu.roll(v, SUBLANES - 1, 0), 0.0)


def _shift_time(x_tm, k, chained):
    n = x_tm.shape[0]
    steps = abs(k)
    edge = []
    for s in range(steps):
        if not chained:
            edge.append(jnp.zeros((SUBLANES, x_tm.shape[1]), F32))
        elif k > 0:
            src = n - (steps - s) * SUBLANES
            edge.append(_chunk_neighbour(x_tm[src:src + SUBLANES], True))
        else:
            edge.append(_chunk_neighbour(x_tm[s * SUBLANES:(s + 1) * SUBLANES], False))
    if k > 0:
        return jnp.concatenate(edge + [x_tm[:n - steps * SUBLANES]], axis=0)
    return jnp.concatenate([x_tm[steps * SUBLANES:]] + edge, axis=0)


def _lru_tm_kernel(*refs, seq_len, latent):
    if latent:
        (x_ref, cw_ref, cb_ref, wg_ref, bg_ref, lam_ref, h0_ref, y_ref,
         io_ref, af_ref, bf_ref, ab_ref, bb_ref, hf_ref, hb_ref) = refs
    else:
        (x_ref, cw_ref, cb_ref, wg_ref, bg_ref, lam_ref, y_ref, fin_ref,
         io_ref, af_ref, bf_ref, ab_ref, bb_ref, hf_ref, hb_ref) = refs
    rows = SCAN_ROWS
    nchunk = rows // SCAN_CHUNK
    nhead = LRU_HEADS_PER_STEP
    chained = seq_len > SCAN_CHUNK
    for hd in range(nhead):
        lanes = slice(hd * LRU_BLOCK, (hd + 1) * LRU_BLOCK)
        x = x_ref[:, lanes].astype(F32)
        for c in range(nchunk):
            io_ref[hd, pl.ds(c * SCAN_PITCH, SCAN_CHUNK), :] = x[c * SCAN_CHUNK:(c + 1) * SCAN_CHUNK]
        x = jnp.concatenate(
            [io_ref[hd, pl.ds(t, nchunk, stride=SCAN_PITCH), :] for t in range(SCAN_CHUNK)], axis=0)
        xc = (_shift_time(x, 2, chained) * cw_ref[0:1, lanes] + _shift_time(x, 1, chained) * cw_ref[1:2, lanes]
              + x * cw_ref[2:3, lanes] + _shift_time(x, -1, chained) * cw_ref[3:4, lanes] + cb_ref[:, lanes])
        th = jnp.tanh(_dot(xc.astype(BF16), wg_ref[hd] * 0.5) + 0.5 * bg_ref[hd])
        xh = 0.5 * xc
        lam = lam_ref[hd]
        for d, (a_ref, b_ref) in enumerate(((af_ref, bf_ref), (ab_ref, bb_ref))):
            th_r = th[:, 2 * d * LRU_BLOCK:(2 * d + 1) * LRU_BLOCK]
            th_i = th[:, (2 * d + 1) * LRU_BLOCK:(2 * d + 2) * LRU_BLOCK]
            nl = -lam[:, d * LRU_BLOCK:(d + 1) * LRU_BLOCK]
            softplus = jnp.maximum(nl, 0.0) + jnp.log(1.0 + jnp.exp(-jnp.abs(nl)))
            ch = (-0.5 * LRU_C * LOG2E) * softplus
            a = jnp.exp2(ch + ch * th_r)
            y = 1.0 - a * a
            a_ref[hd] = a
            b_ref[hd] = (y * lax.rsqrt(jnp.maximum(y, 1e-30))) * ((1.0 + th_i) * xh)

    def step(t, carry):
        rf = pl.ds(pl.multiple_of(t * nchunk, nchunk), nchunk)
        rb = pl.ds(pl.multiple_of((SCAN_CHUNK - 1 - t) * nchunk, nchunk), nchunk)
        out = []
        for hd in range(nhead):
            hf, hb, pf, pb = carry[4 * hd:4 * hd + 4]
            a_f = af_ref[hd, rf, :]
            a_b = ab_ref[hd, rb, :]
            hf = a_f * hf + bf_ref[hd, rf, :]
            hb = a_b * hb + bb_ref[hd, rb, :]
            hf_ref[hd, rf, :] = hf
            hb_ref[hd, rb, :] = hb
            if chained:
                pf = pf * a_f
                pb = pb * a_b
                af_ref[hd, rf, :] = pf
                ab_ref[hd, rb, :] = pb
            out += [hf, hb, pf, pb]
        return tuple(out)

    zero = jnp.zeros((nchunk, LRU_BLOCK), F32)
    one = jnp.ones((nchunk, LRU_BLOCK), F32)
    lax.fori_loop(0, SCAN_CHUNK, step, (zero, zero, one, one) * nhead, unroll=8)

    sub = lax.broadcasted_iota(jnp.int32, (nchunk, LRU_BLOCK), 0)
    last = slice(rows - nchunk, rows)
    first = slice(0, nchunk)
    for hd in range(nhead):
        lanes = slice(hd * LRU_BLOCK, (hd + 1) * LRU_BLOCK)
        hf = hf_ref[hd]
        hb = hb_ref[hd]
        if chained:
            pf = af_ref[hd]
            pb = ab_ref[hd]
            ef = jnp.where(sub == 0, h0_ref[0, 0][:, lanes], 0.0)
            eb = jnp.where(sub == nchunk - 1, h0_ref[1, 0][:, lanes], 0.0)
            for c in range(1, nchunk):
                ef = jnp.where(sub == c, pltpu.roll(hf[last] + pf[last] * ef, 1, 0), ef)
                eb = jnp.where(sub == nchunk - 1 - c,
                               pltpu.roll(hb[first] + pb[first] * eb, nchunk - 1, 0), eb)
            hf = (hf.reshape(SCAN_CHUNK, nchunk, LRU_BLOCK)
                  + pf.reshape(SCAN_CHUNK, nchunk, LRU_BLOCK) * ef[None]).reshape(rows, LRU_BLOCK)
            hb = (hb.reshape(SCAN_CHUNK, nchunk, LRU_BLOCK)
                  + pb.reshape(SCAN_CHUNK, nchunk, LRU_BLOCK) * eb[None]).reshape(rows, LRU_BLOCK)
        else:
            fin_ref[0, :, lanes] = hf[last]
            fin_ref[1, :, lanes] = hb[first]
        y = hf + hb
        for t in range(SCAN_CHUNK):
            io_ref[hd, pl.ds(t, nchunk, stride=SCAN_PITCH), :] = y[t * nchunk:(t + 1) * nchunk]
        for c in range(nchunk):
            y_ref[pl.ds(c * SCAN_CHUNK, SCAN_CHUNK), lanes] = (
                io_ref[hd, pl.ds(c * SCAN_PITCH, SCAN_CHUNK), :].astype(y_ref.dtype))


def _lru(proj, lp, seq_len, h0):
    m = proj.shape[0]
    latent = h0 is not None
    rows = SCAN_ROWS
    nhead = LRU_HEADS_PER_STEP
    width = nhead * LRU_BLOCK
    in_specs = [
        pl.BlockSpec((rows, width), lambda r, h: (r, COL_XA // width + h)),
        pl.BlockSpec((4, width), lambda r, h: (0, h)),
        pl.BlockSpec((1, width), lambda r, h: (0, h)),
        pl.BlockSpec((nhead, LRU_BLOCK, 4 * LRU_BLOCK), lambda r, h: (h, 0, 0)),
        pl.BlockSpec((nhead, 1, 4 * LRU_BLOCK), lambda r, h: (h, 0, 0)),
        pl.BlockSpec((nhead, 1, 2 * LRU_BLOCK), lambda r, h: (h, 0, 0)),
    ]
    args = [proj, lp["conv_w"], lp["conv_b"], lp["wg"], lp["bg"], lp["lam"]]
    y_spec = pl.BlockSpec((rows, width), lambda r, h: (r, h))
    y_shape = jax.ShapeDtypeStruct((m, D_RNN), BF16)
    if latent:
        in_specs.append(pl.BlockSpec((2, 1, 1, width), lambda r, h: (0, r, 0, h)))
        args.append(h0)
        out_specs, out_shape = y_spec, y_shape
    else:
        nseq = m // seq_len
        out_specs = (y_spec, pl.BlockSpec((2, rows // seq_len, width), lambda r, h: (0, r, h)))
        out_shape = (y_shape, jax.ShapeDtypeStruct((2, nseq, D_RNN), F32))
    strided_buf = pltpu.VMEM((nhead, rows // SCAN_CHUNK * SCAN_PITCH, LRU_BLOCK), F32)
    scan_buf = pltpu.VMEM((nhead, rows, LRU_BLOCK), F32)
    return pl.pallas_call(
        functools.partial(_lru_tm_kernel, seq_len=seq_len, latent=latent),
        grid=(m // rows, LRU_HEADS // nhead),
        in_specs=in_specs,
        out_specs=out_specs,
        out_shape=out_shape,
        scratch_shapes=[strided_buf] + [scan_buf] * 6,
        compiler_params=_params("parallel", "parallel"),
        name="rglru_latent" if latent else "rglru_context",
    )(*args)


SOFTMAX_SCALE = HEAD_DIM ** -0.5 * LOG2E


def _stack_heads(q_ref, kh, r0, rows):
    parts = [q_ref[r0:r0 + rows, (kh * KV_GROUPS + g) * HEAD_DIM:(kh * KV_GROUPS + g + 1) * HEAD_DIM]
             for g in range(KV_GROUPS)]
    return jnp.concatenate(parts, axis=0)


def _dot_tn(a, b):
    return lax.dot_general(a, b, (((0,), (0,)), ((), ())), preferred_element_type=F32)


def _sink_row(sink_ref, kh, cols):
    parts = [jnp.full((1, cols), sink_ref[kh * KV_GROUPS + g] * LOG2E, F32) for g in range(KV_GROUPS)]
    return jnp.concatenate(parts, axis=1)


def _softmax_pv_t(t, sink, v):
    m = jnp.maximum(jnp.max(t, axis=0, keepdims=True), sink)
    e = jnp.exp2(t - m)
    denom = jnp.sum(e, axis=0, keepdims=True) + jnp.exp2(sink - m)
    return _dot_tn(v, e.astype(BF16)) * (1.0 / denom)


def _store_heads_t(y_ref, o_t, kh, r0, rows):
    for g in range(KV_GROUPS):
        h = kh * KV_GROUPS + g
        y_ref[r0:r0 + rows, h * HEAD_DIM:(h + 1) * HEAD_DIM] = (
            o_t[:, g * rows:(g + 1) * rows].T.astype(y_ref.dtype))


def _attn_ctx_kernel(sink_ref, q_ref, kv_ref, y_ref, *, seq_len):
    for b in range(q_ref.shape[0] // seq_len):
        r0 = b * seq_len
        for kh in range(N_KV_HEADS):
            sl = slice(kh * HEAD_DIM, (kh + 1) * HEAD_DIM)
            vl = slice(D_KV + kh * HEAD_DIM, D_KV + (kh + 1) * HEAD_DIM)
            q = _stack_heads(q_ref, kh, r0, seq_len)
            t = _dot_nt(kv_ref[r0:r0 + seq_len, sl], q) * SOFTMAX_SCALE
            o_t = _softmax_pv_t(t, _sink_row(sink_ref, kh, seq_len), kv_ref[r0:r0 + seq_len, vl])
            _store_heads_t(y_ref, o_t, kh, r0, seq_len)


ATTN_CTX_SEQS = 4


def _attn_ctx(proj, sink, seq_len):
    m = proj.shape[0]
    rows = ATTN_CTX_SEQS * seq_len
    return pl.pallas_call(
        functools.partial(_attn_ctx_kernel, seq_len=seq_len),
        grid=(m // rows,),
        in_specs=[
            pl.BlockSpec(memory_space=pltpu.SMEM),
            pl.BlockSpec((rows, N_HEADS * HEAD_DIM), lambda b: (b, COL_Q // (N_HEADS * HEAD_DIM))),
            pl.BlockSpec((rows, 2 * D_KV), lambda b: (b, COL_K // (2 * D_KV))),
        ],
        out_specs=pl.BlockSpec((rows, N_HEADS * HEAD_DIM), lambda b: (b, 0)),
        out_shape=jax.ShapeDtypeStruct((m, N_HEADS * HEAD_DIM), BF16),
        compiler_params=_params("parallel"),
        name="attn_context",
    )(sink, proj, proj)


def _rope_tables(seq_len):
    nf = HEAD_DIM // 4
    freqs = ROPE_BASE ** (-np.arange(nf, dtype=np.float64) / nf)
    t = np.arange(seq_len)
    ang_row = (t // GRID_W)[:, None] * freqs[None, :]
    ang_col = (t % GRID_W)[:, None] * freqs[None, :]
    ang = np.concatenate([ang_row, ang_row, ang_col, ang_col], axis=1)
    first = (np.arange(HEAD_DIM) % (2 * nf)) < nf
    cos = np.cos(ang)
    sin = np.sin(ang)
    sin_a = np.where(first[None, :], -sin, 0.0)
    sin_b = np.where(first[None, :], 0.0, sin)
    return tuple(jnp.asarray(a, F32) for a in (cos, sin_a, sin_b))


ATTN_LAT_BLOCKS = 4


def _attn_lat_kernel(sink_ref, q_ref, kvp_ref, kvc_ref, kvn_ref, ck_ref, cv_ref, y_ref, *, nblk):
    step = pl.program_id(1)
    cols = KV_GROUPS * BLOCK_Q
    span = 3 * BLOCK_Q
    key = lax.broadcasted_iota(jnp.int32, (span, cols), 0)
    qry = lax.broadcasted_iota(jnp.int32, (span, cols), 1) & (BLOCK_Q - 1)
    kv = jnp.concatenate([kvp_ref[...], kvc_ref[...], kvn_ref[...]], axis=0)
    for i in range(ATTN_LAT_BLOCKS):
        j = step * ATTN_LAT_BLOCKS + i
        lo = jnp.where(j > 0, qry, BLOCK_Q)
        hi = jnp.where(j < nblk - 1, qry + 2 * BLOCK_Q, 2 * BLOCK_Q - 1)
        bias = jnp.where(jnp.logical_and(key >= lo, key <= hi), 0.0, NEG_INF)
        win = kv[i * BLOCK_Q:i * BLOCK_Q + span]
        for kh in range(N_KV_HEADS):
            sl = slice(kh * HEAD_DIM, (kh + 1) * HEAD_DIM)
            vl = slice(D_KV + kh * HEAD_DIM, D_KV + (kh + 1) * HEAD_DIM)
            q = _stack_heads(q_ref, kh, i * BLOCK_Q, BLOCK_Q)
            keys = jnp.concatenate([win[:, sl], ck_ref[0, :, sl]], axis=0)
            vals = jnp.concatenate([win[:, vl], cv_ref[0, :, sl]], axis=0)
            t = _dot_nt(keys, q) * SOFTMAX_SCALE
            t = jnp.concatenate([t[:span] + bias, t[span:]], axis=0)
            o_t = _softmax_pv_t(t, _sink_row(sink_ref, kh, BLOCK_Q), vals)
            _store_heads_t(y_ref, o_t, kh, i * BLOCK_Q, BLOCK_Q)


def _attn_lat(proj, ck, cv, sink, seq_len):
    m = proj.shape[0]
    nblk = seq_len // BLOCK_Q
    nstep = nblk // ATTN_LAT_BLOCKS
    rows = ATTN_LAT_BLOCKS * BLOCK_Q
    nb = m // seq_len
    past = ck.shape[1]
    kv_col = COL_K // (2 * D_KV)
    assert COL_V == COL_K + D_KV and COL_K % (2 * D_KV) == 0

    def halo(shift):
        def index(b, s):
            return (b * nblk + jnp.clip(s * ATTN_LAT_BLOCKS + shift, 0, nblk - 1), kv_col)
        return index

    return pl.pallas_call(
        functools.partial(_attn_lat_kernel, nblk=nblk),
        grid=(nb, nstep),
        in_specs=[
            pl.BlockSpec(memory_space=pltpu.SMEM),
            pl.BlockSpec((rows, N_HEADS * HEAD_DIM), lambda b, s: (b * nstep + s, COL_Q // (N_HEADS * HEAD_DIM))),
            pl.BlockSpec((BLOCK_Q, 2 * D_KV), halo(-1)),
            pl.BlockSpec((rows, 2 * D_KV), lambda b, s: (b * nstep + s, kv_col)),
            pl.BlockSpec((BLOCK_Q, 2 * D_KV), halo(ATTN_LAT_BLOCKS)),
            pl.BlockSpec((1, past, D_KV), lambda b, s: (b, 0, 0)),
            pl.BlockSpec((1, past, D_KV), lambda b, s: (b, 0, 0)),
        ],
        out_specs=pl.BlockSpec((rows, N_HEADS * HEAD_DIM), lambda b, s: (b * nstep + s, 0)),
        out_shape=jax.ShapeDtypeStruct((m, N_HEADS * HEAD_DIM), BF16),
        compiler_params=_params("parallel", "parallel"),
        name="attn_latent",
    )(sink, proj, proj, proj, proj, ck, cv)


POOL_TILE = 256
POOL_LEAD = BF16_ROWS
POOL_ROWS = 2048


def _pool_plan(seq_len):
    lead = 0 if seq_len == POOL_TILE else POOL_LEAD
    return lead, POOL_TILE - 2 * lead


def _pool_bands(seq_len):
    lead, nout = _pool_plan(seq_len)
    r = np.arange(nout)[:, None]
    c = np.arange(POOL_TILE)[None, :] - lead
    bands = [(c >= r - win // 2) & (c < r + win // 2) for win in POOL_WINDOWS]
    return jnp.asarray(np.stack(bands), BF16)


def _pool_kernel(x0_ref, x1_ref, x2_ref, x3_ref, band_ref, w_ref, s_ref, y_ref, pad_ref, *, seq_len):
    lead, nout = _pool_plan(seq_len)
    edge = lax.broadcasted_iota(jnp.int32, (SUBLANES, POOL_GROUP), 0)
    if lead:
        pad_ref[0:lead, :] = jnp.zeros((lead, POOL_GROUP), BF16)
        pad_ref[lead + seq_len:, :] = jnp.zeros((pad_ref.shape[0] - lead - seq_len, POOL_GROUP), BF16)
    for gi, (win, x_ref) in enumerate(zip(POOL_WINDOWS, (x0_ref, x1_ref, x2_ref, x3_ref))):
        cs = slice(gi * POOL_GROUP, (gi + 1) * POOL_GROUP)
        half = win // 2
        inv_head = 1.0 / ((edge + half) - jnp.maximum(edge - half, 0)).astype(F32)
        inv_tail = 1.0 / (jnp.minimum(SUBLANES - edge, half) + half).astype(F32)
        if lead:
            pad_ref[lead:lead + seq_len, :] = x_ref[...]
        for base in range(0, POOL_ROWS, seq_len):
            for p0 in range(0, seq_len, nout):
                n = min(nout, seq_len - p0)
                x = x_ref[base + p0:base + p0 + n, :]
                src = pad_ref[p0:p0 + POOL_TILE, :] if lead else x
                sums = _dot(band_ref[gi, :n, :], src)
                head = sums[:SUBLANES] * (inv_head if p0 == 0 else 1.0 / win)
                tail = sums[n - SUBLANES:] * (inv_tail if p0 + n == seq_len else 1.0 / win)
                mean = jnp.concatenate([head, sums[SUBLANES:n - SUBLANES] * (1.0 / win), tail], axis=0)
                pooled = (mean - x.astype(F32)).astype(BF16)
                y_ref[base + p0:base + p0 + n, cs] = (
                    _dot(pooled, w_ref[gi]) * s_ref[:, cs]).astype(y_ref.dtype)


def _pool(proj, w, s, seq_len):
    m = proj.shape[0]
    rows = POOL_ROWS
    lead, nout = _pool_plan(seq_len)
    assert seq_len in (POOL_TILE, rows) and max(POOL_WINDOWS) // 2 <= min(SUBLANES, lead or SUBLANES)
    pad_rows = (pl.cdiv(seq_len, nout) - 1) * nout + POOL_TILE
    group = lambda gi: pl.BlockSpec((rows, POOL_GROUP), lambda r: (r, COL_XC // POOL_GROUP + gi))
    whole = lambda shape: pl.BlockSpec(shape, lambda r: (0,) * len(shape))
    nwin = len(POOL_WINDOWS)
    return pl.pallas_call(
        functools.partial(_pool_kernel, seq_len=seq_len),
        grid=(m // rows,),
        in_specs=[
            group(0), group(1), group(2), group(3),
            whole((nwin, nout, POOL_TILE)),
            whole((nwin, POOL_GROUP, POOL_GROUP)),
            whole((1, D_POOL)),
        ],
        out_specs=pl.BlockSpec((rows, D_POOL), lambda r: (r, 0)),
        out_shape=jax.ShapeDtypeStruct((m, D_POOL), BF16),
        scratch_shapes=[pltpu.VMEM((pad_rows, POOL_GROUP), BF16)],
        compiler_params=_params("parallel"),
        name="pool_mix",
    )(proj, proj, proj, proj, _pool_bands(seq_len), w, s)


def _merge_kernel(ya_ref, yb_ref, yc_ref, g0_ref, g1_ref, g2_ref, g3_ref, g4_ref, g5_ref, x_ref, mod_ref,
                  bg_ref, wb_ref, wo_ref, n2_ref, x1_ref, h2_ref):
    mod = mod_ref[0]
    g_refs = (g0_ref, g1_ref, g2_ref, g3_ref, g4_ref, g5_ref)
    half = D_MODEL // 2
    merged = None
    for k, y_ref in enumerate((ya_ref, yb_ref, yc_ref)):
        y = _dot(y_ref[...], wb_ref[k])
        parts = []
        for p in range(2):
            z = g_refs[2 * k + p][...].astype(F32) + bg_ref[:, k * D_MODEL + p * half:k * D_MODEL + (p + 1) * half]
            parts.append((1.0 + jnp.tanh(0.5 * z)) * y[:, p * half:(p + 1) * half])
        term = jnp.concatenate(parts, axis=1)
        merged = term if merged is None else merged + term
    merged = 0.5 * merged
    gate1 = mod[:, 2 * D_MODEL:3 * D_MODEL]
    x1 = x_ref[...] + gate1 * _dot(merged.astype(BF16), wo_ref[...])
    x1_ref[...] = x1
    shift2 = mod[:, 3 * D_MODEL:4 * D_MODEL]
    scale2 = mod[:, 4 * D_MODEL:5 * D_MODEL]
    h2_ref[...] = (_rms(x1) * n2_ref[...] * (1.0 + scale2) + shift2).astype(BF16)


def _merge(ya, yb, yc, proj, x, mod, rows_per_mod, bg, wb, wo, n2, layer):
    m = x.shape[0]
    tm = 512
    per = rows_per_mod // tm
    half = D_MODEL // 2
    row = pl.BlockSpec((tm, D_MODEL), lambda i: (i, 0))
    gate = lambda c: pl.BlockSpec((tm, half), lambda i: (i, COL_G // half + c))
    const = lambda shape: pl.BlockSpec(shape, lambda i: (0,) * len(shape), pipeline_mode=pl.Buffered(1))
    return pl.pallas_call(
        _merge_kernel,
        grid=(m // tm,),
        in_specs=[
            row, row, row,
            gate(0), gate(1), gate(2), gate(3), gate(4), gate(5),
            row,
            pl.BlockSpec((1, 1, 6 * D_MODEL), lambda i: (i // per, 0, 0)),
            const((1, N_BRANCH * D_MODEL)),
            _resident((N_BRANCH, D_MODEL, D_MODEL), layer),
            _resident((D_MODEL, D_MODEL), layer),
            const((1, D_MODEL)),
        ],
        out_specs=(row, row),
        out_shape=(jax.ShapeDtypeStruct((m, D_MODEL), F32), jax.ShapeDtypeStruct((m, D_MODEL), BF16)),
        compiler_params=_params("parallel"),
        name="merge_out",
    )(ya, yb, yc, proj, proj, proj, proj, proj, proj, x, mod, bg, wb, wo, n2)


FFN_CK = 256
FFN_TM = 512
FFN_GAP = SUBLANES
GELU_C = float(np.sqrt(2.0 / np.pi))


def _ffn_kernel(*refs, seq_len, final):
    refs = list(refs)
    h_ref, hp_ref, hn_ref, x_ref, mod_ref, wup_ref, cw_ref, cb_ref, wd_ref = refs[:9]
    fn_ref = refs[9] if final else None
    o_ref, hx_ref, u_ref, act_ref = refs[-4:]
    tm = FFN_TM
    halo = seq_len > tm
    i = pl.program_id(0)

    if halo:
        per_seq = seq_len // tm
        at_start = i % per_seq == 0
        at_end = i % per_seq == per_seq - 1
        zeros = jnp.zeros((BF16_ROWS, D_MODEL), BF16)

        @pl.when(at_start)
        def _():
            hx_ref[0:BF16_ROWS, :] = zeros

        @pl.when(jnp.logical_not(at_start))
        def _():
            hx_ref[0:BF16_ROWS, :] = hp_ref[...]

        @pl.when(at_end)
        def _():
            hx_ref[BF16_ROWS + tm:, :] = zeros

        @pl.when(jnp.logical_not(at_end))
        def _():
            hx_ref[BF16_ROWS + tm:, :] = hn_ref[...]

        hx_ref[BF16_ROWS:BF16_ROWS + tm, :] = h_ref[...]
        bases = (BF16_ROWS,)
        seg = tm
    else:
        nseg = tm // seq_len
        seg = seq_len
        bases = tuple(FFN_GAP + s * (seg + FFN_GAP) for s in range(nseg))
        for s in range(nseg + 1):
            u_ref[s * (seg + FFN_GAP):s * (seg + FFN_GAP) + FFN_GAP, :] = jnp.zeros((FFN_GAP, FFN_CK), F32)

    def taps(offset):
        return jnp.concatenate([u_ref[b + offset:b + offset + seg, :] for b in bases], axis=0)

    for c in range(D_FF // FFN_CK):
        cs = slice(c * FFN_CK, (c + 1) * FFN_CK)
        vs = slice(D_FF + c * FFN_CK, D_FF + (c + 1) * FFN_CK)
        if halo:
            u_ext = _dot(hx_ref[...], wup_ref[:, cs])
            u_ref[...] = u_ext
            u0 = u_ext[BF16_ROWS:BF16_ROWS + tm]
        else:
            u0 = _dot(h_ref[...], wup_ref[:, cs])
            for s, b in enumerate(bases):
                u_ref[b:b + seg, :] = u0[s * seg:(s + 1) * seg]
        uv = _dot(h_ref[...], wup_ref[:, vs])
        gff = taps(-1) * cw_ref[0:1, cs] + u0 * cw_ref[1:2, cs] + taps(1) * cw_ref[2:3, cs] + cb_ref[:, cs]
        inner = gff * (GELU_C + (GELU_C * 0.044715) * (gff * gff))
        act_ref[:, cs] = (0.5 * (gff * uv) * (1.0 + jnp.tanh(inner))).astype(BF16)

    gate2 = mod_ref[0][:, 5 * D_MODEL:6 * D_MODEL]
    out = x_ref[...] + gate2 * _dot(act_ref[...], wd_ref[...])
    if final:
        out = _rms(out) * fn_ref[...]
    o_ref[...] = out


def _ffn(h2, x1, mod, rows_per_mod, wup, cw, cb, wd, layer, seq_len, final_norm):
    m = x1.shape[0]
    tm = FFN_TM
    per = rows_per_mod // tm
    hb = tm // BF16_ROWS
    last_halo = m // BF16_ROWS - 1
    final = final_norm is not None
    halo = seq_len > tm
    const = lambda shape: pl.BlockSpec(shape, lambda i: (0,) * len(shape), pipeline_mode=pl.Buffered(1))
    in_specs = [
        pl.BlockSpec((tm, D_MODEL), lambda i: (i, 0)),
        pl.BlockSpec((BF16_ROWS, D_MODEL), lambda i: (jnp.maximum(i * hb - 1, 0), 0)),
        pl.BlockSpec((BF16_ROWS, D_MODEL), lambda i: (jnp.minimum((i + 1) * hb, last_halo), 0)),
        pl.BlockSpec((tm, D_MODEL), lambda i: (i, 0)),
        pl.BlockSpec((1, 1, 6 * D_MODEL), lambda i: (i // per, 0, 0)),
        _resident((D_MODEL, 2 * D_FF), layer),
        const((3, D_FF)),
        const((1, D_FF)),
        _resident((D_FF, D_MODEL), layer),
    ]
    args = [h2, h2, h2, x1, mod, wup, cw, cb, wd]
    if final:
        in_specs.append(const((1, D_MODEL)))
        args.append(final_norm)
    if halo:
        u_rows = tm + 2 * BF16_ROWS
    else:
        u_rows = FFN_GAP + (tm // seq_len) * (seq_len + FFN_GAP)
    return pl.pallas_call(
        functools.partial(_ffn_kernel, seq_len=seq_len, final=final),
        grid=(m // tm,),
        in_specs=in_specs,
        out_specs=pl.BlockSpec((tm, D_MODEL), lambda i: (i, 0)),
        out_shape=jax.ShapeDtypeStruct((m, D_MODEL), F32),
        scratch_shapes=[
            pltpu.VMEM((tm + 2 * BF16_ROWS, D_MODEL), BF16),
            pltpu.VMEM((u_rows, FFN_CK), F32),
            pltpu.VMEM((tm, D_FF), BF16),
        ],
        compiler_params=_params("parallel"),
        name="conv_glu_ffn",
    )(*args)


def _trunk_layer(x, mod, rows_per_mod, p, big, layer, seq_len, ctx, final_norm):
    proj, k_new, v_new = _inproj(x, mod, rows_per_mod, p["norm1"], big["w_in"], layer, seq_len,
                                 want_kv=ctx is None)
    if ctx is None:
        ya, h_fin = _lru(proj, p["lru"], seq_len, None)
        yb = _attn_ctx(proj, p["sink"], seq_len)
    else:
        ck, cv, h0 = ctx
        ya = _lru(proj, p["lru"], seq_len, h0)
        h_fin = None
        yb = _attn_lat(proj, ck, cv, p["sink"], seq_len)
    yc = _pool(proj, p["pool_w"], p["pool_scale"], seq_len)
    x1, h2 = _merge(ya, yb, yc, proj, x, mod, rows_per_mod, p["b_gate"], big["w_branch"], big["w_out"],
                    p["norm2"], layer)
    out = _ffn(h2, x1, mod, rows_per_mod, big["ffn_up"], p["ffn_conv"], p["ffn_conv_b"], big["ffn_down"],
               layer, seq_len, final_norm)
    return out, k_new, v_new, h_fin


def _layer_params(l, norm1, norm2, b_gate, lru_conv, lru_conv_b, lru_wa, lru_ba, lru_wx, lru_bx,
                  lru_lambda, attn_sink, pool_w, pool_scale, ffn_conv, ffn_conv_b):
    def per_head(v):
        return v.reshape(2, LRU_HEADS, LRU_BLOCK).transpose(1, 0, 2)

    wg = jnp.concatenate([lru_wa[l, 0], lru_wx[l, 0], lru_wa[l, 1], lru_wx[l, 1]], axis=-1).astype(BF16)
    ba = per_head(lru_ba[l])
    bx = per_head(lru_bx[l])
    bg = jnp.concatenate([ba[:, 0], bx[:, 0], ba[:, 1], bx[:, 1]], axis=-1)[:, None, :]
    lam = per_head(lru_lambda[l]).reshape(LRU_HEADS, 1, 2 * LRU_BLOCK)
    return {
        "norm1": norm1[l][None], "norm2": norm2[l][None], "b_gate": b_gate[l][None],
        "lru": {"conv_w": lru_conv[l], "conv_b": lru_conv_b[l][None], "wg": wg, "bg": bg, "lam": lam},
        "sink": attn_sink[l], "pool_w": pool_w[l].astype(BF16), "pool_scale": pool_scale[l][None],
        "ffn_conv": ffn_conv[l], "ffn_conv_b": ffn_conv_b[l][None],
    }


def kernel(x_prompt, x_sample, cache_k, cache_v, state_lru, c, c_ctx, w_ada, b_ada, norm1, norm2, w_in,
           b_gate, lru_conv, lru_conv_b, lru_wa, lru_ba, lru_wx, lru_bx, lru_lambda, attn_sink, pool_w,
           pool_scale, w_branch, w_out, ffn_up, ffn_conv, ffn_conv_b, ffn_down, final_norm):
    batch, seq, _ = x_prompt.shape
    dec_batch, dec_seq, _ = x_sample.shape
    past = cache_k.shape[2]
    assert seq == SCAN_CHUNK and dec_seq % SCAN_ROWS == 0 and (batch * seq) % SCAN_ROWS == 0

    c_rows = jnp.concatenate(
        [c_ctx[None], c, jnp.zeros((SUBLANES - 1 - dec_batch, D_MODEL), F32)], axis=0)
    mods = _ada(c_rows, w_ada, b_ada)

    xp = x_prompt.reshape(batch * seq, D_MODEL)
    xs = x_sample.reshape(dec_batch * dec_seq, D_MODEL)
    fn = final_norm[None]
    big = {"w_in": w_in.astype(BF16), "w_branch": w_branch.astype(BF16), "w_out": w_out.astype(BF16),
           "ffn_up": ffn_up.astype(BF16), "ffn_down": ffn_down.astype(BF16)}
    ks, vs, hs = [], [], []
    for l in range(DEPTH):
        p = _layer_params(l, norm1, norm2, b_gate, lru_conv, lru_conv_b, lru_wa, lru_ba, lru_wx, lru_bx,
                          lru_lambda, attn_sink, pool_w, pool_scale, ffn_conv, ffn_conv_b)
        last = fn if l == DEPTH - 1 else None
        mod_ctx = mods[l, 0:1][:, None, :]
        mod_lat = mods[l, 1:1 + dec_batch][:, None, :]
        xp, k_new, v_new, h_fin = _trunk_layer(xp, mod_ctx, batch * seq, p, big, l, seq, None, last)
        ck = cache_k[:, l].reshape(dec_batch, past, D_KV).astype(BF16)
        cv = cache_v[:, l].reshape(dec_batch, past, D_KV).astype(BF16)
        h0 = state_lru[:, l].transpose(1, 0, 2)[:, :, None, :]
        xs, _, _, _ = _trunk_layer(xs, mod_lat, dec_seq, p, big, l, dec_seq, (ck, cv, h0), last)
        ks.append(k_new.reshape(batch, seq, N_KV_HEADS, HEAD_DIM))
        vs.append(v_new.reshape(batch, seq, N_KV_HEADS, HEAD_DIM))
        hs.append(h_fin.transpose(1, 0, 2))
    y_prompt = xp.reshape(batch, seq, D_MODEL)
    y_sample = xs.reshape(dec_batch, dec_seq, D_MODEL)
    return (y_prompt, y_sample, jnp.stack(ks, axis=1), jnp.stack(vs, axis=1), jnp.stack(hs, axis=1))
```

```python
import functools

import numpy as np
import jax
import jax.numpy as jnp
from jax import lax
from jax.experimental import pallas as pl
from jax.experimental.pallas import tpu as pltpu

F32 = jnp.float32
BF16 = jnp.bfloat16

D_MODEL = 1024
DEPTH = 2
GRID_W = 64
EPS = 1e-6
N_BRANCH = 3
D_RNN = 1024
LRU_HEADS = 8
LRU_BLOCK = D_RNN // LRU_HEADS
LRU_C = 8.0
N_HEADS = 8
N_KV_HEADS = 2
KV_GROUPS = N_HEADS // N_KV_HEADS
HEAD_DIM = 128
D_KV = N_KV_HEADS * HEAD_DIM
WINDOW = 128
BLOCK_Q = 128
ROPE_BASE = 10000.0
NEG_INF = -1e30
D_POOL = 1024
POOL_WINDOWS = (2, 4, 8, 16)
POOL_GROUP = D_POOL // len(POOL_WINDOWS)
D_FF = 2816
D_IN = D_RNN + N_HEADS * HEAD_DIM + 2 * D_KV + D_POOL + N_BRANCH * D_MODEL

COL_XA = 0
COL_Q = COL_XA + D_RNN
COL_K = COL_Q + N_HEADS * HEAD_DIM
COL_V = COL_K + D_KV
COL_XC = COL_V + D_KV
COL_G = COL_XC + D_POOL
LOG2E = float(np.log2(np.e))

VMEM_LIMIT_BYTES = 52 * 1024 * 1024
SUBLANES = 8
LANES = 128
BF16_ROWS = 16

SCAN_CHUNK = 256
SCAN_PITCH = 260
SCAN_ROWS = 2048
POOL_PAD = 8


def _params(*sem):
    return pltpu.CompilerParams(dimension_semantics=sem, vmem_limit_bytes=VMEM_LIMIT_BYTES)


def _dot(a, b):
    return jnp.dot(a, b, preferred_element_type=F32)


def _dot_nt(a, b):
    return lax.dot_general(a, b, (((1,), (1,)), ((), ())), preferred_element_type=F32)


def _sigmoid(z):
    return 0.5 * (1.0 + jnp.tanh(0.5 * z))


def _rms(x):
    return x * lax.rsqrt(jnp.mean(x * x, axis=-1, keepdims=True) + EPS)


def _ada_kernel(c_ref, w_ref, b_ref, o_ref):
    c = c_ref[...]
    s = c * _sigmoid(c)
    o_ref[0] = _dot(s.astype(BF16), w_ref[0].astype(BF16)) + b_ref[0]


def _ada(c_rows, w_ada, b_ada):
    tn = 1536
    return pl.pallas_call(
        _ada_kernel,
        grid=(DEPTH, 6 * D_MODEL // tn),
        in_specs=[
            pl.BlockSpec((SUBLANES, D_MODEL), lambda l, j: (0, 0)),
            pl.BlockSpec((1, D_MODEL, tn), lambda l, j: (l, 0, j)),
            pl.BlockSpec((1, 1, tn), lambda l, j: (l, 0, j)),
        ],
        out_specs=pl.BlockSpec((1, SUBLANES, tn), lambda l, j: (l, 0, j)),
        out_shape=jax.ShapeDtypeStruct((DEPTH, SUBLANES, 6 * D_MODEL), F32),
        compiler_params=_params("parallel", "parallel"),
        name="ada_mod",
    )(c_rows, w_ada, b_ada.reshape(DEPTH, 1, 6 * D_MODEL))


INPROJ_TM = 512
INPROJ_CK = 2 * D_KV


def _resident(shape, layer):
    ndim = len(shape)
    return pl.BlockSpec((None,) + tuple(shape), lambda *_: (layer,) + (0,) * ndim,
                        pipeline_mode=pl.Buffered(1))


def _mod_spec(row0, per):
    return pl.BlockSpec((1, 1, 6 * D_MODEL), lambda i: (row0 + i // per, 0, 0))


def _inproj_kernel(*refs, seq_len, want_kv):
    if want_kv:
        x_ref, mod_ref, g_ref, w_ref, o_ref, k_ref, v_ref = refs
    else:
        x_ref, mod_ref, g_ref, w_ref, cos_ref, sa_ref, sb_ref, o_ref = refs
    mod = mod_ref[0]
    shift = mod[:, 0:D_MODEL]
    scale = mod[:, D_MODEL:2 * D_MODEL]
    h = (_rms(x_ref[...]) * g_ref[...] * (1.0 + scale) + shift).astype(BF16)
    nf = HEAD_DIM // 4

    def rope(x):
        return (x * cos_ref[...] + pltpu.roll(x, HEAD_DIM - nf, 1) * sa_ref[...]
                + pltpu.roll(x, nf, 1) * sb_ref[...])

    for c in range(D_IN // INPROJ_CK):
        acc = _dot(h, w_ref[:, c * INPROJ_CK:(c + 1) * INPROJ_CK])
        if not want_kv:
            lo = c * INPROJ_CK
            heads = [acc[:, j * HEAD_DIM:(j + 1) * HEAD_DIM] for j in range(INPROJ_CK // HEAD_DIM)]
            heads = [rope(hd) if COL_Q <= lo + j * HEAD_DIM < COL_V else hd for j, hd in enumerate(heads)]
            acc = jnp.concatenate(heads, axis=1)
        o_ref[:, c * INPROJ_CK:(c + 1) * INPROJ_CK] = acc.astype(o_ref.dtype)
        if want_kv and c == COL_K // INPROJ_CK:
            for b in range(INPROJ_TM // seq_len):
                rows = slice(b * seq_len, (b + 1) * seq_len)
                for hd in range(N_KV_HEADS):
                    dst = pl.ds(hd, seq_len, stride=N_KV_HEADS)
                    k_ref[b, dst, :] = acc[rows, hd * HEAD_DIM:(hd + 1) * HEAD_DIM]
                    v_ref[b, dst, :] = acc[rows, D_KV + hd * HEAD_DIM:D_KV + (hd + 1) * HEAD_DIM]


def _inproj(x, mod, mod_row0, rows_per_mod, g, w, layer, seq_len, want_kv):
    m = x.shape[0]
    tm = INPROJ_TM
    per = rows_per_mod // tm
    proj_spec = pl.BlockSpec((tm, D_IN), lambda i: (i, 0))
    proj_shape = jax.ShapeDtypeStruct((m, D_IN), BF16)
    if want_kv:
        nb = tm // seq_len
        cache_spec = pl.BlockSpec((nb, seq_len * N_KV_HEADS, HEAD_DIM), lambda i: (i, 0, 0))
        cache_shape = jax.ShapeDtypeStruct((m // seq_len, seq_len * N_KV_HEADS, HEAD_DIM), F32)
        out_specs = (proj_spec, cache_spec, cache_spec)
        out_shape = (proj_shape, cache_shape, cache_shape)
        extra_specs, extra_args = [], []
    else:
        out_specs, out_shape = proj_spec, proj_shape
        tab = pl.BlockSpec((tm, HEAD_DIM), lambda i: (i % (seq_len // tm), 0))
        extra_specs, extra_args = [tab, tab, tab], list(_rope_tables(seq_len))
    out = pl.pallas_call(
        functools.partial(_inproj_kernel, seq_len=seq_len, want_kv=want_kv),
        grid=(m // tm,),
        in_specs=[
            pl.BlockSpec((tm, D_MODEL), lambda i: (i, 0)),
            _mod_spec(mod_row0, per),
            _resident((1, D_MODEL), layer),
            _resident((D_MODEL, D_IN), layer),
        ] + extra_specs,
        out_specs=out_specs,
        out_shape=out_shape,
        compiler_params=_params("parallel"),
        name="in_proj",
    )(x, mod, g, w, *extra_args)
    return out if want_kv else (out, None, None)


LRU_HEADS_PER_STEP = 2


def _chunk_neighbour(v, towards_later):
    sub = lax.broadcasted_iota(jnp.int32, v.shape, 0)
    if towards_later:
        return jnp.where(sub >= 1, pltpu.roll(v, 1, 0), 0.0)
    return jnp.where(sub <= SUBLANES - 2, pltpu.roll(v, SUBLANES - 1, 0), 0.0)


def _shift_time(x_tm, k, chained):
    n = x_tm.shape[0]
    steps = abs(k)
    edge = []
    for s in range(steps):
        if not chained:
            edge.append(jnp.zeros((SUBLANES, x_tm.shape[1]), F32))
        elif k > 0:
            src = n - (steps - s) * SUBLANES
            edge.append(_chunk_neighbour(x_tm[src:src + SUBLANES], True))
        else:
            edge.append(_chunk_neighbour(x_tm[s * SUBLANES:(s + 1) * SUBLANES], False))
    if k > 0:
        return jnp.concatenate(edge + [x_tm[:n - steps * SUBLANES]], axis=0)
    return jnp.concatenate([x_tm[steps * SUBLANES:]] + edge, axis=0)


def _lru_tm_kernel(*refs, seq_len, latent):
    if latent:
        (x_ref, cw_ref, cb_ref, wg_ref, bg_ref, lam_ref, h0_ref, y_ref,
         io_ref, af_ref, bf_ref, ab_ref, bb_ref, hf_ref, hb_ref) = refs
    else:
        (x_ref, cw_ref, cb_ref, wg_ref, bg_ref, lam_ref, y_ref, fin_ref,
         io_ref, af_ref, bf_ref, ab_ref, bb_ref, hf_ref, hb_ref) = refs
    rows = SCAN_ROWS
    nchunk = rows // SCAN_CHUNK
    nhead = LRU_HEADS_PER_STEP
    chained = seq_len > SCAN_CHUNK
    for hd in range(nhead):
        lanes = slice(hd * LRU_BLOCK, (hd + 1) * LRU_BLOCK)
        x = x_ref[:, lanes].astype(F32)
        for c in range(nchunk):
            io_ref[hd, pl.ds(c * SCAN_PITCH, SCAN_CHUNK), :] = x[c * SCAN_CHUNK:(c + 1) * SCAN_CHUNK]
        x = jnp.concatenate(
            [io_ref[hd, pl.ds(t, nchunk, stride=SCAN_PITCH), :] for t in range(SCAN_CHUNK)], axis=0)
        xc = (_shift_time(x, 2, chained) * cw_ref[0:1, lanes] + _shift_time(x, 1, chained) * cw_ref[1:2, lanes]
              + x * cw_ref[2:3, lanes] + _shift_time(x, -1, chained) * cw_ref[3:4, lanes] + cb_ref[:, lanes])
        th = jnp.tanh(_dot(xc.astype(BF16), wg_ref[hd] * 0.5) + 0.5 * bg_ref[hd])
        xh = 0.5 * xc
        lam = lam_ref[hd]
        for d, (a_ref, b_ref) in enumerate(((af_ref, bf_ref), (ab_ref, bb_ref))):
            th_r = th[:, 2 * d * LRU_BLOCK:(2 * d + 1) * LRU_BLOCK]
            th_i = th[:, (2 * d + 1) * LRU_BLOCK:(2 * d + 2) * LRU_BLOCK]
            nl = -lam[:, d * LRU_BLOCK:(d + 1) * LRU_BLOCK]
            softplus = jnp.maximum(nl, 0.0) + jnp.log(1.0 + jnp.exp(-jnp.abs(nl)))
            ch = (-0.5 * LRU_C * LOG2E) * softplus
            a = jnp.exp2(ch + ch * th_r)
            y = 1.0 - a * a
            a_ref[hd] = a
            b_ref[hd] = (y * lax.rsqrt(jnp.maximum(y, 1e-30))) * ((1.0 + th_i) * xh)

    def step(t, carry):
        rf = pl.ds(pl.multiple_of(t * nchunk, nchunk), nchunk)
        rb = pl.ds(pl.multiple_of((SCAN_CHUNK - 1 - t) * nchunk, nchunk), nchunk)
        out = []
        for hd in range(nhead):
            hf, hb, pf, pb = carry[4 * hd:4 * hd + 4]
            a_f = af_ref[hd, rf, :]
            a_b = ab_ref[hd, rb, :]
            hf = a_f * hf + bf_ref[hd, rf, :]
            hb = a_b * hb + bb_ref[hd, rb, :]
            hf_ref[hd, rf, :] = hf
            hb_ref[hd, rb, :] = hb
            if chained:
                pf = pf * a_f
                pb = pb * a_b
                af_ref[hd, rf, :] = pf
                ab_ref[hd, rb, :] = pb
            out += [hf, hb, pf, pb]
        return tuple(out)

    zero = jnp.zeros((nchunk, LRU_BLOCK), F32)
    one = jnp.ones((nchunk, LRU_BLOCK), F32)
    lax.fori_loop(0, SCAN_CHUNK, step, (zero, zero, one, one) * nhead, unroll=8)

    sub = lax.broadcasted_iota(jnp.int32, (nchunk, LRU_BLOCK), 0)
    last = slice(rows - nchunk, rows)
    first = slice(0, nchunk)
    for hd in range(nhead):
        lanes = slice(hd * LRU_BLOCK, (hd + 1) * LRU_BLOCK)
        hf = hf_ref[hd]
        hb = hb_ref[hd]
        if chained:
            pf = af_ref[hd]
            pb = ab_ref[hd]
            ef = jnp.where(sub == 0, h0_ref[0, 0][:, lanes], 0.0)
            eb = jnp.where(sub == nchunk - 1, h0_ref[1, 0][:, lanes], 0.0)
            for c in range(1, nchunk):
                ef = jnp.where(sub == c, pltpu.roll(hf[last] + pf[last] * ef, 1, 0), ef)
                eb = jnp.where(sub == nchunk - 1 - c,
                               pltpu.roll(hb[first] + pb[first] * eb, nchunk - 1, 0), eb)
            hf = (hf.reshape(SCAN_CHUNK, nchunk, LRU_BLOCK)
                  + pf.reshape(SCAN_CHUNK, nchunk, LRU_BLOCK) * ef[None]).reshape(rows, LRU_BLOCK)
            hb = (hb.reshape(SCAN_CHUNK, nchunk, LRU_BLOCK)
                  + pb.reshape(SCAN_CHUNK, nchunk, LRU_BLOCK) * eb[None]).reshape(rows, LRU_BLOCK)
        else:
            fin_ref[0, :, lanes] = hf[last]
            fin_ref[1, :, lanes] = hb[first]
        y = hf + hb
        for t in range(SCAN_CHUNK):
            io_ref[hd, pl.ds(t, nchunk, stride=SCAN_PITCH), :] = y[t * nchunk:(t + 1) * nchunk]
        for c in range(nchunk):
            y_ref[pl.ds(c * SCAN_CHUNK, SCAN_CHUNK), lanes] = (
                io_ref[hd, pl.ds(c * SCAN_PITCH, SCAN_CHUNK), :].astype(y_ref.dtype))


def _lru(proj, lp, layer, seq_len, h0):
    m = proj.shape[0]
    latent = h0 is not None
    rows = SCAN_ROWS
    nhead = LRU_HEADS_PER_STEP
    width = nhead * LRU_BLOCK
    in_specs = [
        pl.BlockSpec((rows, width), lambda r, h: (r, COL_XA // width + h)),
        pl.BlockSpec((None, 4, width), lambda r, h: (layer, 0, h)),
        pl.BlockSpec((None, 1, width), lambda r, h: (layer, 0, h)),
        pl.BlockSpec((None, nhead, LRU_BLOCK, 4 * LRU_BLOCK), lambda r, h: (layer, h, 0, 0)),
        pl.BlockSpec((None, nhead, 1, 4 * LRU_BLOCK), lambda r, h: (layer, h, 0, 0)),
        pl.BlockSpec((None, nhead, 1, 2 * LRU_BLOCK), lambda r, h: (layer, h, 0, 0)),
    ]
    args = [proj, lp["conv_w"], lp["conv_b"], lp["wg"], lp["bg"], lp["lam"]]
    y_spec = pl.BlockSpec((rows, width), lambda r, h: (r, h))
    y_shape = jax.ShapeDtypeStruct((m, D_RNN), BF16)
    if latent:
        in_specs.append(pl.BlockSpec((None, 2, 1, 1, width), lambda r, h: (layer, 0, r, 0, h)))
        args.append(h0)
        out_specs, out_shape = y_spec, y_shape
    else:
        nseq = m // seq_len
        out_specs = (y_spec, pl.BlockSpec((2, rows // seq_len, width), lambda r, h: (0, r, h)))
        out_shape = (y_shape, jax.ShapeDtypeStruct((2, nseq, D_RNN), F32))
    strided_buf = pltpu.VMEM((nhead, rows // SCAN_CHUNK * SCAN_PITCH, LRU_BLOCK), F32)
    scan_buf = pltpu.VMEM((nhead, rows, LRU_BLOCK), F32)
    return pl.pallas_call(
        functools.partial(_lru_tm_kernel, seq_len=seq_len, latent=latent),
        grid=(m // rows, LRU_HEADS // nhead),
        in_specs=in_specs,
        out_specs=out_specs,
        out_shape=out_shape,
        scratch_shapes=[strided_buf] + [scan_buf] * 6,
        compiler_params=_params("parallel", "parallel"),
        name="rglru_latent" if latent else "rglru_context",
    )(*args)


SOFTMAX_SCALE = HEAD_DIM ** -0.5 * LOG2E


def _stack_heads(q_ref, kh, r0, rows):
    parts = [q_ref[r0:r0 + rows, (kh * KV_GROUPS + g) * HEAD_DIM:(kh * KV_GROUPS + g + 1) * HEAD_DIM]
             for g in range(KV_GROUPS)]
    return jnp.concatenate(parts, axis=0)


def _dot_tn(a, b):
    return lax.dot_general(a, b, (((0,), (0,)), ((), ())), preferred_element_type=F32)


def _sink_row(sink_ref, layer, kh, cols):
    parts = [jnp.full((1, cols), sink_ref[layer, kh * KV_GROUPS + g] * LOG2E, F32) for g in range(KV_GROUPS)]
    return jnp.concatenate(parts, axis=1)


def _softmax_pv_t(t, sink, v):
    m = jnp.maximum(jnp.max(t, axis=0, keepdims=True), sink)
    e = jnp.exp2(t - m)
    denom = jnp.sum(e, axis=0, keepdims=True) + jnp.exp2(sink - m)
    return _dot_tn(v, e.astype(BF16)) * (1.0 / denom)


def _store_heads_t(y_ref, o_t, kh, r0, rows):
    for g in range(KV_GROUPS):
        h = kh * KV_GROUPS + g
        y_ref[r0:r0 + rows, h * HEAD_DIM:(h + 1) * HEAD_DIM] = (
            o_t[:, g * rows:(g + 1) * rows].T.astype(y_ref.dtype))


def _attn_ctx_kernel(sink_ref, q_ref, kv_ref, y_ref, *, seq_len, layer):
    for b in range(q_ref.shape[0] // seq_len):
        r0 = b * seq_len
        for kh in range(N_KV_HEADS):
            sl = slice(kh * HEAD_DIM, (kh + 1) * HEAD_DIM)
            vl = slice(D_KV + kh * HEAD_DIM, D_KV + (kh + 1) * HEAD_DIM)
            q = _stack_heads(q_ref, kh, r0, seq_len)
            t = _dot_nt(kv_ref[r0:r0 + seq_len, sl], q) * SOFTMAX_SCALE
            o_t = _softmax_pv_t(t, _sink_row(sink_ref, layer, kh, seq_len), kv_ref[r0:r0 + seq_len, vl])
            _store_heads_t(y_ref, o_t, kh, r0, seq_len)


ATTN_CTX_SEQS = 4


def _attn_ctx(proj, sink, layer, seq_len):
    m = proj.shape[0]
    rows = ATTN_CTX_SEQS * seq_len
    return pl.pallas_call(
        functools.partial(_attn_ctx_kernel, seq_len=seq_len, layer=layer),
        grid=(m // rows,),
        in_specs=[
            pl.BlockSpec(memory_space=pltpu.SMEM),
            pl.BlockSpec((rows, N_HEADS * HEAD_DIM), lambda b: (b, COL_Q // (N_HEADS * HEAD_DIM))),
            pl.BlockSpec((rows, 2 * D_KV), lambda b: (b, COL_K // (2 * D_KV))),
        ],
        out_specs=pl.BlockSpec((rows, N_HEADS * HEAD_DIM), lambda b: (b, 0)),
        out_shape=jax.ShapeDtypeStruct((m, N_HEADS * HEAD_DIM), BF16),
        compiler_params=_params("parallel"),
        name="attn_context",
    )(sink, proj, proj)


def _rope_tables(seq_len):
    nf = HEAD_DIM // 4
    freqs = ROPE_BASE ** (-np.arange(nf, dtype=np.float64) / nf)
    t = np.arange(seq_len)
    ang_row = (t // GRID_W)[:, None] * freqs[None, :]
    ang_col = (t % GRID_W)[:, None] * freqs[None, :]
    ang = np.concatenate([ang_row, ang_row, ang_col, ang_col], axis=1)
    first = (np.arange(HEAD_DIM) % (2 * nf)) < nf
    cos = np.cos(ang)
    sin = np.sin(ang)
    sin_a = np.where(first[None, :], -sin, 0.0)
    sin_b = np.where(first[None, :], 0.0, sin)
    return tuple(jnp.asarray(a, F32) for a in (cos, sin_a, sin_b))


ATTN_LAT_BLOCKS = 4


def _attn_lat_kernel(sink_ref, q_ref, kvp_ref, kvc_ref, kvn_ref, ck_ref, cv_ref, y_ref, *, nblk, layer):
    step = pl.program_id(1)
    cols = KV_GROUPS * BLOCK_Q
    span = 3 * BLOCK_Q
    key = lax.broadcasted_iota(jnp.int32, (span, cols), 0)
    qry = lax.broadcasted_iota(jnp.int32, (span, cols), 1) & (BLOCK_Q - 1)
    kv = jnp.concatenate([kvp_ref[...], kvc_ref[...], kvn_ref[...]], axis=0)
    for i in range(ATTN_LAT_BLOCKS):
        j = step * ATTN_LAT_BLOCKS + i
        lo = jnp.where(j > 0, qry, BLOCK_Q)
        hi = jnp.where(j < nblk - 1, qry + 2 * BLOCK_Q, 2 * BLOCK_Q - 1)
        bias = jnp.where(jnp.logical_and(key >= lo, key <= hi), 0.0, NEG_INF)
        win = kv[i * BLOCK_Q:i * BLOCK_Q + span]
        for kh in range(N_KV_HEADS):
            sl = slice(kh * HEAD_DIM, (kh + 1) * HEAD_DIM)
            vl = slice(D_KV + kh * HEAD_DIM, D_KV + (kh + 1) * HEAD_DIM)
            q = _stack_heads(q_ref, kh, i * BLOCK_Q, BLOCK_Q)
            keys = jnp.concatenate([win[:, sl], ck_ref[0, :, sl]], axis=0)
            vals = jnp.concatenate([win[:, vl], cv_ref[0, :, sl]], axis=0)
            t = _dot_nt(keys, q) * SOFTMAX_SCALE
            t = jnp.concatenate([t[:span] + bias, t[span:]], axis=0)
            o_t = _softmax_pv_t(t, _sink_row(sink_ref, layer, kh, BLOCK_Q), vals)
            _store_heads_t(y_ref, o_t, kh, i * BLOCK_Q, BLOCK_Q)


def _attn_lat(proj, ck, cv, sink, layer, seq_len):
    m = proj.shape[0]
    nblk = seq_len // BLOCK_Q
    nstep = nblk // ATTN_LAT_BLOCKS
    rows = ATTN_LAT_BLOCKS * BLOCK_Q
    nb = m // seq_len
    past = ck.shape[2]
    kv_col = COL_K // (2 * D_KV)
    assert COL_V == COL_K + D_KV and COL_K % (2 * D_KV) == 0

    def halo(shift):
        def index(b, s):
            return (b * nblk + jnp.clip(s * ATTN_LAT_BLOCKS + shift, 0, nblk - 1), kv_col)
        return index

    return pl.pallas_call(
        functools.partial(_attn_lat_kernel, nblk=nblk, layer=layer),
        grid=(nb, nstep),
        in_specs=[
            pl.BlockSpec(memory_space=pltpu.SMEM),
            pl.BlockSpec((rows, N_HEADS * HEAD_DIM), lambda b, s: (b * nstep + s, COL_Q // (N_HEADS * HEAD_DIM))),
            pl.BlockSpec((BLOCK_Q, 2 * D_KV), halo(-1)),
            pl.BlockSpec((rows, 2 * D_KV), lambda b, s: (b * nstep + s, kv_col)),
            pl.BlockSpec((BLOCK_Q, 2 * D_KV), halo(ATTN_LAT_BLOCKS)),
            pl.BlockSpec((1, None, past, D_KV), lambda b, s: (b, layer, 0, 0)),
            pl.BlockSpec((1, None, past, D_KV), lambda b, s: (b, layer, 0, 0)),
        ],
        out_specs=pl.BlockSpec((rows, N_HEADS * HEAD_DIM), lambda b, s: (b * nstep + s, 0)),
        out_shape=jax.ShapeDtypeStruct((m, N_HEADS * HEAD_DIM), BF16),
        compiler_params=_params("parallel", "parallel"),
        name="attn_latent",
    )(sink, proj, proj, proj, proj, ck, cv)


POOL_TILE = 256
POOL_LEAD = BF16_ROWS
POOL_ROWS = 2048


def _pool_plan(seq_len):
    lead = 0 if seq_len == POOL_TILE else POOL_LEAD
    return lead, POOL_TILE - 2 * lead


def _pool_bands(seq_len):
    lead, nout = _pool_plan(seq_len)
    r = np.arange(nout)[:, None]
    c = np.arange(POOL_TILE)[None, :] - lead
    bands = [(c >= r - win // 2) & (c < r + win // 2) for win in POOL_WINDOWS]
    return jnp.asarray(np.stack(bands), BF16)


def _pool_kernel(x0_ref, x1_ref, x2_ref, x3_ref, band_ref, w_ref, s_ref, y_ref, pad_ref, *, seq_len):
    lead, nout = _pool_plan(seq_len)
    edge = lax.broadcasted_iota(jnp.int32, (SUBLANES, POOL_GROUP), 0)
    if lead:
        pad_ref[0:lead, :] = jnp.zeros((lead, POOL_GROUP), BF16)
        pad_ref[lead + seq_len:, :] = jnp.zeros((pad_ref.shape[0] - lead - seq_len, POOL_GROUP), BF16)
    for gi, (win, x_ref) in enumerate(zip(POOL_WINDOWS, (x0_ref, x1_ref, x2_ref, x3_ref))):
        cs = slice(gi * POOL_GROUP, (gi + 1) * POOL_GROUP)
        half = win // 2
        inv_head = 1.0 / ((edge + half) - jnp.maximum(edge - half, 0)).astype(F32)
        inv_tail = 1.0 / (jnp.minimum(SUBLANES - edge, half) + half).astype(F32)
        if lead:
            pad_ref[lead:lead + seq_len, :] = x_ref[...]
        for base in range(0, POOL_ROWS, seq_len):
            for p0 in range(0, seq_len, nout):
                n = min(nout, seq_len - p0)
                x = x_ref[base + p0:base + p0 + n, :]
                src = pad_ref[p0:p0 + POOL_TILE, :] if lead else x
                sums = _dot(band_ref[gi, :n, :], src)
                head = sums[:SUBLANES] * (inv_head if p0 == 0 else 1.0 / win)
                tail = sums[n - SUBLANES:] * (inv_tail if p0 + n == seq_len else 1.0 / win)
                mean = jnp.concatenate([head, sums[SUBLANES:n - SUBLANES] * (1.0 / win), tail], axis=0)
                pooled = (mean - x.astype(F32)).astype(BF16)
                y_ref[base + p0:base + p0 + n, cs] = (
                    _dot(pooled, w_ref[gi]) * s_ref[:, cs]).astype(y_ref.dtype)


def _pool(proj, w, s, layer, seq_len):
    m = proj.shape[0]
    rows = POOL_ROWS
    lead, nout = _pool_plan(seq_len)
    assert seq_len in (POOL_TILE, rows) and max(POOL_WINDOWS) // 2 <= min(SUBLANES, lead or SUBLANES)
    pad_rows = (pl.cdiv(seq_len, nout) - 1) * nout + POOL_TILE
    group = lambda gi: pl.BlockSpec((rows, POOL_GROUP), lambda r: (r, COL_XC // POOL_GROUP + gi))
    whole = lambda shape: pl.BlockSpec(shape, lambda r: (0,) * len(shape))
    nwin = len(POOL_WINDOWS)
    return pl.pallas_call(
        functools.partial(_pool_kernel, seq_len=seq_len),
        grid=(m // rows,),
        in_specs=[
            group(0), group(1), group(2), group(3),
            whole((nwin, nout, POOL_TILE)),
            _resident((nwin, POOL_GROUP, POOL_GROUP), layer),
            _resident((1, D_POOL), layer),
        ],
        out_specs=pl.BlockSpec((rows, D_POOL), lambda r: (r, 0)),
        out_shape=jax.ShapeDtypeStruct((m, D_POOL), BF16),
        scratch_shapes=[pltpu.VMEM((pad_rows, POOL_GROUP), BF16)],
        compiler_params=_params("parallel"),
        name="pool_mix",
    )(proj, proj, proj, proj, _pool_bands(seq_len), w, s)


def _merge_kernel(ya_ref, yb_ref, yc_ref, g0_ref, g1_ref, g2_ref, g3_ref, g4_ref, g5_ref, x_ref, mod_ref,
                  bg_ref, wb_ref, wo_ref, n2_ref, x1_ref, h2_ref):
    mod = mod_ref[0]
    g_refs = (g0_ref, g1_ref, g2_ref, g3_ref, g4_ref, g5_ref)
    half = D_MODEL // 2
    merged = None
    for k, y_ref in enumerate((ya_ref, yb_ref, yc_ref)):
        y = _dot(y_ref[...], wb_ref[k])
        parts = []
        for p in range(2):
            z = g_refs[2 * k + p][...].astype(F32) + bg_ref[:, k * D_MODEL + p * half:k * D_MODEL + (p + 1) * half]
            parts.append((1.0 + jnp.tanh(0.5 * z)) * y[:, p * half:(p + 1) * half])
        term = jnp.concatenate(parts, axis=1)
        merged = term if merged is None else merged + term
    merged = 0.5 * merged
    gate1 = mod[:, 2 * D_MODEL:3 * D_MODEL]
    x1 = x_ref[...] + gate1 * _dot(merged.astype(BF16), wo_ref[...])
    x1_ref[...] = x1
    shift2 = mod[:, 3 * D_MODEL:4 * D_MODEL]
    scale2 = mod[:, 4 * D_MODEL:5 * D_MODEL]
    h2_ref[...] = (_rms(x1) * n2_ref[...] * (1.0 + scale2) + shift2).astype(BF16)


def _merge(ya, yb, yc, proj, x, mod, mod_row0, rows_per_mod, bg, wb, wo, n2, layer):
    m = x.shape[0]
    tm = 512
    per = rows_per_mod // tm
    half = D_MODEL // 2
    row = pl.BlockSpec((tm, D_MODEL), lambda i: (i, 0))
    gate = lambda c: pl.BlockSpec((tm, half), lambda i: (i, COL_G // half + c))
    const = lambda shape: pl.BlockSpec(shape, lambda i: (0,) * len(shape), pipeline_mode=pl.Buffered(1))
    return pl.pallas_call(
        _merge_kernel,
        grid=(m // tm,),
        in_specs=[
            row, row, row,
            gate(0), gate(1), gate(2), gate(3), gate(4), gate(5),
            row,
            _mod_spec(mod_row0, per),
            _resident((1, N_BRANCH * D_MODEL), layer),
            _resident((N_BRANCH, D_MODEL, D_MODEL), layer),
            _resident((D_MODEL, D_MODEL), layer),
            _resident((1, D_MODEL), layer),
        ],
        out_specs=(row, row),
        out_shape=(jax.ShapeDtypeStruct((m, D_MODEL), F32), jax.ShapeDtypeStruct((m, D_MODEL), BF16)),
        compiler_params=_params("parallel"),
        name="merge_out",
    )(ya, yb, yc, proj, proj, proj, proj, proj, proj, x, mod, bg, wb, wo, n2)


FFN_CK = 256
FFN_TM = 512
FFN_GAP = SUBLANES
GELU_C = float(np.sqrt(2.0 / np.pi))


def _ffn_kernel(*refs, seq_len, final):
    refs = list(refs)
    h_ref, hp_ref, hn_ref, x_ref, mod_ref, wup_ref, cw_ref, cb_ref, wd_ref = refs[:9]
    fn_ref = refs[9] if final else None
    o_ref, hx_ref, u_ref, act_ref = refs[-4:]
    tm = FFN_TM
    halo = seq_len > tm
    i = pl.program_id(0)

    if halo:
        per_seq = seq_len // tm
        at_start = i % per_seq == 0
        at_end = i % per_seq == per_seq - 1
        zeros = jnp.zeros((BF16_ROWS, D_MODEL), BF16)

        @pl.when(at_start)
        def _():
            hx_ref[0:BF16_ROWS, :] = zeros

        @pl.when(jnp.logical_not(at_start))
        def _():
            hx_ref[0:BF16_ROWS, :] = hp_ref[...]

        @pl.when(at_end)
        def _():
            hx_ref[BF16_ROWS + tm:, :] = zeros

        @pl.when(jnp.logical_not(at_end))
        def _():
            hx_ref[BF16_ROWS + tm:, :] = hn_ref[...]

        hx_ref[BF16_ROWS:BF16_ROWS + tm, :] = h_ref[...]
        bases = (BF16_ROWS,)
        seg = tm
    else:
        nseg = tm // seq_len
        seg = seq_len
        bases = tuple(FFN_GAP + s * (seg + FFN_GAP) for s in range(nseg))
        for s in range(nseg + 1):
            u_ref[s * (seg + FFN_GAP):s * (seg + FFN_GAP) + FFN_GAP, :] = jnp.zeros((FFN_GAP, FFN_CK), F32)

    def taps(offset):
        return jnp.concatenate([u_ref[b + offset:b + offset + seg, :] for b in bases], axis=0)

    for c in range(D_FF // FFN_CK):
        cs = slice(c * FFN_CK, (c + 1) * FFN_CK)
        vs = slice(D_FF + c * FFN_CK, D_FF + (c + 1) * FFN_CK)
        if halo:
            u_ext = _dot(hx_ref[...], wup_ref[:, cs])
            u_ref[...] = u_ext
            u0 = u_ext[BF16_ROWS:BF16_ROWS + tm]
        else:
            u0 = _dot(h_ref[...], wup_ref[:, cs])
            for s, b in enumerate(bases):
                u_ref[b:b + seg, :] = u0[s * seg:(s + 1) * seg]
        uv = _dot(h_ref[...], wup_ref[:, vs])
        gff = taps(-1) * cw_ref[0:1, cs] + u0 * cw_ref[1:2, cs] + taps(1) * cw_ref[2:3, cs] + cb_ref[:, cs]
        inner = gff * (GELU_C + (GELU_C * 0.044715) * (gff * gff))
        act_ref[:, cs] = (0.5 * (gff * uv) * (1.0 + jnp.tanh(inner))).astype(BF16)

    gate2 = mod_ref[0][:, 5 * D_MODEL:6 * D_MODEL]
    out = x_ref[...] + gate2 * _dot(act_ref[...], wd_ref[...])
    if final:
        out = _rms(out) * fn_ref[...]
    o_ref[...] = out


def _ffn(h2, x1, mod, mod_row0, rows_per_mod, wup, cw, cb, wd, layer, seq_len, final_norm):
    m = x1.shape[0]
    tm = FFN_TM
    per = rows_per_mod // tm
    hb = tm // BF16_ROWS
    last_halo = m // BF16_ROWS - 1
    final = final_norm is not None
    halo = seq_len > tm
    const = lambda shape: pl.BlockSpec(shape, lambda i: (0,) * len(shape), pipeline_mode=pl.Buffered(1))
    in_specs = [
        pl.BlockSpec((tm, D_MODEL), lambda i: (i, 0)),
        pl.BlockSpec((BF16_ROWS, D_MODEL), lambda i: (jnp.maximum(i * hb - 1, 0), 0)),
        pl.BlockSpec((BF16_ROWS, D_MODEL), lambda i: (jnp.minimum((i + 1) * hb, last_halo), 0)),
        pl.BlockSpec((tm, D_MODEL), lambda i: (i, 0)),
        _mod_spec(mod_row0, per),
        _resident((D_MODEL, 2 * D_FF), layer),
        _resident((3, D_FF), layer),
        _resident((1, D_FF), layer),
        _resident((D_FF, D_MODEL), layer),
    ]
    args = [h2, h2, h2, x1, mod, wup, cw, cb, wd]
    if final:
        in_specs.append(const((1, D_MODEL)))
        args.append(final_norm)
    if halo:
        u_rows = tm + 2 * BF16_ROWS
    else:
        u_rows = FFN_GAP + (tm // seq_len) * (seq_len + FFN_GAP)
    return pl.pallas_call(
        functools.partial(_ffn_kernel, seq_len=seq_len, final=final),
        grid=(m // tm,),
        in_specs=in_specs,
        out_specs=pl.BlockSpec((tm, D_MODEL), lambda i: (i, 0)),
        out_shape=jax.ShapeDtypeStruct((m, D_MODEL), F32),
        scratch_shapes=[
            pltpu.VMEM((tm + 2 * BF16_ROWS, D_MODEL), BF16),
            pltpu.VMEM((u_rows, FFN_CK), F32),
            pltpu.VMEM((tm, D_FF), BF16),
        ],
        compiler_params=_params("parallel"),
        name="conv_glu_ffn",
    )(*args)


def _trunk_layer(x, mod, mod_row0, rows_per_mod, p, layer, seq_len, ctx, final_norm):
    proj, k_new, v_new = _inproj(x, mod, mod_row0, rows_per_mod, p["norm1"], p["w_in"], layer, seq_len,
                                 want_kv=ctx is None)
    if ctx is None:
        ya, h_fin = _lru(proj, p["lru"], layer, seq_len, None)
        yb = _attn_ctx(proj, p["sink"], layer, seq_len)
    else:
        ck, cv, h0 = ctx
        ya = _lru(proj, p["lru"], layer, seq_len, h0)
        h_fin = None
        yb = _attn_lat(proj, ck, cv, p["sink"], layer, seq_len)
    yc = _pool(proj, p["pool_w"], p["pool_scale"], layer, seq_len)
    x1, h2 = _merge(ya, yb, yc, proj, x, mod, mod_row0, rows_per_mod, p["b_gate"], p["w_branch"], p["w_out"],
                    p["norm2"], layer)
    out = _ffn(h2, x1, mod, mod_row0, rows_per_mod, p["ffn_up"], p["ffn_conv"], p["ffn_conv_b"],
               p["ffn_down"], layer, seq_len, final_norm)
    return out, k_new, v_new, h_fin


def _stacked_params(norm1, norm2, w_in, b_gate, lru_conv, lru_conv_b, lru_wa, lru_ba, lru_wx, lru_bx,
                    lru_lambda, attn_sink, pool_w, pool_scale, w_branch, w_out, ffn_up, ffn_conv,
                    ffn_conv_b, ffn_down):
    def per_head(v):
        return v.reshape(DEPTH, 2, LRU_HEADS, LRU_BLOCK).transpose(0, 2, 1, 3)

    row = lambda v: v[:, None, :]
    wg = jnp.concatenate([lru_wa[:, 0], lru_wx[:, 0], lru_wa[:, 1], lru_wx[:, 1]], axis=-1).astype(BF16)
    ba = per_head(lru_ba)
    bx = per_head(lru_bx)
    bg = jnp.concatenate([ba[:, :, 0], bx[:, :, 0], ba[:, :, 1], bx[:, :, 1]], axis=-1)[:, :, None, :]
    lam = per_head(lru_lambda).reshape(DEPTH, LRU_HEADS, 1, 2 * LRU_BLOCK)
    return {
        "norm1": row(norm1), "norm2": row(norm2), "b_gate": row(b_gate),
        "lru": {"conv_w": lru_conv, "conv_b": row(lru_conv_b), "wg": wg, "bg": bg, "lam": lam},
        "sink": attn_sink, "pool_w": pool_w.astype(BF16), "pool_scale": row(pool_scale),
        "ffn_conv": ffn_conv, "ffn_conv_b": row(ffn_conv_b),
        "w_in": w_in.astype(BF16), "w_branch": w_branch.astype(BF16), "w_out": w_out.astype(BF16),
        "ffn_up": ffn_up.astype(BF16), "ffn_down": ffn_down.astype(BF16),
    }


def kernel(x_prompt, x_sample, cache_k, cache_v, state_lru, c, c_ctx, w_ada, b_ada, norm1, norm2, w_in,
           b_gate, lru_conv, lru_conv_b, lru_wa, lru_ba, lru_wx, lru_bx, lru_lambda, attn_sink, pool_w,
           pool_scale, w_branch, w_out, ffn_up, ffn_conv, ffn_conv_b, ffn_down, final_norm):
    batch, seq, _ = x_prompt.shape
    dec_batch, dec_seq, _ = x_sample.shape
    past = cache_k.shape[2]
    assert seq == SCAN_CHUNK and dec_seq % SCAN_ROWS == 0 and (batch * seq) % SCAN_ROWS == 0

    c_rows = jnp.concatenate(
        [c_ctx[None], c, jnp.zeros((SUBLANES - 1 - dec_batch, D_MODEL), F32)], axis=0)
    mods = _ada(c_rows, w_ada, b_ada)

    xp = x_prompt.reshape(batch * seq, D_MODEL)
    xs = x_sample.reshape(dec_batch * dec_seq, D_MODEL)
    fn = final_norm[None]
    p = _stacked_params(norm1, norm2, w_in, b_gate, lru_conv, lru_conv_b, lru_wa, lru_ba, lru_wx, lru_bx,
                        lru_lambda, attn_sink, pool_w, pool_scale, w_branch, w_out, ffn_up, ffn_conv,
                        ffn_conv_b, ffn_down)
    mod_rows = mods.reshape(DEPTH * SUBLANES, 1, 6 * D_MODEL)
    ck = cache_k.reshape(dec_batch, DEPTH, past, D_KV).astype(BF16)
    cv = cache_v.reshape(dec_batch, DEPTH, past, D_KV).astype(BF16)
    h0 = state_lru.transpose(1, 2, 0, 3)[:, :, :, None, :]
    ks, vs, hs = [], [], []
    for l in range(DEPTH):
        last = fn if l == DEPTH - 1 else None
        xp, k_new, v_new, h_fin = _trunk_layer(xp, mod_rows, l * SUBLANES, batch * seq, p, l, seq, None, last)
        xs, _, _, _ = _trunk_layer(xs, mod_rows, l * SUBLANES + 1, dec_seq, p, l, dec_seq, (ck, cv, h0), last)
        ks.append(k_new.reshape(batch, seq, N_KV_HEADS, HEAD_DIM))
        vs.append(v_new.reshape(batch, seq, N_KV_HEADS, HEAD_DIM))
        hs.append(h_fin.transpose(1, 0, 2))
    y_prompt = xp.reshape(batch, seq, D_MODEL)
    y_sample = xs.reshape(dec_batch, dec_seq, D_MODEL)
    return (y_prompt, y_sample, jnp.stack(ks, axis=1), jnp.stack(vs, axis=1), jnp.stack(hs, axis=1))
```

```python
import functools

import numpy as np
import jax
import jax.numpy as jnp
from jax import lax
from jax.experimental import pallas as pl
from jax.experimental.pallas import tpu as pltpu

F32 = jnp.float32
BF16 = jnp.bfloat16

D_MODEL = 1024
DEPTH = 2
GRID_W = 64
EPS = 1e-6
N_BRANCH = 3
D_RNN = 1024
LRU_HEADS = 8
LRU_BLOCK = D_RNN // LRU_HEADS
LRU_C = 8.0
N_HEADS = 8
N_KV_HEADS = 2
KV_GROUPS = N_HEADS // N_KV_HEADS
HEAD_DIM = 128
D_KV = N_KV_HEADS * HEAD_DIM
WINDOW = 128
BLOCK_Q = 128
ROPE_BASE = 10000.0
NEG_INF = -1e30
D_POOL = 1024
POOL_WINDOWS = (2, 4, 8, 16)
POOL_GROUP = D_POOL // len(POOL_WINDOWS)
D_FF = 2816
D_IN = D_RNN + N_HEADS * HEAD_DIM + 2 * D_KV + D_POOL + N_BRANCH * D_MODEL

COL_XA = 0
COL_Q = COL_XA + D_RNN
COL_K = COL_Q + N_HEADS * HEAD_DIM
COL_V = COL_K + D_KV
COL_XC = COL_V + D_KV
COL_G = COL_XC + D_POOL
LOG2E = float(np.log2(np.e))

VMEM_LIMIT_BYTES = 52 * 1024 * 1024
SUBLANES = 8
LANES = 128
BF16_ROWS = 16

SCAN_CHUNK = 256
SCAN_PITCH = 260
SCAN_ROWS = 2048
POOL_PAD = 8


def _params(*sem):
    return pltpu.CompilerParams(dimension_semantics=sem, vmem_limit_bytes=VMEM_LIMIT_BYTES)


def _dot(a, b):
    return jnp.dot(a, b, preferred_element_type=F32)


def _dot_nt(a, b):
    return lax.dot_general(a, b, (((1,), (1,)), ((), ())), preferred_element_type=F32)


def _sigmoid(z):
    return 0.5 * (1.0 + jnp.tanh(0.5 * z))


def _rms(x):
    return x * lax.rsqrt(jnp.mean(x * x, axis=-1, keepdims=True) + EPS)


def _ada_kernel(c_ref, w_ref, b_ref, o_ref):
    c = c_ref[...]
    s = c * _sigmoid(c)
    o_ref[0] = _dot(s.astype(BF16), w_ref[0].astype(BF16)) + b_ref[0]


def _ada(c_rows, w_ada, b_ada):
    tn = 1536
    return pl.pallas_call(
        _ada_kernel,
        grid=(DEPTH, 6 * D_MODEL // tn),
        in_specs=[
            pl.BlockSpec((SUBLANES, D_MODEL), lambda l, j: (0, 0)),
            pl.BlockSpec((1, D_MODEL, tn), lambda l, j: (l, 0, j)),
            pl.BlockSpec((1, 1, tn), lambda l, j: (l, 0, j)),
        ],
        out_specs=pl.BlockSpec((1, SUBLANES, tn), lambda l, j: (l, 0, j)),
        out_shape=jax.ShapeDtypeStruct((DEPTH, SUBLANES, 6 * D_MODEL), F32),
        compiler_params=_params("parallel", "parallel"),
        name="ada_mod",
    )(c_rows, w_ada, b_ada.reshape(DEPTH, 1, 6 * D_MODEL))


INPROJ_TM = 512
INPROJ_CK = 2 * D_KV


def _resident(shape, layer):
    ndim = len(shape)
    return pl.BlockSpec((None,) + tuple(shape), lambda *_: (layer,) + (0,) * ndim,
                        pipeline_mode=pl.Buffered(1))


def _mod_spec(row0, per):
    return pl.BlockSpec((1, 1, 6 * D_MODEL), lambda i: (row0 + i // per, 0, 0))


def _inproj_kernel(*refs, seq_len, want_kv, layer, owns_cache):
    if want_kv:
        x_ref, mod_ref, g_ref, w_ref = refs[:4]
        o_ref, k_ref, v_ref = refs[-3:]
        if owns_cache:
            for other in range(DEPTH):
                if other != layer:
                    k_ref[:, other] = jnp.zeros(k_ref.shape[:1] + k_ref.shape[2:], F32)
                    v_ref[:, other] = jnp.zeros(v_ref.shape[:1] + v_ref.shape[2:], F32)
    else:
        x_ref, mod_ref, g_ref, w_ref, cos_ref, sa_ref, sb_ref, o_ref = refs
    mod = mod_ref[0]
    shift = mod[:, 0:D_MODEL]
    scale = mod[:, D_MODEL:2 * D_MODEL]
    h = (_rms(x_ref[...]) * g_ref[...] * (1.0 + scale) + shift).astype(BF16)
    nf = HEAD_DIM // 4

    def rope(x):
        return (x * cos_ref[...] + pltpu.roll(x, HEAD_DIM - nf, 1) * sa_ref[...]
                + pltpu.roll(x, nf, 1) * sb_ref[...])

    for c in range(D_IN // INPROJ_CK):
        acc = _dot(h, w_ref[:, c * INPROJ_CK:(c + 1) * INPROJ_CK])
        if not want_kv:
            lo = c * INPROJ_CK
            heads = [acc[:, j * HEAD_DIM:(j + 1) * HEAD_DIM] for j in range(INPROJ_CK // HEAD_DIM)]
            heads = [rope(hd) if COL_Q <= lo + j * HEAD_DIM < COL_V else hd for j, hd in enumerate(heads)]
            acc = jnp.concatenate(heads, axis=1)
        o_ref[:, c * INPROJ_CK:(c + 1) * INPROJ_CK] = acc.astype(o_ref.dtype)
        if want_kv and c == COL_K // INPROJ_CK:
            for b in range(INPROJ_TM // seq_len):
                rows = slice(b * seq_len, (b + 1) * seq_len)
                for hd in range(N_KV_HEADS):
                    dst = pl.ds(hd, seq_len, stride=N_KV_HEADS)
                    at = (b, layer, dst, slice(None)) if owns_cache else (b, dst, slice(None))
                    k_ref[at] = acc[rows, hd * HEAD_DIM:(hd + 1) * HEAD_DIM]
                    v_ref[at] = acc[rows, D_KV + hd * HEAD_DIM:D_KV + (hd + 1) * HEAD_DIM]


def _inproj(x, mod, mod_row0, rows_per_mod, g, w, layer, seq_len, want_kv, caches=None):
    m = x.shape[0]
    tm = INPROJ_TM
    per = rows_per_mod // tm
    proj_spec = pl.BlockSpec((tm, D_IN), lambda i: (i, 0))
    proj_shape = jax.ShapeDtypeStruct((m, D_IN), BF16)
    aliases = {}
    owns_cache = want_kv and caches is None
    if want_kv:
        nb = tm // seq_len
        if owns_cache:
            cache_spec = pl.BlockSpec((nb, DEPTH, seq_len * N_KV_HEADS, HEAD_DIM), lambda i: (i, 0, 0, 0))
        else:
            cache_spec = pl.BlockSpec((nb, None, seq_len * N_KV_HEADS, HEAD_DIM), lambda i: (i, layer, 0, 0))
        cache_shape = jax.ShapeDtypeStruct((m // seq_len, DEPTH, seq_len * N_KV_HEADS, HEAD_DIM), F32)
        out_specs = (proj_spec, cache_spec, cache_spec)
        out_shape = (proj_shape, cache_shape, cache_shape)
        extra_specs, extra_args = [], []
        if caches is not None:
            extra_specs = [pl.BlockSpec(memory_space=pl.ANY)] * 2
            extra_args = list(caches)
            aliases = {4: 1, 5: 2}
    else:
        out_specs, out_shape = proj_spec, proj_shape
        tab = pl.BlockSpec((tm, HEAD_DIM), lambda i: (i % (seq_len // tm), 0))
        extra_specs, extra_args = [tab, tab, tab], list(_rope_tables(seq_len))
    out = pl.pallas_call(
        functools.partial(_inproj_kernel, seq_len=seq_len, want_kv=want_kv, layer=layer, owns_cache=owns_cache),
        grid=(m // tm,),
        in_specs=[
            pl.BlockSpec((tm, D_MODEL), lambda i: (i, 0)),
            _mod_spec(mod_row0, per),
            _resident((1, D_MODEL), layer),
            _resident((D_MODEL, D_IN), layer),
        ] + extra_specs,
        out_specs=out_specs,
        out_shape=out_shape,
        input_output_aliases=aliases,
        compiler_params=_params("parallel"),
        name="in_proj",
    )(x, mod, g, w, *extra_args)
    return out if want_kv else (out, None, None)


LRU_HEADS_PER_STEP = 2


def _chunk_neighbour(v, towards_later):
    sub = lax.broadcasted_iota(jnp.int32, v.shape, 0)
    if towards_later:
        return jnp.where(sub >= 1, pltpu.roll(v, 1, 0), 0.0)
    return jnp.where(sub <= SUBLANES - 2, pltpu.roll(v, SUBLANES - 1, 0), 0.0)


def _shift_time(x_tm, k, chained):
    n = x_tm.shape[0]
    steps = abs(k)
    edge = []
    for s in range(steps):
        if not chained:
            edge.append(jnp.zeros((SUBLANES, x_tm.shape[1]), F32))
        elif k > 0:
            src = n - (steps - s) * SUBLANES
            edge.append(_chunk_neighbour(x_tm[src:src + SUBLANES], True))
        else:
            edge.append(_chunk_neighbour(x_tm[s * SUBLANES:(s + 1) * SUBLANES], False))
    if k > 0:
        return jnp.concatenate(edge + [x_tm[:n - steps * SUBLANES]], axis=0)
    return jnp.concatenate([x_tm[steps * SUBLANES:]] + edge, axis=0)


def _lru_tm_kernel(*refs, seq_len, latent):
    if latent:
        (x_ref, cw_ref, cb_ref, wg_ref, bg_ref, lam_ref, h0_ref, y_ref,
         io_ref, af_ref, bf_ref, ab_ref, bb_ref, hf_ref, hb_ref) = refs
    else:
        (x_ref, cw_ref, cb_ref, wg_ref, bg_ref, lam_ref, y_ref, fin_ref,
         io_ref, af_ref, bf_ref, ab_ref, bb_ref, hf_ref, hb_ref) = refs
    rows = SCAN_ROWS
    nchunk = rows // SCAN_CHUNK
    nhead = LRU_HEADS_PER_STEP
    chained = seq_len > SCAN_CHUNK
    for hd in range(nhead):
        lanes = slice(hd * LRU_BLOCK, (hd + 1) * LRU_BLOCK)
        x = x_ref[:, lanes].astype(F32)
        for c in range(nchunk):
            io_ref[hd, pl.ds(c * SCAN_PITCH, SCAN_CHUNK), :] = x[c * SCAN_CHUNK:(c + 1) * SCAN_CHUNK]
        x = jnp.concatenate(
            [io_ref[hd, pl.ds(t, nchunk, stride=SCAN_PITCH), :] for t in range(SCAN_CHUNK)], axis=0)
        xc = (_shift_time(x, 2, chained) * cw_ref[0:1, lanes] + _shift_time(x, 1, chained) * cw_ref[1:2, lanes]
              + x * cw_ref[2:3, lanes] + _shift_time(x, -1, chained) * cw_ref[3:4, lanes] + cb_ref[:, lanes])
        th = jnp.tanh(_dot(xc.astype(BF16), wg_ref[hd] * 0.5) + 0.5 * bg_ref[hd])
        xh = 0.5 * xc
        lam = lam_ref[hd]
        for d, (a_ref, b_ref) in enumerate(((af_ref, bf_ref), (ab_ref, bb_ref))):
            th_r = th[:, 2 * d * LRU_BLOCK:(2 * d + 1) * LRU_BLOCK]
            th_i = th[:, (2 * d + 1) * LRU_BLOCK:(2 * d + 2) * LRU_BLOCK]
            nl = -lam[:, d * LRU_BLOCK:(d + 1) * LRU_BLOCK]
            softplus = jnp.maximum(nl, 0.0) + jnp.log(1.0 + jnp.exp(-jnp.abs(nl)))
            ch = (-0.5 * LRU_C * LOG2E) * softplus
            a = jnp.exp2(ch + ch * th_r)
            y = 1.0 - a * a
            a_ref[hd] = a
            b_ref[hd] = (y * lax.rsqrt(jnp.maximum(y, 1e-30))) * ((1.0 + th_i) * xh)

    def step(t, carry):
        rf = pl.ds(pl.multiple_of(t * nchunk, nchunk), nchunk)
        rb = pl.ds(pl.multiple_of((SCAN_CHUNK - 1 - t) * nchunk, nchunk), nchunk)
        out = []
        for hd in range(nhead):
            hf, hb, pf, pb = carry[4 * hd:4 * hd + 4]
            a_f = af_ref[hd, rf, :]
            a_b = ab_ref[hd, rb, :]
            hf = a_f * hf + bf_ref[hd, rf, :]
            hb = a_b * hb + bb_ref[hd, rb, :]
            hf_ref[hd, rf, :] = hf
            hb_ref[hd, rb, :] = hb
            if chained:
                pf = pf * a_f
                pb = pb * a_b
                af_ref[hd, rf, :] = pf
                ab_ref[hd, rb, :] = pb
            out += [hf, hb, pf, pb]
        return tuple(out)

    zero = jnp.zeros((nchunk, LRU_BLOCK), F32)
    one = jnp.ones((nchunk, LRU_BLOCK), F32)
    lax.fori_loop(0, SCAN_CHUNK, step, (zero, zero, one, one) * nhead, unroll=8)

    sub = lax.broadcasted_iota(jnp.int32, (nchunk, LRU_BLOCK), 0)
    last = slice(rows - nchunk, rows)
    first = slice(0, nchunk)
    for hd in range(nhead):
        lanes = slice(hd * LRU_BLOCK, (hd + 1) * LRU_BLOCK)
        hf = hf_ref[hd]
        hb = hb_ref[hd]
        if chained:
            pf = af_ref[hd]
            pb = ab_ref[hd]
            ef = jnp.where(sub == 0, h0_ref[0, 0][:, lanes], 0.0)
            eb = jnp.where(sub == nchunk - 1, h0_ref[1, 0][:, lanes], 0.0)
            for c in range(1, nchunk):
                ef = jnp.where(sub == c, pltpu.roll(hf[last] + pf[last] * ef, 1, 0), ef)
                eb = jnp.where(sub == nchunk - 1 - c,
                               pltpu.roll(hb[first] + pb[first] * eb, nchunk - 1, 0), eb)
            hf = (hf.reshape(SCAN_CHUNK, nchunk, LRU_BLOCK)
                  + pf.reshape(SCAN_CHUNK, nchunk, LRU_BLOCK) * ef[None]).reshape(rows, LRU_BLOCK)
            hb = (hb.reshape(SCAN_CHUNK, nchunk, LRU_BLOCK)
                  + pb.reshape(SCAN_CHUNK, nchunk, LRU_BLOCK) * eb[None]).reshape(rows, LRU_BLOCK)
        else:
            fin_ref[0, :, lanes] = hf[last]
            fin_ref[1, :, lanes] = hb[first]
        y = hf + hb
        for t in range(SCAN_CHUNK):
            io_ref[hd, pl.ds(t, nchunk, stride=SCAN_PITCH), :] = y[t * nchunk:(t + 1) * nchunk]
        for c in range(nchunk):
            y_ref[pl.ds(c * SCAN_CHUNK, SCAN_CHUNK), lanes] = (
                io_ref[hd, pl.ds(c * SCAN_PITCH, SCAN_CHUNK), :].astype(y_ref.dtype))


def _lru(proj, lp, layer, seq_len, h0):
    m = proj.shape[0]
    latent = h0 is not None
    rows = SCAN_ROWS
    nhead = LRU_HEADS_PER_STEP
    width = nhead * LRU_BLOCK
    in_specs = [
        pl.BlockSpec((rows, width), lambda r, h: (r, COL_XA // width + h)),
        pl.BlockSpec((None, 4, width), lambda r, h: (layer, 0, h)),
        pl.BlockSpec((None, 1, width), lambda r, h: (layer, 0, h)),
        pl.BlockSpec((None, nhead, LRU_BLOCK, 4 * LRU_BLOCK), lambda r, h: (layer, h, 0, 0)),
        pl.BlockSpec((None, nhead, 1, 4 * LRU_BLOCK), lambda r, h: (layer, h, 0, 0)),
        pl.BlockSpec((None, nhead, 1, 2 * LRU_BLOCK), lambda r, h: (layer, h, 0, 0)),
    ]
    args = [proj, lp["conv_w"], lp["conv_b"], lp["wg"], lp["bg"], lp["lam"]]
    y_spec = pl.BlockSpec((rows, width), lambda r, h: (r, h))
    y_shape = jax.ShapeDtypeStruct((m, D_RNN), BF16)
    if latent:
        in_specs.append(pl.BlockSpec((None, 2, 1, 1, width), lambda r, h: (layer, 0, r, 0, h)))
        args.append(h0)
        out_specs, out_shape = y_spec, y_shape
    else:
        nseq = m // seq_len
        out_specs = (y_spec, pl.BlockSpec((2, rows // seq_len, width), lambda r, h: (0, r, h)))
        out_shape = (y_shape, jax.ShapeDtypeStruct((2, nseq, D_RNN), F32))
    strided_buf = pltpu.VMEM((nhead, rows // SCAN_CHUNK * SCAN_PITCH, LRU_BLOCK), F32)
    scan_buf = pltpu.VMEM((nhead, rows, LRU_BLOCK), F32)
    return pl.pallas_call(
        functools.partial(_lru_tm_kernel, seq_len=seq_len, latent=latent),
        grid=(m // rows, LRU_HEADS // nhead),
        in_specs=in_specs,
        out_specs=out_specs,
        out_shape=out_shape,
        scratch_shapes=[strided_buf] + [scan_buf] * 6,
        compiler_params=_params("parallel", "parallel"),
        name="rglru_latent" if latent else "rglru_context",
    )(*args)


SOFTMAX_SCALE = HEAD_DIM ** -0.5 * LOG2E


def _stack_heads(q_ref, kh, r0, rows):
    parts = [q_ref[r0:r0 + rows, (kh * KV_GROUPS + g) * HEAD_DIM:(kh * KV_GROUPS + g + 1) * HEAD_DIM]
             for g in range(KV_GROUPS)]
    return jnp.concatenate(parts, axis=0)


def _dot_tn(a, b):
    return lax.dot_general(a, b, (((0,), (0,)), ((), ())), preferred_element_type=F32)


def _sink_row(sink_ref, layer, kh, cols):
    parts = [jnp.full((1, cols), sink_ref[layer, kh * KV_GROUPS + g] * LOG2E, F32) for g in range(KV_GROUPS)]
    return jnp.concatenate(parts, axis=1)


def _softmax_pv_t(t, sink, v):
    m = jnp.maximum(jnp.max(t, axis=0, keepdims=True), sink)
    e = jnp.exp2(t - m)
    denom = jnp.sum(e, axis=0, keepdims=True) + jnp.exp2(sink - m)
    return _dot_tn(v, e.astype(BF16)) * (1.0 / denom)


def _store_heads_t(y_ref, o_t, kh, r0, rows):
    for g in range(KV_GROUPS):
        h = kh * KV_GROUPS + g
        y_ref[r0:r0 + rows, h * HEAD_DIM:(h + 1) * HEAD_DIM] = (
            o_t[:, g * rows:(g + 1) * rows].T.astype(y_ref.dtype))


def _attn_ctx_kernel(sink_ref, q_ref, kv_ref, y_ref, *, seq_len, layer):
    for b in range(q_ref.shape[0] // seq_len):
        r0 = b * seq_len
        for kh in range(N_KV_HEADS):
            sl = slice(kh * HEAD_DIM, (kh + 1) * HEAD_DIM)
            vl = slice(D_KV + kh * HEAD_DIM, D_KV + (kh + 1) * HEAD_DIM)
            q = _stack_heads(q_ref, kh, r0, seq_len)
            t = _dot_nt(kv_ref[r0:r0 + seq_len, sl], q) * SOFTMAX_SCALE
            o_t = _softmax_pv_t(t, _sink_row(sink_ref, layer, kh, seq_len), kv_ref[r0:r0 + seq_len, vl])
            _store_heads_t(y_ref, o_t, kh, r0, seq_len)


ATTN_CTX_SEQS = 4


def _attn_ctx(proj, sink, layer, seq_len):
    m = proj.shape[0]
    rows = ATTN_CTX_SEQS * seq_len
    return pl.pallas_call(
        functools.partial(_attn_ctx_kernel, seq_len=seq_len, layer=layer),
        grid=(m // rows,),
        in_specs=[
            pl.BlockSpec(memory_space=pltpu.SMEM),
            pl.BlockSpec((rows, N_HEADS * HEAD_DIM), lambda b: (b, COL_Q // (N_HEADS * HEAD_DIM))),
            pl.BlockSpec((rows, 2 * D_KV), lambda b: (b, COL_K // (2 * D_KV))),
        ],
        out_specs=pl.BlockSpec((rows, N_HEADS * HEAD_DIM), lambda b: (b, 0)),
        out_shape=jax.ShapeDtypeStruct((m, N_HEADS * HEAD_DIM), BF16),
        compiler_params=_params("parallel"),
        name="attn_context",
    )(sink, proj, proj)


def _rope_tables(seq_len):
    nf = HEAD_DIM // 4
    freqs = ROPE_BASE ** (-np.arange(nf, dtype=np.float64) / nf)
    t = np.arange(seq_len)
    ang_row = (t // GRID_W)[:, None] * freqs[None, :]
    ang_col = (t % GRID_W)[:, None] * freqs[None, :]
    ang = np.concatenate([ang_row, ang_row, ang_col, ang_col], axis=1)
    first = (np.arange(HEAD_DIM) % (2 * nf)) < nf
    cos = np.cos(ang)
    sin = np.sin(ang)
    sin_a = np.where(first[None, :], -sin, 0.0)
    sin_b = np.where(first[None, :], 0.0, sin)
    return tuple(jnp.asarray(a, F32) for a in (cos, sin_a, sin_b))


ATTN_LAT_BLOCKS = 4


def _attn_lat_kernel(sink_ref, q_ref, kvp_ref, kvc_ref, kvn_ref, ck_ref, cv_ref, y_ref, *, nblk, layer):
    step = pl.program_id(1)
    cols = KV_GROUPS * BLOCK_Q
    span = 3 * BLOCK_Q
    key = lax.broadcasted_iota(jnp.int32, (span, cols), 0)
    qry = lax.broadcasted_iota(jnp.int32, (span, cols), 1) & (BLOCK_Q - 1)
    kv = jnp.concatenate([kvp_ref[...], kvc_ref[...], kvn_ref[...]], axis=0)
    for i in range(ATTN_LAT_BLOCKS):
        j = step * ATTN_LAT_BLOCKS + i
        lo = jnp.where(j > 0, qry, BLOCK_Q)
        hi = jnp.where(j < nblk - 1, qry + 2 * BLOCK_Q, 2 * BLOCK_Q - 1)
        bias = jnp.where(jnp.logical_and(key >= lo, key <= hi), 0.0, NEG_INF)
        win = kv[i * BLOCK_Q:i * BLOCK_Q + span]
        for kh in range(N_KV_HEADS):
            sl = slice(kh * HEAD_DIM, (kh + 1) * HEAD_DIM)
            vl = slice(D_KV + kh * HEAD_DIM, D_KV + (kh + 1) * HEAD_DIM)
            q = _stack_heads(q_ref, kh, i * BLOCK_Q, BLOCK_Q)
            keys = jnp.concatenate([win[:, sl], ck_ref[0, :, sl]], axis=0)
            vals = jnp.concatenate([win[:, vl], cv_ref[0, :, sl]], axis=0)
            t = _dot_nt(keys, q) * SOFTMAX_SCALE
            t = jnp.concatenate([t[:span] + bias, t[span:]], axis=0)
            o_t = _softmax_pv_t(t, _sink_row(sink_ref, layer, kh, BLOCK_Q), vals)
            _store_heads_t(y_ref, o_t, kh, i * BLOCK_Q, BLOCK_Q)


def _attn_lat(proj, ck, cv, sink, layer, seq_len):
    m = proj.shape[0]
    nblk = seq_len // BLOCK_Q
    nstep = nblk // ATTN_LAT_BLOCKS
    rows = ATTN_LAT_BLOCKS * BLOCK_Q
    nb = m // seq_len
    past = ck.shape[2]
    kv_col = COL_K // (2 * D_KV)
    assert COL_V == COL_K + D_KV and COL_K % (2 * D_KV) == 0

    def halo(shift):
        def index(b, s):
            return (b * nblk + jnp.clip(s * ATTN_LAT_BLOCKS + shift, 0, nblk - 1), kv_col)
        return index

    return pl.pallas_call(
        functools.partial(_attn_lat_kernel, nblk=nblk, layer=layer),
        grid=(nb, nstep),
        in_specs=[
            pl.BlockSpec(memory_space=pltpu.SMEM),
            pl.BlockSpec((rows, N_HEADS * HEAD_DIM), lambda b, s: (b * nstep + s, COL_Q // (N_HEADS * HEAD_DIM))),
            pl.BlockSpec((BLOCK_Q, 2 * D_KV), halo(-1)),
            pl.BlockSpec((rows, 2 * D_KV), lambda b, s: (b * nstep + s, kv_col)),
            pl.BlockSpec((BLOCK_Q, 2 * D_KV), halo(ATTN_LAT_BLOCKS)),
            pl.BlockSpec((1, None, past, D_KV), lambda b, s: (b, layer, 0, 0)),
            pl.BlockSpec((1, None, past, D_KV), lambda b, s: (b, layer, 0, 0)),
        ],
        out_specs=pl.BlockSpec((rows, N_HEADS * HEAD_DIM), lambda b, s: (b * nstep + s, 0)),
        out_shape=jax.ShapeDtypeStruct((m, N_HEADS * HEAD_DIM), BF16),
        compiler_params=_params("parallel", "parallel"),
        name="attn_latent",
    )(sink, proj, proj, proj, proj, ck, cv)


POOL_TILE = 256
POOL_LEAD = BF16_ROWS
POOL_ROWS = 2048


def _pool_plan(seq_len):
    lead = 0 if seq_len == POOL_TILE else POOL_LEAD
    return lead, POOL_TILE - 2 * lead


def _pool_bands(seq_len):
    lead, nout = _pool_plan(seq_len)
    r = np.arange(nout)[:, None]
    c = np.arange(POOL_TILE)[None, :] - lead
    bands = [(c >= r - win // 2) & (c < r + win // 2) for win in POOL_WINDOWS]
    return jnp.asarray(np.stack(bands), BF16)


def _pool_kernel(x0_ref, x1_ref, x2_ref, x3_ref, band_ref, w_ref, s_ref, y_ref, pad_ref, *, seq_len):
    lead, nout = _pool_plan(seq_len)
    edge = lax.broadcasted_iota(jnp.int32, (SUBLANES, POOL_GROUP), 0)
    if lead:
        pad_ref[0:lead, :] = jnp.zeros((lead, POOL_GROUP), BF16)
        pad_ref[lead + seq_len:, :] = jnp.zeros((pad_ref.shape[0] - lead - seq_len, POOL_GROUP), BF16)
    for gi, (win, x_ref) in enumerate(zip(POOL_WINDOWS, (x0_ref, x1_ref, x2_ref, x3_ref))):
        cs = slice(gi * POOL_GROUP, (gi + 1) * POOL_GROUP)
        half = win // 2
        inv_head = 1.0 / ((edge + half) - jnp.maximum(edge - half, 0)).astype(F32)
        inv_tail = 1.0 / (jnp.minimum(SUBLANES - edge, half) + half).astype(F32)
        if lead:
            pad_ref[lead:lead + seq_len, :] = x_ref[...]
        for base in range(0, POOL_ROWS, seq_len):
            for p0 in range(0, seq_len, nout):
                n = min(nout, seq_len - p0)
                x = x_ref[base + p0:base + p0 + n, :]
                src = pad_ref[p0:p0 + POOL_TILE, :] if lead else x
                sums = _dot(band_ref[gi, :n, :], src)
                head = sums[:SUBLANES] * (inv_head if p0 == 0 else 1.0 / win)
                tail = sums[n - SUBLANES:] * (inv_tail if p0 + n == seq_len else 1.0 / win)
                mean = jnp.concatenate([head, sums[SUBLANES:n - SUBLANES] * (1.0 / win), tail], axis=0)
                pooled = (mean - x.astype(F32)).astype(BF16)
                y_ref[base + p0:base + p0 + n, cs] = (
                    _dot(pooled, w_ref[gi]) * s_ref[:, cs]).astype(y_ref.dtype)


def _pool(proj, w, s, layer, seq_len):
    m = proj.shape[0]
    rows = POOL_ROWS
    lead, nout = _pool_plan(seq_len)
    assert seq_len in (POOL_TILE, rows) and max(POOL_WINDOWS) // 2 <= min(SUBLANES, lead or SUBLANES)
    pad_rows = (pl.cdiv(seq_len, nout) - 1) * nout + POOL_TILE
    group = lambda gi: pl.BlockSpec((rows, POOL_GROUP), lambda r: (r, COL_XC // POOL_GROUP + gi))
    whole = lambda shape: pl.BlockSpec(shape, lambda r: (0,) * len(shape))
    nwin = len(POOL_WINDOWS)
    return pl.pallas_call(
        functools.partial(_pool_kernel, seq_len=seq_len),
        grid=(m // rows,),
        in_specs=[
            group(0), group(1), group(2), group(3),
            whole((nwin, nout, POOL_TILE)),
            _resident((nwin, POOL_GROUP, POOL_GROUP), layer),
            _resident((1, D_POOL), layer),
        ],
        out_specs=pl.BlockSpec((rows, D_POOL), lambda r: (r, 0)),
        out_shape=jax.ShapeDtypeStruct((m, D_POOL), BF16),
        scratch_shapes=[pltpu.VMEM((pad_rows, POOL_GROUP), BF16)],
        compiler_params=_params("parallel"),
        name="pool_mix",
    )(proj, proj, proj, proj, _pool_bands(seq_len), w, s)


def _merge_kernel(ya_ref, yb_ref, yc_ref, g0_ref, g1_ref, g2_ref, g3_ref, g4_ref, g5_ref, x_ref, mod_ref,
                  bg_ref, wb_ref, wo_ref, n2_ref, x1_ref, h2_ref):
    mod = mod_ref[0]
    g_refs = (g0_ref, g1_ref, g2_ref, g3_ref, g4_ref, g5_ref)
    half = D_MODEL // 2
    merged = None
    for k, y_ref in enumerate((ya_ref, yb_ref, yc_ref)):
        y = _dot(y_ref[...], wb_ref[k])
        parts = []
        for p in range(2):
            z = g_refs[2 * k + p][...].astype(F32) + bg_ref[:, k * D_MODEL + p * half:k * D_MODEL + (p + 1) * half]
            parts.append((1.0 + jnp.tanh(0.5 * z)) * y[:, p * half:(p + 1) * half])
        term = jnp.concatenate(parts, axis=1)
        merged = term if merged is None else merged + term
    merged = 0.5 * merged
    gate1 = mod[:, 2 * D_MODEL:3 * D_MODEL]
    x1 = x_ref[...] + gate1 * _dot(merged.astype(BF16), wo_ref[...])
    x1_ref[...] = x1
    shift2 = mod[:, 3 * D_MODEL:4 * D_MODEL]
    scale2 = mod[:, 4 * D_MODEL:5 * D_MODEL]
    h2_ref[...] = (_rms(x1) * n2_ref[...] * (1.0 + scale2) + shift2).astype(BF16)


def _merge(ya, yb, yc, proj, x, mod, mod_row0, rows_per_mod, bg, wb, wo, n2, layer):
    m = x.shape[0]
    tm = 512
    per = rows_per_mod // tm
    half = D_MODEL // 2
    row = pl.BlockSpec((tm, D_MODEL), lambda i: (i, 0))
    gate = lambda c: pl.BlockSpec((tm, half), lambda i: (i, COL_G // half + c))
    const = lambda shape: pl.BlockSpec(shape, lambda i: (0,) * len(shape), pipeline_mode=pl.Buffered(1))
    return pl.pallas_call(
        _merge_kernel,
        grid=(m // tm,),
        in_specs=[
            row, row, row,
            gate(0), gate(1), gate(2), gate(3), gate(4), gate(5),
            row,
            _mod_spec(mod_row0, per),
            _resident((1, N_BRANCH * D_MODEL), layer),
            _resident((N_BRANCH, D_MODEL, D_MODEL), layer),
            _resident((D_MODEL, D_MODEL), layer),
            _resident((1, D_MODEL), layer),
        ],
        out_specs=(row, row),
        out_shape=(jax.ShapeDtypeStruct((m, D_MODEL), F32), jax.ShapeDtypeStruct((m, D_MODEL), BF16)),
        compiler_params=_params("parallel"),
        name="merge_out",
    )(ya, yb, yc, proj, proj, proj, proj, proj, proj, x, mod, bg, wb, wo, n2)


FFN_CK = 256
FFN_TM = 512
FFN_GAP = SUBLANES
GELU_C = float(np.sqrt(2.0 / np.pi))


def _ffn_kernel(*refs, seq_len, final):
    refs = list(refs)
    h_ref, hp_ref, hn_ref, x_ref, mod_ref, wup_ref, cw_ref, cb_ref, wd_ref = refs[:9]
    fn_ref = refs[9] if final else None
    o_ref, hx_ref, u_ref, act_ref = refs[-4:]
    tm = FFN_TM
    halo = seq_len > tm
    i = pl.program_id(0)

    if halo:
        per_seq = seq_len // tm
        at_start = i % per_seq == 0
        at_end = i % per_seq == per_seq - 1
        zeros = jnp.zeros((BF16_ROWS, D_MODEL), BF16)

        @pl.when(at_start)
        def _():
            hx_ref[0:BF16_ROWS, :] = zeros

        @pl.when(jnp.logical_not(at_start))
        def _():
            hx_ref[0:BF16_ROWS, :] = hp_ref[...]

        @pl.when(at_end)
        def _():
            hx_ref[BF16_ROWS + tm:, :] = zeros

        @pl.when(jnp.logical_not(at_end))
        def _():
            hx_ref[BF16_ROWS + tm:, :] = hn_ref[...]

        hx_ref[BF16_ROWS:BF16_ROWS + tm, :] = h_ref[...]
        bases = (BF16_ROWS,)
        seg = tm
    else:
        nseg = tm // seq_len
        seg = seq_len
        bases = tuple(FFN_GAP + s * (seg + FFN_GAP) for s in range(nseg))
        for s in range(nseg + 1):
            u_ref[s * (seg + FFN_GAP):s * (seg + FFN_GAP) + FFN_GAP, :] = jnp.zeros((FFN_GAP, FFN_CK), F32)

    def taps(offset):
        return jnp.concatenate([u_ref[b + offset:b + offset + seg, :] for b in bases], axis=0)

    for c in range(D_FF // FFN_CK):
        cs = slice(c * FFN_CK, (c + 1) * FFN_CK)
        vs = slice(D_FF + c * FFN_CK, D_FF + (c + 1) * FFN_CK)
        if halo:
            u_ext = _dot(hx_ref[...], wup_ref[:, cs])
            u_ref[...] = u_ext
            u0 = u_ext[BF16_ROWS:BF16_ROWS + tm]
        else:
            u0 = _dot(h_ref[...], wup_ref[:, cs])
            for s, b in enumerate(bases):
                u_ref[b:b + seg, :] = u0[s * seg:(s + 1) * seg]
        uv = _dot(h_ref[...], wup_ref[:, vs])
        gff = taps(-1) * cw_ref[0:1, cs] + u0 * cw_ref[1:2, cs] + taps(1) * cw_ref[2:3, cs] + cb_ref[:, cs]
        inner = gff * (GELU_C + (GELU_C * 0.044715) * (gff * gff))
        act_ref[:, cs] = (0.5 * (gff * uv) * (1.0 + jnp.tanh(inner))).astype(BF16)

    gate2 = mod_ref[0][:, 5 * D_MODEL:6 * D_MODEL]
    out = x_ref[...] + gate2 * _dot(act_ref[...], wd_ref[...])
    if final:
        out = _rms(out) * fn_ref[...]
    o_ref[...] = out


def _ffn(h2, x1, mod, mod_row0, rows_per_mod, wup, cw, cb, wd, layer, seq_len, final_norm):
    m = x1.shape[0]
    tm = FFN_TM
    per = rows_per_mod // tm
    hb = tm // BF16_ROWS
    last_halo = m // BF16_ROWS - 1
    final = final_norm is not None
    halo = seq_len > tm
    const = lambda shape: pl.BlockSpec(shape, lambda i: (0,) * len(shape), pipeline_mode=pl.Buffered(1))
    in_specs = [
        pl.BlockSpec((tm, D_MODEL), lambda i: (i, 0)),
        pl.BlockSpec((BF16_ROWS, D_MODEL), lambda i: (jnp.maximum(i * hb - 1, 0), 0)),
        pl.BlockSpec((BF16_ROWS, D_MODEL), lambda i: (jnp.minimum((i + 1) * hb, last_halo), 0)),
        pl.BlockSpec((tm, D_MODEL), lambda i: (i, 0)),
        _mod_spec(mod_row0, per),
        _resident((D_MODEL, 2 * D_FF), layer),
        _resident((3, D_FF), layer),
        _resident((1, D_FF), layer),
        _resident((D_FF, D_MODEL), layer),
    ]
    args = [h2, h2, h2, x1, mod, wup, cw, cb, wd]
    if final:
        in_specs.append(const((1, D_MODEL)))
        args.append(final_norm)
    if halo:
        u_rows = tm + 2 * BF16_ROWS
    else:
        u_rows = FFN_GAP + (tm // seq_len) * (seq_len + FFN_GAP)
    return pl.pallas_call(
        functools.partial(_ffn_kernel, seq_len=seq_len, final=final),
        grid=(m // tm,),
        in_specs=in_specs,
        out_specs=pl.BlockSpec((tm, D_MODEL), lambda i: (i, 0)),
        out_shape=jax.ShapeDtypeStruct((m, D_MODEL), F32),
        scratch_shapes=[
            pltpu.VMEM((tm + 2 * BF16_ROWS, D_MODEL), BF16),
            pltpu.VMEM((u_rows, FFN_CK), F32),
            pltpu.VMEM((tm, D_FF), BF16),
        ],
        compiler_params=_params("parallel"),
        name="conv_glu_ffn",
    )(*args)


def _trunk_layer(x, mod, mod_row0, rows_per_mod, p, layer, seq_len, ctx, final_norm, caches=None):
    proj, k_new, v_new = _inproj(x, mod, mod_row0, rows_per_mod, p["norm1"], p["w_in"], layer, seq_len,
                                 want_kv=ctx is None, caches=caches)
    if ctx is None:
        ya, h_fin = _lru(proj, p["lru"], layer, seq_len, None)
        yb = _attn_ctx(proj, p["sink"], layer, seq_len)
    else:
        ck, cv, h0 = ctx
        ya = _lru(proj, p["lru"], layer, seq_len, h0)
        h_fin = None
        yb = _attn_lat(proj, ck, cv, p["sink"], layer, seq_len)
    yc = _pool(proj, p["pool_w"], p["pool_scale"], layer, seq_len)
    x1, h2 = _merge(ya, yb, yc, proj, x, mod, mod_row0, rows_per_mod, p["b_gate"], p["w_branch"], p["w_out"],
                    p["norm2"], layer)
    out = _ffn(h2, x1, mod, mod_row0, rows_per_mod, p["ffn_up"], p["ffn_conv"], p["ffn_conv_b"],
               p["ffn_down"], layer, seq_len, final_norm)
    return out, k_new, v_new, h_fin


def _stacked_params(norm1, norm2, w_in, b_gate, lru_conv, lru_conv_b, lru_wa, lru_ba, lru_wx, lru_bx,
                    lru_lambda, attn_sink, pool_w, pool_scale, w_branch, w_out, ffn_up, ffn_conv,
                    ffn_conv_b, ffn_down):
    def per_head(v):
        return v.reshape(DEPTH, 2, LRU_HEADS, LRU_BLOCK).transpose(0, 2, 1, 3)

    row = lambda v: v[:, None, :]
    wg = jnp.concatenate([lru_wa[:, 0], lru_wx[:, 0], lru_wa[:, 1], lru_wx[:, 1]], axis=-1).astype(BF16)
    ba = per_head(lru_ba)
    bx = per_head(lru_bx)
    bg = jnp.concatenate([ba[:, :, 0], bx[:, :, 0], ba[:, :, 1], bx[:, :, 1]], axis=-1)[:, :, None, :]
    lam = per_head(lru_lambda).reshape(DEPTH, LRU_HEADS, 1, 2 * LRU_BLOCK)
    return {
        "norm1": row(norm1), "norm2": row(norm2), "b_gate": row(b_gate),
        "lru": {"conv_w": lru_conv, "conv_b": row(lru_conv_b), "wg": wg, "bg": bg, "lam": lam},
        "sink": attn_sink, "pool_w": pool_w.astype(BF16), "pool_scale": row(pool_scale),
        "ffn_conv": ffn_conv, "ffn_conv_b": row(ffn_conv_b),
        "w_in": w_in.astype(BF16), "w_branch": w_branch.astype(BF16), "w_out": w_out.astype(BF16),
        "ffn_up": ffn_up.astype(BF16), "ffn_down": ffn_down.astype(BF16),
    }


def kernel(x_prompt, x_sample, cache_k, cache_v, state_lru, c, c_ctx, w_ada, b_ada, norm1, norm2, w_in,
           b_gate, lru_conv, lru_conv_b, lru_wa, lru_ba, lru_wx, lru_bx, lru_lambda, attn_sink, pool_w,
           pool_scale, w_branch, w_out, ffn_up, ffn_conv, ffn_conv_b, ffn_down, final_norm):
    batch, seq, _ = x_prompt.shape
    dec_batch, dec_seq, _ = x_sample.shape
    past = cache_k.shape[2]
    assert seq == SCAN_CHUNK and dec_seq % SCAN_ROWS == 0 and (batch * seq) % SCAN_ROWS == 0

    c_rows = jnp.concatenate(
        [c_ctx[None], c, jnp.zeros((SUBLANES - 1 - dec_batch, D_MODEL), F32)], axis=0)
    mods = _ada(c_rows, w_ada, b_ada)

    xp = x_prompt.reshape(batch * seq, D_MODEL)
    xs = x_sample.reshape(dec_batch * dec_seq, D_MODEL)
    fn = final_norm[None]
    p = _stacked_params(norm1, norm2, w_in, b_gate, lru_conv, lru_conv_b, lru_wa, lru_ba, lru_wx, lru_bx,
                        lru_lambda, attn_sink, pool_w, pool_scale, w_branch, w_out, ffn_up, ffn_conv,
                        ffn_conv_b, ffn_down)
    mod_rows = mods.reshape(DEPTH * SUBLANES, 1, 6 * D_MODEL)
    ck = cache_k.reshape(dec_batch, DEPTH, past, D_KV).astype(BF16)
    cv = cache_v.reshape(dec_batch, DEPTH, past, D_KV).astype(BF16)
    h0 = state_lru.transpose(1, 2, 0, 3)[:, :, :, None, :]
    caches, hs = None, []
    for l in range(DEPTH):
        last = fn if l == DEPTH - 1 else None
        xp, k_all, v_all, h_fin = _trunk_layer(xp, mod_rows, l * SUBLANES, batch * seq, p, l, seq, None, last,
                                               caches)
        caches = (k_all, v_all)
        xs, _, _, _ = _trunk_layer(xs, mod_rows, l * SUBLANES + 1, dec_seq, p, l, dec_seq, (ck, cv, h0), last)
        hs.append(h_fin.transpose(1, 0, 2))
    y_prompt = xp.reshape(batch, seq, D_MODEL)
    y_sample = xs.reshape(dec_batch, dec_seq, D_MODEL)
    cache_dims = (batch, DEPTH, seq, N_KV_HEADS, HEAD_DIM)
    return (y_prompt, y_sample, k_all.reshape(cache_dims), v_all.reshape(cache_dims), jnp.stack(hs, axis=1))
```

```python
import functools

import numpy as np
import jax
import jax.numpy as jnp
from jax import lax
from jax.experimental import pallas as pl
from jax.experimental.pallas import tpu as pltpu

F32 = jnp.float32
BF16 = jnp.bfloat16

D_MODEL = 1024
DEPTH = 2
GRID_W = 64
EPS = 1e-6
N_BRANCH = 3
D_RNN = 1024
LRU_HEADS = 8
LRU_BLOCK = D_RNN // LRU_HEADS
LRU_C = 8.0
N_HEADS = 8
N_KV_HEADS = 2
KV_GROUPS = N_HEADS // N_KV_HEADS
HEAD_DIM = 128
D_KV = N_KV_HEADS * HEAD_DIM
WINDOW = 128
BLOCK_Q = 128
ROPE_BASE = 10000.0
NEG_INF = -1e30
D_POOL = 1024
POOL_WINDOWS = (2, 4, 8, 16)
POOL_GROUP = D_POOL // len(POOL_WINDOWS)
D_FF = 2816
D_IN = D_RNN + N_HEADS * HEAD_DIM + 2 * D_KV + D_POOL + N_BRANCH * D_MODEL

COL_XA = 0
COL_Q = COL_XA + D_RNN
COL_K = COL_Q + N_HEADS * HEAD_DIM
COL_V = COL_K + D_KV
COL_XC = COL_V + D_KV
COL_G = COL_XC + D_POOL
LOG2E = float(np.log2(np.e))

VMEM_LIMIT_BYTES = 52 * 1024 * 1024
SUBLANES = 8
LANES = 128
BF16_ROWS = 16

SCAN_CHUNK = 256
SCAN_PITCH = 260
SCAN_ROWS = 2048
POOL_PAD = 8


def _params(*sem):
    return pltpu.CompilerParams(dimension_semantics=sem, vmem_limit_bytes=VMEM_LIMIT_BYTES)


def _dot(a, b):
    return jnp.dot(a, b, preferred_element_type=F32)


def _dot_nt(a, b):
    return lax.dot_general(a, b, (((1,), (1,)), ((), ())), preferred_element_type=F32)


def _sigmoid(z):
    return 0.5 * (1.0 + jnp.tanh(0.5 * z))


def _rms(x):
    return x * lax.rsqrt(jnp.mean(x * x, axis=-1, keepdims=True) + EPS)


def _ada_kernel(c_ref, w_ref, b_ref, o_ref):
    c = c_ref[...]
    s = c * _sigmoid(c)
    o_ref[0] = _dot(s.astype(BF16), w_ref[0].astype(BF16)) + b_ref[0]


def _ada(c_rows, w_ada, b_ada):
    tn = 1536
    return pl.pallas_call(
        _ada_kernel,
        grid=(DEPTH, 6 * D_MODEL // tn),
        in_specs=[
            pl.BlockSpec((SUBLANES, D_MODEL), lambda l, j: (0, 0)),
            pl.BlockSpec((1, D_MODEL, tn), lambda l, j: (l, 0, j)),
            pl.BlockSpec((1, 1, tn), lambda l, j: (l, 0, j)),
        ],
        out_specs=pl.BlockSpec((1, SUBLANES, tn), lambda l, j: (l, 0, j)),
        out_shape=jax.ShapeDtypeStruct((DEPTH, SUBLANES, 6 * D_MODEL), F32),
        compiler_params=_params("parallel", "parallel"),
        name="ada_mod",
    )(c_rows, w_ada, b_ada.reshape(DEPTH, 1, 6 * D_MODEL))


INPROJ_TM = 512
INPROJ_CK = 2 * D_KV


def _resident(shape, layer):
    ndim = len(shape)
    return pl.BlockSpec((None,) + tuple(shape), lambda *_: (layer,) + (0,) * ndim,
                        pipeline_mode=pl.Buffered(1))


def _whole(shape):
    return pl.BlockSpec(tuple(shape), lambda *_: (0,) * len(shape), pipeline_mode=pl.Buffered(1))


def _side_cast_specs(side, steps):
    in_specs = [pl.BlockSpec((rows, arr.shape[1]), lambda i, first=first: (first + i, 0))
                for arr, rows, first in side]
    out_specs = [pl.BlockSpec((rows, arr.shape[1]), lambda i: (i, 0)) for arr, rows, _ in side]
    out_shapes = [jax.ShapeDtypeStruct((rows * steps, arr.shape[1]), BF16) for arr, rows, _ in side]
    return in_specs, out_specs, out_shapes


def _run_side_casts(refs, n_in, n_out, n_side):
    refs = list(refs)
    side_in = refs[n_in:n_in + n_side]
    side_out = refs[n_in + n_side + n_out:n_in + 2 * n_side + n_out]
    for src, dst in zip(side_in, side_out):
        dst[...] = src[...].astype(BF16)
    return refs[:n_in] + refs[n_in + n_side:n_in + n_side + n_out] + refs[n_in + 2 * n_side + n_out:]


def _layer_slabs(stacked, layer, steps):
    cols = stacked.shape[-1]
    rows = int(np.prod(stacked.shape[1:-1]))
    assert rows % (steps * BF16_ROWS) == 0
    return (stacked.reshape(stacked.shape[0] * rows, cols), rows // steps, layer * steps)


def _mod_spec(row0, per):
    return pl.BlockSpec((1, 1, 6 * D_MODEL), lambda i: (row0 + i // per, 0, 0))


def _inproj_kernel(*refs, seq_len, want_kv, layer, owns_cache, n_in, n_side):
    refs = _run_side_casts(refs, n_in, 3 if want_kv else 1, n_side)
    _inproj_body(*refs, seq_len=seq_len, want_kv=want_kv, layer=layer, owns_cache=owns_cache)


def _inproj_body(*refs, seq_len, want_kv, layer, owns_cache):
    if want_kv:
        x_ref, mod_ref, g_ref, w_ref = refs[:4]
        o_ref, k_ref, v_ref = refs[-3:]
        if owns_cache:
            for other in range(DEPTH):
                if other != layer:
                    k_ref[:, other] = jnp.zeros(k_ref.shape[:1] + k_ref.shape[2:], F32)
                    v_ref[:, other] = jnp.zeros(v_ref.shape[:1] + v_ref.shape[2:], F32)
    else:
        x_ref, mod_ref, g_ref, w_ref, cos_ref, sa_ref, sb_ref, o_ref = refs
    mod = mod_ref[0]
    shift = mod[:, 0:D_MODEL]
    scale = mod[:, D_MODEL:2 * D_MODEL]
    h = (_rms(x_ref[...]) * g_ref[...] * (1.0 + scale) + shift).astype(BF16)
    nf = HEAD_DIM // 4

    def rope(x):
        return (x * cos_ref[...] + pltpu.roll(x, HEAD_DIM - nf, 1) * sa_ref[...]
                + pltpu.roll(x, nf, 1) * sb_ref[...])

    for c in range(D_IN // INPROJ_CK):
        acc = _dot(h, w_ref[:, c * INPROJ_CK:(c + 1) * INPROJ_CK])
        if not want_kv:
            lo = c * INPROJ_CK
            heads = [acc[:, j * HEAD_DIM:(j + 1) * HEAD_DIM] for j in range(INPROJ_CK // HEAD_DIM)]
            heads = [rope(hd) if COL_Q <= lo + j * HEAD_DIM < COL_V else hd for j, hd in enumerate(heads)]
            acc = jnp.concatenate(heads, axis=1)
        o_ref[:, c * INPROJ_CK:(c + 1) * INPROJ_CK] = acc.astype(o_ref.dtype)
        if want_kv and c == COL_K // INPROJ_CK:
            for b in range(INPROJ_TM // seq_len):
                rows = slice(b * seq_len, (b + 1) * seq_len)
                for hd in range(N_KV_HEADS):
                    dst = pl.ds(hd, seq_len, stride=N_KV_HEADS)
                    at = (b, layer, dst, slice(None)) if owns_cache else (b, dst, slice(None))
                    k_ref[at] = acc[rows, hd * HEAD_DIM:(hd + 1) * HEAD_DIM]
                    v_ref[at] = acc[rows, D_KV + hd * HEAD_DIM:D_KV + (hd + 1) * HEAD_DIM]


def _inproj(x, mod, mod_row0, rows_per_mod, g, w, layer, seq_len, want_kv, caches=None, side=()):
    m = x.shape[0]
    tm = INPROJ_TM
    per = rows_per_mod // tm
    proj_spec = pl.BlockSpec((tm, D_IN), lambda i: (i, 0))
    proj_shape = jax.ShapeDtypeStruct((m, D_IN), BF16)
    aliases = {}
    owns_cache = want_kv and caches is None
    if want_kv:
        nb = tm // seq_len
        if owns_cache:
            cache_spec = pl.BlockSpec((nb, DEPTH, seq_len * N_KV_HEADS, HEAD_DIM), lambda i: (i, 0, 0, 0))
        else:
            cache_spec = pl.BlockSpec((nb, None, seq_len * N_KV_HEADS, HEAD_DIM), lambda i: (i, layer, 0, 0))
        cache_shape = jax.ShapeDtypeStruct((m // seq_len, DEPTH, seq_len * N_KV_HEADS, HEAD_DIM), F32)
        out_specs = (proj_spec, cache_spec, cache_spec)
        out_shape = (proj_shape, cache_shape, cache_shape)
        extra_specs, extra_args = [], []
        if caches is not None:
            extra_specs = [pl.BlockSpec(memory_space=pl.ANY)] * 2
            extra_args = list(caches)
            aliases = {4: 1, 5: 2}
    else:
        out_specs, out_shape = (proj_spec,), (proj_shape,)
        tab = pl.BlockSpec((tm, HEAD_DIM), lambda i: (i % (seq_len // tm), 0))
        extra_specs, extra_args = [tab, tab, tab], list(_rope_tables(seq_len))
    side_in, side_out, side_shapes = _side_cast_specs(side, m // tm)
    out = pl.pallas_call(
        functools.partial(_inproj_kernel, seq_len=seq_len, want_kv=want_kv, layer=layer, owns_cache=owns_cache,
                          n_in=4 + len(extra_specs), n_side=len(side)),
        grid=(m // tm,),
        in_specs=[
            pl.BlockSpec((tm, D_MODEL), lambda i: (i, 0)),
            _mod_spec(mod_row0, per),
            _resident((1, D_MODEL), layer),
            _whole((D_MODEL, D_IN)),
        ] + extra_specs + side_in,
        out_specs=tuple(out_specs) + tuple(side_out),
        out_shape=tuple(out_shape) + tuple(side_shapes),
        input_output_aliases=aliases,
        compiler_params=_params("parallel"),
        name="in_proj",
    )(x, mod, g, w, *extra_args, *[item[0] for item in side])
    n_own = len(out_specs)
    own = tuple(out[:n_own]) if want_kv else (out[0], None, None)
    return own + (list(out[n_own:]),)


LRU_HEADS_PER_STEP = 2


def _chunk_neighbour(v, towards_later):
    sub = lax.broadcasted_iota(jnp.int32, v.shape, 0)
    if towards_later:
        return jnp.where(sub >= 1, pltpu.roll(v, 1, 0), 0.0)
    return jnp.where(sub <= SUBLANES - 2, pltpu.roll(v, SUBLANES - 1, 0), 0.0)


def _shift_time(x_tm, k, chained):
    n = x_tm.shape[0]
    steps = abs(k)
    edge = []
    for s in range(steps):
        if not chained:
            edge.append(jnp.zeros((SUBLANES, x_tm.shape[1]), F32))
        elif k > 0:
            src = n - (steps - s) * SUBLANES
            edge.append(_chunk_neighbour(x_tm[src:src + SUBLANES], True))
        else:
            edge.append(_chunk_neighbour(x_tm[s * SUBLANES:(s + 1) * SUBLANES], False))
    if k > 0:
        return jnp.concatenate(edge + [x_tm[:n - steps * SUBLANES]], axis=0)
    return jnp.concatenate([x_tm[steps * SUBLANES:]] + edge, axis=0)


def _lru_tm_kernel(*refs, seq_len, latent):
    if latent:
        (x_ref, cw_ref, cb_ref, wg_ref, bg_ref, lam_ref, h0_ref, y_ref,
         io_ref, af_ref, bf_ref, ab_ref, bb_ref, hf_ref, hb_ref) = refs
    else:
        (x_ref, cw_ref, cb_ref, wg_ref, bg_ref, lam_ref, y_ref, fin_ref,
         io_ref, af_ref, bf_ref, ab_ref, bb_ref, hf_ref, hb_ref) = refs
    rows = SCAN_ROWS
    nchunk = rows // SCAN_CHUNK
    nhead = LRU_HEADS_PER_STEP
    chained = seq_len > SCAN_CHUNK
    for hd in range(nhead):
        lanes = slice(hd * LRU_BLOCK, (hd + 1) * LRU_BLOCK)
        x = x_ref[:, lanes].astype(F32)
        for c in range(nchunk):
            io_ref[hd, pl.ds(c * SCAN_PITCH, SCAN_CHUNK), :] = x[c * SCAN_CHUNK:(c + 1) * SCAN_CHUNK]
        x = jnp.concatenate(
            [io_ref[hd, pl.ds(t, nchunk, stride=SCAN_PITCH), :] for t in range(SCAN_CHUNK)], axis=0)
        xc = (_shift_time(x, 2, chained) * cw_ref[0:1, lanes] + _shift_time(x, 1, chained) * cw_ref[1:2, lanes]
              + x * cw_ref[2:3, lanes] + _shift_time(x, -1, chained) * cw_ref[3:4, lanes] + cb_ref[:, lanes])
        th = jnp.tanh(_dot(xc.astype(BF16), wg_ref[hd] * 0.5) + 0.5 * bg_ref[hd])
        xh = 0.5 * xc
        lam = lam_ref[hd]
        for d, (a_ref, b_ref) in enumerate(((af_ref, bf_ref), (ab_ref, bb_ref))):
            th_r = th[:, 2 * d * LRU_BLOCK:(2 * d + 1) * LRU_BLOCK]
            th_i = th[:, (2 * d + 1) * LRU_BLOCK:(2 * d + 2) * LRU_BLOCK]
            nl = -lam[:, d * LRU_BLOCK:(d + 1) * LRU_BLOCK]
            softplus = jnp.maximum(nl, 0.0) + jnp.log(1.0 + jnp.exp(-jnp.abs(nl)))
            ch = (-0.5 * LRU_C * LOG2E) * softplus
            a = jnp.exp2(ch + ch * th_r)
            y = 1.0 - a * a
            a_ref[hd] = a
            b_ref[hd] = (y * lax.rsqrt(jnp.maximum(y, 1e-30))) * ((1.0 + th_i) * xh)

    def step(t, carry):
        rf = pl.ds(pl.multiple_of(t * nchunk, nchunk), nchunk)
        rb = pl.ds(pl.multiple_of((SCAN_CHUNK - 1 - t) * nchunk, nchunk), nchunk)
        out = []
        for hd in range(nhead):
            hf, hb, pf, pb = carry[4 * hd:4 * hd + 4]
            a_f = af_ref[hd, rf, :]
            a_b = ab_ref[hd, rb, :]
            hf = a_f * hf + bf_ref[hd, rf, :]
            hb = a_b * hb + bb_ref[hd, rb, :]
            hf_ref[hd, rf, :] = hf
            hb_ref[hd, rb, :] = hb
            if chained:
                pf = pf * a_f
                pb = pb * a_b
                af_ref[hd, rf, :] = pf
                ab_ref[hd, rb, :] = pb
            out += [hf, hb, pf, pb]
        return tuple(out)

    zero = jnp.zeros((nchunk, LRU_BLOCK), F32)
    one = jnp.ones((nchunk, LRU_BLOCK), F32)
    lax.fori_loop(0, SCAN_CHUNK, step, (zero, zero, one, one) * nhead, unroll=8)

    sub = lax.broadcasted_iota(jnp.int32, (nchunk, LRU_BLOCK), 0)
    last = slice(rows - nchunk, rows)
    first = slice(0, nchunk)
    for hd in range(nhead):
        lanes = slice(hd * LRU_BLOCK, (hd + 1) * LRU_BLOCK)
        hf = hf_ref[hd]
        hb = hb_ref[hd]
        if chained:
            pf = af_ref[hd]
            pb = ab_ref[hd]
            ef = jnp.where(sub == 0, h0_ref[0, 0][:, lanes], 0.0)
            eb = jnp.where(sub == nchunk - 1, h0_ref[1, 0][:, lanes], 0.0)
            for c in range(1, nchunk):
                ef = jnp.where(sub == c, pltpu.roll(hf[last] + pf[last] * ef, 1, 0), ef)
                eb = jnp.where(sub == nchunk - 1 - c,
                               pltpu.roll(hb[first] + pb[first] * eb, nchunk - 1, 0), eb)
            hf = (hf.reshape(SCAN_CHUNK, nchunk, LRU_BLOCK)
                  + pf.reshape(SCAN_CHUNK, nchunk, LRU_BLOCK) * ef[None]).reshape(rows, LRU_BLOCK)
            hb = (hb.reshape(SCAN_CHUNK, nchunk, LRU_BLOCK)
                  + pb.reshape(SCAN_CHUNK, nchunk, LRU_BLOCK) * eb[None]).reshape(rows, LRU_BLOCK)
        else:
            fin_ref[0, :, lanes] = hf[last]
            fin_ref[1, :, lanes] = hb[first]
        y = hf + hb
        for t in range(SCAN_CHUNK):
            io_ref[hd, pl.ds(t, nchunk, stride=SCAN_PITCH), :] = y[t * nchunk:(t + 1) * nchunk]
        for c in range(nchunk):
            y_ref[pl.ds(c * SCAN_CHUNK, SCAN_CHUNK), lanes] = (
                io_ref[hd, pl.ds(c * SCAN_PITCH, SCAN_CHUNK), :].astype(y_ref.dtype))


def _lru(proj, lp, layer, seq_len, h0):
    m = proj.shape[0]
    latent = h0 is not None
    rows = SCAN_ROWS
    nhead = LRU_HEADS_PER_STEP
    width = nhead * LRU_BLOCK
    in_specs = [
        pl.BlockSpec((rows, width), lambda r, h: (r, COL_XA // width + h)),
        pl.BlockSpec((None, 4, width), lambda r, h: (layer, 0, h)),
        pl.BlockSpec((None, 1, width), lambda r, h: (layer, 0, h)),
        pl.BlockSpec((None, nhead, LRU_BLOCK, 4 * LRU_BLOCK), lambda r, h: (layer, h, 0, 0)),
        pl.BlockSpec((None, nhead, 1, 4 * LRU_BLOCK), lambda r, h: (layer, h, 0, 0)),
        pl.BlockSpec((None, nhead, 1, 2 * LRU_BLOCK), lambda r, h: (layer, h, 0, 0)),
    ]
    args = [proj, lp["conv_w"], lp["conv_b"], lp["wg"], lp["bg"], lp["lam"]]
    y_spec = pl.BlockSpec((rows, width), lambda r, h: (r, h))
    y_shape = jax.ShapeDtypeStruct((m, D_RNN), BF16)
    if latent:
        in_specs.append(pl.BlockSpec((None, 2, 1, 1, width), lambda r, h: (layer, 0, r, 0, h)))
        args.append(h0)
        out_specs, out_shape = y_spec, y_shape
    else:
        nseq = m // seq_len
        out_specs = (y_spec, pl.BlockSpec((2, rows // seq_len, width), lambda r, h: (0, r, h)))
        out_shape = (y_shape, jax.ShapeDtypeStruct((2, nseq, D_RNN), F32))
    strided_buf = pltpu.VMEM((nhead, rows // SCAN_CHUNK * SCAN_PITCH, LRU_BLOCK), F32)
    scan_buf = pltpu.VMEM((nhead, rows, LRU_BLOCK), F32)
    return pl.pallas_call(
        functools.partial(_lru_tm_kernel, seq_len=seq_len, latent=latent),
        grid=(m // rows, LRU_HEADS // nhead),
        in_specs=in_specs,
        out_specs=out_specs,
        out_shape=out_shape,
        scratch_shapes=[strided_buf] + [scan_buf] * 6,
        compiler_params=_params("parallel", "parallel"),
        name="rglru_latent" if latent else "rglru_context",
    )(*args)


SOFTMAX_SCALE = HEAD_DIM ** -0.5 * LOG2E


def _stack_heads(q_ref, kh, r0, rows):
    parts = [q_ref[r0:r0 + rows, (kh * KV_GROUPS + g) * HEAD_DIM:(kh * KV_GROUPS + g + 1) * HEAD_DIM]
             for g in range(KV_GROUPS)]
    return jnp.concatenate(parts, axis=0)


def _dot_tn(a, b):
    return lax.dot_general(a, b, (((0,), (0,)), ((), ())), preferred_element_type=F32)


def _sink_row(sink_ref, layer, kh, cols):
    parts = [jnp.full((1, cols), sink_ref[layer, kh * KV_GROUPS + g] * LOG2E, F32) for g in range(KV_GROUPS)]
    return jnp.concatenate(parts, axis=1)


def _softmax_pv_t(t, sink, v):
    m = jnp.maximum(jnp.max(t, axis=0, keepdims=True), sink)
    e = jnp.exp2(t - m)
    denom = jnp.sum(e, axis=0, keepdims=True) + jnp.exp2(sink - m)
    return _dot_tn(v, e.astype(BF16)) * (1.0 / denom)


def _store_heads_t(y_ref, o_t, kh, r0, rows):
    for g in range(KV_GROUPS):
        h = kh * KV_GROUPS + g
        y_ref[r0:r0 + rows, h * HEAD_DIM:(h + 1) * HEAD_DIM] = (
            o_t[:, g * rows:(g + 1) * rows].T.astype(y_ref.dtype))


def _attn_ctx_kernel(sink_ref, q_ref, kv_ref, y_ref, *, seq_len, layer):
    for b in range(q_ref.shape[0] // seq_len):
        r0 = b * seq_len
        for kh in range(N_KV_HEADS):
            sl = slice(kh * HEAD_DIM, (kh + 1) * HEAD_DIM)
            vl = slice(D_KV + kh * HEAD_DIM, D_KV + (kh + 1) * HEAD_DIM)
            q = _stack_heads(q_ref, kh, r0, seq_len)
            t = _dot_nt(kv_ref[r0:r0 + seq_len, sl], q) * SOFTMAX_SCALE
            o_t = _softmax_pv_t(t, _sink_row(sink_ref, layer, kh, seq_len), kv_ref[r0:r0 + seq_len, vl])
            _store_heads_t(y_ref, o_t, kh, r0, seq_len)


ATTN_CTX_SEQS = 4


def _attn_ctx(proj, sink, layer, seq_len):
    m = proj.shape[0]
    rows = ATTN_CTX_SEQS * seq_len
    return pl.pallas_call(
        functools.partial(_attn_ctx_kernel, seq_len=seq_len, layer=layer),
        grid=(m // rows,),
        in_specs=[
            pl.BlockSpec(memory_space=pltpu.SMEM),
            pl.BlockSpec((rows, N_HEADS * HEAD_DIM), lambda b: (b, COL_Q // (N_HEADS * HEAD_DIM))),
            pl.BlockSpec((rows, 2 * D_KV), lambda b: (b, COL_K // (2 * D_KV))),
        ],
        out_specs=pl.BlockSpec((rows, N_HEADS * HEAD_DIM), lambda b: (b, 0)),
        out_shape=jax.ShapeDtypeStruct((m, N_HEADS * HEAD_DIM), BF16),
        compiler_params=_params("parallel"),
        name="attn_context",
    )(sink, proj, proj)


def _rope_tables(seq_len):
    nf = HEAD_DIM // 4
    freqs = ROPE_BASE ** (-np.arange(nf, dtype=np.float64) / nf)
    t = np.arange(seq_len)
    ang_row = (t // GRID_W)[:, None] * freqs[None, :]
    ang_col = (t % GRID_W)[:, None] * freqs[None, :]
    ang = np.concatenate([ang_row, ang_row, ang_col, ang_col], axis=1)
    first = (np.arange(HEAD_DIM) % (2 * nf)) < nf
    cos = np.cos(ang)
    sin = np.sin(ang)
    sin_a = np.where(first[None, :], -sin, 0.0)
    sin_b = np.where(first[None, :], 0.0, sin)
    return tuple(jnp.asarray(a, F32) for a in (cos, sin_a, sin_b))


ATTN_LAT_BLOCKS = 4


def _attn_lat_kernel(sink_ref, q_ref, kvp_ref, kvc_ref, kvn_ref, ck_ref, cv_ref, y_ref, *, nblk, layer):
    step = pl.program_id(1)
    cols = KV_GROUPS * BLOCK_Q
    span = 3 * BLOCK_Q
    key = lax.broadcasted_iota(jnp.int32, (span, cols), 0)
    qry = lax.broadcasted_iota(jnp.int32, (span, cols), 1) & (BLOCK_Q - 1)
    kv = jnp.concatenate([kvp_ref[...], kvc_ref[...], kvn_ref[...]], axis=0)
    for i in range(ATTN_LAT_BLOCKS):
        j = step * ATTN_LAT_BLOCKS + i
        lo = jnp.where(j > 0, qry, BLOCK_Q)
        hi = jnp.where(j < nblk - 1, qry + 2 * BLOCK_Q, 2 * BLOCK_Q - 1)
        bias = jnp.where(jnp.logical_and(key >= lo, key <= hi), 0.0, NEG_INF)
        win = kv[i * BLOCK_Q:i * BLOCK_Q + span]
        for kh in range(N_KV_HEADS):
            sl = slice(kh * HEAD_DIM, (kh + 1) * HEAD_DIM)
            vl = slice(D_KV + kh * HEAD_DIM, D_KV + (kh + 1) * HEAD_DIM)
            q = _stack_heads(q_ref, kh, i * BLOCK_Q, BLOCK_Q)
            keys = jnp.concatenate([win[:, sl], ck_ref[0, :, sl]], axis=0)
            vals = jnp.concatenate([win[:, vl], cv_ref[0, :, sl]], axis=0)
            t = _dot_nt(keys, q) * SOFTMAX_SCALE
            t = jnp.concatenate([t[:span] + bias, t[span:]], axis=0)
            o_t = _softmax_pv_t(t, _sink_row(sink_ref, layer, kh, BLOCK_Q), vals)
            _store_heads_t(y_ref, o_t, kh, i * BLOCK_Q, BLOCK_Q)


def _attn_lat(proj, ck, cv, sink, layer, seq_len):
    m = proj.shape[0]
    nblk = seq_len // BLOCK_Q
    nstep = nblk // ATTN_LAT_BLOCKS
    rows = ATTN_LAT_BLOCKS * BLOCK_Q
    nb = m // seq_len
    past = ck.shape[2]
    kv_col = COL_K // (2 * D_KV)
    assert COL_V == COL_K + D_KV and COL_K % (2 * D_KV) == 0

    def halo(shift):
        def index(b, s):
            return (b * nblk + jnp.clip(s * ATTN_LAT_BLOCKS + shift, 0, nblk - 1), kv_col)
        return index

    return pl.pallas_call(
        functools.partial(_attn_lat_kernel, nblk=nblk, layer=layer),
        grid=(nb, nstep),
        in_specs=[
            pl.BlockSpec(memory_space=pltpu.SMEM),
            pl.BlockSpec((rows, N_HEADS * HEAD_DIM), lambda b, s: (b * nstep + s, COL_Q // (N_HEADS * HEAD_DIM))),
            pl.BlockSpec((BLOCK_Q, 2 * D_KV), halo(-1)),
            pl.BlockSpec((rows, 2 * D_KV), lambda b, s: (b * nstep + s, kv_col)),
            pl.BlockSpec((BLOCK_Q, 2 * D_KV), halo(ATTN_LAT_BLOCKS)),
            pl.BlockSpec((1, None, past, D_KV), lambda b, s: (b, layer, 0, 0)),
            pl.BlockSpec((1, None, past, D_KV), lambda b, s: (b, layer, 0, 0)),
        ],
        out_specs=pl.BlockSpec((rows, N_HEADS * HEAD_DIM), lambda b, s: (b * nstep + s, 0)),
        out_shape=jax.ShapeDtypeStruct((m, N_HEADS * HEAD_DIM), BF16),
        compiler_params=_params("parallel", "parallel"),
        name="attn_latent",
    )(sink, proj, proj, proj, proj, ck, cv)


POOL_TILE = 256
POOL_LEAD = BF16_ROWS
POOL_ROWS = 2048


def _pool_plan(seq_len):
    lead = 0 if seq_len == POOL_TILE else POOL_LEAD
    return lead, POOL_TILE - 2 * lead


def _pool_bands(seq_len):
    lead, nout = _pool_plan(seq_len)
    r = np.arange(nout)[:, None]
    c = np.arange(POOL_TILE)[None, :] - lead
    bands = [(c >= r - win // 2) & (c < r + win // 2) for win in POOL_WINDOWS]
    return jnp.asarray(np.stack(bands), BF16)


def _pool_kernel(x0_ref, x1_ref, x2_ref, x3_ref, band_ref, w_ref, s_ref, y_ref, pad_ref, *, seq_len):
    lead, nout = _pool_plan(seq_len)
    edge = lax.broadcasted_iota(jnp.int32, (SUBLANES, POOL_GROUP), 0)
    if lead:
        pad_ref[0:lead, :] = jnp.zeros((lead, POOL_GROUP), BF16)
        pad_ref[lead + seq_len:, :] = jnp.zeros((pad_ref.shape[0] - lead - seq_len, POOL_GROUP), BF16)
    for gi, (win, x_ref) in enumerate(zip(POOL_WINDOWS, (x0_ref, x1_ref, x2_ref, x3_ref))):
        cs = slice(gi * POOL_GROUP, (gi + 1) * POOL_GROUP)
        half = win // 2
        inv_head = 1.0 / ((edge + half) - jnp.maximum(edge - half, 0)).astype(F32)
        inv_tail = 1.0 / (jnp.minimum(SUBLANES - edge, half) + half).astype(F32)
        if lead:
            pad_ref[lead:lead + seq_len, :] = x_ref[...]
        for base in range(0, POOL_ROWS, seq_len):
            for p0 in range(0, seq_len, nout):
                n = min(nout, seq_len - p0)
                x = x_ref[base + p0:base + p0 + n, :]
                src = pad_ref[p0:p0 + POOL_TILE, :] if lead else x
                sums = _dot(band_ref[gi, :n, :], src)
                head = sums[:SUBLANES] * (inv_head if p0 == 0 else 1.0 / win)
                tail = sums[n - SUBLANES:] * (inv_tail if p0 + n == seq_len else 1.0 / win)
                mean = jnp.concatenate([head, sums[SUBLANES:n - SUBLANES] * (1.0 / win), tail], axis=0)
                pooled = (mean - x.astype(F32)).astype(BF16)
                y_ref[base + p0:base + p0 + n, cs] = (
                    _dot(pooled, w_ref[gi]) * s_ref[:, cs]).astype(y_ref.dtype)


def _pool(proj, w, s, layer, seq_len):
    m = proj.shape[0]
    rows = POOL_ROWS
    lead, nout = _pool_plan(seq_len)
    assert seq_len in (POOL_TILE, rows) and max(POOL_WINDOWS) // 2 <= min(SUBLANES, lead or SUBLANES)
    pad_rows = (pl.cdiv(seq_len, nout) - 1) * nout + POOL_TILE
    group = lambda gi: pl.BlockSpec((rows, POOL_GROUP), lambda r: (r, COL_XC // POOL_GROUP + gi))
    whole = lambda shape: pl.BlockSpec(shape, lambda r: (0,) * len(shape))
    nwin = len(POOL_WINDOWS)
    return pl.pallas_call(
        functools.partial(_pool_kernel, seq_len=seq_len),
        grid=(m // rows,),
        in_specs=[
            group(0), group(1), group(2), group(3),
            whole((nwin, nout, POOL_TILE)),
            _resident((nwin, POOL_GROUP, POOL_GROUP), layer),
            _resident((1, D_POOL), layer),
        ],
        out_specs=pl.BlockSpec((rows, D_POOL), lambda r: (r, 0)),
        out_shape=jax.ShapeDtypeStruct((m, D_POOL), BF16),
        scratch_shapes=[pltpu.VMEM((pad_rows, POOL_GROUP), BF16)],
        compiler_params=_params("parallel"),
        name="pool_mix",
    )(proj, proj, proj, proj, _pool_bands(seq_len), w, s)


MERGE_N_IN = 15
MERGE_TM = 512


def _merge_kernel(*refs, n_side):
    (ya_ref, yb_ref, yc_ref, g0_ref, g1_ref, g2_ref, g3_ref, g4_ref, g5_ref, x_ref, mod_ref,
     bg_ref, wb_ref, wo_ref, n2_ref, x1_ref, h2_ref) = _run_side_casts(refs, MERGE_N_IN, 2, n_side)
    mod = mod_ref[0]
    g_refs = (g0_ref, g1_ref, g2_ref, g3_ref, g4_ref, g5_ref)
    half = D_MODEL // 2
    merged = None
    for k, y_ref in enumerate((ya_ref, yb_ref, yc_ref)):
        y = _dot(y_ref[...], wb_ref[k])
        parts = []
        for p in range(2):
            z = g_refs[2 * k + p][...].astype(F32) + bg_ref[:, k * D_MODEL + p * half:k * D_MODEL + (p + 1) * half]
            parts.append((1.0 + jnp.tanh(0.5 * z)) * y[:, p * half:(p + 1) * half])
        term = jnp.concatenate(parts, axis=1)
        merged = term if merged is None else merged + term
    merged = 0.5 * merged
    gate1 = mod[:, 2 * D_MODEL:3 * D_MODEL]
    x1 = x_ref[...] + gate1 * _dot(merged.astype(BF16), wo_ref[...])
    x1_ref[...] = x1
    shift2 = mod[:, 3 * D_MODEL:4 * D_MODEL]
    scale2 = mod[:, 4 * D_MODEL:5 * D_MODEL]
    h2_ref[...] = (_rms(x1) * n2_ref[...] * (1.0 + scale2) + shift2).astype(BF16)


def _merge(ya, yb, yc, proj, x, mod, mod_row0, rows_per_mod, bg, wb, wo, n2, layer, side=()):
    m = x.shape[0]
    tm = MERGE_TM
    per = rows_per_mod // tm
    half = D_MODEL // 2
    row = pl.BlockSpec((tm, D_MODEL), lambda i: (i, 0))
    gate = lambda c: pl.BlockSpec((tm, half), lambda i: (i, COL_G // half + c))
    side_in, side_out, side_shapes = _side_cast_specs(side, m // tm)
    in_specs = [
        row, row, row,
        gate(0), gate(1), gate(2), gate(3), gate(4), gate(5),
        row,
        _mod_spec(mod_row0, per),
        _resident((1, N_BRANCH * D_MODEL), layer),
        _whole((N_BRANCH, D_MODEL, D_MODEL)),
        _whole((D_MODEL, D_MODEL)),
        _resident((1, D_MODEL), layer),
    ]
    assert len(in_specs) == MERGE_N_IN
    out = pl.pallas_call(
        functools.partial(_merge_kernel, n_side=len(side)),
        grid=(m // tm,),
        in_specs=in_specs + side_in,
        out_specs=(row, row) + tuple(side_out),
        out_shape=(jax.ShapeDtypeStruct((m, D_MODEL), F32), jax.ShapeDtypeStruct((m, D_MODEL), BF16))
        + tuple(side_shapes),
        compiler_params=_params("parallel"),
        name="merge_out",
    )(ya, yb, yc, proj, proj, proj, proj, proj, proj, x, mod, bg, wb, wo, n2, *[item[0] for item in side])
    return out[0], out[1], list(out[2:])


FFN_CK = 256
FFN_TM = 512
FFN_GAP = SUBLANES
GELU_C = float(np.sqrt(2.0 / np.pi))


def _ffn_kernel(*refs, seq_len, final, n_side):
    refs = _run_side_casts(refs, 9 + int(final), 1, n_side)
    h_ref, hp_ref, hn_ref, x_ref, mod_ref, wup_ref, cw_ref, cb_ref, wd_ref = refs[:9]
    fn_ref = refs[9] if final else None
    o_ref, hx_ref, u_ref, act_ref = refs[-4:]
    tm = FFN_TM
    halo = seq_len > tm
    i = pl.program_id(0)

    if halo:
        per_seq = seq_len // tm
        at_start = i % per_seq == 0
        at_end = i % per_seq == per_seq - 1
        zeros = jnp.zeros((BF16_ROWS, D_MODEL), BF16)

        @pl.when(at_start)
        def _():
            hx_ref[0:BF16_ROWS, :] = zeros

        @pl.when(jnp.logical_not(at_start))
        def _():
            hx_ref[0:BF16_ROWS, :] = hp_ref[...]

        @pl.when(at_end)
        def _():
            hx_ref[BF16_ROWS + tm:, :] = zeros

        @pl.when(jnp.logical_not(at_end))
        def _():
            hx_ref[BF16_ROWS + tm:, :] = hn_ref[...]

        hx_ref[BF16_ROWS:BF16_ROWS + tm, :] = h_ref[...]
        bases = (BF16_ROWS,)
        seg = tm
    else:
        nseg = tm // seq_len
        seg = seq_len
        bases = tuple(FFN_GAP + s * (seg + FFN_GAP) for s in range(nseg))
        for s in range(nseg + 1):
            u_ref[s * (seg + FFN_GAP):s * (seg + FFN_GAP) + FFN_GAP, :] = jnp.zeros((FFN_GAP, FFN_CK), F32)

    def taps(offset):
        return jnp.concatenate([u_ref[b + offset:b + offset + seg, :] for b in bases], axis=0)

    for c in range(D_FF // FFN_CK):
        cs = slice(c * FFN_CK, (c + 1) * FFN_CK)
        vs = slice(D_FF + c * FFN_CK, D_FF + (c + 1) * FFN_CK)
        if halo:
            u_ext = _dot(hx_ref[...], wup_ref[:, cs])
            u_ref[...] = u_ext
            u0 = u_ext[BF16_ROWS:BF16_ROWS + tm]
        else:
            u0 = _dot(h_ref[...], wup_ref[:, cs])
            for s, b in enumerate(bases):
                u_ref[b:b + seg, :] = u0[s * seg:(s + 1) * seg]
        uv = _dot(h_ref[...], wup_ref[:, vs])
        gff = taps(-1) * cw_ref[0:1, cs] + u0 * cw_ref[1:2, cs] + taps(1) * cw_ref[2:3, cs] + cb_ref[:, cs]
        inner = gff * (GELU_C + (GELU_C * 0.044715) * (gff * gff))
        act_ref[:, cs] = (0.5 * (gff * uv) * (1.0 + jnp.tanh(inner))).astype(BF16)

    gate2 = mod_ref[0][:, 5 * D_MODEL:6 * D_MODEL]
    out = x_ref[...] + gate2 * _dot(act_ref[...], wd_ref[...])
    if final:
        out = _rms(out) * fn_ref[...]
    o_ref[...] = out


def _ffn(h2, x1, mod, mod_row0, rows_per_mod, wup, cw, cb, wd, layer, seq_len, final_norm, side=()):
    m = x1.shape[0]
    tm = FFN_TM
    per = rows_per_mod // tm
    hb = tm // BF16_ROWS
    last_halo = m // BF16_ROWS - 1
    final = final_norm is not None
    halo = seq_len > tm
    in_specs = [
        pl.BlockSpec((tm, D_MODEL), lambda i: (i, 0)),
        pl.BlockSpec((BF16_ROWS, D_MODEL), lambda i: (jnp.maximum(i * hb - 1, 0), 0)),
        pl.BlockSpec((BF16_ROWS, D_MODEL), lambda i: (jnp.minimum((i + 1) * hb, last_halo), 0)),
        pl.BlockSpec((tm, D_MODEL), lambda i: (i, 0)),
        _mod_spec(mod_row0, per),
        _whole((D_MODEL, 2 * D_FF)),
        _resident((3, D_FF), layer),
        _resident((1, D_FF), layer),
        _whole((D_FF, D_MODEL)),
    ]
    args = [h2, h2, h2, x1, mod, wup, cw, cb, wd]
    if final:
        in_specs.append(_whole((1, D_MODEL)))
        args.append(final_norm)
    if halo:
        u_rows = tm + 2 * BF16_ROWS
    else:
        u_rows = FFN_GAP + (tm // seq_len) * (seq_len + FFN_GAP)
    side_in, side_out, side_shapes = _side_cast_specs(side, m // tm)
    out = pl.pallas_call(
        functools.partial(_ffn_kernel, seq_len=seq_len, final=final, n_side=len(side)),
        grid=(m // tm,),
        in_specs=in_specs + side_in,
        out_specs=(pl.BlockSpec((tm, D_MODEL), lambda i: (i, 0)),) + tuple(side_out),
        out_shape=(jax.ShapeDtypeStruct((m, D_MODEL), F32),) + tuple(side_shapes),
        scratch_shapes=[
            pltpu.VMEM((tm + 2 * BF16_ROWS, D_MODEL), BF16),
            pltpu.VMEM((u_rows, FFN_CK), F32),
            pltpu.VMEM((tm, D_FF), BF16),
        ],
        compiler_params=_params("parallel"),
        name="conv_glu_ffn",
    )(*args, *[item[0] for item in side])
    return out[0], list(out[1:])


def _trunk_layer(x, mod, mod_row0, rows_per_mod, p, wts, layer, seq_len, ctx, final_norm, caches=None,
                 raw=None):
    m = x.shape[0]
    cast = raw is not None
    side = [_layer_slabs(raw[k], layer, m // INPROJ_TM) for k in ("w_branch", "w_out")] if cast else ()
    proj, k_new, v_new, done = _inproj(x, mod, mod_row0, rows_per_mod, p["norm1"], wts["w_in"], layer, seq_len,
                                       want_kv=ctx is None, caches=caches, side=side)
    if cast:
        wts["w_branch"] = done[0].reshape(N_BRANCH, D_MODEL, D_MODEL)
        wts["w_out"] = done[1]
    if ctx is None:
        ya, h_fin = _lru(proj, p["lru"], layer, seq_len, None)
        yb = _attn_ctx(proj, p["sink"], layer, seq_len)
    else:
        ck, cv, h0 = ctx
        ya = _lru(proj, p["lru"], layer, seq_len, h0)
        h_fin = None
        yb = _attn_lat(proj, ck, cv, p["sink"], layer, seq_len)
    yc = _pool(proj, p["pool_w"], p["pool_scale"], layer, seq_len)
    side = [_layer_slabs(raw[k], layer, m // MERGE_TM) for k in ("ffn_up", "ffn_down")] if cast else ()
    x1, h2, done = _merge(ya, yb, yc, proj, x, mod, mod_row0, rows_per_mod, p["b_gate"], wts["w_branch"],
                          wts["w_out"], p["norm2"], layer, side=side)
    if cast:
        wts["ffn_up"], wts["ffn_down"] = done
    side = [_layer_slabs(raw["w_in"], layer + 1, m // FFN_TM)] if cast and layer + 1 < DEPTH else ()
    out, done = _ffn(h2, x1, mod, mod_row0, rows_per_mod, wts["ffn_up"], p["ffn_conv"], p["ffn_conv_b"],
                     wts["ffn_down"], layer, seq_len, final_norm, side=side)
    return out, k_new, v_new, h_fin, (done[0] if side else None)


def _stacked_params(norm1, norm2, b_gate, lru_conv, lru_conv_b, lru_wa, lru_ba, lru_wx, lru_bx,
                    lru_lambda, attn_sink, pool_w, pool_scale, ffn_conv, ffn_conv_b):
    def per_head(v):
        return v.reshape(DEPTH, 2, LRU_HEADS, LRU_BLOCK).transpose(0, 2, 1, 3)

    row = lambda v: v[:, None, :]
    wg = jnp.concatenate([lru_wa[:, 0], lru_wx[:, 0], lru_wa[:, 1], lru_wx[:, 1]], axis=-1).astype(BF16)
    ba = per_head(lru_ba)
    bx = per_head(lru_bx)
    bg = jnp.concatenate([ba[:, :, 0], bx[:, :, 0], ba[:, :, 1], bx[:, :, 1]], axis=-1)[:, :, None, :]
    lam = per_head(lru_lambda).reshape(DEPTH, LRU_HEADS, 1, 2 * LRU_BLOCK)
    return {
        "norm1": row(norm1), "norm2": row(norm2), "b_gate": row(b_gate),
        "lru": {"conv_w": lru_conv, "conv_b": row(lru_conv_b), "wg": wg, "bg": bg, "lam": lam},
        "sink": attn_sink, "pool_w": pool_w.astype(BF16), "pool_scale": row(pool_scale),
        "ffn_conv": ffn_conv, "ffn_conv_b": row(ffn_conv_b),
    }


def kernel(x_prompt, x_sample, cache_k, cache_v, state_lru, c, c_ctx, w_ada, b_ada, norm1, norm2, w_in,
           b_gate, lru_conv, lru_conv_b, lru_wa, lru_ba, lru_wx, lru_bx, lru_lambda, attn_sink, pool_w,
           pool_scale, w_branch, w_out, ffn_up, ffn_conv, ffn_conv_b, ffn_down, final_norm):
    batch, seq, _ = x_prompt.shape
    dec_batch, dec_seq, _ = x_sample.shape
    past = cache_k.shape[2]
    assert seq == SCAN_CHUNK and dec_seq % SCAN_ROWS == 0 and (batch * seq) % SCAN_ROWS == 0

    c_rows = jnp.concatenate(
        [c_ctx[None], c, jnp.zeros((SUBLANES - 1 - dec_batch, D_MODEL), F32)], axis=0)
    mods = _ada(c_rows, w_ada, b_ada)

    xp = x_prompt.reshape(batch * seq, D_MODEL)
    xs = x_sample.reshape(dec_batch * dec_seq, D_MODEL)
    fn = final_norm[None]
    p = _stacked_params(norm1, norm2, b_gate, lru_conv, lru_conv_b, lru_wa, lru_ba, lru_wx, lru_bx,
                        lru_lambda, attn_sink, pool_w, pool_scale, ffn_conv, ffn_conv_b)
    raw = {"w_in": w_in, "w_branch": w_branch, "w_out": w_out, "ffn_up": ffn_up, "ffn_down": ffn_down}
    w_in_l = w_in[0].astype(BF16)
    mod_rows = mods.reshape(DEPTH * SUBLANES, 1, 6 * D_MODEL)
    ck = cache_k.reshape(dec_batch, DEPTH, past, D_KV).astype(BF16)
    cv = cache_v.reshape(dec_batch, DEPTH, past, D_KV).astype(BF16)
    h0 = state_lru.transpose(1, 2, 0, 3)[:, :, :, None, :]
    caches, hs = None, []
    for l in range(DEPTH):
        last = fn if l == DEPTH - 1 else None
        wts = {"w_in": w_in_l}
        xp, k_all, v_all, h_fin, w_in_l = _trunk_layer(xp, mod_rows, l * SUBLANES, batch * seq, p, wts, l, seq,
                                                       None, last, caches, raw)
        caches = (k_all, v_all)
        xs = _trunk_layer(xs, mod_rows, l * SUBLANES + 1, dec_seq, p, wts, l, dec_seq, (ck, cv, h0), last)[0]
        hs.append(h_fin.transpose(1, 0, 2))
    y_prompt = xp.reshape(batch, seq, D_MODEL)
    y_sample = xs.reshape(dec_batch, dec_seq, D_MODEL)
    cache_dims = (batch, DEPTH, seq, N_KV_HEADS, HEAD_DIM)
    return (y_prompt, y_sample, k_all.reshape(cache_dims), v_all.reshape(cache_dims), jnp.stack(hs, axis=1))
```

```python
import functools

import numpy as np
import jax
import jax.numpy as jnp
from jax import lax
from jax.experimental import pallas as pl
from jax.experimental.pallas import tpu as pltpu

F32 = jnp.float32
BF16 = jnp.bfloat16

D_MODEL = 1024
DEPTH = 2
GRID_W = 64
EPS = 1e-6
N_BRANCH = 3
D_RNN = 1024
LRU_HEADS = 8
LRU_BLOCK = D_RNN // LRU_HEADS
LRU_C = 8.0
N_HEADS = 8
N_KV_HEADS = 2
KV_GROUPS = N_HEADS // N_KV_HEADS
HEAD_DIM = 128
D_KV = N_KV_HEADS * HEAD_DIM
WINDOW = 128
BLOCK_Q = 128
ROPE_BASE = 10000.0
NEG_INF = -1e30
D_POOL = 1024
POOL_WINDOWS = (2, 4, 8, 16)
POOL_GROUP = D_POOL // len(POOL_WINDOWS)
D_FF = 2816
D_IN = D_RNN + N_HEADS * HEAD_DIM + 2 * D_KV + D_POOL + N_BRANCH * D_MODEL

COL_XA = 0
COL_Q = COL_XA + D_RNN
COL_K = COL_Q + N_HEADS * HEAD_DIM
COL_V = COL_K + D_KV
COL_XC = COL_V + D_KV
COL_G = COL_XC + D_POOL
LOG2E = float(np.log2(np.e))

VMEM_LIMIT_BYTES = 52 * 1024 * 1024
SUBLANES = 8
LANES = 128
BF16_ROWS = 16

SCAN_CHUNK = 256
SCAN_PITCH = 260
SCAN_ROWS = 2048
POOL_PAD = 8


def _params(*sem):
    return pltpu.CompilerParams(dimension_semantics=sem, vmem_limit_bytes=VMEM_LIMIT_BYTES)


def _dot(a, b):
    return jnp.dot(a, b, preferred_element_type=F32)


def _dot_nt(a, b):
    return lax.dot_general(a, b, (((1,), (1,)), ((), ())), preferred_element_type=F32)


def _sigmoid(z):
    return 0.5 * (1.0 + jnp.tanh(0.5 * z))


def _rms(x):
    return x * lax.rsqrt(jnp.mean(x * x, axis=-1, keepdims=True) + EPS)


def _ada_kernel(c_ref, w_ref, b_ref, o_ref):
    c = c_ref[...]
    s = c * _sigmoid(c)
    o_ref[0] = _dot(s.astype(BF16), w_ref[0].astype(BF16)) + b_ref[0]


def _ada(c_rows, w_ada, b_ada):
    tn = 1536
    return pl.pallas_call(
        _ada_kernel,
        grid=(DEPTH, 6 * D_MODEL // tn),
        in_specs=[
            pl.BlockSpec((SUBLANES, D_MODEL), lambda l, j: (0, 0)),
            pl.BlockSpec((1, D_MODEL, tn), lambda l, j: (l, 0, j)),
            pl.BlockSpec((1, 1, tn), lambda l, j: (l, 0, j)),
        ],
        out_specs=pl.BlockSpec((1, SUBLANES, tn), lambda l, j: (l, 0, j)),
        out_shape=jax.ShapeDtypeStruct((DEPTH, SUBLANES, 6 * D_MODEL), F32),
        compiler_params=_params("parallel", "parallel"),
        name="ada_mod",
    )(c_rows, w_ada, b_ada.reshape(DEPTH, 1, 6 * D_MODEL))


INPROJ_TM = 512
INPROJ_CK = 2 * D_KV


def _resident(shape, layer):
    ndim = len(shape)
    return pl.BlockSpec((None,) + tuple(shape), lambda *_: (layer,) + (0,) * ndim,
                        pipeline_mode=pl.Buffered(1))


def _whole(shape):
    return pl.BlockSpec(tuple(shape), lambda *_: (0,) * len(shape), pipeline_mode=pl.Buffered(1))


def _side_cast_specs(side, steps):
    in_specs = [pl.BlockSpec((rows, arr.shape[1]), lambda i, first=first: (first + i, 0))
                for arr, rows, first in side]
    out_specs = [pl.BlockSpec((rows, arr.shape[1]), lambda i: (i, 0)) for arr, rows, _ in side]
    out_shapes = [jax.ShapeDtypeStruct((rows * steps, arr.shape[1]), BF16) for arr, rows, _ in side]
    return in_specs, out_specs, out_shapes


def _run_side_casts(refs, n_in, n_out, n_side):
    refs = list(refs)
    side_in = refs[n_in:n_in + n_side]
    side_out = refs[n_in + n_side + n_out:n_in + 2 * n_side + n_out]
    for src, dst in zip(side_in, side_out):
        dst[...] = src[...].astype(BF16)
    return refs[:n_in] + refs[n_in + n_side:n_in + n_side + n_out] + refs[n_in + 2 * n_side + n_out:]


def _layer_slabs(stacked, layer, steps):
    cols = stacked.shape[-1]
    rows = int(np.prod(stacked.shape[1:-1]))
    assert rows % (steps * BF16_ROWS) == 0
    return (stacked.reshape(stacked.shape[0] * rows, cols), rows // steps, layer * steps)


def _mod_spec(row0, per):
    return pl.BlockSpec((1, 1, 6 * D_MODEL), lambda i: (row0 + i // per, 0, 0))


def _inproj_kernel(*refs, seq_len, want_kv, layer, owns_cache, n_in, n_side):
    refs = _run_side_casts(refs, n_in, 4 if want_kv else 1, n_side)
    _inproj_body(*refs, seq_len=seq_len, want_kv=want_kv, layer=layer, owns_cache=owns_cache)


def _inproj_body(*refs, seq_len, want_kv, layer, owns_cache):
    if want_kv:
        x_ref, mod_ref, g_ref, w_ref, sink_ref = refs[:5]
        o_ref, k_ref, v_ref, y_ref = refs[-4:]
        q_chunks = []
        if owns_cache:
            for other in range(DEPTH):
                if other != layer:
                    k_ref[:, other] = jnp.zeros(k_ref.shape[:1] + k_ref.shape[2:], F32)
                    v_ref[:, other] = jnp.zeros(v_ref.shape[:1] + v_ref.shape[2:], F32)
    else:
        x_ref, mod_ref, g_ref, w_ref, cos_ref, sa_ref, sb_ref, o_ref = refs
    mod = mod_ref[0]
    shift = mod[:, 0:D_MODEL]
    scale = mod[:, D_MODEL:2 * D_MODEL]
    h = (_rms(x_ref[...]) * g_ref[...] * (1.0 + scale) + shift).astype(BF16)
    nf = HEAD_DIM // 4

    def rope(x):
        return (x * cos_ref[...] + pltpu.roll(x, HEAD_DIM - nf, 1) * sa_ref[...]
                + pltpu.roll(x, nf, 1) * sb_ref[...])

    for c in range(D_IN // INPROJ_CK):
        acc = _dot(h, w_ref[:, c * INPROJ_CK:(c + 1) * INPROJ_CK])
        if not want_kv:
            lo = c * INPROJ_CK
            heads = [acc[:, j * HEAD_DIM:(j + 1) * HEAD_DIM] for j in range(INPROJ_CK // HEAD_DIM)]
            heads = [rope(hd) if COL_Q <= lo + j * HEAD_DIM < COL_V else hd for j, hd in enumerate(heads)]
            acc = jnp.concatenate(heads, axis=1)
        o_ref[:, c * INPROJ_CK:(c + 1) * INPROJ_CK] = acc.astype(o_ref.dtype)
        if want_kv and COL_Q <= c * INPROJ_CK < COL_K:
            q_chunks.append(acc.astype(BF16))
        if want_kv and c == COL_K // INPROJ_CK:
            for b in range(INPROJ_TM // seq_len):
                rows = slice(b * seq_len, (b + 1) * seq_len)
                for hd in range(N_KV_HEADS):
                    dst = pl.ds(hd, seq_len, stride=N_KV_HEADS)
                    at = (b, layer, dst, slice(None)) if owns_cache else (b, dst, slice(None))
                    k_ref[at] = acc[rows, hd * HEAD_DIM:(hd + 1) * HEAD_DIM]
                    v_ref[at] = acc[rows, D_KV + hd * HEAD_DIM:D_KV + (hd + 1) * HEAD_DIM]
            kv = acc.astype(BF16)
            for b in range(INPROJ_TM // seq_len):
                rows = slice(b * seq_len, (b + 1) * seq_len)
                for kh in range(N_KV_HEADS):
                    q = jnp.concatenate([q_chunks[kh][rows, g * HEAD_DIM:(g + 1) * HEAD_DIM]
                                         for g in range(KV_GROUPS)], axis=0)
                    t = _dot_nt(kv[rows, kh * HEAD_DIM:(kh + 1) * HEAD_DIM], q) * SOFTMAX_SCALE
                    o_t = _softmax_pv_t(t, _sink_row(sink_ref, layer, kh, seq_len),
                                        kv[rows, D_KV + kh * HEAD_DIM:D_KV + (kh + 1) * HEAD_DIM])
                    _store_heads_t(y_ref, o_t, kh, b * seq_len, seq_len)


def _inproj(x, mod, mod_row0, rows_per_mod, g, w, layer, seq_len, want_kv, sink=None, caches=None, side=()):
    m = x.shape[0]
    tm = INPROJ_TM
    per = rows_per_mod // tm
    proj_spec = pl.BlockSpec((tm, D_IN), lambda i: (i, 0))
    proj_shape = jax.ShapeDtypeStruct((m, D_IN), BF16)
    aliases = {}
    owns_cache = want_kv and caches is None
    if want_kv:
        nb = tm // seq_len
        if owns_cache:
            cache_spec = pl.BlockSpec((nb, DEPTH, seq_len * N_KV_HEADS, HEAD_DIM), lambda i: (i, 0, 0, 0))
        else:
            cache_spec = pl.BlockSpec((nb, None, seq_len * N_KV_HEADS, HEAD_DIM), lambda i: (i, layer, 0, 0))
        cache_shape = jax.ShapeDtypeStruct((m // seq_len, DEPTH, seq_len * N_KV_HEADS, HEAD_DIM), F32)
        assert tm % seq_len == 0 and INPROJ_CK == KV_GROUPS * HEAD_DIM and COL_Q % INPROJ_CK == 0
        attn_spec = pl.BlockSpec((tm, N_HEADS * HEAD_DIM), lambda i: (i, 0))
        out_specs = (proj_spec, cache_spec, cache_spec, attn_spec)
        out_shape = (proj_shape, cache_shape, cache_shape, jax.ShapeDtypeStruct((m, N_HEADS * HEAD_DIM), BF16))
        extra_specs, extra_args = [pl.BlockSpec(memory_space=pltpu.SMEM)], [sink]
        if caches is not None:
            extra_specs += [pl.BlockSpec(memory_space=pl.ANY)] * 2
            extra_args += list(caches)
            aliases = {5: 1, 6: 2}
    else:
        out_specs, out_shape = (proj_spec,), (proj_shape,)
        tab = pl.BlockSpec((tm, HEAD_DIM), lambda i: (i % (seq_len // tm), 0))
        extra_specs, extra_args = [tab, tab, tab], list(_rope_tables(seq_len))
    side_in, side_out, side_shapes = _side_cast_specs(side, m // tm)
    out = pl.pallas_call(
        functools.partial(_inproj_kernel, seq_len=seq_len, want_kv=want_kv, layer=layer, owns_cache=owns_cache,
                          n_in=4 + len(extra_specs), n_side=len(side)),
        grid=(m // tm,),
        in_specs=[
            pl.BlockSpec((tm, D_MODEL), lambda i: (i, 0)),
            _mod_spec(mod_row0, per),
            _resident((1, D_MODEL), layer),
            _whole((D_MODEL, D_IN)),
        ] + extra_specs + side_in,
        out_specs=tuple(out_specs) + tuple(side_out),
        out_shape=tuple(out_shape) + tuple(side_shapes),
        input_output_aliases=aliases,
        compiler_params=_params("parallel"),
        name="in_proj",
    )(x, mod, g, w, *extra_args, *[item[0] for item in side])
    n_own = len(out_specs)
    own = tuple(out[:n_own]) if want_kv else (out[0], None, None, None)
    return own + (list(out[n_own:]),)


LRU_HEADS_PER_STEP = 2


def _chunk_neighbour(v, towards_later):
    sub = lax.broadcasted_iota(jnp.int32, v.shape, 0)
    if towards_later:
        return jnp.where(sub >= 1, pltpu.roll(v, 1, 0), 0.0)
    return jnp.where(sub <= SUBLANES - 2, pltpu.roll(v, SUBLANES - 1, 0), 0.0)


def _shift_time(x_tm, k, chained):
    n = x_tm.shape[0]
    steps = abs(k)
    edge = []
    for s in range(steps):
        if not chained:
            edge.append(jnp.zeros((SUBLANES, x_tm.shape[1]), F32))
        elif k > 0:
            src = n - (steps - s) * SUBLANES
            edge.append(_chunk_neighbour(x_tm[src:src + SUBLANES], True))
        else:
            edge.append(_chunk_neighbour(x_tm[s * SUBLANES:(s + 1) * SUBLANES], False))
    if k > 0:
        return jnp.concatenate(edge + [x_tm[:n - steps * SUBLANES]], axis=0)
    return jnp.concatenate([x_tm[steps * SUBLANES:]] + edge, axis=0)


def _lru_tm_kernel(*refs, seq_len, latent):
    if latent:
        (x_ref, cw_ref, cb_ref, wg_ref, bg_ref, lam_ref, h0_ref, y_ref,
         io_ref, af_ref, bf_ref, ab_ref, bb_ref, hf_ref, hb_ref) = refs
    else:
        (x_ref, cw_ref, cb_ref, wg_ref, bg_ref, lam_ref, y_ref, fin_ref,
         io_ref, af_ref, bf_ref, ab_ref, bb_ref, hf_ref, hb_ref) = refs
    rows = SCAN_ROWS
    nchunk = rows // SCAN_CHUNK
    nhead = LRU_HEADS_PER_STEP
    chained = seq_len > SCAN_CHUNK
    for hd in range(nhead):
        lanes = slice(hd * LRU_BLOCK, (hd + 1) * LRU_BLOCK)
        x = x_ref[:, lanes].astype(F32)
        for c in range(nchunk):
            io_ref[hd, pl.ds(c * SCAN_PITCH, SCAN_CHUNK), :] = x[c * SCAN_CHUNK:(c + 1) * SCAN_CHUNK]
        x = jnp.concatenate(
            [io_ref[hd, pl.ds(t, nchunk, stride=SCAN_PITCH), :] for t in range(SCAN_CHUNK)], axis=0)
        xc = (_shift_time(x, 2, chained) * cw_ref[0:1, lanes] + _shift_time(x, 1, chained) * cw_ref[1:2, lanes]
              + x * cw_ref[2:3, lanes] + _shift_time(x, -1, chained) * cw_ref[3:4, lanes] + cb_ref[:, lanes])
        th = jnp.tanh(_dot(xc.astype(BF16), wg_ref[hd] * 0.5) + 0.5 * bg_ref[hd])
        xh = 0.5 * xc
        lam = lam_ref[hd]
        for d, (a_ref, b_ref) in enumerate(((af_ref, bf_ref), (ab_ref, bb_ref))):
            th_r = th[:, 2 * d * LRU_BLOCK:(2 * d + 1) * LRU_BLOCK]
            th_i = th[:, (2 * d + 1) * LRU_BLOCK:(2 * d + 2) * LRU_BLOCK]
            nl = -lam[:, d * LRU_BLOCK:(d + 1) * LRU_BLOCK]
            softplus = jnp.maximum(nl, 0.0) + jnp.log(1.0 + jnp.exp(-jnp.abs(nl)))
            ch = (-0.5 * LRU_C * LOG2E) * softplus
            a = jnp.exp2(ch + ch * th_r)
            y = 1.0 - a * a
            a_ref[hd] = a
            b_ref[hd] = (y * lax.rsqrt(jnp.maximum(y, 1e-30))) * ((1.0 + th_i) * xh)

    def step(t, carry):
        rf = pl.ds(pl.multiple_of(t * nchunk, nchunk), nchunk)
        rb = pl.ds(pl.multiple_of((SCAN_CHUNK - 1 - t) * nchunk, nchunk), nchunk)
        out = []
        for hd in range(nhead):
            hf, hb, pf, pb = carry[4 * hd:4 * hd + 4]
            a_f = af_ref[hd, rf, :]
            a_b = ab_ref[hd, rb, :]
            hf = a_f * hf + bf_ref[hd, rf, :]
            hb = a_b * hb + bb_ref[hd, rb, :]
            hf_ref[hd, rf, :] = hf
            hb_ref[hd, rb, :] = hb
            if chained:
                pf = pf * a_f
                pb = pb * a_b
                af_ref[hd, rf, :] = pf
                ab_ref[hd, rb, :] = pb
            out += [hf, hb, pf, pb]
        return tuple(out)

    zero = jnp.zeros((nchunk, LRU_BLOCK), F32)
    one = jnp.ones((nchunk, LRU_BLOCK), F32)
    lax.fori_loop(0, SCAN_CHUNK, step, (zero, zero, one, one) * nhead, unroll=8)

    sub = lax.broadcasted_iota(jnp.int32, (nchunk, LRU_BLOCK), 0)
    last = slice(rows - nchunk, rows)
    first = slice(0, nchunk)
    for hd in range(nhead):
        lanes = slice(hd * LRU_BLOCK, (hd + 1) * LRU_BLOCK)
        hf = hf_ref[hd]
        hb = hb_ref[hd]
        if chained:
            pf = af_ref[hd]
            pb = ab_ref[hd]
            ef = jnp.where(sub == 0, h0_ref[0, 0][:, lanes], 0.0)
            eb = jnp.where(sub == nchunk - 1, h0_ref[1, 0][:, lanes], 0.0)
            for c in range(1, nchunk):
                ef = jnp.where(sub == c, pltpu.roll(hf[last] + pf[last] * ef, 1, 0), ef)
                eb = jnp.where(sub == nchunk - 1 - c,
                               pltpu.roll(hb[first] + pb[first] * eb, nchunk - 1, 0), eb)
            hf = (hf.reshape(SCAN_CHUNK, nchunk, LRU_BLOCK)
                  + pf.reshape(SCAN_CHUNK, nchunk, LRU_BLOCK) * ef[None]).reshape(rows, LRU_BLOCK)
            hb = (hb.reshape(SCAN_CHUNK, nchunk, LRU_BLOCK)
                  + pb.reshape(SCAN_CHUNK, nchunk, LRU_BLOCK) * eb[None]).reshape(rows, LRU_BLOCK)
        else:
            fin_ref[0, :, lanes] = hf[last]
            fin_ref[1, :, lanes] = hb[first]
        y = hf + hb
        for t in range(SCAN_CHUNK):
            io_ref[hd, pl.ds(t, nchunk, stride=SCAN_PITCH), :] = y[t * nchunk:(t + 1) * nchunk]
        for c in range(nchunk):
            y_ref[pl.ds(c * SCAN_CHUNK, SCAN_CHUNK), lanes] = (
                io_ref[hd, pl.ds(c * SCAN_PITCH, SCAN_CHUNK), :].astype(y_ref.dtype))


def _lru(proj, lp, layer, seq_len, h0):
    m = proj.shape[0]
    latent = h0 is not None
    rows = SCAN_ROWS
    nhead = LRU_HEADS_PER_STEP
    width = nhead * LRU_BLOCK
    in_specs = [
        pl.BlockSpec((rows, width), lambda r, h: (r, COL_XA // width + h)),
        pl.BlockSpec((None, 4, width), lambda r, h: (layer, 0, h)),
        pl.BlockSpec((None, 1, width), lambda r, h: (layer, 0, h)),
        pl.BlockSpec((None, nhead, LRU_BLOCK, 4 * LRU_BLOCK), lambda r, h: (layer, h, 0, 0)),
        pl.BlockSpec((None, nhead, 1, 4 * LRU_BLOCK), lambda r, h: (layer, h, 0, 0)),
        pl.BlockSpec((None, nhead, 1, 2 * LRU_BLOCK), lambda r, h: (layer, h, 0, 0)),
    ]
    args = [proj, lp["conv_w"], lp["conv_b"], lp["wg"], lp["bg"], lp["lam"]]
    y_spec = pl.BlockSpec((rows, width), lambda r, h: (r, h))
    y_shape = jax.ShapeDtypeStruct((m, D_RNN), BF16)
    if latent:
        in_specs.append(pl.BlockSpec((None, 2, 1, 1, width), lambda r, h: (layer, 0, r, 0, h)))
        args.append(h0)
        out_specs, out_shape = y_spec, y_shape
    else:
        nseq = m // seq_len
        out_specs = (y_spec, pl.BlockSpec((2, rows // seq_len, width), lambda r, h: (0, r, h)))
        out_shape = (y_shape, jax.ShapeDtypeStruct((2, nseq, D_RNN), F32))
    strided_buf = pltpu.VMEM((nhead, rows // SCAN_CHUNK * SCAN_PITCH, LRU_BLOCK), F32)
    scan_buf = pltpu.VMEM((nhead, rows, LRU_BLOCK), F32)
    return pl.pallas_call(
        functools.partial(_lru_tm_kernel, seq_len=seq_len, latent=latent),
        grid=(m // rows, LRU_HEADS // nhead),
        in_specs=in_specs,
        out_specs=out_specs,
        out_shape=out_shape,
        scratch_shapes=[strided_buf] + [scan_buf] * 6,
        compiler_params=_params("parallel", "parallel"),
        name="rglru_latent" if latent else "rglru_context",
    )(*args)


SOFTMAX_SCALE = HEAD_DIM ** -0.5 * LOG2E


def _stack_heads(q_ref, kh, r0, rows):
    parts = [q_ref[r0:r0 + rows, (kh * KV_GROUPS + g) * HEAD_DIM:(kh * KV_GROUPS + g + 1) * HEAD_DIM]
             for g in range(KV_GROUPS)]
    return jnp.concatenate(parts, axis=0)


def _dot_tn(a, b):
    return lax.dot_general(a, b, (((0,), (0,)), ((), ())), preferred_element_type=F32)


def _sink_row(sink_ref, layer, kh, cols):
    parts = [jnp.full((1, cols), sink_ref[layer, kh * KV_GROUPS + g] * LOG2E, F32) for g in range(KV_GROUPS)]
    return jnp.concatenate(parts, axis=1)


def _softmax_pv_t(t, sink, v):
    m = jnp.maximum(jnp.max(t, axis=0, keepdims=True), sink)
    e = jnp.exp2(t - m)
    denom = jnp.sum(e, axis=0, keepdims=True) + jnp.exp2(sink - m)
    return _dot_tn(v, e.astype(BF16)) * (1.0 / denom)


def _store_heads_t(y_ref, o_t, kh, r0, rows):
    for g in range(KV_GROUPS):
        h = kh * KV_GROUPS + g
        y_ref[r0:r0 + rows, h * HEAD_DIM:(h + 1) * HEAD_DIM] = (
            o_t[:, g * rows:(g + 1) * rows].T.astype(y_ref.dtype))


def _attn_ctx_kernel(sink_ref, q_ref, kv_ref, y_ref, *, seq_len, layer):
    for b in range(q_ref.shape[0] // seq_len):
        r0 = b * seq_len
        for kh in range(N_KV_HEADS):
            sl = slice(kh * HEAD_DIM, (kh + 1) * HEAD_DIM)
            vl = slice(D_KV + kh * HEAD_DIM, D_KV + (kh + 1) * HEAD_DIM)
            q = _stack_heads(q_ref, kh, r0, seq_len)
            t = _dot_nt(kv_ref[r0:r0 + seq_len, sl], q) * SOFTMAX_SCALE
            o_t = _softmax_pv_t(t, _sink_row(sink_ref, layer, kh, seq_len), kv_ref[r0:r0 + seq_len, vl])
            _store_heads_t(y_ref, o_t, kh, r0, seq_len)


ATTN_CTX_SEQS = 4


def _attn_ctx(proj, sink, layer, seq_len):
    m = proj.shape[0]
    rows = ATTN_CTX_SEQS * seq_len
    return pl.pallas_call(
        functools.partial(_attn_ctx_kernel, seq_len=seq_len, layer=layer),
        grid=(m // rows,),
        in_specs=[
            pl.BlockSpec(memory_space=pltpu.SMEM),
            pl.BlockSpec((rows, N_HEADS * HEAD_DIM), lambda b: (b, COL_Q // (N_HEADS * HEAD_DIM))),
            pl.BlockSpec((rows, 2 * D_KV), lambda b: (b, COL_K // (2 * D_KV))),
        ],
        out_specs=pl.BlockSpec((rows, N_HEADS * HEAD_DIM), lambda b: (b, 0)),
        out_shape=jax.ShapeDtypeStruct((m, N_HEADS * HEAD_DIM), BF16),
        compiler_params=_params("parallel"),
        name="attn_context",
    )(sink, proj, proj)


def _rope_tables(seq_len):
    nf = HEAD_DIM // 4
    freqs = ROPE_BASE ** (-np.arange(nf, dtype=np.float64) / nf)
    t = np.arange(seq_len)
    ang_row = (t // GRID_W)[:, None] * freqs[None, :]
    ang_col = (t % GRID_W)[:, None] * freqs[None, :]
    ang = np.concatenate([ang_row, ang_row, ang_col, ang_col], axis=1)
    first = (np.arange(HEAD_DIM) % (2 * nf)) < nf
    cos = np.cos(ang)
    sin = np.sin(ang)
    sin_a = np.where(first[None, :], -sin, 0.0)
    sin_b = np.where(first[None, :], 0.0, sin)
    return tuple(jnp.asarray(a, F32) for a in (cos, sin_a, sin_b))


ATTN_LAT_BLOCKS = 4


def _attn_lat_kernel(sink_ref, q_ref, kvp_ref, kvc_ref, kvn_ref, ck_ref, cv_ref, y_ref, *, nblk, layer):
    step = pl.program_id(1)
    cols = KV_GROUPS * BLOCK_Q
    span = 3 * BLOCK_Q
    key = lax.broadcasted_iota(jnp.int32, (span, cols), 0)
    qry = lax.broadcasted_iota(jnp.int32, (span, cols), 1) & (BLOCK_Q - 1)
    kv = jnp.concatenate([kvp_ref[...], kvc_ref[...], kvn_ref[...]], axis=0)
    for i in range(ATTN_LAT_BLOCKS):
        j = step * ATTN_LAT_BLOCKS + i
        lo = jnp.where(j > 0, qry, BLOCK_Q)
        hi = jnp.where(j < nblk - 1, qry + 2 * BLOCK_Q, 2 * BLOCK_Q - 1)
        bias = jnp.where(jnp.logical_and(key >= lo, key <= hi), 0.0, NEG_INF)
        win = kv[i * BLOCK_Q:i * BLOCK_Q + span]
        for kh in range(N_KV_HEADS):
            sl = slice(kh * HEAD_DIM, (kh + 1) * HEAD_DIM)
            vl = slice(D_KV + kh * HEAD_DIM, D_KV + (kh + 1) * HEAD_DIM)
            q = _stack_heads(q_ref, kh, i * BLOCK_Q, BLOCK_Q)
            keys = jnp.concatenate([win[:, sl], ck_ref[0, :, sl]], axis=0)
            vals = jnp.concatenate([win[:, vl], cv_ref[0, :, sl]], axis=0)
            t = _dot_nt(keys, q) * SOFTMAX_SCALE
            t = jnp.concatenate([t[:span] + bias, t[span:]], axis=0)
            o_t = _softmax_pv_t(t, _sink_row(sink_ref, layer, kh, BLOCK_Q), vals)
            _store_heads_t(y_ref, o_t, kh, i * BLOCK_Q, BLOCK_Q)


def _attn_lat(proj, ck, cv, sink, layer, seq_len):
    m = proj.shape[0]
    nblk = seq_len // BLOCK_Q
    nstep = nblk // ATTN_LAT_BLOCKS
    rows = ATTN_LAT_BLOCKS * BLOCK_Q
    nb = m // seq_len
    past = ck.shape[2]
    kv_col = COL_K // (2 * D_KV)
    assert COL_V == COL_K + D_KV and COL_K % (2 * D_KV) == 0

    def halo(shift):
        def index(b, s):
            return (b * nblk + jnp.clip(s * ATTN_LAT_BLOCKS + shift, 0, nblk - 1), kv_col)
        return index

    return pl.pallas_call(
        functools.partial(_attn_lat_kernel, nblk=nblk, layer=layer),
        grid=(nb, nstep),
        in_specs=[
            pl.BlockSpec(memory_space=pltpu.SMEM),
            pl.BlockSpec((rows, N_HEADS * HEAD_DIM), lambda b, s: (b * nstep + s, COL_Q // (N_HEADS * HEAD_DIM))),
            pl.BlockSpec((BLOCK_Q, 2 * D_KV), halo(-1)),
            pl.BlockSpec((rows, 2 * D_KV), lambda b, s: (b * nstep + s, kv_col)),
            pl.BlockSpec((BLOCK_Q, 2 * D_KV), halo(ATTN_LAT_BLOCKS)),
            pl.BlockSpec((1, None, past, D_KV), lambda b, s: (b, layer, 0, 0)),
            pl.BlockSpec((1, None, past, D_KV), lambda b, s: (b, layer, 0, 0)),
        ],
        out_specs=pl.BlockSpec((rows, N_HEADS * HEAD_DIM), lambda b, s: (b * nstep + s, 0)),
        out_shape=jax.ShapeDtypeStruct((m, N_HEADS * HEAD_DIM), BF16),
        compiler_params=_params("parallel", "parallel"),
        name="attn_latent",
    )(sink, proj, proj, proj, proj, ck, cv)


POOL_TILE = 256
POOL_LEAD = BF16_ROWS
POOL_ROWS = 2048


def _pool_plan(seq_len):
    lead = 0 if seq_len == POOL_TILE else POOL_LEAD
    return lead, POOL_TILE - 2 * lead


def _pool_bands(seq_len):
    lead, nout = _pool_plan(seq_len)
    r = np.arange(nout)[:, None]
    c = np.arange(POOL_TILE)[None, :] - lead
    bands = [(c >= r - win // 2) & (c < r + win // 2) for win in POOL_WINDOWS]
    return jnp.asarray(np.stack(bands), BF16)


def _pool_kernel(x0_ref, x1_ref, x2_ref, x3_ref, band_ref, w_ref, s_ref, y_ref, pad_ref, *, seq_len):
    lead, nout = _pool_plan(seq_len)
    edge = lax.broadcasted_iota(jnp.int32, (SUBLANES, POOL_GROUP), 0)
    if lead:
        pad_ref[0:lead, :] = jnp.zeros((lead, POOL_GROUP), BF16)
        pad_ref[lead + seq_len:, :] = jnp.zeros((pad_ref.shape[0] - lead - seq_len, POOL_GROUP), BF16)
    for gi, (win, x_ref) in enumerate(zip(POOL_WINDOWS, (x0_ref, x1_ref, x2_ref, x3_ref))):
        cs = slice(gi * POOL_GROUP, (gi + 1) * POOL_GROUP)
        half = win // 2
        inv_head = 1.0 / ((edge + half) - jnp.maximum(edge - half, 0)).astype(F32)
        inv_tail = 1.0 / (jnp.minimum(SUBLANES - edge, half) + half).astype(F32)
        if lead:
            pad_ref[lead:lead + seq_len, :] = x_ref[...]
        for base in range(0, POOL_ROWS, seq_len):
            for p0 in range(0, seq_len, nout):
                n = min(nout, seq_len - p0)
                x = x_ref[base + p0:base + p0 + n, :]
                src = pad_ref[p0:p0 + POOL_TILE, :] if lead else x
                sums = _dot(band_ref[gi, :n, :], src)
                head = sums[:SUBLANES] * (inv_head if p0 == 0 else 1.0 / win)
                tail = sums[n - SUBLANES:] * (inv_tail if p0 + n == seq_len else 1.0 / win)
                mean = jnp.concatenate([head, sums[SUBLANES:n - SUBLANES] * (1.0 / win), tail], axis=0)
                pooled = (mean - x.astype(F32)).astype(BF16)
                y_ref[base + p0:base + p0 + n, cs] = (
                    _dot(pooled, w_ref[gi]) * s_ref[:, cs]).astype(y_ref.dtype)


def _pool(proj, w, s, layer, seq_len):
    m = proj.shape[0]
    rows = POOL_ROWS
    lead, nout = _pool_plan(seq_len)
    assert seq_len in (POOL_TILE, rows) and max(POOL_WINDOWS) // 2 <= min(SUBLANES, lead or SUBLANES)
    pad_rows = (pl.cdiv(seq_len, nout) - 1) * nout + POOL_TILE
    group = lambda gi: pl.BlockSpec((rows, POOL_GROUP), lambda r: (r, COL_XC // POOL_GROUP + gi))
    whole = lambda shape: pl.BlockSpec(shape, lambda r: (0,) * len(shape))
    nwin = len(POOL_WINDOWS)
    return pl.pallas_call(
        functools.partial(_pool_kernel, seq_len=seq_len),
        grid=(m // rows,),
        in_specs=[
            group(0), group(1), group(2), group(3),
            whole((nwin, nout, POOL_TILE)),
            _resident((nwin, POOL_GROUP, POOL_GROUP), layer),
            _resident((1, D_POOL), layer),
        ],
        out_specs=pl.BlockSpec((rows, D_POOL), lambda r: (r, 0)),
        out_shape=jax.ShapeDtypeStruct((m, D_POOL), BF16),
        scratch_shapes=[pltpu.VMEM((pad_rows, POOL_GROUP), BF16)],
        compiler_params=_params("parallel"),
        name="pool_mix",
    )(proj, proj, proj, proj, _pool_bands(seq_len), w, s)


MERGE_N_IN = 15
MERGE_TM = 512


def _merge_kernel(*refs, n_side):
    (ya_ref, yb_ref, yc_ref, g0_ref, g1_ref, g2_ref, g3_ref, g4_ref, g5_ref, x_ref, mod_ref,
     bg_ref, wb_ref, wo_ref, n2_ref, x1_ref, h2_ref) = _run_side_casts(refs, MERGE_N_IN, 2, n_side)
    mod = mod_ref[0]
    g_refs = (g0_ref, g1_ref, g2_ref, g3_ref, g4_ref, g5_ref)
    half = D_MODEL // 2
    merged = None
    for k, y_ref in enumerate((ya_ref, yb_ref, yc_ref)):
        y = _dot(y_ref[...], wb_ref[k])
        parts = []
        for p in range(2):
            z = g_refs[2 * k + p][...].astype(F32) + bg_ref[:, k * D_MODEL + p * half:k * D_MODEL + (p + 1) * half]
            parts.append((1.0 + jnp.tanh(0.5 * z)) * y[:, p * half:(p + 1) * half])
        term = jnp.concatenate(parts, axis=1)
        merged = term if merged is None else merged + term
    merged = 0.5 * merged
    gate1 = mod[:, 2 * D_MODEL:3 * D_MODEL]
    x1 = x_ref[...] + gate1 * _dot(merged.astype(BF16), wo_ref[...])
    x1_ref[...] = x1
    shift2 = mod[:, 3 * D_MODEL:4 * D_MODEL]
    scale2 = mod[:, 4 * D_MODEL:5 * D_MODEL]
    h2_ref[...] = (_rms(x1) * n2_ref[...] * (1.0 + scale2) + shift2).astype(BF16)


def _merge(ya, yb, yc, proj, x, mod, mod_row0, rows_per_mod, bg, wb, wo, n2, layer, side=()):
    m = x.shape[0]
    tm = MERGE_TM
    per = rows_per_mod // tm
    half = D_MODEL // 2
    row = pl.BlockSpec((tm, D_MODEL), lambda i: (i, 0))
    gate = lambda c: pl.BlockSpec((tm, half), lambda i: (i, COL_G // half + c))
    side_in, side_out, side_shapes = _side_cast_specs(side, m // tm)
    in_specs = [
        row, row, row,
        gate(0), gate(1), gate(2), gate(3), gate(4), gate(5),
        row,
        _mod_spec(mod_row0, per),
        _resident((1, N_BRANCH * D_MODEL), layer),
        _whole((N_BRANCH, D_MODEL, D_MODEL)),
        _whole((D_MODEL, D_MODEL)),
        _resident((1, D_MODEL), layer),
    ]
    assert len(in_specs) == MERGE_N_IN
    out = pl.pallas_call(
        functools.partial(_merge_kernel, n_side=len(side)),
        grid=(m // tm,),
        in_specs=in_specs + side_in,
        out_specs=(row, row) + tuple(side_out),
        out_shape=(jax.ShapeDtypeStruct((m, D_MODEL), F32), jax.ShapeDtypeStruct((m, D_MODEL), BF16))
        + tuple(side_shapes),
        compiler_params=_params("parallel"),
        name="merge_out",
    )(ya, yb, yc, proj, proj, proj, proj, proj, proj, x, mod, bg, wb, wo, n2, *[item[0] for item in side])
    return out[0], out[1], list(out[2:])


FFN_CK = 256
FFN_TM = 512
FFN_GAP = SUBLANES
GELU_C = float(np.sqrt(2.0 / np.pi))


def _ffn_kernel(*refs, seq_len, final, n_side):
    refs = _run_side_casts(refs, 9 + int(final), 1, n_side)
    h_ref, hp_ref, hn_ref, x_ref, mod_ref, wup_ref, cw_ref, cb_ref, wd_ref = refs[:9]
    fn_ref = refs[9] if final else None
    o_ref, hx_ref, u_ref, act_ref = refs[-4:]
    tm = FFN_TM
    halo = seq_len > tm
    i = pl.program_id(0)

    if halo:
        per_seq = seq_len // tm
        at_start = i % per_seq == 0
        at_end = i % per_seq == per_seq - 1
        zeros = jnp.zeros((BF16_ROWS, D_MODEL), BF16)

        @pl.when(at_start)
        def _():
            hx_ref[0:BF16_ROWS, :] = zeros

        @pl.when(jnp.logical_not(at_start))
        def _():
            hx_ref[0:BF16_ROWS, :] = hp_ref[...]

        @pl.when(at_end)
        def _():
            hx_ref[BF16_ROWS + tm:, :] = zeros

        @pl.when(jnp.logical_not(at_end))
        def _():
            hx_ref[BF16_ROWS + tm:, :] = hn_ref[...]

        hx_ref[BF16_ROWS:BF16_ROWS + tm, :] = h_ref[...]
        bases = (BF16_ROWS,)
        seg = tm
    else:
        nseg = tm // seq_len
        seg = seq_len
        bases = tuple(FFN_GAP + s * (seg + FFN_GAP) for s in range(nseg))
        for s in range(nseg + 1):
            u_ref[s * (seg + FFN_GAP):s * (seg + FFN_GAP) + FFN_GAP, :] = jnp.zeros((FFN_GAP, FFN_CK), F32)

    def taps(offset):
        return jnp.concatenate([u_ref[b + offset:b + offset + seg, :] for b in bases], axis=0)

    for c in range(D_FF // FFN_CK):
        cs = slice(c * FFN_CK, (c + 1) * FFN_CK)
        vs = slice(D_FF + c * FFN_CK, D_FF + (c + 1) * FFN_CK)
        if halo:
            u_ext = _dot(hx_ref[...], wup_ref[:, cs])
            u_ref[...] = u_ext
            u0 = u_ext[BF16_ROWS:BF16_ROWS + tm]
        else:
            u0 = _dot(h_ref[...], wup_ref[:, cs])
            for s, b in enumerate(bases):
                u_ref[b:b + seg, :] = u0[s * seg:(s + 1) * seg]
        uv = _dot(h_ref[...], wup_ref[:, vs])
        gff = taps(-1) * cw_ref[0:1, cs] + u0 * cw_ref[1:2, cs] + taps(1) * cw_ref[2:3, cs] + cb_ref[:, cs]
        inner = gff * (GELU_C + (GELU_C * 0.044715) * (gff * gff))
        act_ref[:, cs] = (0.5 * (gff * uv) * (1.0 + jnp.tanh(inner))).astype(BF16)

    gate2 = mod_ref[0][:, 5 * D_MODEL:6 * D_MODEL]
    out = x_ref[...] + gate2 * _dot(act_ref[...], wd_ref[...])
    if final:
        out = _rms(out) * fn_ref[...]
    o_ref[...] = out


def _ffn(h2, x1, mod, mod_row0, rows_per_mod, wup, cw, cb, wd, layer, seq_len, final_norm, side=()):
    m = x1.shape[0]
    tm = FFN_TM
    per = rows_per_mod // tm
    hb = tm // BF16_ROWS
    last_halo = m // BF16_ROWS - 1
    final = final_norm is not None
    halo = seq_len > tm
    in_specs = [
        pl.BlockSpec((tm, D_MODEL), lambda i: (i, 0)),
        pl.BlockSpec((BF16_ROWS, D_MODEL), lambda i: (jnp.maximum(i * hb - 1, 0), 0)),
        pl.BlockSpec((BF16_ROWS, D_MODEL), lambda i: (jnp.minimum((i + 1) * hb, last_halo), 0)),
        pl.BlockSpec((tm, D_MODEL), lambda i: (i, 0)),
        _mod_spec(mod_row0, per),
        _whole((D_MODEL, 2 * D_FF)),
        _resident((3, D_FF), layer),
        _resident((1, D_FF), layer),
        _whole((D_FF, D_MODEL)),
    ]
    args = [h2, h2, h2, x1, mod, wup, cw, cb, wd]
    if final:
        in_specs.append(_whole((1, D_MODEL)))
        args.append(final_norm)
    if halo:
        u_rows = tm + 2 * BF16_ROWS
    else:
        u_rows = FFN_GAP + (tm // seq_len) * (seq_len + FFN_GAP)
    side_in, side_out, side_shapes = _side_cast_specs(side, m // tm)
    out = pl.pallas_call(
        functools.partial(_ffn_kernel, seq_len=seq_len, final=final, n_side=len(side)),
        grid=(m // tm,),
        in_specs=in_specs + side_in,
        out_specs=(pl.BlockSpec((tm, D_MODEL), lambda i: (i, 0)),) + tuple(side_out),
        out_shape=(jax.ShapeDtypeStruct((m, D_MODEL), F32),) + tuple(side_shapes),
        scratch_shapes=[
            pltpu.VMEM((tm + 2 * BF16_ROWS, D_MODEL), BF16),
            pltpu.VMEM((u_rows, FFN_CK), F32),
            pltpu.VMEM((tm, D_FF), BF16),
        ],
        compiler_params=_params("parallel"),
        name="conv_glu_ffn",
    )(*args, *[item[0] for item in side])
    return out[0], list(out[1:])


def _trunk_layer(x, mod, mod_row0, rows_per_mod, p, wts, layer, seq_len, ctx, final_norm, caches=None,
                 raw=None):
    m = x.shape[0]
    cast = raw is not None
    side = [_layer_slabs(raw[k], layer, m // INPROJ_TM) for k in ("w_branch", "w_out")] if cast else ()
    proj, k_new, v_new, yb, done = _inproj(x, mod, mod_row0, rows_per_mod, p["norm1"], wts["w_in"], layer,
                                           seq_len, want_kv=ctx is None, sink=p["sink"], caches=caches, side=side)
    if cast:
        wts["w_branch"] = done[0].reshape(N_BRANCH, D_MODEL, D_MODEL)
        wts["w_out"] = done[1]
    if ctx is None:
        ya, h_fin = _lru(proj, p["lru"], layer, seq_len, None)
    else:
        ck, cv, h0 = ctx
        ya = _lru(proj, p["lru"], layer, seq_len, h0)
        h_fin = None
        yb = _attn_lat(proj, ck, cv, p["sink"], layer, seq_len)
    yc = _pool(proj, p["pool_w"], p["pool_scale"], layer, seq_len)
    side = [_layer_slabs(raw[k], layer, m // MERGE_TM) for k in ("ffn_up", "ffn_down")] if cast else ()
    x1, h2, done = _merge(ya, yb, yc, proj, x, mod, mod_row0, rows_per_mod, p["b_gate"], wts["w_branch"],
                          wts["w_out"], p["norm2"], layer, side=side)
    if cast:
        wts["ffn_up"], wts["ffn_down"] = done
    side = [_layer_slabs(raw["w_in"], layer + 1, m // FFN_TM)] if cast and layer + 1 < DEPTH else ()
    out, done = _ffn(h2, x1, mod, mod_row0, rows_per_mod, wts["ffn_up"], p["ffn_conv"], p["ffn_conv_b"],
                     wts["ffn_down"], layer, seq_len, final_norm, side=side)
    return out, k_new, v_new, h_fin, (done[0] if side else None)


def _stacked_params(norm1, norm2, b_gate, lru_conv, lru_conv_b, lru_wa, lru_ba, lru_wx, lru_bx,
                    lru_lambda, attn_sink, pool_w, pool_scale, ffn_conv, ffn_conv_b):
    def per_head(v):
        return v.reshape(DEPTH, 2, LRU_HEADS, LRU_BLOCK).transpose(0, 2, 1, 3)

    row = lambda v: v[:, None, :]
    wg = jnp.concatenate([lru_wa[:, 0], lru_wx[:, 0], lru_wa[:, 1], lru_wx[:, 1]], axis=-1).astype(BF16)
    ba = per_head(lru_ba)
    bx = per_head(lru_bx)
    bg = jnp.concatenate([ba[:, :, 0], bx[:, :, 0], ba[:, :, 1], bx[:, :, 1]], axis=-1)[:, :, None, :]
    lam = per_head(lru_lambda).reshape(DEPTH, LRU_HEADS, 1, 2 * LRU_BLOCK)
    return {
        "norm1": row(norm1), "norm2": row(norm2), "b_gate": row(b_gate),
        "lru": {"conv_w": lru_conv, "conv_b": row(lru_conv_b), "wg": wg, "bg": bg, "lam": lam},
        "sink": attn_sink, "pool_w": pool_w.astype(BF16), "pool_scale": row(pool_scale),
        "ffn_conv": ffn_conv, "ffn_conv_b": row(ffn_conv_b),
    }


def kernel(x_prompt, x_sample, cache_k, cache_v, state_lru, c, c_ctx, w_ada, b_ada, norm1, norm2, w_in,
           b_gate, lru_conv, lru_conv_b, lru_wa, lru_ba, lru_wx, lru_bx, lru_lambda, attn_sink, pool_w,
           pool_scale, w_branch, w_out, ffn_up, ffn_conv, ffn_conv_b, ffn_down, final_norm):
    batch, seq, _ = x_prompt.shape
    dec_batch, dec_seq, _ = x_sample.shape
    past = cache_k.shape[2]
    assert seq == SCAN_CHUNK and dec_seq % SCAN_ROWS == 0 and (batch * seq) % SCAN_ROWS == 0

    c_rows = jnp.concatenate(
        [c_ctx[None], c, jnp.zeros((SUBLANES - 1 - dec_batch, D_MODEL), F32)], axis=0)
    mods = _ada(c_rows, w_ada, b_ada)

    xp = x_prompt.reshape(batch * seq, D_MODEL)
    xs = x_sample.reshape(dec_batch * dec_seq, D_MODEL)
    fn = final_norm[None]
    p = _stacked_params(norm1, norm2, b_gate, lru_conv, lru_conv_b, lru_wa, lru_ba, lru_wx, lru_bx,
                        lru_lambda, attn_sink, pool_w, pool_scale, ffn_conv, ffn_conv_b)
    raw = {"w_in": w_in, "w_branch": w_branch, "w_out": w_out, "ffn_up": ffn_up, "ffn_down": ffn_down}
    w_in_l = w_in[0].astype(BF16)
    mod_rows = mods.reshape(DEPTH * SUBLANES, 1, 6 * D_MODEL)
    ck = cache_k.reshape(dec_batch, DEPTH, past, D_KV).astype(BF16)
    cv = cache_v.reshape(dec_batch, DEPTH, past, D_KV).astype(BF16)
    h0 = state_lru.transpose(1, 2, 0, 3)[:, :, :, None, :]
    caches, hs = None, []
    for l in range(DEPTH):
        last = fn if l == DEPTH - 1 else None
        wts = {"w_in": w_in_l}
        xp, k_all, v_all, h_fin, w_in_l = _trunk_layer(xp, mod_rows, l * SUBLANES, batch * seq, p, wts, l, seq,
                                                       None, last, caches, raw)
        caches = (k_all, v_all)
        xs = _trunk_layer(xs, mod_rows, l * SUBLANES + 1, dec_seq, p, wts, l, dec_seq, (ck, cv, h0), last)[0]
        hs.append(h_fin.transpose(1, 0, 2))
    y_prompt = xp.reshape(batch, seq, D_MODEL)
    y_sample = xs.reshape(dec_batch, dec_seq, D_MODEL)
    cache_dims = (batch, DEPTH, seq, N_KV_HEADS, HEAD_DIM)
    return (y_prompt, y_sample, k_all.reshape(cache_dims), v_all.reshape(cache_dims), jnp.stack(hs, axis=1))
```

```python
import functools

import numpy as np
import jax
import jax.numpy as jnp
from jax import lax
from jax.experimental import pallas as pl
from jax.experimental.pallas import tpu as pltpu

F32 = jnp.float32
BF16 = jnp.bfloat16

D_MODEL = 1024
DEPTH = 2
GRID_W = 64
EPS = 1e-6
N_BRANCH = 3
D_RNN = 1024
LRU_HEADS = 8
LRU_BLOCK = D_RNN // LRU_HEADS
LRU_C = 8.0
N_HEADS = 8
N_KV_HEADS = 2
KV_GROUPS = N_HEADS // N_KV_HEADS
HEAD_DIM = 128
D_KV = N_KV_HEADS * HEAD_DIM
WINDOW = 128
BLOCK_Q = 128
ROPE_BASE = 10000.0
NEG_INF = -1e30
D_POOL = 1024
POOL_WINDOWS = (2, 4, 8, 16)
POOL_GROUP = D_POOL // len(POOL_WINDOWS)
D_FF = 2816
D_IN = D_RNN + N_HEADS * HEAD_DIM + 2 * D_KV + D_POOL + N_BRANCH * D_MODEL

COL_XA = 0
COL_Q = COL_XA + D_RNN
COL_K = COL_Q + N_HEADS * HEAD_DIM
COL_V = COL_K + D_KV
COL_XC = COL_V + D_KV
COL_G = COL_XC + D_POOL
LOG2E = float(np.log2(np.e))

VMEM_LIMIT_BYTES = 52 * 1024 * 1024
SUBLANES = 8
LANES = 128
BF16_ROWS = 16

SCAN_CHUNK = 256
SCAN_PITCH = 260
SCAN_ROWS = 2048
POOL_PAD = 8


def _params(*sem):
    return pltpu.CompilerParams(dimension_semantics=sem, vmem_limit_bytes=VMEM_LIMIT_BYTES)


def _dot(a, b):
    return jnp.dot(a, b, preferred_element_type=F32)


def _dot_nt(a, b):
    return lax.dot_general(a, b, (((1,), (1,)), ((), ())), preferred_element_type=F32)


def _sigmoid(z):
    return 0.5 * (1.0 + jnp.tanh(0.5 * z))


def _rms(x):
    return x * lax.rsqrt(jnp.mean(x * x, axis=-1, keepdims=True) + EPS)


def _ada_kernel(c_ref, w_ref, b_ref, o_ref):
    c = c_ref[...]
    s = c * _sigmoid(c)
    o_ref[0] = _dot(s.astype(BF16), w_ref[0].astype(BF16)) + b_ref[0]


def _ada(c_rows, w_ada, b_ada):
    tn = 1536
    return pl.pallas_call(
        _ada_kernel,
        grid=(DEPTH, 6 * D_MODEL // tn),
        in_specs=[
            pl.BlockSpec((SUBLANES, D_MODEL), lambda l, j: (0, 0)),
            pl.BlockSpec((1, D_MODEL, tn), lambda l, j: (l, 0, j)),
            pl.BlockSpec((1, 1, tn), lambda l, j: (l, 0, j)),
        ],
        out_specs=pl.BlockSpec((1, SUBLANES, tn), lambda l, j: (l, 0, j)),
        out_shape=jax.ShapeDtypeStruct((DEPTH, SUBLANES, 6 * D_MODEL), F32),
        compiler_params=_params("parallel", "parallel"),
        name="ada_mod",
    )(c_rows, w_ada, b_ada.reshape(DEPTH, 1, 6 * D_MODEL))


INPROJ_TM = 512
INPROJ_CK = 2 * D_KV


def _resident(shape, layer):
    ndim = len(shape)
    return pl.BlockSpec((None,) + tuple(shape), lambda *_: (layer,) + (0,) * ndim,
                        pipeline_mode=pl.Buffered(1))


def _whole(shape):
    return pl.BlockSpec(tuple(shape), lambda *_: (0,) * len(shape), pipeline_mode=pl.Buffered(1))


def _side_cast_specs(side, steps):
    in_specs = [pl.BlockSpec((rows, arr.shape[1]), lambda i, first=first: (first + i, 0))
                for arr, rows, first in side]
    out_specs = [pl.BlockSpec((rows, arr.shape[1]), lambda i: (i, 0)) for arr, rows, _ in side]
    out_shapes = [jax.ShapeDtypeStruct((rows * steps, arr.shape[1]), BF16) for arr, rows, _ in side]
    return in_specs, out_specs, out_shapes


def _run_side_casts(refs, n_in, n_out, n_side):
    refs = list(refs)
    side_in = refs[n_in:n_in + n_side]
    side_out = refs[n_in + n_side + n_out:n_in + 2 * n_side + n_out]
    for src, dst in zip(side_in, side_out):
        dst[...] = src[...].astype(BF16)
    return refs[:n_in] + refs[n_in + n_side:n_in + n_side + n_out] + refs[n_in + 2 * n_side + n_out:]


def _layer_slabs(stacked, layer, steps):
    cols = stacked.shape[-1]
    rows = int(np.prod(stacked.shape[1:-1]))
    assert rows % (steps * BF16_ROWS) == 0
    return (stacked.reshape(stacked.shape[0] * rows, cols), rows // steps, layer * steps)


def _mod_spec(row0, per):
    return pl.BlockSpec((1, 1, 6 * D_MODEL), lambda i: (row0 + i // per, 0, 0))


def _inproj_kernel(*refs, seq_len, want_kv, layer, owns_cache, n_in, n_side):
    refs = _run_side_casts(refs, n_in, 4 if want_kv else 1, n_side)
    _inproj_body(*refs, seq_len=seq_len, want_kv=want_kv, layer=layer, owns_cache=owns_cache)


def _inproj_body(*refs, seq_len, want_kv, layer, owns_cache):
    if want_kv:
        x_ref, mod_ref, g_ref, w_ref, sink_ref = refs[:5]
        o_ref, k_ref, v_ref, y_ref = refs[-4:]
        q_chunks = []
        if owns_cache:
            for other in range(DEPTH):
                if other != layer:
                    k_ref[:, other] = jnp.zeros(k_ref.shape[:1] + k_ref.shape[2:], F32)
                    v_ref[:, other] = jnp.zeros(v_ref.shape[:1] + v_ref.shape[2:], F32)
    else:
        x_ref, mod_ref, g_ref, w_ref, cos_ref, sa_ref, sb_ref, o_ref = refs
    mod = mod_ref[0]
    shift = mod[:, 0:D_MODEL]
    scale = mod[:, D_MODEL:2 * D_MODEL]
    h = (_rms(x_ref[...]) * g_ref[...] * (1.0 + scale) + shift).astype(BF16)
    nf = HEAD_DIM // 4

    def rope(x):
        return (x * cos_ref[...] + pltpu.roll(x, HEAD_DIM - nf, 1) * sa_ref[...]
                + pltpu.roll(x, nf, 1) * sb_ref[...])

    for c in range(D_IN // INPROJ_CK):
        acc = _dot(h, w_ref[:, c * INPROJ_CK:(c + 1) * INPROJ_CK])
        if not want_kv:
            lo = c * INPROJ_CK
            heads = [acc[:, j * HEAD_DIM:(j + 1) * HEAD_DIM] for j in range(INPROJ_CK // HEAD_DIM)]
            heads = [rope(hd) if COL_Q <= lo + j * HEAD_DIM < COL_V else hd for j, hd in enumerate(heads)]
            heads = [hd * SOFTMAX_SCALE if COL_Q <= lo + j * HEAD_DIM < COL_K else hd
                     for j, hd in enumerate(heads)]
            acc = jnp.concatenate(heads, axis=1)
        o_ref[:, c * INPROJ_CK:(c + 1) * INPROJ_CK] = acc.astype(o_ref.dtype)
        if want_kv and COL_Q <= c * INPROJ_CK < COL_K:
            q_chunks.append((acc * SOFTMAX_SCALE).astype(BF16))
        if want_kv and c == COL_K // INPROJ_CK:
            for b in range(INPROJ_TM // seq_len):
                rows = slice(b * seq_len, (b + 1) * seq_len)
                for hd in range(N_KV_HEADS):
                    dst = pl.ds(hd, seq_len, stride=N_KV_HEADS)
                    at = (b, layer, dst, slice(None)) if owns_cache else (b, dst, slice(None))
                    k_ref[at] = acc[rows, hd * HEAD_DIM:(hd + 1) * HEAD_DIM]
                    v_ref[at] = acc[rows, D_KV + hd * HEAD_DIM:D_KV + (hd + 1) * HEAD_DIM]
            kv = acc.astype(BF16)
            chains = []
            for b in range(INPROJ_TM // seq_len):
                rows = slice(b * seq_len, (b + 1) * seq_len)
                for kh in range(N_KV_HEADS):
                    q = jnp.concatenate([q_chunks[kh][rows, g * HEAD_DIM:(g + 1) * HEAD_DIM]
                                         for g in range(KV_GROUPS)], axis=0)
                    chains.append((b, kh, _dot_nt(kv[rows, kh * HEAD_DIM:(kh + 1) * HEAD_DIM], q),
                                   kv[rows, D_KV + kh * HEAD_DIM:D_KV + (kh + 1) * HEAD_DIM]))
            for b, kh, t, vals in chains:
                o_t = _softmax_pv_t(t, _sink_row(sink_ref, layer, kh, seq_len), vals)
                _store_heads_t(y_ref, o_t, kh, b * seq_len, seq_len)


def _inproj(x, mod, mod_row0, rows_per_mod, g, w, layer, seq_len, want_kv, sink=None, caches=None, side=()):
    m = x.shape[0]
    tm = INPROJ_TM
    per = rows_per_mod // tm
    proj_spec = pl.BlockSpec((tm, D_IN), lambda i: (i, 0))
    proj_shape = jax.ShapeDtypeStruct((m, D_IN), BF16)
    aliases = {}
    owns_cache = want_kv and caches is None
    if want_kv:
        nb = tm // seq_len
        if owns_cache:
            cache_spec = pl.BlockSpec((nb, DEPTH, seq_len * N_KV_HEADS, HEAD_DIM), lambda i: (i, 0, 0, 0))
        else:
            cache_spec = pl.BlockSpec((nb, None, seq_len * N_KV_HEADS, HEAD_DIM), lambda i: (i, layer, 0, 0))
        cache_shape = jax.ShapeDtypeStruct((m // seq_len, DEPTH, seq_len * N_KV_HEADS, HEAD_DIM), F32)
        assert tm % seq_len == 0 and INPROJ_CK == KV_GROUPS * HEAD_DIM and COL_Q % INPROJ_CK == 0
        attn_spec = pl.BlockSpec((tm, N_HEADS * HEAD_DIM), lambda i: (i, 0))
        out_specs = (proj_spec, cache_spec, cache_spec, attn_spec)
        out_shape = (proj_shape, cache_shape, cache_shape, jax.ShapeDtypeStruct((m, N_HEADS * HEAD_DIM), BF16))
        extra_specs, extra_args = [pl.BlockSpec(memory_space=pltpu.SMEM)], [sink]
        if caches is not None:
            extra_specs += [pl.BlockSpec(memory_space=pl.ANY)] * 2
            extra_args += list(caches)
            aliases = {5: 1, 6: 2}
    else:
        out_specs, out_shape = (proj_spec,), (proj_shape,)
        tab = pl.BlockSpec((tm, HEAD_DIM), lambda i: (i % (seq_len // tm), 0))
        extra_specs, extra_args = [tab, tab, tab], list(_rope_tables(seq_len))
    side_in, side_out, side_shapes = _side_cast_specs(side, m // tm)
    out = pl.pallas_call(
        functools.partial(_inproj_kernel, seq_len=seq_len, want_kv=want_kv, layer=layer, owns_cache=owns_cache,
                          n_in=4 + len(extra_specs), n_side=len(side)),
        grid=(m // tm,),
        in_specs=[
            pl.BlockSpec((tm, D_MODEL), lambda i: (i, 0)),
            _mod_spec(mod_row0, per),
            _resident((1, D_MODEL), layer),
            _whole((D_MODEL, D_IN)),
        ] + extra_specs + side_in,
        out_specs=tuple(out_specs) + tuple(side_out),
        out_shape=tuple(out_shape) + tuple(side_shapes),
        input_output_aliases=aliases,
        compiler_params=_params("parallel"),
        name="in_proj",
    )(x, mod, g, w, *extra_args, *[item[0] for item in side])
    n_own = len(out_specs)
    own = tuple(out[:n_own]) if want_kv else (out[0], None, None, None)
    return own + (list(out[n_own:]),)


LRU_HEADS_PER_STEP = 2


def _chunk_neighbour(v, towards_later):
    sub = lax.broadcasted_iota(jnp.int32, v.shape, 0)
    if towards_later:
        return jnp.where(sub >= 1, pltpu.roll(v, 1, 0), 0.0)
    return jnp.where(sub <= SUBLANES - 2, pltpu.roll(v, SUBLANES - 1, 0), 0.0)


def _shift_time(x_tm, k, chained):
    n = x_tm.shape[0]
    steps = abs(k)
    edge = []
    for s in range(steps):
        if not chained:
            edge.append(jnp.zeros((SUBLANES, x_tm.shape[1]), F32))
        elif k > 0:
            src = n - (steps - s) * SUBLANES
            edge.append(_chunk_neighbour(x_tm[src:src + SUBLANES], True))
        else:
            edge.append(_chunk_neighbour(x_tm[s * SUBLANES:(s + 1) * SUBLANES], False))
    if k > 0:
        return jnp.concatenate(edge + [x_tm[:n - steps * SUBLANES]], axis=0)
    return jnp.concatenate([x_tm[steps * SUBLANES:]] + edge, axis=0)


def _lru_tm_kernel(*refs, seq_len, latent):
    if latent:
        (x_ref, cw_ref, cb_ref, wg_ref, bg_ref, lam_ref, h0_ref, y_ref,
         io_ref, af_ref, bf_ref, ab_ref, bb_ref, hf_ref, hb_ref) = refs
    else:
        (x_ref, cw_ref, cb_ref, wg_ref, bg_ref, lam_ref, y_ref, fin_ref,
         io_ref, af_ref, bf_ref, ab_ref, bb_ref, hf_ref, hb_ref) = refs
    rows = SCAN_ROWS
    nchunk = rows // SCAN_CHUNK
    nhead = LRU_HEADS_PER_STEP
    chained = seq_len > SCAN_CHUNK
    for hd in range(nhead):
        lanes = slice(hd * LRU_BLOCK, (hd + 1) * LRU_BLOCK)
        x = x_ref[:, lanes].astype(F32)
        for c in range(nchunk):
            io_ref[hd, pl.ds(c * SCAN_PITCH, SCAN_CHUNK), :] = x[c * SCAN_CHUNK:(c + 1) * SCAN_CHUNK]
        x = jnp.concatenate(
            [io_ref[hd, pl.ds(t, nchunk, stride=SCAN_PITCH), :] for t in range(SCAN_CHUNK)], axis=0)
        xc = (_shift_time(x, 2, chained) * cw_ref[0:1, lanes] + _shift_time(x, 1, chained) * cw_ref[1:2, lanes]
              + x * cw_ref[2:3, lanes] + _shift_time(x, -1, chained) * cw_ref[3:4, lanes] + cb_ref[:, lanes])
        th = jnp.tanh(_dot(xc.astype(BF16), wg_ref[hd] * 0.5) + 0.5 * bg_ref[hd])
        xh = 0.5 * xc
        lam = lam_ref[hd]
        for d, (a_ref, b_ref) in enumerate(((af_ref, bf_ref), (ab_ref, bb_ref))):
            th_r = th[:, 2 * d * LRU_BLOCK:(2 * d + 1) * LRU_BLOCK]
            th_i = th[:, (2 * d + 1) * LRU_BLOCK:(2 * d + 2) * LRU_BLOCK]
            nl = -lam[:, d * LRU_BLOCK:(d + 1) * LRU_BLOCK]
            softplus = jnp.maximum(nl, 0.0) + jnp.log(1.0 + jnp.exp(-jnp.abs(nl)))
            ch = (-0.5 * LRU_C * LOG2E) * softplus
            a = jnp.exp2(ch + ch * th_r)
            y = 1.0 - a * a
            a_ref[hd] = a
            b_ref[hd] = (y * lax.rsqrt(jnp.maximum(y, 1e-30))) * ((1.0 + th_i) * xh)

    def step(t, carry):
        rf = pl.ds(pl.multiple_of(t * nchunk, nchunk), nchunk)
        rb = pl.ds(pl.multiple_of((SCAN_CHUNK - 1 - t) * nchunk, nchunk), nchunk)
        out = []
        for hd in range(nhead):
            hf, hb, pf, pb = carry[4 * hd:4 * hd + 4]
            a_f = af_ref[hd, rf, :]
            a_b = ab_ref[hd, rb, :]
            hf = a_f * hf + bf_ref[hd, rf, :]
            hb = a_b * hb + bb_ref[hd, rb, :]
            hf_ref[hd, rf, :] = hf
            hb_ref[hd, rb, :] = hb
            if chained:
                pf = pf * a_f
                pb = pb * a_b
                af_ref[hd, rf, :] = pf
                ab_ref[hd, rb, :] = pb
            out += [hf, hb, pf, pb]
        return tuple(out)

    zero = jnp.zeros((nchunk, LRU_BLOCK), F32)
    one = jnp.ones((nchunk, LRU_BLOCK), F32)
    lax.fori_loop(0, SCAN_CHUNK, step, (zero, zero, one, one) * nhead, unroll=8)

    sub = lax.broadcasted_iota(jnp.int32, (nchunk, LRU_BLOCK), 0)
    last = slice(rows - nchunk, rows)
    first = slice(0, nchunk)
    for hd in range(nhead):
        lanes = slice(hd * LRU_BLOCK, (hd + 1) * LRU_BLOCK)
        hf = hf_ref[hd]
        hb = hb_ref[hd]
        if chained:
            pf = af_ref[hd]
            pb = ab_ref[hd]
            ef = jnp.where(sub == 0, h0_ref[0, 0][:, lanes], 0.0)
            eb = jnp.where(sub == nchunk - 1, h0_ref[1, 0][:, lanes], 0.0)
            for c in range(1, nchunk):
                ef = jnp.where(sub == c, pltpu.roll(hf[last] + pf[last] * ef, 1, 0), ef)
                eb = jnp.where(sub == nchunk - 1 - c,
                               pltpu.roll(hb[first] + pb[first] * eb, nchunk - 1, 0), eb)
            hf = (hf.reshape(SCAN_CHUNK, nchunk, LRU_BLOCK)
                  + pf.reshape(SCAN_CHUNK, nchunk, LRU_BLOCK) * ef[None]).reshape(rows, LRU_BLOCK)
            hb = (hb.reshape(SCAN_CHUNK, nchunk, LRU_BLOCK)
                  + pb.reshape(SCAN_CHUNK, nchunk, LRU_BLOCK) * eb[None]).reshape(rows, LRU_BLOCK)
        else:
            fin_ref[0, :, lanes] = hf[last]
            fin_ref[1, :, lanes] = hb[first]
        y = hf + hb
        for t in range(SCAN_CHUNK):
            io_ref[hd, pl.ds(t, nchunk, stride=SCAN_PITCH), :] = y[t * nchunk:(t + 1) * nchunk]
        for c in range(nchunk):
            y_ref[pl.ds(c * SCAN_CHUNK, SCAN_CHUNK), lanes] = (
                io_ref[hd, pl.ds(c * SCAN_PITCH, SCAN_CHUNK), :].astype(y_ref.dtype))


def _lru(proj, lp, layer, seq_len, h0):
    m = proj.shape[0]
    latent = h0 is not None
    rows = SCAN_ROWS
    nhead = LRU_HEADS_PER_STEP
    width = nhead * LRU_BLOCK
    in_specs = [
        pl.BlockSpec((rows, width), lambda r, h: (r, COL_XA // width + h)),
        pl.BlockSpec((None, 4, width), lambda r, h: (layer, 0, h)),
        pl.BlockSpec((None, 1, width), lambda r, h: (layer, 0, h)),
        pl.BlockSpec((None, nhead, LRU_BLOCK, 4 * LRU_BLOCK), lambda r, h: (layer, h, 0, 0)),
        pl.BlockSpec((None, nhead, 1, 4 * LRU_BLOCK), lambda r, h: (layer, h, 0, 0)),
        pl.BlockSpec((None, nhead, 1, 2 * LRU_BLOCK), lambda r, h: (layer, h, 0, 0)),
    ]
    args = [proj, lp["conv_w"], lp["conv_b"], lp["wg"], lp["bg"], lp["lam"]]
    y_spec = pl.BlockSpec((rows, width), lambda r, h: (r, h))
    y_shape = jax.ShapeDtypeStruct((m, D_RNN), BF16)
    if latent:
        in_specs.append(pl.BlockSpec((None, 2, 1, 1, width), lambda r, h: (layer, 0, r, 0, h)))
        args.append(h0)
        out_specs, out_shape = y_spec, y_shape
    else:
        nseq = m // seq_len
        out_specs = (y_spec, pl.BlockSpec((2, rows // seq_len, width), lambda r, h: (0, r, h)))
        out_shape = (y_shape, jax.ShapeDtypeStruct((2, nseq, D_RNN), F32))
    strided_buf = pltpu.VMEM((nhead, rows // SCAN_CHUNK * SCAN_PITCH, LRU_BLOCK), F32)
    scan_buf = pltpu.VMEM((nhead, rows, LRU_BLOCK), F32)
    return pl.pallas_call(
        functools.partial(_lru_tm_kernel, seq_len=seq_len, latent=latent),
        grid=(m // rows, LRU_HEADS // nhead),
        in_specs=in_specs,
        out_specs=out_specs,
        out_shape=out_shape,
        scratch_shapes=[strided_buf] + [scan_buf] * 6,
        compiler_params=_params("parallel", "parallel"),
        name="rglru_latent" if latent else "rglru_context",
    )(*args)


SOFTMAX_SCALE = HEAD_DIM ** -0.5 * LOG2E


def _stack_heads(q_ref, kh, r0, rows):
    parts = [q_ref[r0:r0 + rows, (kh * KV_GROUPS + g) * HEAD_DIM:(kh * KV_GROUPS + g + 1) * HEAD_DIM]
             for g in range(KV_GROUPS)]
    return jnp.concatenate(parts, axis=0)


def _dot_tn(a, b):
    return lax.dot_general(a, b, (((0,), (0,)), ((), ())), preferred_element_type=F32)


def _sink_row(sink_ref, layer, kh, cols):
    parts = [jnp.full((1, cols), sink_ref[layer, kh * KV_GROUPS + g] * LOG2E, F32) for g in range(KV_GROUPS)]
    return jnp.concatenate(parts, axis=1)


def _softmax_pv_t(t, sink, v):
    m = jnp.maximum(jnp.max(t, axis=0, keepdims=True), sink)
    e = jnp.exp2(t - m).astype(BF16)
    d = v.shape[1]
    v_ones = jnp.concatenate([v, jnp.ones((v.shape[0], SUBLANES), BF16)], axis=1)
    o_sum = _dot_tn(v_ones, e)
    denom = o_sum[d:d + 1] + jnp.exp2(sink - m)
    return o_sum[:d] * (1.0 / denom)


def _store_heads_t(y_ref, o_t, kh, r0, rows):
    for g in range(KV_GROUPS):
        h = kh * KV_GROUPS + g
        y_ref[r0:r0 + rows, h * HEAD_DIM:(h + 1) * HEAD_DIM] = (
            o_t[:, g * rows:(g + 1) * rows].T.astype(y_ref.dtype))


def _rope_tables(seq_len):
    nf = HEAD_DIM // 4
    freqs = ROPE_BASE ** (-np.arange(nf, dtype=np.float64) / nf)
    t = np.arange(seq_len)
    ang_row = (t // GRID_W)[:, None] * freqs[None, :]
    ang_col = (t % GRID_W)[:, None] * freqs[None, :]
    ang = np.concatenate([ang_row, ang_row, ang_col, ang_col], axis=1)
    first = (np.arange(HEAD_DIM) % (2 * nf)) < nf
    cos = np.cos(ang)
    sin = np.sin(ang)
    sin_a = np.where(first[None, :], -sin, 0.0)
    sin_b = np.where(first[None, :], 0.0, sin)
    return tuple(jnp.asarray(a, F32) for a in (cos, sin_a, sin_b))


ATTN_LAT_BLOCKS = 4


def _attn_lat_kernel(sink_ref, q_ref, kvp_ref, kvc_ref, kvn_ref, ck_ref, cv_ref, y_ref, *, nblk, layer):
    step = pl.program_id(1)
    cols = KV_GROUPS * BLOCK_Q
    span = 3 * BLOCK_Q
    key = lax.broadcasted_iota(jnp.int32, (span, cols), 0)
    qry = lax.broadcasted_iota(jnp.int32, (span, cols), 1) & (BLOCK_Q - 1)
    kv = jnp.concatenate([kvp_ref[...], kvc_ref[...], kvn_ref[...]], axis=0)
    chains = []
    for i in range(ATTN_LAT_BLOCKS):
        win = kv[i * BLOCK_Q:i * BLOCK_Q + span]
        for kh in range(N_KV_HEADS):
            sl = slice(kh * HEAD_DIM, (kh + 1) * HEAD_DIM)
            vl = slice(D_KV + kh * HEAD_DIM, D_KV + (kh + 1) * HEAD_DIM)
            q = _stack_heads(q_ref, kh, i * BLOCK_Q, BLOCK_Q)
            keys = jnp.concatenate([win[:, sl], ck_ref[0, :, sl]], axis=0)
            vals = jnp.concatenate([win[:, vl], cv_ref[0, :, sl]], axis=0)
            chains.append((i, kh, _dot_nt(keys, q), vals))
    for i, kh, t, vals in chains:
        j = step * ATTN_LAT_BLOCKS + i
        lo = jnp.where(j > 0, qry, BLOCK_Q)
        hi = jnp.where(j < nblk - 1, qry + 2 * BLOCK_Q, 2 * BLOCK_Q - 1)
        bias = jnp.where(jnp.logical_and(key >= lo, key <= hi), 0.0, NEG_INF)
        t = jnp.concatenate([t[:span] + bias, t[span:]], axis=0)
        o_t = _softmax_pv_t(t, _sink_row(sink_ref, layer, kh, BLOCK_Q), vals)
        _store_heads_t(y_ref, o_t, kh, i * BLOCK_Q, BLOCK_Q)


def _attn_lat(proj, ck, cv, sink, layer, seq_len):
    m = proj.shape[0]
    nblk = seq_len // BLOCK_Q
    nstep = nblk // ATTN_LAT_BLOCKS
    rows = ATTN_LAT_BLOCKS * BLOCK_Q
    nb = m // seq_len
    past = ck.shape[2]
    kv_col = COL_K // (2 * D_KV)
    assert COL_V == COL_K + D_KV and COL_K % (2 * D_KV) == 0

    def halo(shift):
        def index(b, s):
            return (b * nblk + jnp.clip(s * ATTN_LAT_BLOCKS + shift, 0, nblk - 1), kv_col)
        return index

    return pl.pallas_call(
        functools.partial(_attn_lat_kernel, nblk=nblk, layer=layer),
        grid=(nb, nstep),
        in_specs=[
            pl.BlockSpec(memory_space=pltpu.SMEM),
            pl.BlockSpec((rows, N_HEADS * HEAD_DIM), lambda b, s: (b * nstep + s, COL_Q // (N_HEADS * HEAD_DIM))),
            pl.BlockSpec((BLOCK_Q, 2 * D_KV), halo(-1)),
            pl.BlockSpec((rows, 2 * D_KV), lambda b, s: (b * nstep + s, kv_col)),
            pl.BlockSpec((BLOCK_Q, 2 * D_KV), halo(ATTN_LAT_BLOCKS)),
            pl.BlockSpec((1, None, past, D_KV), lambda b, s: (b, layer, 0, 0)),
            pl.BlockSpec((1, None, past, D_KV), lambda b, s: (b, layer, 0, 0)),
        ],
        out_specs=pl.BlockSpec((rows, N_HEADS * HEAD_DIM), lambda b, s: (b * nstep + s, 0)),
        out_shape=jax.ShapeDtypeStruct((m, N_HEADS * HEAD_DIM), BF16),
        compiler_params=_params("parallel", "parallel"),
        name="attn_latent",
    )(sink, proj, proj, proj, proj, ck, cv)


POOL_TILE = 256
POOL_LEAD = BF16_ROWS
POOL_ROWS = 2048


def _pool_plan(seq_len):
    lead = 0 if seq_len == POOL_TILE else POOL_LEAD
    return lead, POOL_TILE - 2 * lead


def _pool_bands(seq_len):
    lead, nout = _pool_plan(seq_len)
    r = np.arange(nout)[:, None]
    c = np.arange(POOL_TILE)[None, :] - lead
    bands = [(c >= r - win // 2) & (c < r + win // 2) for win in POOL_WINDOWS]
    return jnp.asarray(np.stack(bands), BF16)


def _pool_kernel(x0_ref, x1_ref, x2_ref, x3_ref, band_ref, w_ref, s_ref, y_ref, pad_ref, *, seq_len):
    lead, nout = _pool_plan(seq_len)
    edge = lax.broadcasted_iota(jnp.int32, (SUBLANES, POOL_GROUP), 0)
    if lead:
        pad_ref[0:lead, :] = jnp.zeros((lead, POOL_GROUP), BF16)
        pad_ref[lead + seq_len:, :] = jnp.zeros((pad_ref.shape[0] - lead - seq_len, POOL_GROUP), BF16)
    for gi, (win, x_ref) in enumerate(zip(POOL_WINDOWS, (x0_ref, x1_ref, x2_ref, x3_ref))):
        cs = slice(gi * POOL_GROUP, (gi + 1) * POOL_GROUP)
        half = win // 2
        inv_head = 1.0 / ((edge + half) - jnp.maximum(edge - half, 0)).astype(F32)
        inv_tail = 1.0 / (jnp.minimum(SUBLANES - edge, half) + half).astype(F32)
        if lead:
            pad_ref[lead:lead + seq_len, :] = x_ref[...]
        for base in range(0, POOL_ROWS, seq_len):
            for p0 in range(0, seq_len, nout):
                n = min(nout, seq_len - p0)
                x = x_ref[base + p0:base + p0 + n, :]
                src = pad_ref[p0:p0 + POOL_TILE, :] if lead else x
                sums = _dot(band_ref[gi, :n, :], src)
                head = sums[:SUBLANES] * (inv_head if p0 == 0 else 1.0 / win)
                tail = sums[n - SUBLANES:] * (inv_tail if p0 + n == seq_len else 1.0 / win)
                mean = jnp.concatenate([head, sums[SUBLANES:n - SUBLANES] * (1.0 / win), tail], axis=0)
                pooled = (mean - x.astype(F32)).astype(BF16)
                y_ref[base + p0:base + p0 + n, cs] = (
                    _dot(pooled, w_ref[gi]) * s_ref[:, cs]).astype(y_ref.dtype)


def _pool(proj, w, s, layer, seq_len):
    m = proj.shape[0]
    rows = POOL_ROWS
    lead, nout = _pool_plan(seq_len)
    assert seq_len in (POOL_TILE, rows) and max(POOL_WINDOWS) // 2 <= min(SUBLANES, lead or SUBLANES)
    pad_rows = (pl.cdiv(seq_len, nout) - 1) * nout + POOL_TILE
    group = lambda gi: pl.BlockSpec((rows, POOL_GROUP), lambda r: (r, COL_XC // POOL_GROUP + gi))
    whole = lambda shape: pl.BlockSpec(shape, lambda r: (0,) * len(shape))
    nwin = len(POOL_WINDOWS)
    return pl.pallas_call(
        functools.partial(_pool_kernel, seq_len=seq_len),
        grid=(m // rows,),
        in_specs=[
            group(0), group(1), group(2), group(3),
            whole((nwin, nout, POOL_TILE)),
            _resident((nwin, POOL_GROUP, POOL_GROUP), layer),
            _resident((1, D_POOL), layer),
        ],
        out_specs=pl.BlockSpec((rows, D_POOL), lambda r: (r, 0)),
        out_shape=jax.ShapeDtypeStruct((m, D_POOL), BF16),
        scratch_shapes=[pltpu.VMEM((pad_rows, POOL_GROUP), BF16)],
        compiler_params=_params("parallel"),
        name="pool_mix",
    )(proj, proj, proj, proj, _pool_bands(seq_len), w, s)


MERGE_N_IN = 15
MERGE_TM = 512


def _merge_kernel(*refs, n_side):
    (ya_ref, yb_ref, yc_ref, g0_ref, g1_ref, g2_ref, g3_ref, g4_ref, g5_ref, x_ref, mod_ref,
     bg_ref, wb_ref, wo_ref, n2_ref, x1_ref, h2_ref) = _run_side_casts(refs, MERGE_N_IN, 2, n_side)
    mod = mod_ref[0]
    g_refs = (g0_ref, g1_ref, g2_ref, g3_ref, g4_ref, g5_ref)
    half = D_MODEL // 2
    merged = None
    for k, y_ref in enumerate((ya_ref, yb_ref, yc_ref)):
        y = _dot(y_ref[...], wb_ref[k])
        parts = []
        for p in range(2):
            z = g_refs[2 * k + p][...].astype(F32) + bg_ref[:, k * D_MODEL + p * half:k * D_MODEL + (p + 1) * half]
            parts.append((1.0 + jnp.tanh(0.5 * z)) * y[:, p * half:(p + 1) * half])
        term = jnp.concatenate(parts, axis=1)
        merged = term if merged is None else merged + term
    merged = 0.5 * merged
    gate1 = mod[:, 2 * D_MODEL:3 * D_MODEL]
    x1 = x_ref[...] + gate1 * _dot(merged.astype(BF16), wo_ref[...])
    x1_ref[...] = x1
    shift2 = mod[:, 3 * D_MODEL:4 * D_MODEL]
    scale2 = mod[:, 4 * D_MODEL:5 * D_MODEL]
    h2_ref[...] = (_rms(x1) * n2_ref[...] * (1.0 + scale2) + shift2).astype(BF16)


def _merge(ya, yb, yc, proj, x, mod, mod_row0, rows_per_mod, bg, wb, wo, n2, layer, side=()):
    m = x.shape[0]
    tm = MERGE_TM
    per = rows_per_mod // tm
    half = D_MODEL // 2
    row = pl.BlockSpec((tm, D_MODEL), lambda i: (i, 0))
    gate = lambda c: pl.BlockSpec((tm, half), lambda i: (i, COL_G // half + c))
    side_in, side_out, side_shapes = _side_cast_specs(side, m // tm)
    in_specs = [
        row, row, row,
        gate(0), gate(1), gate(2), gate(3), gate(4), gate(5),
        row,
        _mod_spec(mod_row0, per),
        _resident((1, N_BRANCH * D_MODEL), layer),
        _whole((N_BRANCH, D_MODEL, D_MODEL)),
        _whole((D_MODEL, D_MODEL)),
        _resident((1, D_MODEL), layer),
    ]
    assert len(in_specs) == MERGE_N_IN
    out = pl.pallas_call(
        functools.partial(_merge_kernel, n_side=len(side)),
        grid=(m // tm,),
        in_specs=in_specs + side_in,
        out_specs=(row, row) + tuple(side_out),
        out_shape=(jax.ShapeDtypeStruct((m, D_MODEL), F32), jax.ShapeDtypeStruct((m, D_MODEL), BF16))
        + tuple(side_shapes),
        compiler_params=_params("parallel"),
        name="merge_out",
    )(ya, yb, yc, proj, proj, proj, proj, proj, proj, x, mod, bg, wb, wo, n2, *[item[0] for item in side])
    return out[0], out[1], list(out[2:])


FFN_CK = 256
FFN_TM = 512
FFN_GAP = SUBLANES
GELU_C = float(np.sqrt(2.0 / np.pi))


def _ffn_kernel(*refs, seq_len, final, n_side):
    refs = _run_side_casts(refs, 9 + int(final), 1, n_side)
    h_ref, hp_ref, hn_ref, x_ref, mod_ref, wup_ref, cw_ref, cb_ref, wd_ref = refs[:9]
    fn_ref = refs[9] if final else None
    o_ref, hx_ref, u_ref, act_ref = refs[-4:]
    tm = FFN_TM
    halo = seq_len > tm
    i = pl.program_id(0)

    if halo:
        per_seq = seq_len // tm
        at_start = i % per_seq == 0
        at_end = i % per_seq == per_seq - 1
        zeros = jnp.zeros((BF16_ROWS, D_MODEL), BF16)

        @pl.when(at_start)
        def _():
            hx_ref[0:BF16_ROWS, :] = zeros

        @pl.when(jnp.logical_not(at_start))
        def _():
            hx_ref[0:BF16_ROWS, :] = hp_ref[...]

        @pl.when(at_end)
        def _():
            hx_ref[BF16_ROWS + tm:, :] = zeros

        @pl.when(jnp.logical_not(at_end))
        def _():
            hx_ref[BF16_ROWS + tm:, :] = hn_ref[...]

        hx_ref[BF16_ROWS:BF16_ROWS + tm, :] = h_ref[...]
        bases = (BF16_ROWS,)
        seg = tm
    else:
        nseg = tm // seq_len
        seg = seq_len
        bases = tuple(FFN_GAP + s * (seg + FFN_GAP) for s in range(nseg))
        for s in range(nseg + 1):
            u_ref[s * (seg + FFN_GAP):s * (seg + FFN_GAP) + FFN_GAP, :] = jnp.zeros((FFN_GAP, FFN_CK), F32)

    def taps(offset):
        return jnp.concatenate([u_ref[b + offset:b + offset + seg, :] for b in bases], axis=0)

    for c in range(D_FF // FFN_CK):
        cs = slice(c * FFN_CK, (c + 1) * FFN_CK)
        vs = slice(D_FF + c * FFN_CK, D_FF + (c + 1) * FFN_CK)
        if halo:
            u_ext = _dot(hx_ref[...], wup_ref[:, cs])
            u_ref[...] = u_ext
            u0 = u_ext[BF16_ROWS:BF16_ROWS + tm]
        else:
            u0 = _dot(h_ref[...], wup_ref[:, cs])
            for s, b in enumerate(bases):
                u_ref[b:b + seg, :] = u0[s * seg:(s + 1) * seg]
        uv = _dot(h_ref[...], wup_ref[:, vs])
        gff = taps(-1) * cw_ref[0:1, cs] + u0 * cw_ref[1:2, cs] + taps(1) * cw_ref[2:3, cs] + cb_ref[:, cs]
        inner = gff * (GELU_C + (GELU_C * 0.044715) * (gff * gff))
        act_ref[:, cs] = (0.5 * (gff * uv) * (1.0 + jnp.tanh(inner))).astype(BF16)

    gate2 = mod_ref[0][:, 5 * D_MODEL:6 * D_MODEL]
    out = x_ref[...] + gate2 * _dot(act_ref[...], wd_ref[...])
    if final:
        out = _rms(out) * fn_ref[...]
    o_ref[...] = out


def _ffn(h2, x1, mod, mod_row0, rows_per_mod, wup, cw, cb, wd, layer, seq_len, final_norm, side=()):
    m = x1.shape[0]
    tm = FFN_TM
    per = rows_per_mod // tm
    hb = tm // BF16_ROWS
    last_halo = m // BF16_ROWS - 1
    final = final_norm is not None
    halo = seq_len > tm
    in_specs = [
        pl.BlockSpec((tm, D_MODEL), lambda i: (i, 0)),
        pl.BlockSpec((BF16_ROWS, D_MODEL), lambda i: (jnp.maximum(i * hb - 1, 0), 0)),
        pl.BlockSpec((BF16_ROWS, D_MODEL), lambda i: (jnp.minimum((i + 1) * hb, last_halo), 0)),
        pl.BlockSpec((tm, D_MODEL), lambda i: (i, 0)),
        _mod_spec(mod_row0, per),
        _whole((D_MODEL, 2 * D_FF)),
        _resident((3, D_FF), layer),
        _resident((1, D_FF), layer),
        _whole((D_FF, D_MODEL)),
    ]
    args = [h2, h2, h2, x1, mod, wup, cw, cb, wd]
    if final:
        in_specs.append(_whole((1, D_MODEL)))
        args.append(final_norm)
    if halo:
        u_rows = tm + 2 * BF16_ROWS
    else:
        u_rows = FFN_GAP + (tm // seq_len) * (seq_len + FFN_GAP)
    side_in, side_out, side_shapes = _side_cast_specs(side, m // tm)
    out = pl.pallas_call(
        functools.partial(_ffn_kernel, seq_len=seq_len, final=final, n_side=len(side)),
        grid=(m // tm,),
        in_specs=in_specs + side_in,
        out_specs=(pl.BlockSpec((tm, D_MODEL), lambda i: (i, 0)),) + tuple(side_out),
        out_shape=(jax.ShapeDtypeStruct((m, D_MODEL), F32),) + tuple(side_shapes),
        scratch_shapes=[
            pltpu.VMEM((tm + 2 * BF16_ROWS, D_MODEL), BF16),
            pltpu.VMEM((u_rows, FFN_CK), F32),
            pltpu.VMEM((tm, D_FF), BF16),
        ],
        compiler_params=_params("parallel"),
        name="conv_glu_ffn",
    )(*args, *[item[0] for item in side])
    return out[0], list(out[1:])


def _trunk_layer(x, mod, mod_row0, rows_per_mod, p, wts, layer, seq_len, ctx, final_norm, caches=None,
                 raw=None):
    m = x.shape[0]
    cast = raw is not None
    side = [_layer_slabs(raw[k], layer, m // INPROJ_TM) for k in ("w_branch", "w_out")] if cast else ()
    proj, k_new, v_new, yb, done = _inproj(x, mod, mod_row0, rows_per_mod, p["norm1"], wts["w_in"], layer,
                                           seq_len, want_kv=ctx is None, sink=p["sink"], caches=caches, side=side)
    if cast:
        wts["w_branch"] = done[0].reshape(N_BRANCH, D_MODEL, D_MODEL)
        wts["w_out"] = done[1]
    if ctx is None:
        ya, h_fin = _lru(proj, p["lru"], layer, seq_len, None)
    else:
        ck, cv, h0 = ctx
        ya = _lru(proj, p["lru"], layer, seq_len, h0)
        h_fin = None
        yb = _attn_lat(proj, ck, cv, p["sink"], layer, seq_len)
    yc = _pool(proj, p["pool_w"], p["pool_scale"], layer, seq_len)
    side = [_layer_slabs(raw[k], layer, m // MERGE_TM) for k in ("ffn_up", "ffn_down")] if cast else ()
    x1, h2, done = _merge(ya, yb, yc, proj, x, mod, mod_row0, rows_per_mod, p["b_gate"], wts["w_branch"],
                          wts["w_out"], p["norm2"], layer, side=side)
    if cast:
        wts["ffn_up"], wts["ffn_down"] = done
    side = [_layer_slabs(raw["w_in"], layer + 1, m // FFN_TM)] if cast and layer + 1 < DEPTH else ()
    out, done = _ffn(h2, x1, mod, mod_row0, rows_per_mod, wts["ffn_up"], p["ffn_conv"], p["ffn_conv_b"],
                     wts["ffn_down"], layer, seq_len, final_norm, side=side)
    return out, k_new, v_new, h_fin, (done[0] if side else None)


def _stacked_params(norm1, norm2, b_gate, lru_conv, lru_conv_b, lru_wa, lru_ba, lru_wx, lru_bx,
                    lru_lambda, attn_sink, pool_w, pool_scale, ffn_conv, ffn_conv_b):
    def per_head(v):
        return v.reshape(DEPTH, 2, LRU_HEADS, LRU_BLOCK).transpose(0, 2, 1, 3)

    row = lambda v: v[:, None, :]
    wg = jnp.concatenate([lru_wa[:, 0], lru_wx[:, 0], lru_wa[:, 1], lru_wx[:, 1]], axis=-1).astype(BF16)
    ba = per_head(lru_ba)
    bx = per_head(lru_bx)
    bg = jnp.concatenate([ba[:, :, 0], bx[:, :, 0], ba[:, :, 1], bx[:, :, 1]], axis=-1)[:, :, None, :]
    lam = per_head(lru_lambda).reshape(DEPTH, LRU_HEADS, 1, 2 * LRU_BLOCK)
    return {
        "norm1": row(norm1), "norm2": row(norm2), "b_gate": row(b_gate),
        "lru": {"conv_w": lru_conv, "conv_b": row(lru_conv_b), "wg": wg, "bg": bg, "lam": lam},
        "sink": attn_sink, "pool_w": pool_w.astype(BF16), "pool_scale": row(pool_scale),
        "ffn_conv": ffn_conv, "ffn_conv_b": row(ffn_conv_b),
    }


def kernel(x_prompt, x_sample, cache_k, cache_v, state_lru, c, c_ctx, w_ada, b_ada, norm1, norm2, w_in,
           b_gate, lru_conv, lru_conv_b, lru_wa, lru_ba, lru_wx, lru_bx, lru_lambda, attn_sink, pool_w,
           pool_scale, w_branch, w_out, ffn_up, ffn_conv, ffn_conv_b, ffn_down, final_norm):
    batch, seq, _ = x_prompt.shape
    dec_batch, dec_seq, _ = x_sample.shape
    past = cache_k.shape[2]
    assert seq == SCAN_CHUNK and dec_seq % SCAN_ROWS == 0 and (batch * seq) % SCAN_ROWS == 0

    c_rows = jnp.concatenate(
        [c_ctx[None], c, jnp.zeros((SUBLANES - 1 - dec_batch, D_MODEL), F32)], axis=0)
    mods = _ada(c_rows, w_ada, b_ada)

    xp = x_prompt.reshape(batch * seq, D_MODEL)
    xs = x_sample.reshape(dec_batch * dec_seq, D_MODEL)
    fn = final_norm[None]
    p = _stacked_params(norm1, norm2, b_gate, lru_conv, lru_conv_b, lru_wa, lru_ba, lru_wx, lru_bx,
                        lru_lambda, attn_sink, pool_w, pool_scale, ffn_conv, ffn_conv_b)
    raw = {"w_in": w_in, "w_branch": w_branch, "w_out": w_out, "ffn_up": ffn_up, "ffn_down": ffn_down}
    w_in_l = w_in[0].astype(BF16)
    mod_rows = mods.reshape(DEPTH * SUBLANES, 1, 6 * D_MODEL)
    ck = cache_k.reshape(dec_batch, DEPTH, past, D_KV).astype(BF16)
    cv = cache_v.reshape(dec_batch, DEPTH, past, D_KV).astype(BF16)
    h0 = state_lru.transpose(1, 2, 0, 3)[:, :, :, None, :]
    caches, hs = None, []
    for l in range(DEPTH):
        last = fn if l == DEPTH - 1 else None
        wts = {"w_in": w_in_l}
        xp, k_all, v_all, h_fin, w_in_l = _trunk_layer(xp, mod_rows, l * SUBLANES, batch * seq, p, wts, l, seq,
                                                       None, last, caches, raw)
        caches = (k_all, v_all)
        xs = _trunk_layer(xs, mod_rows, l * SUBLANES + 1, dec_seq, p, wts, l, dec_seq, (ck, cv, h0), last)[0]
        hs.append(h_fin.transpose(1, 0, 2))
    y_prompt = xp.reshape(batch, seq, D_MODEL)
    y_sample = xs.reshape(dec_batch, dec_seq, D_MODEL)
    cache_dims = (batch, DEPTH, seq, N_KV_HEADS, HEAD_DIM)
    return (y_prompt, y_sample, k_all.reshape(cache_dims), v_all.reshape(cache_dims), jnp.stack(hs, axis=1))
```

```python
import functools

import numpy as np
import jax
import jax.numpy as jnp
from jax import lax
from jax.experimental import pallas as pl
from jax.experimental.pallas import tpu as pltpu

F32 = jnp.float32
BF16 = jnp.bfloat16

D_MODEL = 1024
DEPTH = 2
GRID_W = 64
EPS = 1e-6
N_BRANCH = 3
D_RNN = 1024
LRU_HEADS = 8
LRU_BLOCK = D_RNN // LRU_HEADS
LRU_C = 8.0
N_HEADS = 8
N_KV_HEADS = 2
KV_GROUPS = N_HEADS // N_KV_HEADS
HEAD_DIM = 128
D_KV = N_KV_HEADS * HEAD_DIM
WINDOW = 128
BLOCK_Q = 128
ROPE_BASE = 10000.0
NEG_INF = -1e30
D_POOL = 1024
POOL_WINDOWS = (2, 4, 8, 16)
POOL_GROUP = D_POOL // len(POOL_WINDOWS)
D_FF = 2816
D_IN = D_RNN + N_HEADS * HEAD_DIM + 2 * D_KV + D_POOL + N_BRANCH * D_MODEL

COL_XA = 0
COL_Q = COL_XA + D_RNN
COL_K = COL_Q + N_HEADS * HEAD_DIM
COL_V = COL_K + D_KV
COL_XC = COL_V + D_KV
COL_G = COL_XC + D_POOL
LOG2E = float(np.log2(np.e))

VMEM_LIMIT_BYTES = 52 * 1024 * 1024
SUBLANES = 8
LANES = 128
BF16_ROWS = 16

SCAN_CHUNK = 256
SCAN_PITCH = 260
SCAN_ROWS = 2048
POOL_PAD = 8


def _params(*sem):
    return pltpu.CompilerParams(dimension_semantics=sem, vmem_limit_bytes=VMEM_LIMIT_BYTES)


def _dot(a, b):
    return jnp.dot(a, b, preferred_element_type=F32)


def _dot_nt(a, b):
    return lax.dot_general(a, b, (((1,), (1,)), ((), ())), preferred_element_type=F32)


def _sigmoid(z):
    return 0.5 * (1.0 + jnp.tanh(0.5 * z))


def _rms(x):
    return x * lax.rsqrt(jnp.mean(x * x, axis=-1, keepdims=True) + EPS)


def _ada_kernel(c_ref, w_ref, b_ref, o_ref):
    c = c_ref[...]
    s = c * _sigmoid(c)
    o_ref[0] = _dot(s.astype(BF16), w_ref[0].astype(BF16)) + b_ref[0]


def _ada(c_rows, w_ada, b_ada):
    tn = 1536
    return pl.pallas_call(
        _ada_kernel,
        grid=(DEPTH, 6 * D_MODEL // tn),
        in_specs=[
            pl.BlockSpec((SUBLANES, D_MODEL), lambda l, j: (0, 0)),
            pl.BlockSpec((1, D_MODEL, tn), lambda l, j: (l, 0, j)),
            pl.BlockSpec((1, 1, tn), lambda l, j: (l, 0, j)),
        ],
        out_specs=pl.BlockSpec((1, SUBLANES, tn), lambda l, j: (l, 0, j)),
        out_shape=jax.ShapeDtypeStruct((DEPTH, SUBLANES, 6 * D_MODEL), F32),
        compiler_params=_params("parallel", "parallel"),
        name="ada_mod",
    )(c_rows, w_ada, b_ada.reshape(DEPTH, 1, 6 * D_MODEL))


INPROJ_TM = 512
INPROJ_CK = 2 * D_KV


def _resident(shape, layer):
    ndim = len(shape)
    return pl.BlockSpec((None,) + tuple(shape), lambda *_: (layer,) + (0,) * ndim,
                        pipeline_mode=pl.Buffered(1))


def _whole(shape):
    return pl.BlockSpec(tuple(shape), lambda *_: (0,) * len(shape), pipeline_mode=pl.Buffered(1))


def _side_cast_specs(side, steps):
    in_specs = [pl.BlockSpec((rows, arr.shape[1]), lambda i, first=first: (first + i, 0))
                for arr, rows, first in side]
    out_specs = [pl.BlockSpec((rows, arr.shape[1]), lambda i: (i, 0)) for arr, rows, _ in side]
    out_shapes = [jax.ShapeDtypeStruct((rows * steps, arr.shape[1]), BF16) for arr, rows, _ in side]
    return in_specs, out_specs, out_shapes


def _run_side_casts(refs, n_in, n_out, n_side):
    refs = list(refs)
    side_in = refs[n_in:n_in + n_side]
    side_out = refs[n_in + n_side + n_out:n_in + 2 * n_side + n_out]
    for src, dst in zip(side_in, side_out):
        dst[...] = src[...].astype(BF16)
    return refs[:n_in] + refs[n_in + n_side:n_in + n_side + n_out] + refs[n_in + 2 * n_side + n_out:]


def _layer_slabs(stacked, layer, steps):
    cols = stacked.shape[-1]
    rows = int(np.prod(stacked.shape[1:-1]))
    assert rows % (steps * BF16_ROWS) == 0
    return (stacked.reshape(stacked.shape[0] * rows, cols), rows // steps, layer * steps)


def _mod_spec(row0, per):
    return pl.BlockSpec((1, 1, 6 * D_MODEL), lambda i: (row0 + i // per, 0, 0))


def _inproj_kernel(*refs, seq_len, want_kv, layer, owns_cache, n_in, n_side):
    refs = _run_side_casts(refs, n_in, 4 if want_kv else 1, n_side)
    _inproj_body(*refs, seq_len=seq_len, want_kv=want_kv, layer=layer, owns_cache=owns_cache)


def _inproj_body(*refs, seq_len, want_kv, layer, owns_cache):
    if want_kv:
        x_ref, mod_ref, g_ref, w_ref, sink_ref = refs[:5]
        o_ref, k_ref, v_ref, y_ref = refs[-4:]
        q_chunks = []
        if owns_cache:
            for other in range(DEPTH):
                if other != layer:
                    k_ref[:, other] = jnp.zeros(k_ref.shape[:1] + k_ref.shape[2:], F32)
                    v_ref[:, other] = jnp.zeros(v_ref.shape[:1] + v_ref.shape[2:], F32)
    else:
        x_ref, mod_ref, g_ref, w_ref, cos_ref, sa_ref, sb_ref, o_ref = refs
    mod = mod_ref[0]
    shift = mod[:, 0:D_MODEL]
    scale = mod[:, D_MODEL:2 * D_MODEL]
    h = (_rms(x_ref[...]) * g_ref[...] * (1.0 + scale) + shift).astype(BF16)
    nf = HEAD_DIM // 4

    def rope(x):
        return (x * cos_ref[...] + pltpu.roll(x, HEAD_DIM - nf, 1) * sa_ref[...]
                + pltpu.roll(x, nf, 1) * sb_ref[...])

    for c in range(D_IN // INPROJ_CK):
        acc = _dot(h, w_ref[:, c * INPROJ_CK:(c + 1) * INPROJ_CK])
        if not want_kv:
            lo = c * INPROJ_CK
            heads = [acc[:, j * HEAD_DIM:(j + 1) * HEAD_DIM] for j in range(INPROJ_CK // HEAD_DIM)]
            heads = [rope(hd) if COL_Q <= lo + j * HEAD_DIM < COL_V else hd for j, hd in enumerate(heads)]
            heads = [hd * SOFTMAX_SCALE if COL_Q <= lo + j * HEAD_DIM < COL_K else hd
                     for j, hd in enumerate(heads)]
            acc = jnp.concatenate(heads, axis=1)
        o_ref[:, c * INPROJ_CK:(c + 1) * INPROJ_CK] = acc.astype(o_ref.dtype)
        if want_kv and COL_Q <= c * INPROJ_CK < COL_K:
            q_chunks.append((acc * SOFTMAX_SCALE).astype(BF16))
        if want_kv and c == COL_K // INPROJ_CK:
            for b in range(INPROJ_TM // seq_len):
                rows = slice(b * seq_len, (b + 1) * seq_len)
                for hd in range(N_KV_HEADS):
                    dst = pl.ds(hd, seq_len, stride=N_KV_HEADS)
                    at = (b, layer, dst, slice(None)) if owns_cache else (b, dst, slice(None))
                    k_ref[at] = acc[rows, hd * HEAD_DIM:(hd + 1) * HEAD_DIM]
                    v_ref[at] = acc[rows, D_KV + hd * HEAD_DIM:D_KV + (hd + 1) * HEAD_DIM]
            kv = acc.astype(BF16)

            def scores(b, kh):
                rows = slice(b * seq_len, (b + 1) * seq_len)
                q = jnp.concatenate([q_chunks[kh][rows, g * HEAD_DIM:(g + 1) * HEAD_DIM]
                                     for g in range(KV_GROUPS)], axis=0)
                return (b, kh, _dot_nt(kv[rows, kh * HEAD_DIM:(kh + 1) * HEAD_DIM], q),
                        kv[rows, D_KV + kh * HEAD_DIM:D_KV + (kh + 1) * HEAD_DIM])

            order = [(b, kh) for b in range(INPROJ_TM // seq_len) for kh in range(N_KV_HEADS)]
            pending = [scores(*ch) for ch in order[:ATTN_LOOKAHEAD]]
            for n in range(len(order)):
                if n + ATTN_LOOKAHEAD < len(order):
                    pending.append(scores(*order[n + ATTN_LOOKAHEAD]))
                b, kh, t, vals = pending.pop(0)
                o_t = _softmax_pv_t(t, _sink_row(sink_ref, layer, kh, seq_len), vals)
                _store_heads_t(y_ref, o_t, kh, b * seq_len, seq_len)


def _inproj(x, mod, mod_row0, rows_per_mod, g, w, layer, seq_len, want_kv, sink=None, caches=None, side=()):
    m = x.shape[0]
    tm = INPROJ_TM
    per = rows_per_mod // tm
    proj_spec = pl.BlockSpec((tm, D_IN), lambda i: (i, 0))
    proj_shape = jax.ShapeDtypeStruct((m, D_IN), BF16)
    aliases = {}
    owns_cache = want_kv and caches is None
    if want_kv:
        nb = tm // seq_len
        if owns_cache:
            cache_spec = pl.BlockSpec((nb, DEPTH, seq_len * N_KV_HEADS, HEAD_DIM), lambda i: (i, 0, 0, 0))
        else:
            cache_spec = pl.BlockSpec((nb, None, seq_len * N_KV_HEADS, HEAD_DIM), lambda i: (i, layer, 0, 0))
        cache_shape = jax.ShapeDtypeStruct((m // seq_len, DEPTH, seq_len * N_KV_HEADS, HEAD_DIM), F32)
        assert tm % seq_len == 0 and INPROJ_CK == KV_GROUPS * HEAD_DIM and COL_Q % INPROJ_CK == 0
        attn_spec = pl.BlockSpec((tm, N_HEADS * HEAD_DIM), lambda i: (i, 0))
        out_specs = (proj_spec, cache_spec, cache_spec, attn_spec)
        out_shape = (proj_shape, cache_shape, cache_shape, jax.ShapeDtypeStruct((m, N_HEADS * HEAD_DIM), BF16))
        extra_specs, extra_args = [pl.BlockSpec(memory_space=pltpu.SMEM)], [sink]
        if caches is not None:
            extra_specs += [pl.BlockSpec(memory_space=pl.ANY)] * 2
            extra_args += list(caches)
            aliases = {5: 1, 6: 2}
    else:
        out_specs, out_shape = (proj_spec,), (proj_shape,)
        tab = pl.BlockSpec((tm, HEAD_DIM), lambda i: (i % (seq_len // tm), 0))
        extra_specs, extra_args = [tab, tab, tab], list(_rope_tables(seq_len))
    side_in, side_out, side_shapes = _side_cast_specs(side, m // tm)
    out = pl.pallas_call(
        functools.partial(_inproj_kernel, seq_len=seq_len, want_kv=want_kv, layer=layer, owns_cache=owns_cache,
                          n_in=4 + len(extra_specs), n_side=len(side)),
        grid=(m // tm,),
        in_specs=[
            pl.BlockSpec((tm, D_MODEL), lambda i: (i, 0)),
            _mod_spec(mod_row0, per),
            _resident((1, D_MODEL), layer),
            _whole((D_MODEL, D_IN)),
        ] + extra_specs + side_in,
        out_specs=tuple(out_specs) + tuple(side_out),
        out_shape=tuple(out_shape) + tuple(side_shapes),
        input_output_aliases=aliases,
        compiler_params=_params("parallel"),
        name="in_proj",
    )(x, mod, g, w, *extra_args, *[item[0] for item in side])
    n_own = len(out_specs)
    own = tuple(out[:n_own]) if want_kv else (out[0], None, None, None)
    return own + (list(out[n_own:]),)


LRU_HEADS_PER_STEP = 2


def _chunk_neighbour(v, towards_later):
    sub = lax.broadcasted_iota(jnp.int32, v.shape, 0)
    if towards_later:
        return jnp.where(sub >= 1, pltpu.roll(v, 1, 0), 0.0)
    return jnp.where(sub <= SUBLANES - 2, pltpu.roll(v, SUBLANES - 1, 0), 0.0)


def _shift_time(x_tm, k, chained):
    n = x_tm.shape[0]
    steps = abs(k)
    edge = []
    for s in range(steps):
        if not chained:
            edge.append(jnp.zeros((SUBLANES, x_tm.shape[1]), F32))
        elif k > 0:
            src = n - (steps - s) * SUBLANES
            edge.append(_chunk_neighbour(x_tm[src:src + SUBLANES], True))
        else:
            edge.append(_chunk_neighbour(x_tm[s * SUBLANES:(s + 1) * SUBLANES], False))
    if k > 0:
        return jnp.concatenate(edge + [x_tm[:n - steps * SUBLANES]], axis=0)
    return jnp.concatenate([x_tm[steps * SUBLANES:]] + edge, axis=0)


def _lru_tm_kernel(*refs, seq_len, latent):
    if latent:
        (x_ref, cw_ref, cb_ref, wg_ref, bg_ref, lam_ref, h0_ref, y_ref,
         io_ref, af_ref, bf_ref, ab_ref, bb_ref, hf_ref, hb_ref) = refs
    else:
        (x_ref, cw_ref, cb_ref, wg_ref, bg_ref, lam_ref, y_ref, fin_ref,
         io_ref, af_ref, bf_ref, ab_ref, bb_ref, hf_ref, hb_ref) = refs
    rows = SCAN_ROWS
    nchunk = rows // SCAN_CHUNK
    nhead = LRU_HEADS_PER_STEP
    chained = seq_len > SCAN_CHUNK
    for hd in range(nhead):
        lanes = slice(hd * LRU_BLOCK, (hd + 1) * LRU_BLOCK)
        x = x_ref[:, lanes].astype(F32)
        for c in range(nchunk):
            io_ref[hd, pl.ds(c * SCAN_PITCH, SCAN_CHUNK), :] = x[c * SCAN_CHUNK:(c + 1) * SCAN_CHUNK]
        x = jnp.concatenate(
            [io_ref[hd, pl.ds(t, nchunk, stride=SCAN_PITCH), :] for t in range(SCAN_CHUNK)], axis=0)
        xc = (_shift_time(x, 2, chained) * cw_ref[0:1, lanes] + _shift_time(x, 1, chained) * cw_ref[1:2, lanes]
              + x * cw_ref[2:3, lanes] + _shift_time(x, -1, chained) * cw_ref[3:4, lanes] + cb_ref[:, lanes])
        th = jnp.tanh(_dot(xc.astype(BF16), wg_ref[hd] * 0.5) + 0.5 * bg_ref[hd])
        xh = 0.5 * xc
        lam = lam_ref[hd]
        for d, (a_ref, b_ref) in enumerate(((af_ref, bf_ref), (ab_ref, bb_ref))):
            th_r = th[:, 2 * d * LRU_BLOCK:(2 * d + 1) * LRU_BLOCK]
            th_i = th[:, (2 * d + 1) * LRU_BLOCK:(2 * d + 2) * LRU_BLOCK]
            nl = -lam[:, d * LRU_BLOCK:(d + 1) * LRU_BLOCK]
            softplus = jnp.maximum(nl, 0.0) + jnp.log(1.0 + jnp.exp(-jnp.abs(nl)))
            ch = (-0.5 * LRU_C * LOG2E) * softplus
            a = jnp.exp2(ch + ch * th_r)
            y = 1.0 - a * a
            a_ref[hd] = a
            b_ref[hd] = (y * lax.rsqrt(jnp.maximum(y, 1e-30))) * ((1.0 + th_i) * xh)

    def step(t, carry):
        rf = pl.ds(pl.multiple_of(t * nchunk, nchunk), nchunk)
        rb = pl.ds(pl.multiple_of((SCAN_CHUNK - 1 - t) * nchunk, nchunk), nchunk)
        out = []
        for hd in range(nhead):
            hf, hb, pf, pb = carry[4 * hd:4 * hd + 4]
            a_f = af_ref[hd, rf, :]
            a_b = ab_ref[hd, rb, :]
            hf = a_f * hf + bf_ref[hd, rf, :]
            hb = a_b * hb + bb_ref[hd, rb, :]
            hf_ref[hd, rf, :] = hf
            hb_ref[hd, rb, :] = hb
            if chained:
                pf = pf * a_f
                pb = pb * a_b
                af_ref[hd, rf, :] = pf
                ab_ref[hd, rb, :] = pb
            out += [hf, hb, pf, pb]
        return tuple(out)

    zero = jnp.zeros((nchunk, LRU_BLOCK), F32)
    one = jnp.ones((nchunk, LRU_BLOCK), F32)
    lax.fori_loop(0, SCAN_CHUNK, step, (zero, zero, one, one) * nhead, unroll=8)

    sub = lax.broadcasted_iota(jnp.int32, (nchunk, LRU_BLOCK), 0)
    last = slice(rows - nchunk, rows)
    first = slice(0, nchunk)
    for hd in range(nhead):
        lanes = slice(hd * LRU_BLOCK, (hd + 1) * LRU_BLOCK)
        hf = hf_ref[hd]
        hb = hb_ref[hd]
        if chained:
            pf = af_ref[hd]
            pb = ab_ref[hd]
            ef = jnp.where(sub == 0, h0_ref[0, 0][:, lanes], 0.0)
            eb = jnp.where(sub == nchunk - 1, h0_ref[1, 0][:, lanes], 0.0)
            for c in range(1, nchunk):
                ef = jnp.where(sub == c, pltpu.roll(hf[last] + pf[last] * ef, 1, 0), ef)
                eb = jnp.where(sub == nchunk - 1 - c,
                               pltpu.roll(hb[first] + pb[first] * eb, nchunk - 1, 0), eb)
            hf = (hf.reshape(SCAN_CHUNK, nchunk, LRU_BLOCK)
                  + pf.reshape(SCAN_CHUNK, nchunk, LRU_BLOCK) * ef[None]).reshape(rows, LRU_BLOCK)
            hb = (hb.reshape(SCAN_CHUNK, nchunk, LRU_BLOCK)
                  + pb.reshape(SCAN_CHUNK, nchunk, LRU_BLOCK) * eb[None]).reshape(rows, LRU_BLOCK)
        else:
            fin_ref[0, :, lanes] = hf[last]
            fin_ref[1, :, lanes] = hb[first]
        y = hf + hb
        for t in range(SCAN_CHUNK):
            io_ref[hd, pl.ds(t, nchunk, stride=SCAN_PITCH), :] = y[t * nchunk:(t + 1) * nchunk]
        for c in range(nchunk):
            y_ref[pl.ds(c * SCAN_CHUNK, SCAN_CHUNK), lanes] = (
                io_ref[hd, pl.ds(c * SCAN_PITCH, SCAN_CHUNK), :].astype(y_ref.dtype))


def _lru(proj, lp, layer, seq_len, h0):
    m = proj.shape[0]
    latent = h0 is not None
    rows = SCAN_ROWS
    nhead = LRU_HEADS_PER_STEP
    width = nhead * LRU_BLOCK
    in_specs = [
        pl.BlockSpec((rows, width), lambda r, h: (r, COL_XA // width + h)),
        pl.BlockSpec((None, 4, width), lambda r, h: (layer, 0, h)),
        pl.BlockSpec((None, 1, width), lambda r, h: (layer, 0, h)),
        pl.BlockSpec((None, nhead, LRU_BLOCK, 4 * LRU_BLOCK), lambda r, h: (layer, h, 0, 0)),
        pl.BlockSpec((None, nhead, 1, 4 * LRU_BLOCK), lambda r, h: (layer, h, 0, 0)),
        pl.BlockSpec((None, nhead, 1, 2 * LRU_BLOCK), lambda r, h: (layer, h, 0, 0)),
    ]
    args = [proj, lp["conv_w"], lp["conv_b"], lp["wg"], lp["bg"], lp["lam"]]
    y_spec = pl.BlockSpec((rows, width), lambda r, h: (r, h))
    y_shape = jax.ShapeDtypeStruct((m, D_RNN), BF16)
    if latent:
        in_specs.append(pl.BlockSpec((None, 2, 1, 1, width), lambda r, h: (layer, 0, r, 0, h)))
        args.append(h0)
        out_specs, out_shape = y_spec, y_shape
    else:
        nseq = m // seq_len
        out_specs = (y_spec, pl.BlockSpec((2, rows // seq_len, width), lambda r, h: (0, r, h)))
        out_shape = (y_shape, jax.ShapeDtypeStruct((2, nseq, D_RNN), F32))
    strided_buf = pltpu.VMEM((nhead, rows // SCAN_CHUNK * SCAN_PITCH, LRU_BLOCK), F32)
    scan_buf = pltpu.VMEM((nhead, rows, LRU_BLOCK), F32)
    return pl.pallas_call(
        functools.partial(_lru_tm_kernel, seq_len=seq_len, latent=latent),
        grid=(m // rows, LRU_HEADS // nhead),
        in_specs=in_specs,
        out_specs=out_specs,
        out_shape=out_shape,
        scratch_shapes=[strided_buf] + [scan_buf] * 6,
        compiler_params=_params("parallel", "parallel"),
        name="rglru_latent" if latent else "rglru_context",
    )(*args)


SOFTMAX_SCALE = HEAD_DIM ** -0.5 * LOG2E


def _stack_heads(q_ref, kh, r0, rows):
    parts = [q_ref[r0:r0 + rows, (kh * KV_GROUPS + g) * HEAD_DIM:(kh * KV_GROUPS + g + 1) * HEAD_DIM]
             for g in range(KV_GROUPS)]
    return jnp.concatenate(parts, axis=0)


def _dot_tn(a, b):
    return lax.dot_general(a, b, (((0,), (0,)), ((), ())), preferred_element_type=F32)


def _sink_row(sink_ref, layer, kh, cols):
    parts = [jnp.full((1, cols), sink_ref[layer, kh * KV_GROUPS + g] * LOG2E, F32) for g in range(KV_GROUPS)]
    return jnp.concatenate(parts, axis=1)


def _softmax_pv_t(t, sink, v):
    m = jnp.maximum(jnp.max(t, axis=0, keepdims=True), sink)
    e = jnp.exp2(t - m).astype(BF16)
    d = v.shape[1]
    v_ones = jnp.concatenate([v, jnp.ones((v.shape[0], SUBLANES), BF16)], axis=1)
    o_sum = _dot_tn(v_ones, e)
    denom = o_sum[d:d + 1] + jnp.exp2(sink - m)
    return o_sum[:d] * (1.0 / denom)


def _store_heads_t(y_ref, o_t, kh, r0, rows):
    for g in range(KV_GROUPS):
        h = kh * KV_GROUPS + g
        y_ref[r0:r0 + rows, h * HEAD_DIM:(h + 1) * HEAD_DIM] = (
            o_t[:, g * rows:(g + 1) * rows].T.astype(y_ref.dtype))


def _rope_tables(seq_len):
    nf = HEAD_DIM // 4
    freqs = ROPE_BASE ** (-np.arange(nf, dtype=np.float64) / nf)
    t = np.arange(seq_len)
    ang_row = (t // GRID_W)[:, None] * freqs[None, :]
    ang_col = (t % GRID_W)[:, None] * freqs[None, :]
    ang = np.concatenate([ang_row, ang_row, ang_col, ang_col], axis=1)
    first = (np.arange(HEAD_DIM) % (2 * nf)) < nf
    cos = np.cos(ang)
    sin = np.sin(ang)
    sin_a = np.where(first[None, :], -sin, 0.0)
    sin_b = np.where(first[None, :], 0.0, sin)
    return tuple(jnp.asarray(a, F32) for a in (cos, sin_a, sin_b))


ATTN_LAT_BLOCKS = 4
ATTN_LOOKAHEAD = 3


def _attn_lat_kernel(sink_ref, q_ref, kvp_ref, kvc_ref, kvn_ref, ck_ref, cv_ref, y_ref, *, nblk, layer):
    step = pl.program_id(1)
    cols = KV_GROUPS * BLOCK_Q
    span = 3 * BLOCK_Q
    key = lax.broadcasted_iota(jnp.int32, (span, cols), 0)
    qry = lax.broadcasted_iota(jnp.int32, (span, cols), 1) & (BLOCK_Q - 1)
    kv = jnp.concatenate([kvp_ref[...], kvc_ref[...], kvn_ref[...]], axis=0)
    def scores(i, kh):
        win = kv[i * BLOCK_Q:i * BLOCK_Q + span]
        sl = slice(kh * HEAD_DIM, (kh + 1) * HEAD_DIM)
        vl = slice(D_KV + kh * HEAD_DIM, D_KV + (kh + 1) * HEAD_DIM)
        q = _stack_heads(q_ref, kh, i * BLOCK_Q, BLOCK_Q)
        keys = jnp.concatenate([win[:, sl], ck_ref[0, :, sl]], axis=0)
        vals = jnp.concatenate([win[:, vl], cv_ref[0, :, sl]], axis=0)
        return i, kh, _dot_nt(keys, q), vals

    order = [(i, kh) for i in range(ATTN_LAT_BLOCKS) for kh in range(N_KV_HEADS)]
    pending = [scores(*c) for c in order[:ATTN_LOOKAHEAD]]
    for n in range(len(order)):
        if n + ATTN_LOOKAHEAD < len(order):
            pending.append(scores(*order[n + ATTN_LOOKAHEAD]))
        i, kh, t, vals = pending.pop(0)
        j = step * ATTN_LAT_BLOCKS + i
        lo = jnp.where(j > 0, qry, BLOCK_Q)
        hi = jnp.where(j < nblk - 1, qry + 2 * BLOCK_Q, 2 * BLOCK_Q - 1)
        bias = jnp.where(jnp.logical_and(key >= lo, key <= hi), 0.0, NEG_INF)
        t = jnp.concatenate([t[:span] + bias, t[span:]], axis=0)
        o_t = _softmax_pv_t(t, _sink_row(sink_ref, layer, kh, BLOCK_Q), vals)
        _store_heads_t(y_ref, o_t, kh, i * BLOCK_Q, BLOCK_Q)


def _attn_lat(proj, ck, cv, sink, layer, seq_len):
    m = proj.shape[0]
    nblk = seq_len // BLOCK_Q
    nstep = nblk // ATTN_LAT_BLOCKS
    rows = ATTN_LAT_BLOCKS * BLOCK_Q
    nb = m // seq_len
    past = ck.shape[2]
    kv_col = COL_K // (2 * D_KV)
    assert COL_V == COL_K + D_KV and COL_K % (2 * D_KV) == 0

    def halo(shift):
        def index(b, s):
            return (b * nblk + jnp.clip(s * ATTN_LAT_BLOCKS + shift, 0, nblk - 1), kv_col)
        return index

    return pl.pallas_call(
        functools.partial(_attn_lat_kernel, nblk=nblk, layer=layer),
        grid=(nb, nstep),
        in_specs=[
            pl.BlockSpec(memory_space=pltpu.SMEM),
            pl.BlockSpec((rows, N_HEADS * HEAD_DIM), lambda b, s: (b * nstep + s, COL_Q // (N_HEADS * HEAD_DIM))),
            pl.BlockSpec((BLOCK_Q, 2 * D_KV), halo(-1)),
            pl.BlockSpec((rows, 2 * D_KV), lambda b, s: (b * nstep + s, kv_col)),
            pl.BlockSpec((BLOCK_Q, 2 * D_KV), halo(ATTN_LAT_BLOCKS)),
            pl.BlockSpec((1, None, past, D_KV), lambda b, s: (b, layer, 0, 0)),
            pl.BlockSpec((1, None, past, D_KV), lambda b, s: (b, layer, 0, 0)),
        ],
        out_specs=pl.BlockSpec((rows, N_HEADS * HEAD_DIM), lambda b, s: (b * nstep + s, 0)),
        out_shape=jax.ShapeDtypeStruct((m, N_HEADS * HEAD_DIM), BF16),
        compiler_params=_params("parallel", "parallel"),
        name="attn_latent",
    )(sink, proj, proj, proj, proj, ck, cv)


POOL_TILE = 256
POOL_LEAD = BF16_ROWS
POOL_ROWS = 2048


def _pool_plan(seq_len):
    lead = 0 if seq_len == POOL_TILE else POOL_LEAD
    return lead, POOL_TILE - 2 * lead


def _pool_bands(seq_len):
    lead, nout = _pool_plan(seq_len)
    r = np.arange(nout)[:, None]
    c = np.arange(POOL_TILE)[None, :] - lead
    bands = [(c >= r - win // 2) & (c < r + win // 2) for win in POOL_WINDOWS]
    return jnp.asarray(np.stack(bands), BF16)


def _pool_kernel(x0_ref, x1_ref, x2_ref, x3_ref, band_ref, w_ref, s_ref, y_ref, pad_ref, *, seq_len):
    lead, nout = _pool_plan(seq_len)
    edge = lax.broadcasted_iota(jnp.int32, (SUBLANES, POOL_GROUP), 0)
    if lead:
        pad_ref[0:lead, :] = jnp.zeros((lead, POOL_GROUP), BF16)
        pad_ref[lead + seq_len:, :] = jnp.zeros((pad_ref.shape[0] - lead - seq_len, POOL_GROUP), BF16)
    for gi, (win, x_ref) in enumerate(zip(POOL_WINDOWS, (x0_ref, x1_ref, x2_ref, x3_ref))):
        cs = slice(gi * POOL_GROUP, (gi + 1) * POOL_GROUP)
        half = win // 2
        inv_head = 1.0 / ((edge + half) - jnp.maximum(edge - half, 0)).astype(F32)
        inv_tail = 1.0 / (jnp.minimum(SUBLANES - edge, half) + half).astype(F32)
        if lead:
            pad_ref[lead:lead + seq_len, :] = x_ref[...]
        for base in range(0, POOL_ROWS, seq_len):
            for p0 in range(0, seq_len, nout):
                n = min(nout, seq_len - p0)
                x = x_ref[base + p0:base + p0 + n, :]
                src = pad_ref[p0:p0 + POOL_TILE, :] if lead else x
                sums = _dot(band_ref[gi, :n, :], src)
                head = sums[:SUBLANES] * (inv_head if p0 == 0 else 1.0 / win)
                tail = sums[n - SUBLANES:] * (inv_tail if p0 + n == seq_len else 1.0 / win)
                mean = jnp.concatenate([head, sums[SUBLANES:n - SUBLANES] * (1.0 / win), tail], axis=0)
                pooled = (mean - x.astype(F32)).astype(BF16)
                y_ref[base + p0:base + p0 + n, cs] = (
                    _dot(pooled, w_ref[gi]) * s_ref[:, cs]).astype(y_ref.dtype)


def _pool(proj, w, s, layer, seq_len):
    m = proj.shape[0]
    rows = POOL_ROWS
    lead, nout = _pool_plan(seq_len)
    assert seq_len in (POOL_TILE, rows) and max(POOL_WINDOWS) // 2 <= min(SUBLANES, lead or SUBLANES)
    pad_rows = (pl.cdiv(seq_len, nout) - 1) * nout + POOL_TILE
    group = lambda gi: pl.BlockSpec((rows, POOL_GROUP), lambda r: (r, COL_XC // POOL_GROUP + gi))
    whole = lambda shape: pl.BlockSpec(shape, lambda r: (0,) * len(shape))
    nwin = len(POOL_WINDOWS)
    return pl.pallas_call(
        functools.partial(_pool_kernel, seq_len=seq_len),
        grid=(m // rows,),
        in_specs=[
            group(0), group(1), group(2), group(3),
            whole((nwin, nout, POOL_TILE)),
            _resident((nwin, POOL_GROUP, POOL_GROUP), layer),
            _resident((1, D_POOL), layer),
        ],
        out_specs=pl.BlockSpec((rows, D_POOL), lambda r: (r, 0)),
        out_shape=jax.ShapeDtypeStruct((m, D_POOL), BF16),
        scratch_shapes=[pltpu.VMEM((pad_rows, POOL_GROUP), BF16)],
        compiler_params=_params("parallel"),
        name="pool_mix",
    )(proj, proj, proj, proj, _pool_bands(seq_len), w, s)


MERGE_N_IN = 15
MERGE_TM = 512


def _merge_kernel(*refs, n_side):
    (ya_ref, yb_ref, yc_ref, g0_ref, g1_ref, g2_ref, g3_ref, g4_ref, g5_ref, x_ref, mod_ref,
     bg_ref, wb_ref, wo_ref, n2_ref, x1_ref, h2_ref) = _run_side_casts(refs, MERGE_N_IN, 2, n_side)
    mod = mod_ref[0]
    g_refs = (g0_ref, g1_ref, g2_ref, g3_ref, g4_ref, g5_ref)
    half = D_MODEL // 2
    merged = None
    for k, y_ref in enumerate((ya_ref, yb_ref, yc_ref)):
        y = _dot(y_ref[...], wb_ref[k])
        parts = []
        for p in range(2):
            z = g_refs[2 * k + p][...].astype(F32) + bg_ref[:, k * D_MODEL + p * half:k * D_MODEL + (p + 1) * half]
            parts.append((1.0 + jnp.tanh(0.5 * z)) * y[:, p * half:(p + 1) * half])
        term = jnp.concatenate(parts, axis=1)
        merged = term if merged is None else merged + term
    merged = 0.5 * merged
    gate1 = mod[:, 2 * D_MODEL:3 * D_MODEL]
    x1 = x_ref[...] + gate1 * _dot(merged.astype(BF16), wo_ref[...])
    x1_ref[...] = x1
    shift2 = mod[:, 3 * D_MODEL:4 * D_MODEL]
    scale2 = mod[:, 4 * D_MODEL:5 * D_MODEL]
    h2_ref[...] = (_rms(x1) * n2_ref[...] * (1.0 + scale2) + shift2).astype(BF16)


def _merge(ya, yb, yc, proj, x, mod, mod_row0, rows_per_mod, bg, wb, wo, n2, layer, side=()):
    m = x.shape[0]
    tm = MERGE_TM
    per = rows_per_mod // tm
    half = D_MODEL // 2
    row = pl.BlockSpec((tm, D_MODEL), lambda i: (i, 0))
    gate = lambda c: pl.BlockSpec((tm, half), lambda i: (i, COL_G // half + c))
    side_in, side_out, side_shapes = _side_cast_specs(side, m // tm)
    in_specs = [
        row, row, row,
        gate(0), gate(1), gate(2), gate(3), gate(4), gate(5),
        row,
        _mod_spec(mod_row0, per),
        _resident((1, N_BRANCH * D_MODEL), layer),
        _whole((N_BRANCH, D_MODEL, D_MODEL)),
        _whole((D_MODEL, D_MODEL)),
        _resident((1, D_MODEL), layer),
    ]
    assert len(in_specs) == MERGE_N_IN
    out = pl.pallas_call(
        functools.partial(_merge_kernel, n_side=len(side)),
        grid=(m // tm,),
        in_specs=in_specs + side_in,
        out_specs=(row, row) + tuple(side_out),
        out_shape=(jax.ShapeDtypeStruct((m, D_MODEL), F32), jax.ShapeDtypeStruct((m, D_MODEL), BF16))
        + tuple(side_shapes),
        compiler_params=_params("parallel"),
        name="merge_out",
    )(ya, yb, yc, proj, proj, proj, proj, proj, proj, x, mod, bg, wb, wo, n2, *[item[0] for item in side])
    return out[0], out[1], list(out[2:])


FFN_CK = 256
FFN_TM = 512
FFN_GAP = SUBLANES
GELU_C = float(np.sqrt(2.0 / np.pi))


def _ffn_kernel(*refs, seq_len, final, n_side):
    refs = _run_side_casts(refs, 9 + int(final), 1, n_side)
    h_ref, hp_ref, hn_ref, x_ref, mod_ref, wup_ref, cw_ref, cb_ref, wd_ref = refs[:9]
    fn_ref = refs[9] if final else None
    o_ref, hx_ref, u_ref, act_ref = refs[-4:]
    tm = FFN_TM
    halo = seq_len > tm
    i = pl.program_id(0)

    if halo:
        per_seq = seq_len // tm
        at_start = i % per_seq == 0
        at_end = i % per_seq == per_seq - 1
        zeros = jnp.zeros((BF16_ROWS, D_MODEL), BF16)

        @pl.when(at_start)
        def _():
            hx_ref[0:BF16_ROWS, :] = zeros

        @pl.when(jnp.logical_not(at_start))
        def _():
            hx_ref[0:BF16_ROWS, :] = hp_ref[...]

        @pl.when(at_end)
        def _():
            hx_ref[BF16_ROWS + tm:, :] = zeros

        @pl.when(jnp.logical_not(at_end))
        def _():
            hx_ref[BF16_ROWS + tm:, :] = hn_ref[...]

        hx_ref[BF16_ROWS:BF16_ROWS + tm, :] = h_ref[...]
        bases = (BF16_ROWS,)
        seg = tm
    else:
        nseg = tm // seq_len
        seg = seq_len
        bases = tuple(FFN_GAP + s * (seg + FFN_GAP) for s in range(nseg))
        for s in range(nseg + 1):
            u_ref[s * (seg + FFN_GAP):s * (seg + FFN_GAP) + FFN_GAP, :] = jnp.zeros((FFN_GAP, FFN_CK), F32)

    def taps(offset):
        return jnp.concatenate([u_ref[b + offset:b + offset + seg, :] for b in bases], axis=0)

    for c in range(D_FF // FFN_CK):
        cs = slice(c * FFN_CK, (c + 1) * FFN_CK)
        vs = slice(D_FF + c * FFN_CK, D_FF + (c + 1) * FFN_CK)
        if halo:
            u_ext = _dot(hx_ref[...], wup_ref[:, cs])
            u_ref[...] = u_ext
            u0 = u_ext[BF16_ROWS:BF16_ROWS + tm]
        else:
            u0 = _dot(h_ref[...], wup_ref[:, cs])
            for s, b in enumerate(bases):
                u_ref[b:b + seg, :] = u0[s * seg:(s + 1) * seg]
        uv = _dot(h_ref[...], wup_ref[:, vs])
        gff = taps(-1) * cw_ref[0:1, cs] + u0 * cw_ref[1:2, cs] + taps(1) * cw_ref[2:3, cs] + cb_ref[:, cs]
        inner = gff * (GELU_C + (GELU_C * 0.044715) * (gff * gff))
        act_ref[:, cs] = (0.5 * (gff * uv) * (1.0 + jnp.tanh(inner))).astype(BF16)

    gate2 = mod_ref[0][:, 5 * D_MODEL:6 * D_MODEL]
    out = x_ref[...] + gate2 * _dot(act_ref[...], wd_ref[...])
    if final:
        out = _rms(out) * fn_ref[...]
    o_ref[...] = out


def _ffn(h2, x1, mod, mod_row0, rows_per_mod, wup, cw, cb, wd, layer, seq_len, final_norm, side=()):
    m = x1.shape[0]
    tm = FFN_TM
    per = rows_per_mod // tm
    hb = tm // BF16_ROWS
    last_halo = m // BF16_ROWS - 1
    final = final_norm is not None
    halo = seq_len > tm
    in_specs = [
        pl.BlockSpec((tm, D_MODEL), lambda i: (i, 0)),
        pl.BlockSpec((BF16_ROWS, D_MODEL), lambda i: (jnp.maximum(i * hb - 1, 0), 0)),
        pl.BlockSpec((BF16_ROWS, D_MODEL), lambda i: (jnp.minimum((i + 1) * hb, last_halo), 0)),
        pl.BlockSpec((tm, D_MODEL), lambda i: (i, 0)),
        _mod_spec(mod_row0, per),
        _whole((D_MODEL, 2 * D_FF)),
        _resident((3, D_FF), layer),
        _resident((1, D_FF), layer),
        _whole((D_FF, D_MODEL)),
    ]
    args = [h2, h2, h2, x1, mod, wup, cw, cb, wd]
    if final:
        in_specs.append(_whole((1, D_MODEL)))
        args.append(final_norm)
    if halo:
        u_rows = tm + 2 * BF16_ROWS
    else:
        u_rows = FFN_GAP + (tm // seq_len) * (seq_len + FFN_GAP)
    side_in, side_out, side_shapes = _side_cast_specs(side, m // tm)
    out = pl.pallas_call(
        functools.partial(_ffn_kernel, seq_len=seq_len, final=final, n_side=len(side)),
        grid=(m // tm,),
        in_specs=in_specs + side_in,
        out_specs=(pl.BlockSpec((tm, D_MODEL), lambda i: (i, 0)),) + tuple(side_out),
        out_shape=(jax.ShapeDtypeStruct((m, D_MODEL), F32),) + tuple(side_shapes),
        scratch_shapes=[
            pltpu.VMEM((tm + 2 * BF16_ROWS, D_MODEL), BF16),
            pltpu.VMEM((u_rows, FFN_CK), F32),
            pltpu.VMEM((tm, D_FF), BF16),
        ],
        compiler_params=_params("parallel"),
        name="conv_glu_ffn",
    )(*args, *[item[0] for item in side])
    return out[0], list(out[1:])


def _trunk_layer(x, mod, mod_row0, rows_per_mod, p, wts, layer, seq_len, ctx, final_norm, caches=None,
                 raw=None):
    m = x.shape[0]
    cast = raw is not None
    side = [_layer_slabs(raw[k], layer, m // INPROJ_TM) for k in ("w_branch", "w_out")] if cast else ()
    proj, k_new, v_new, yb, done = _inproj(x, mod, mod_row0, rows_per_mod, p["norm1"], wts["w_in"], layer,
                                           seq_len, want_kv=ctx is None, sink=p["sink"], caches=caches, side=side)
    if cast:
        wts["w_branch"] = done[0].reshape(N_BRANCH, D_MODEL, D_MODEL)
        wts["w_out"] = done[1]
    if ctx is None:
        ya, h_fin = _lru(proj, p["lru"], layer, seq_len, None)
    else:
        ck, cv, h0 = ctx
        ya = _lru(proj, p["lru"], layer, seq_len, h0)
        h_fin = None
        yb = _attn_lat(proj, ck, cv, p["sink"], layer, seq_len)
    yc = _pool(proj, p["pool_w"], p["pool_scale"], layer, seq_len)
    side = [_layer_slabs(raw[k], layer, m // MERGE_TM) for k in ("ffn_up", "ffn_down")] if cast else ()
    x1, h2, done = _merge(ya, yb, yc, proj, x, mod, mod_row0, rows_per_mod, p["b_gate"], wts["w_branch"],
                          wts["w_out"], p["norm2"], layer, side=side)
    if cast:
        wts["ffn_up"], wts["ffn_down"] = done
    side = [_layer_slabs(raw["w_in"], layer + 1, m // FFN_TM)] if cast and layer + 1 < DEPTH else ()
    out, done = _ffn(h2, x1, mod, mod_row0, rows_per_mod, wts["ffn_up"], p["ffn_conv"], p["ffn_conv_b"],
                     wts["ffn_down"], layer, seq_len, final_norm, side=side)
    return out, k_new, v_new, h_fin, (done[0] if side else None)


def _stacked_params(norm1, norm2, b_gate, lru_conv, lru_conv_b, lru_wa, lru_ba, lru_wx, lru_bx,
                    lru_lambda, attn_sink, pool_w, pool_scale, ffn_conv, ffn_conv_b):
    def per_head(v):
        return v.reshape(DEPTH, 2, LRU_HEADS, LRU_BLOCK).transpose(0, 2, 1, 3)

    row = lambda v: v[:, None, :]
    wg = jnp.concatenate([lru_wa[:, 0], lru_wx[:, 0], lru_wa[:, 1], lru_wx[:, 1]], axis=-1).astype(BF16)
    ba = per_head(lru_ba)
    bx = per_head(lru_bx)
    bg = jnp.concatenate([ba[:, :, 0], bx[:, :, 0], ba[:, :, 1], bx[:, :, 1]], axis=-1)[:, :, None, :]
    lam = per_head(lru_lambda).reshape(DEPTH, LRU_HEADS, 1, 2 * LRU_BLOCK)
    return {
        "norm1": row(norm1), "norm2": row(norm2), "b_gate": row(b_gate),
        "lru": {"conv_w": lru_conv, "conv_b": row(lru_conv_b), "wg": wg, "bg": bg, "lam": lam},
        "sink": attn_sink, "pool_w": pool_w.astype(BF16), "pool_scale": row(pool_scale),
        "ffn_conv": ffn_conv, "ffn_conv_b": row(ffn_conv_b),
    }


def kernel(x_prompt, x_sample, cache_k, cache_v, state_lru, c, c_ctx, w_ada, b_ada, norm1, norm2, w_in,
           b_gate, lru_conv, lru_conv_b, lru_wa, lru_ba, lru_wx, lru_bx, lru_lambda, attn_sink, pool_w,
           pool_scale, w_branch, w_out, ffn_up, ffn_conv, ffn_conv_b, ffn_down, final_norm):
    batch, seq, _ = x_prompt.shape
    dec_batch, dec_seq, _ = x_sample.shape
    past = cache_k.shape[2]
    assert seq == SCAN_CHUNK and dec_seq % SCAN_ROWS == 0 and (batch * seq) % SCAN_ROWS == 0

    c_rows = jnp.concatenate(
        [c_ctx[None], c, jnp.zeros((SUBLANES - 1 - dec_batch, D_MODEL), F32)], axis=0)
    mods = _ada(c_rows, w_ada, b_ada)

    xp = x_prompt.reshape(batch * seq, D_MODEL)
    xs = x_sample.reshape(dec_batch * dec_seq, D_MODEL)
    fn = final_norm[None]
    p = _stacked_params(norm1, norm2, b_gate, lru_conv, lru_conv_b, lru_wa, lru_ba, lru_wx, lru_bx,
                        lru_lambda, attn_sink, pool_w, pool_scale, ffn_conv, ffn_conv_b)
    raw = {"w_in": w_in, "w_branch": w_branch, "w_out": w_out, "ffn_up": ffn_up, "ffn_down": ffn_down}
    w_in_l = w_in[0].astype(BF16)
    mod_rows = mods.reshape(DEPTH * SUBLANES, 1, 6 * D_MODEL)
    ck = cache_k.reshape(dec_batch, DEPTH, past, D_KV).astype(BF16)
    cv = cache_v.reshape(dec_batch, DEPTH, past, D_KV).astype(BF16)
    h0 = state_lru.transpose(1, 2, 0, 3)[:, :, :, None, :]
    caches, hs = None, []
    for l in range(DEPTH):
        last = fn if l == DEPTH - 1 else None
        wts = {"w_in": w_in_l}
        xp, k_all, v_all, h_fin, w_in_l = _trunk_layer(xp, mod_rows, l * SUBLANES, batch * seq, p, wts, l, seq,
                                                       None, last, caches, raw)
        caches = (k_all, v_all)
        xs = _trunk_layer(xs, mod_rows, l * SUBLANES + 1, dec_seq, p, wts, l, dec_seq, (ck, cv, h0), last)[0]
        hs.append(h_fin.transpose(1, 0, 2))
    y_prompt = xp.reshape(batch, seq, D_MODEL)
    y_sample = xs.reshape(dec_batch, dec_seq, D_MODEL)
    cache_dims = (batch, DEPTH, seq, N_KV_HEADS, HEAD_DIM)
    return (y_prompt, y_sample, k_all.reshape(cache_dims), v_all.reshape(cache_dims), jnp.stack(hs, axis=1))
```

```python
import functools

import numpy as np
import jax
import jax.numpy as jnp
from jax import lax
from jax.experimental import pallas as pl
from jax.experimental.pallas import tpu as pltpu

F32 = jnp.float32
BF16 = jnp.bfloat16

D_MODEL = 1024
DEPTH = 2
GRID_W = 64
EPS = 1e-6
N_BRANCH = 3
D_RNN = 1024
LRU_HEADS = 8
LRU_BLOCK = D_RNN // LRU_HEADS
LRU_C = 8.0
N_HEADS = 8
N_KV_HEADS = 2
KV_GROUPS = N_HEADS // N_KV_HEADS
HEAD_DIM = 128
D_KV = N_KV_HEADS * HEAD_DIM
WINDOW = 128
BLOCK_Q = 128
ROPE_BASE = 10000.0
NEG_INF = -1e30
D_POOL = 1024
POOL_WINDOWS = (2, 4, 8, 16)
POOL_GROUP = D_POOL // len(POOL_WINDOWS)
D_FF = 2816
D_IN = D_RNN + N_HEADS * HEAD_DIM + 2 * D_KV + D_POOL + N_BRANCH * D_MODEL

COL_XA = 0
COL_Q = COL_XA + D_RNN
COL_K = COL_Q + N_HEADS * HEAD_DIM
COL_V = COL_K + D_KV
COL_XC = COL_V + D_KV
COL_G = COL_XC + D_POOL
LOG2E = float(np.log2(np.e))

VMEM_LIMIT_BYTES = 52 * 1024 * 1024
SUBLANES = 8
LANES = 128
BF16_ROWS = 16

SCAN_CHUNK = 256
SCAN_PITCH = 260
SCAN_ROWS = 2048


def _params(*sem):
    return pltpu.CompilerParams(dimension_semantics=sem, vmem_limit_bytes=VMEM_LIMIT_BYTES)


def _dot(a, b):
    return jnp.dot(a, b, preferred_element_type=F32)


def _dot_nt(a, b):
    return lax.dot_general(a, b, (((1,), (1,)), ((), ())), preferred_element_type=F32)


def _sigmoid(z):
    return 0.5 * (1.0 + jnp.tanh(0.5 * z))


def _rms(x):
    return x * lax.rsqrt(jnp.mean(x * x, axis=-1, keepdims=True) + EPS)


def _ada_kernel(c_ref, w_ref, b_ref, o_ref):
    c = c_ref[...]
    s = c * _sigmoid(c)
    o_ref[0] = _dot(s.astype(BF16), w_ref[0].astype(BF16)) + b_ref[0]


def _ada(c_rows, w_ada, b_ada):
    tn = 1536
    return pl.pallas_call(
        _ada_kernel,
        grid=(DEPTH, 6 * D_MODEL // tn),
        in_specs=[
            pl.BlockSpec((SUBLANES, D_MODEL), lambda l, j: (0, 0)),
            pl.BlockSpec((1, D_MODEL, tn), lambda l, j: (l, 0, j)),
            pl.BlockSpec((1, 1, tn), lambda l, j: (l, 0, j)),
        ],
        out_specs=pl.BlockSpec((1, SUBLANES, tn), lambda l, j: (l, 0, j)),
        out_shape=jax.ShapeDtypeStruct((DEPTH, SUBLANES, 6 * D_MODEL), F32),
        compiler_params=_params("parallel", "parallel"),
        name="ada_mod",
    )(c_rows, w_ada, b_ada.reshape(DEPTH, 1, 6 * D_MODEL))


INPROJ_TM = 512
INPROJ_CK = 2 * D_KV


def _resident(shape, layer):
    ndim = len(shape)
    return pl.BlockSpec((None,) + tuple(shape), lambda *_: (layer,) + (0,) * ndim,
                        pipeline_mode=pl.Buffered(1))


def _whole(shape):
    return pl.BlockSpec(tuple(shape), lambda *_: (0,) * len(shape), pipeline_mode=pl.Buffered(1))


def _side_cast_specs(side, steps):
    in_specs = [pl.BlockSpec((rows, arr.shape[1]), lambda i, first=first: (first + i, 0))
                for arr, rows, first in side]
    out_specs = [pl.BlockSpec((rows, arr.shape[1]), lambda i: (i, 0)) for arr, rows, _ in side]
    out_shapes = [jax.ShapeDtypeStruct((rows * steps, arr.shape[1]), BF16) for arr, rows, _ in side]
    return in_specs, out_specs, out_shapes


def _run_side_casts(refs, n_in, n_out, n_side):
    refs = list(refs)
    side_in = refs[n_in:n_in + n_side]
    side_out = refs[n_in + n_side + n_out:n_in + 2 * n_side + n_out]
    for src, dst in zip(side_in, side_out):
        dst[...] = src[...].astype(BF16)
    return refs[:n_in] + refs[n_in + n_side:n_in + n_side + n_out] + refs[n_in + 2 * n_side + n_out:]


def _layer_slabs(stacked, layer, steps):
    cols = stacked.shape[-1]
    rows = int(np.prod(stacked.shape[1:-1]))
    assert rows % (steps * BF16_ROWS) == 0
    return (stacked.reshape(stacked.shape[0] * rows, cols), rows // steps, layer * steps)


def _mod_spec(row0, per):
    return pl.BlockSpec((1, 1, 6 * D_MODEL), lambda i: (row0 + i // per, 0, 0))


def _inproj_kernel(*refs, seq_len, want_kv, layer, owns_cache, n_in, n_side):
    refs = _run_side_casts(refs, n_in, 4 if want_kv else 1, n_side)
    _inproj_body(*refs, seq_len=seq_len, want_kv=want_kv, layer=layer, owns_cache=owns_cache)


def _inproj_body(*refs, seq_len, want_kv, layer, owns_cache):
    if want_kv:
        x_ref, mod_ref, g_ref, w_ref, sink_ref = refs[:5]
        o_ref, k_ref, v_ref, y_ref = refs[-4:]
        q_chunks = []
        if owns_cache:
            for other in range(DEPTH):
                if other != layer:
                    k_ref[:, other] = jnp.zeros(k_ref.shape[:1] + k_ref.shape[2:], F32)
                    v_ref[:, other] = jnp.zeros(v_ref.shape[:1] + v_ref.shape[2:], F32)
    else:
        x_ref, mod_ref, g_ref, w_ref, cos_ref, sa_ref, sb_ref, o_ref = refs
    mod = mod_ref[0]
    shift = mod[:, 0:D_MODEL]
    scale = mod[:, D_MODEL:2 * D_MODEL]
    h = (_rms(x_ref[...]) * g_ref[...] * (1.0 + scale) + shift).astype(BF16)
    nf = HEAD_DIM // 4

    def rope(x):
        return (x * cos_ref[...] + pltpu.roll(x, HEAD_DIM - nf, 1) * sa_ref[...]
                + pltpu.roll(x, nf, 1) * sb_ref[...])

    for c in range(D_IN // INPROJ_CK):
        acc = _dot(h, w_ref[:, c * INPROJ_CK:(c + 1) * INPROJ_CK])
        if not want_kv:
            lo = c * INPROJ_CK
            heads = [acc[:, j * HEAD_DIM:(j + 1) * HEAD_DIM] for j in range(INPROJ_CK // HEAD_DIM)]
            heads = [rope(hd) if COL_Q <= lo + j * HEAD_DIM < COL_V else hd for j, hd in enumerate(heads)]
            heads = [hd * SOFTMAX_SCALE if COL_Q <= lo + j * HEAD_DIM < COL_K else hd
                     for j, hd in enumerate(heads)]
            acc = jnp.concatenate(heads, axis=1)
        o_ref[:, c * INPROJ_CK:(c + 1) * INPROJ_CK] = acc.astype(o_ref.dtype)
        if want_kv and COL_Q <= c * INPROJ_CK < COL_K:
            q_chunks.append((acc * SOFTMAX_SCALE).astype(BF16))
        if want_kv and c == COL_K // INPROJ_CK:
            for b in range(INPROJ_TM // seq_len):
                rows = slice(b * seq_len, (b + 1) * seq_len)
                for hd in range(N_KV_HEADS):
                    dst = pl.ds(hd, seq_len, stride=N_KV_HEADS)
                    at = (b, layer, dst, slice(None)) if owns_cache else (b, dst, slice(None))
                    k_ref[at] = acc[rows, hd * HEAD_DIM:(hd + 1) * HEAD_DIM]
                    v_ref[at] = acc[rows, D_KV + hd * HEAD_DIM:D_KV + (hd + 1) * HEAD_DIM]
            kv = acc.astype(BF16)

            def scores(b, kh):
                rows = slice(b * seq_len, (b + 1) * seq_len)
                q = jnp.concatenate([q_chunks[kh][rows, g * HEAD_DIM:(g + 1) * HEAD_DIM]
                                     for g in range(KV_GROUPS)], axis=0)
                return (b, kh, _dot_nt(kv[rows, kh * HEAD_DIM:(kh + 1) * HEAD_DIM], q),
                        kv[rows, D_KV + kh * HEAD_DIM:D_KV + (kh + 1) * HEAD_DIM])

            order = [(b, kh) for b in range(INPROJ_TM // seq_len) for kh in range(N_KV_HEADS)]
            pending = [scores(*ch) for ch in order[:ATTN_LOOKAHEAD]]
            for n in range(len(order)):
                if n + ATTN_LOOKAHEAD < len(order):
                    pending.append(scores(*order[n + ATTN_LOOKAHEAD]))
                b, kh, t, vals = pending.pop(0)
                o_t = _softmax_pv_t(t, _sink_row(sink_ref, layer, kh, seq_len), vals)
                _store_heads_t(y_ref, o_t, kh, b * seq_len, seq_len)


def _inproj(x, mod, mod_row0, rows_per_mod, g, w, layer, seq_len, want_kv, sink=None, caches=None, side=()):
    m = x.shape[0]
    tm = INPROJ_TM
    per = rows_per_mod // tm
    proj_spec = pl.BlockSpec((tm, D_IN), lambda i: (i, 0))
    proj_shape = jax.ShapeDtypeStruct((m, D_IN), BF16)
    aliases = {}
    owns_cache = want_kv and caches is None
    if want_kv:
        nb = tm // seq_len
        if owns_cache:
            cache_spec = pl.BlockSpec((nb, DEPTH, seq_len * N_KV_HEADS, HEAD_DIM), lambda i: (i, 0, 0, 0))
        else:
            cache_spec = pl.BlockSpec((nb, None, seq_len * N_KV_HEADS, HEAD_DIM), lambda i: (i, layer, 0, 0))
        cache_shape = jax.ShapeDtypeStruct((m // seq_len, DEPTH, seq_len * N_KV_HEADS, HEAD_DIM), F32)
        assert tm % seq_len == 0 and INPROJ_CK == KV_GROUPS * HEAD_DIM and COL_Q % INPROJ_CK == 0
        attn_spec = pl.BlockSpec((tm, N_HEADS * HEAD_DIM), lambda i: (i, 0))
        out_specs = (proj_spec, cache_spec, cache_spec, attn_spec)
        out_shape = (proj_shape, cache_shape, cache_shape, jax.ShapeDtypeStruct((m, N_HEADS * HEAD_DIM), BF16))
        extra_specs, extra_args = [pl.BlockSpec(memory_space=pltpu.SMEM)], [sink]
        if caches is not None:
            extra_specs += [pl.BlockSpec(memory_space=pl.ANY)] * 2
            extra_args += list(caches)
            aliases = {5: 1, 6: 2}
    else:
        out_specs, out_shape = (proj_spec,), (proj_shape,)
        tab = pl.BlockSpec((tm, HEAD_DIM), lambda i: (i % (seq_len // tm), 0))
        extra_specs, extra_args = [tab, tab, tab], list(_rope_tables(seq_len))
    side_in, side_out, side_shapes = _side_cast_specs(side, m // tm)
    out = pl.pallas_call(
        functools.partial(_inproj_kernel, seq_len=seq_len, want_kv=want_kv, layer=layer, owns_cache=owns_cache,
                          n_in=4 + len(extra_specs), n_side=len(side)),
        grid=(m // tm,),
        in_specs=[
            pl.BlockSpec((tm, D_MODEL), lambda i: (i, 0)),
            _mod_spec(mod_row0, per),
            _resident((1, D_MODEL), layer),
            _whole((D_MODEL, D_IN)),
        ] + extra_specs + side_in,
        out_specs=tuple(out_specs) + tuple(side_out),
        out_shape=tuple(out_shape) + tuple(side_shapes),
        input_output_aliases=aliases,
        compiler_params=_params("parallel"),
        name="in_proj",
    )(x, mod, g, w, *extra_args, *[item[0] for item in side])
    n_own = len(out_specs)
    own = tuple(out[:n_own]) if want_kv else (out[0], None, None, None)
    return own + (list(out[n_own:]),)


LRU_HEADS_PER_STEP = 2


def _chunk_neighbour(v, towards_later):
    sub = lax.broadcasted_iota(jnp.int32, v.shape, 0)
    if towards_later:
        return jnp.where(sub >= 1, pltpu.roll(v, 1, 0), 0.0)
    return jnp.where(sub <= SUBLANES - 2, pltpu.roll(v, SUBLANES - 1, 0), 0.0)


def _shift_time(x_tm, k, chained):
    n = x_tm.shape[0]
    steps = abs(k)
    edge = []
    for s in range(steps):
        if not chained:
            edge.append(jnp.zeros((SUBLANES, x_tm.shape[1]), F32))
        elif k > 0:
            src = n - (steps - s) * SUBLANES
            edge.append(_chunk_neighbour(x_tm[src:src + SUBLANES], True))
        else:
            edge.append(_chunk_neighbour(x_tm[s * SUBLANES:(s + 1) * SUBLANES], False))
    if k > 0:
        return jnp.concatenate(edge + [x_tm[:n - steps * SUBLANES]], axis=0)
    return jnp.concatenate([x_tm[steps * SUBLANES:]] + edge, axis=0)


def _lru_tm_kernel(*refs, seq_len, latent):
    if latent:
        (x_ref, cw_ref, cb_ref, wg_ref, bg_ref, lam_ref, h0_ref, y_ref,
         io_ref, af_ref, bf_ref, ab_ref, bb_ref, hf_ref, hb_ref) = refs
    else:
        (x_ref, cw_ref, cb_ref, wg_ref, bg_ref, lam_ref, y_ref, fin_ref,
         io_ref, af_ref, bf_ref, ab_ref, bb_ref, hf_ref, hb_ref) = refs
    rows = SCAN_ROWS
    nchunk = rows // SCAN_CHUNK
    nhead = LRU_HEADS_PER_STEP
    chained = seq_len > SCAN_CHUNK
    for hd in range(nhead):
        lanes = slice(hd * LRU_BLOCK, (hd + 1) * LRU_BLOCK)
        x = x_ref[:, lanes].astype(F32)
        for c in range(nchunk):
            io_ref[hd, pl.ds(c * SCAN_PITCH, SCAN_CHUNK), :] = x[c * SCAN_CHUNK:(c + 1) * SCAN_CHUNK]
        x = jnp.concatenate(
            [io_ref[hd, pl.ds(t, nchunk, stride=SCAN_PITCH), :] for t in range(SCAN_CHUNK)], axis=0)
        cw = 0.5 * cw_ref[:, lanes]
        xh = (_shift_time(x, 2, chained) * cw[0:1] + _shift_time(x, 1, chained) * cw[1:2]
              + x * cw[2:3] + _shift_time(x, -1, chained) * cw[3:4] + 0.5 * cb_ref[:, lanes])
        half_bias = 0.5 * bg_ref[hd]
        bias_hi = half_bias.astype(BF16).astype(F32)
        bias_rows = jnp.concatenate(
            [bias_hi, half_bias - bias_hi, jnp.zeros((BF16_ROWS - 2, 4 * LRU_BLOCK), F32)], axis=0)
        lhs = jnp.concatenate([xh.astype(BF16), jnp.ones((rows, BF16_ROWS), BF16)], axis=1)
        rhs = jnp.concatenate([wg_ref[hd], bias_rows.astype(BF16)], axis=0)
        th = jnp.tanh(_dot(lhs, rhs))
        lam = lam_ref[hd]
        for d, (a_ref, b_ref) in enumerate(((af_ref, bf_ref), (ab_ref, bb_ref))):
            th_r = th[:, 2 * d * LRU_BLOCK:(2 * d + 1) * LRU_BLOCK]
            th_i = th[:, (2 * d + 1) * LRU_BLOCK:(2 * d + 2) * LRU_BLOCK]
            nl = -lam[:, d * LRU_BLOCK:(d + 1) * LRU_BLOCK]
            softplus = jnp.maximum(nl, 0.0) + jnp.log(1.0 + jnp.exp(-jnp.abs(nl)))
            ch = (-0.5 * LRU_C * LOG2E) * softplus
            a = jnp.exp2(ch + ch * th_r)
            y = 1.0 - a * a
            a_ref[hd] = a
            b_ref[hd] = (y * lax.rsqrt(jnp.maximum(y, 1e-30))) * ((1.0 + th_i) * xh)

    def step(t, carry):
        rf = pl.ds(pl.multiple_of(t * nchunk, nchunk), nchunk)
        rb = pl.ds(pl.multiple_of((SCAN_CHUNK - 1 - t) * nchunk, nchunk), nchunk)
        out = []
        for hd in range(nhead):
            hf, hb, pf, pb = carry[4 * hd:4 * hd + 4]
            a_f = af_ref[hd, rf, :]
            a_b = ab_ref[hd, rb, :]
            hf = a_f * hf + bf_ref[hd, rf, :]
            hb = a_b * hb + bb_ref[hd, rb, :]
            hf_ref[hd, rf, :] = hf
            hb_ref[hd, rb, :] = hb
            if chained:
                pf = pf * a_f
                pb = pb * a_b
                af_ref[hd, rf, :] = pf
                ab_ref[hd, rb, :] = pb
            out += [hf, hb, pf, pb]
        return tuple(out)

    zero = jnp.zeros((nchunk, LRU_BLOCK), F32)
    one = jnp.ones((nchunk, LRU_BLOCK), F32)
    lax.fori_loop(0, SCAN_CHUNK, step, (zero, zero, one, one) * nhead, unroll=8)

    sub = lax.broadcasted_iota(jnp.int32, (nchunk, LRU_BLOCK), 0)
    last = slice(rows - nchunk, rows)
    first = slice(0, nchunk)
    for hd in range(nhead):
        lanes = slice(hd * LRU_BLOCK, (hd + 1) * LRU_BLOCK)
        hf = hf_ref[hd]
        hb = hb_ref[hd]
        if chained:
            pf = af_ref[hd]
            pb = ab_ref[hd]
            ef = jnp.where(sub == 0, h0_ref[0, 0][:, lanes], 0.0)
            eb = jnp.where(sub == nchunk - 1, h0_ref[1, 0][:, lanes], 0.0)
            for c in range(1, nchunk):
                ef = jnp.where(sub == c, pltpu.roll(hf[last] + pf[last] * ef, 1, 0), ef)
                eb = jnp.where(sub == nchunk - 1 - c,
                               pltpu.roll(hb[first] + pb[first] * eb, nchunk - 1, 0), eb)
            hf = (hf.reshape(SCAN_CHUNK, nchunk, LRU_BLOCK)
                  + pf.reshape(SCAN_CHUNK, nchunk, LRU_BLOCK) * ef[None]).reshape(rows, LRU_BLOCK)
            hb = (hb.reshape(SCAN_CHUNK, nchunk, LRU_BLOCK)
                  + pb.reshape(SCAN_CHUNK, nchunk, LRU_BLOCK) * eb[None]).reshape(rows, LRU_BLOCK)
        else:
            fin_ref[0, :, lanes] = hf[last]
            fin_ref[1, :, lanes] = hb[first]
        y = hf + hb
        for t in range(SCAN_CHUNK):
            io_ref[hd, pl.ds(t, nchunk, stride=SCAN_PITCH), :] = y[t * nchunk:(t + 1) * nchunk]
        for c in range(nchunk):
            y_ref[pl.ds(c * SCAN_CHUNK, SCAN_CHUNK), lanes] = (
                io_ref[hd, pl.ds(c * SCAN_PITCH, SCAN_CHUNK), :].astype(y_ref.dtype))


def _lru(proj, lp, layer, seq_len, h0):
    m = proj.shape[0]
    latent = h0 is not None
    rows = SCAN_ROWS
    nhead = LRU_HEADS_PER_STEP
    width = nhead * LRU_BLOCK
    in_specs = [
        pl.BlockSpec((rows, width), lambda r, h: (r, COL_XA // width + h)),
        pl.BlockSpec((None, 4, width), lambda r, h: (layer, 0, h)),
        pl.BlockSpec((None, 1, width), lambda r, h: (layer, 0, h)),
        pl.BlockSpec((None, nhead, LRU_BLOCK, 4 * LRU_BLOCK), lambda r, h: (layer, h, 0, 0)),
        pl.BlockSpec((None, nhead, 1, 4 * LRU_BLOCK), lambda r, h: (layer, h, 0, 0)),
        pl.BlockSpec((None, nhead, 1, 2 * LRU_BLOCK), lambda r, h: (layer, h, 0, 0)),
    ]
    args = [proj, lp["conv_w"], lp["conv_b"], lp["wg"], lp["bg"], lp["lam"]]
    y_spec = pl.BlockSpec((rows, width), lambda r, h: (r, h))
    y_shape = jax.ShapeDtypeStruct((m, D_RNN), BF16)
    if latent:
        in_specs.append(pl.BlockSpec((None, 2, 1, 1, width), lambda r, h: (layer, 0, r, 0, h)))
        args.append(h0)
        out_specs, out_shape = y_spec, y_shape
    else:
        nseq = m // seq_len
        out_specs = (y_spec, pl.BlockSpec((2, rows // seq_len, width), lambda r, h: (0, r, h)))
        out_shape = (y_shape, jax.ShapeDtypeStruct((2, nseq, D_RNN), F32))
    strided_buf = pltpu.VMEM((nhead, rows // SCAN_CHUNK * SCAN_PITCH, LRU_BLOCK), F32)
    scan_buf = pltpu.VMEM((nhead, rows, LRU_BLOCK), F32)
    return pl.pallas_call(
        functools.partial(_lru_tm_kernel, seq_len=seq_len, latent=latent),
        grid=(m // rows, LRU_HEADS // nhead),
        in_specs=in_specs,
        out_specs=out_specs,
        out_shape=out_shape,
        scratch_shapes=[strided_buf] + [scan_buf] * 6,
        compiler_params=_params("parallel", "parallel"),
        name="rglru_latent" if latent else "rglru_context",
    )(*args)


SOFTMAX_SCALE = HEAD_DIM ** -0.5 * LOG2E


def _stack_heads(q_ref, kh, r0, rows):
    parts = [q_ref[r0:r0 + rows, (kh * KV_GROUPS + g) * HEAD_DIM:(kh * KV_GROUPS + g + 1) * HEAD_DIM]
             for g in range(KV_GROUPS)]
    return jnp.concatenate(parts, axis=0)


def _dot_tn(a, b):
    return lax.dot_general(a, b, (((0,), (0,)), ((), ())), preferred_element_type=F32)


def _sink_row(sink_ref, layer, kh, cols):
    parts = [jnp.full((1, cols), sink_ref[layer, kh * KV_GROUPS + g] * LOG2E, F32) for g in range(KV_GROUPS)]
    return jnp.concatenate(parts, axis=1)


def _softmax_pv_t(t, sink, v):
    m = jnp.maximum(jnp.max(t, axis=0, keepdims=True), sink)
    e = jnp.exp2(t - m).astype(BF16)
    d = v.shape[1]
    v_ones = jnp.concatenate([v, jnp.ones((v.shape[0], SUBLANES), BF16)], axis=1)
    o_sum = _dot_tn(v_ones, e)
    denom = o_sum[d:d + 1] + jnp.exp2(sink - m)
    return o_sum[:d] * (1.0 / denom)


def _store_heads_t(y_ref, o_t, kh, r0, rows):
    for g in range(KV_GROUPS):
        h = kh * KV_GROUPS + g
        y_ref[r0:r0 + rows, h * HEAD_DIM:(h + 1) * HEAD_DIM] = (
            o_t[:, g * rows:(g + 1) * rows].T.astype(y_ref.dtype))


def _rope_tables(seq_len):
    nf = HEAD_DIM // 4
    freqs = ROPE_BASE ** (-np.arange(nf, dtype=np.float64) / nf)
    t = np.arange(seq_len)
    ang_row = (t // GRID_W)[:, None] * freqs[None, :]
    ang_col = (t % GRID_W)[:, None] * freqs[None, :]
    ang = np.concatenate([ang_row, ang_row, ang_col, ang_col], axis=1)
    first = (np.arange(HEAD_DIM) % (2 * nf)) < nf
    cos = np.cos(ang)
    sin = np.sin(ang)
    sin_a = np.where(first[None, :], -sin, 0.0)
    sin_b = np.where(first[None, :], 0.0, sin)
    return tuple(jnp.asarray(a, F32) for a in (cos, sin_a, sin_b))


ATTN_LAT_BLOCKS = 4
ATTN_LOOKAHEAD = 3


def _attn_lat_kernel(sink_ref, q_ref, kvp_ref, kvc_ref, kvn_ref, ck_ref, cv_ref, y_ref, *, nblk, layer):
    step = pl.program_id(1)
    cols = KV_GROUPS * BLOCK_Q
    span = 3 * BLOCK_Q
    key = lax.broadcasted_iota(jnp.int32, (span, cols), 0)
    qry = lax.broadcasted_iota(jnp.int32, (span, cols), 1) & (BLOCK_Q - 1)
    kv = jnp.concatenate([kvp_ref[...], kvc_ref[...], kvn_ref[...]], axis=0)
    def scores(i, kh):
        win = kv[i * BLOCK_Q:i * BLOCK_Q + span]
        sl = slice(kh * HEAD_DIM, (kh + 1) * HEAD_DIM)
        vl = slice(D_KV + kh * HEAD_DIM, D_KV + (kh + 1) * HEAD_DIM)
        q = _stack_heads(q_ref, kh, i * BLOCK_Q, BLOCK_Q)
        keys = jnp.concatenate([win[:, sl], ck_ref[0, :, sl]], axis=0)
        vals = jnp.concatenate([win[:, vl], cv_ref[0, :, sl]], axis=0)
        return i, kh, _dot_nt(keys, q), vals

    order = [(i, kh) for i in range(ATTN_LAT_BLOCKS) for kh in range(N_KV_HEADS)]
    pending = [scores(*c) for c in order[:ATTN_LOOKAHEAD]]
    for n in range(len(order)):
        if n + ATTN_LOOKAHEAD < len(order):
            pending.append(scores(*order[n + ATTN_LOOKAHEAD]))
        i, kh, t, vals = pending.pop(0)
        j = step * ATTN_LAT_BLOCKS + i
        lo = jnp.where(j > 0, qry, BLOCK_Q)
        hi = jnp.where(j < nblk - 1, qry + 2 * BLOCK_Q, 2 * BLOCK_Q - 1)
        bias = jnp.where(jnp.logical_and(key >= lo, key <= hi), 0.0, NEG_INF)
        t = jnp.concatenate([t[:span] + bias, t[span:]], axis=0)
        o_t = _softmax_pv_t(t, _sink_row(sink_ref, layer, kh, BLOCK_Q), vals)
        _store_heads_t(y_ref, o_t, kh, i * BLOCK_Q, BLOCK_Q)


def _attn_lat(proj, ck, cv, sink, layer, seq_len):
    m = proj.shape[0]
    nblk = seq_len // BLOCK_Q
    nstep = nblk // ATTN_LAT_BLOCKS
    rows = ATTN_LAT_BLOCKS * BLOCK_Q
    nb = m // seq_len
    past = ck.shape[2]
    kv_col = COL_K // (2 * D_KV)
    assert COL_V == COL_K + D_KV and COL_K % (2 * D_KV) == 0

    def halo(shift):
        def index(b, s):
            return (b * nblk + jnp.clip(s * ATTN_LAT_BLOCKS + shift, 0, nblk - 1), kv_col)
        return index

    return pl.pallas_call(
        functools.partial(_attn_lat_kernel, nblk=nblk, layer=layer),
        grid=(nb, nstep),
        in_specs=[
            pl.BlockSpec(memory_space=pltpu.SMEM),
            pl.BlockSpec((rows, N_HEADS * HEAD_DIM), lambda b, s: (b * nstep + s, COL_Q // (N_HEADS * HEAD_DIM))),
            pl.BlockSpec((BLOCK_Q, 2 * D_KV), halo(-1)),
            pl.BlockSpec((rows, 2 * D_KV), lambda b, s: (b * nstep + s, kv_col)),
            pl.BlockSpec((BLOCK_Q, 2 * D_KV), halo(ATTN_LAT_BLOCKS)),
            pl.BlockSpec((1, None, past, D_KV), lambda b, s: (b, layer, 0, 0)),
            pl.BlockSpec((1, None, past, D_KV), lambda b, s: (b, layer, 0, 0)),
        ],
        out_specs=pl.BlockSpec((rows, N_HEADS * HEAD_DIM), lambda b, s: (b * nstep + s, 0)),
        out_shape=jax.ShapeDtypeStruct((m, N_HEADS * HEAD_DIM), BF16),
        compiler_params=_params("parallel", "parallel"),
        name="attn_latent",
    )(sink, proj, proj, proj, proj, ck, cv)


POOL_TILE = 256
POOL_LEAD = BF16_ROWS
POOL_ROWS = 2048


def _pool_plan(seq_len):
    lead = 0 if seq_len == POOL_TILE else POOL_LEAD
    return lead, POOL_TILE - 2 * lead


def _pool_bands(seq_len):
    lead, nout = _pool_plan(seq_len)
    r = np.arange(nout)[:, None]
    c = np.arange(POOL_TILE)[None, :] - lead
    bands = [(c >= r - win // 2) & (c < r + win // 2) for win in POOL_WINDOWS]
    return jnp.asarray(np.stack(bands), BF16)


def _pool_kernel(x0_ref, x1_ref, x2_ref, x3_ref, band_ref, w_ref, s_ref, y_ref, pad_ref, *, seq_len):
    lead, nout = _pool_plan(seq_len)
    edge = lax.broadcasted_iota(jnp.int32, (SUBLANES, POOL_GROUP), 0)
    if lead:
        pad_ref[0:lead, :] = jnp.zeros((lead, POOL_GROUP), BF16)
        pad_ref[lead + seq_len:, :] = jnp.zeros((pad_ref.shape[0] - lead - seq_len, POOL_GROUP), BF16)
    for gi, (win, x_ref) in enumerate(zip(POOL_WINDOWS, (x0_ref, x1_ref, x2_ref, x3_ref))):
        cs = slice(gi * POOL_GROUP, (gi + 1) * POOL_GROUP)
        half = win // 2
        inv_head = 1.0 / ((edge + half) - jnp.maximum(edge - half, 0)).astype(F32)
        inv_tail = 1.0 / (jnp.minimum(SUBLANES - edge, half) + half).astype(F32)
        if lead:
            pad_ref[lead:lead + seq_len, :] = x_ref[...]
        for base in range(0, POOL_ROWS, seq_len):
            for p0 in range(0, seq_len, nout):
                n = min(nout, seq_len - p0)
                x = x_ref[base + p0:base + p0 + n, :]
                src = pad_ref[p0:p0 + POOL_TILE, :] if lead else x
                sums = _dot(band_ref[gi, :n, :], src)
                head = sums[:SUBLANES] * (inv_head if p0 == 0 else 1.0 / win)
                tail = sums[n - SUBLANES:] * (inv_tail if p0 + n == seq_len else 1.0 / win)
                mean = jnp.concatenate([head, sums[SUBLANES:n - SUBLANES] * (1.0 / win), tail], axis=0)
                pooled = (mean - x.astype(F32)).astype(BF16)
                y_ref[base + p0:base + p0 + n, cs] = (
                    _dot(pooled, w_ref[gi]) * s_ref[:, cs]).astype(y_ref.dtype)


def _pool(proj, w, s, layer, seq_len):
    m = proj.shape[0]
    rows = POOL_ROWS
    lead, nout = _pool_plan(seq_len)
    assert seq_len in (POOL_TILE, rows) and max(POOL_WINDOWS) // 2 <= min(SUBLANES, lead or SUBLANES)
    pad_rows = (pl.cdiv(seq_len, nout) - 1) * nout + POOL_TILE
    group = lambda gi: pl.BlockSpec((rows, POOL_GROUP), lambda r: (r, COL_XC // POOL_GROUP + gi))
    whole = lambda shape: pl.BlockSpec(shape, lambda r: (0,) * len(shape))
    nwin = len(POOL_WINDOWS)
    return pl.pallas_call(
        functools.partial(_pool_kernel, seq_len=seq_len),
        grid=(m // rows,),
        in_specs=[
            group(0), group(1), group(2), group(3),
            whole((nwin, nout, POOL_TILE)),
            _resident((nwin, POOL_GROUP, POOL_GROUP), layer),
            _resident((1, D_POOL), layer),
        ],
        out_specs=pl.BlockSpec((rows, D_POOL), lambda r: (r, 0)),
        out_shape=jax.ShapeDtypeStruct((m, D_POOL), BF16),
        scratch_shapes=[pltpu.VMEM((pad_rows, POOL_GROUP), BF16)],
        compiler_params=_params("parallel"),
        name="pool_mix",
    )(proj, proj, proj, proj, _pool_bands(seq_len), w, s)


MERGE_N_IN = 15
MERGE_TM = 512


def _merge_kernel(*refs, n_side):
    (ya_ref, yb_ref, yc_ref, g0_ref, g1_ref, g2_ref, g3_ref, g4_ref, g5_ref, x_ref, mod_ref,
     bg_ref, wb_ref, wo_ref, n2_ref, x1_ref, h2_ref) = _run_side_casts(refs, MERGE_N_IN, 2, n_side)
    mod = mod_ref[0]
    g_refs = (g0_ref, g1_ref, g2_ref, g3_ref, g4_ref, g5_ref)
    half = D_MODEL // 2
    merged = None
    for k, y_ref in enumerate((ya_ref, yb_ref, yc_ref)):
        y = _dot(y_ref[...], wb_ref[k])
        parts = []
        for p in range(2):
            z = g_refs[2 * k + p][...].astype(F32) + bg_ref[:, k * D_MODEL + p * half:k * D_MODEL + (p + 1) * half]
            parts.append((1.0 + jnp.tanh(0.5 * z)) * y[:, p * half:(p + 1) * half])
        term = jnp.concatenate(parts, axis=1)
        merged = term if merged is None else merged + term
    merged = 0.5 * merged
    gate1 = mod[:, 2 * D_MODEL:3 * D_MODEL]
    x1 = x_ref[...] + gate1 * _dot(merged.astype(BF16), wo_ref[...])
    x1_ref[...] = x1
    shift2 = mod[:, 3 * D_MODEL:4 * D_MODEL]
    scale2 = mod[:, 4 * D_MODEL:5 * D_MODEL]
    h2_ref[...] = (_rms(x1) * n2_ref[...] * (1.0 + scale2) + shift2).astype(BF16)


def _merge(ya, yb, yc, proj, x, mod, mod_row0, rows_per_mod, bg, wb, wo, n2, layer, side=()):
    m = x.shape[0]
    tm = MERGE_TM
    per = rows_per_mod // tm
    half = D_MODEL // 2
    row = pl.BlockSpec((tm, D_MODEL), lambda i: (i, 0))
    gate = lambda c: pl.BlockSpec((tm, half), lambda i: (i, COL_G // half + c))
    side_in, side_out, side_shapes = _side_cast_specs(side, m // tm)
    in_specs = [
        row, row, row,
        gate(0), gate(1), gate(2), gate(3), gate(4), gate(5),
        row,
        _mod_spec(mod_row0, per),
        _resident((1, N_BRANCH * D_MODEL), layer),
        _whole((N_BRANCH, D_MODEL, D_MODEL)),
        _whole((D_MODEL, D_MODEL)),
        _resident((1, D_MODEL), layer),
    ]
    assert len(in_specs) == MERGE_N_IN
    out = pl.pallas_call(
        functools.partial(_merge_kernel, n_side=len(side)),
        grid=(m // tm,),
        in_specs=in_specs + side_in,
        out_specs=(row, row) + tuple(side_out),
        out_shape=(jax.ShapeDtypeStruct((m, D_MODEL), F32), jax.ShapeDtypeStruct((m, D_MODEL), BF16))
        + tuple(side_shapes),
        compiler_params=_params("parallel"),
        name="merge_out",
    )(ya, yb, yc, proj, proj, proj, proj, proj, proj, x, mod, bg, wb, wo, n2, *[item[0] for item in side])
    return out[0], out[1], list(out[2:])


FFN_CK = 256
FFN_TM = 512
FFN_GAP = SUBLANES
GELU_C = float(np.sqrt(2.0 / np.pi))


def _ffn_kernel(*refs, seq_len, final, n_side):
    refs = _run_side_casts(refs, 9 + int(final), 1, n_side)
    h_ref, hp_ref, hn_ref, x_ref, mod_ref, wup_ref, cw_ref, cb_ref, wd_ref = refs[:9]
    fn_ref = refs[9] if final else None
    o_ref, hx_ref, u_ref, act_ref = refs[-4:]
    tm = FFN_TM
    halo = seq_len > tm
    i = pl.program_id(0)

    if halo:
        per_seq = seq_len // tm
        at_start = i % per_seq == 0
        at_end = i % per_seq == per_seq - 1
        zeros = jnp.zeros((BF16_ROWS, D_MODEL), BF16)

        @pl.when(at_start)
        def _():
            hx_ref[0:BF16_ROWS, :] = zeros

        @pl.when(jnp.logical_not(at_start))
        def _():
            hx_ref[0:BF16_ROWS, :] = hp_ref[...]

        @pl.when(at_end)
        def _():
            hx_ref[BF16_ROWS + tm:, :] = zeros

        @pl.when(jnp.logical_not(at_end))
        def _():
            hx_ref[BF16_ROWS + tm:, :] = hn_ref[...]

        hx_ref[BF16_ROWS:BF16_ROWS + tm, :] = h_ref[...]
        bases = (BF16_ROWS,)
        seg = tm
    else:
        nseg = tm // seq_len
        seg = seq_len
        bases = tuple(FFN_GAP + s * (seg + FFN_GAP) for s in range(nseg))
        for s in range(nseg + 1):
            u_ref[s * (seg + FFN_GAP):s * (seg + FFN_GAP) + FFN_GAP, :] = jnp.zeros((FFN_GAP, FFN_CK), F32)

    def taps(offset):
        return jnp.concatenate([u_ref[b + offset:b + offset + seg, :] for b in bases], axis=0)

    for c in range(D_FF // FFN_CK):
        cs = slice(c * FFN_CK, (c + 1) * FFN_CK)
        vs = slice(D_FF + c * FFN_CK, D_FF + (c + 1) * FFN_CK)
        if halo:
            u_ext = _dot(hx_ref[...], wup_ref[:, cs])
            u_ref[...] = u_ext
            u0 = u_ext[BF16_ROWS:BF16_ROWS + tm]
        else:
            u0 = _dot(h_ref[...], wup_ref[:, cs])
            for s, b in enumerate(bases):
                u_ref[b:b + seg, :] = u0[s * seg:(s + 1) * seg]
        uv = _dot(h_ref[...], wup_ref[:, vs])
        gff = taps(-1) * cw_ref[0:1, cs] + u0 * cw_ref[1:2, cs] + taps(1) * cw_ref[2:3, cs] + cb_ref[:, cs]
        inner = gff * (GELU_C + (GELU_C * 0.044715) * (gff * gff))
        act_ref[:, cs] = (0.5 * (gff * uv) * (1.0 + jnp.tanh(inner))).astype(BF16)

    gate2 = mod_ref[0][:, 5 * D_MODEL:6 * D_MODEL]
    out = x_ref[...] + gate2 * _dot(act_ref[...], wd_ref[...])
    if final:
        out = _rms(out) * fn_ref[...]
    o_ref[...] = out


def _ffn(h2, x1, mod, mod_row0, rows_per_mod, wup, cw, cb, wd, layer, seq_len, final_norm, side=()):
    m = x1.shape[0]
    tm = FFN_TM
    per = rows_per_mod // tm
    hb = tm // BF16_ROWS
    last_halo = m // BF16_ROWS - 1
    final = final_norm is not None
    halo = seq_len > tm
    in_specs = [
        pl.BlockSpec((tm, D_MODEL), lambda i: (i, 0)),
        pl.BlockSpec((BF16_ROWS, D_MODEL), lambda i: (jnp.maximum(i * hb - 1, 0), 0)),
        pl.BlockSpec((BF16_ROWS, D_MODEL), lambda i: (jnp.minimum((i + 1) * hb, last_halo), 0)),
        pl.BlockSpec((tm, D_MODEL), lambda i: (i, 0)),
        _mod_spec(mod_row0, per),
        _whole((D_MODEL, 2 * D_FF)),
        _resident((3, D_FF), layer),
        _resident((1, D_FF), layer),
        _whole((D_FF, D_MODEL)),
    ]
    args = [h2, h2, h2, x1, mod, wup, cw, cb, wd]
    if final:
        in_specs.append(_whole((1, D_MODEL)))
        args.append(final_norm)
    if halo:
        u_rows = tm + 2 * BF16_ROWS
    else:
        u_rows = FFN_GAP + (tm // seq_len) * (seq_len + FFN_GAP)
    side_in, side_out, side_shapes = _side_cast_specs(side, m // tm)
    out = pl.pallas_call(
        functools.partial(_ffn_kernel, seq_len=seq_len, final=final, n_side=len(side)),
        grid=(m // tm,),
        in_specs=in_specs + side_in,
        out_specs=(pl.BlockSpec((tm, D_MODEL), lambda i: (i, 0)),) + tuple(side_out),
        out_shape=(jax.ShapeDtypeStruct((m, D_MODEL), F32),) + tuple(side_shapes),
        scratch_shapes=[
            pltpu.VMEM((tm + 2 * BF16_ROWS, D_MODEL), BF16),
            pltpu.VMEM((u_rows, FFN_CK), F32),
            pltpu.VMEM((tm, D_FF), BF16),
        ],
        compiler_params=_params("parallel"),
        name="conv_glu_ffn",
    )(*args, *[item[0] for item in side])
    return out[0], list(out[1:])


def _trunk_layer(x, mod, mod_row0, rows_per_mod, p, wts, layer, seq_len, ctx, final_norm, caches=None,
                 raw=None):
    m = x.shape[0]
    cast = raw is not None
    side = [_layer_slabs(raw[k], layer, m // INPROJ_TM) for k in ("w_branch", "w_out")] if cast else ()
    proj, k_new, v_new, yb, done = _inproj(x, mod, mod_row0, rows_per_mod, p["norm1"], wts["w_in"], layer,
                                           seq_len, want_kv=ctx is None, sink=p["sink"], caches=caches, side=side)
    if cast:
        wts["w_branch"] = done[0].reshape(N_BRANCH, D_MODEL, D_MODEL)
        wts["w_out"] = done[1]
    if ctx is None:
        ya, h_fin = _lru(proj, p["lru"], layer, seq_len, None)
    else:
        ck, cv, h0 = ctx
        ya = _lru(proj, p["lru"], layer, seq_len, h0)
        h_fin = None
        yb = _attn_lat(proj, ck, cv, p["sink"], layer, seq_len)
    yc = _pool(proj, p["pool_w"], p["pool_scale"], layer, seq_len)
    side = [_layer_slabs(raw[k], layer, m // MERGE_TM) for k in ("ffn_up", "ffn_down")] if cast else ()
    x1, h2, done = _merge(ya, yb, yc, proj, x, mod, mod_row0, rows_per_mod, p["b_gate"], wts["w_branch"],
                          wts["w_out"], p["norm2"], layer, side=side)
    if cast:
        wts["ffn_up"], wts["ffn_down"] = done
    side = [_layer_slabs(raw["w_in"], layer + 1, m // FFN_TM)] if cast and layer + 1 < DEPTH else ()
    out, done = _ffn(h2, x1, mod, mod_row0, rows_per_mod, wts["ffn_up"], p["ffn_conv"], p["ffn_conv_b"],
                     wts["ffn_down"], layer, seq_len, final_norm, side=side)
    return out, k_new, v_new, h_fin, (done[0] if side else None)


def _stacked_params(norm1, norm2, b_gate, lru_conv, lru_conv_b, lru_wa, lru_ba, lru_wx, lru_bx,
                    lru_lambda, attn_sink, pool_w, pool_scale, ffn_conv, ffn_conv_b):
    def per_head(v):
        return v.reshape(DEPTH, 2, LRU_HEADS, LRU_BLOCK).transpose(0, 2, 1, 3)

    row = lambda v: v[:, None, :]
    wg = jnp.concatenate([lru_wa[:, 0], lru_wx[:, 0], lru_wa[:, 1], lru_wx[:, 1]], axis=-1).astype(BF16)
    ba = per_head(lru_ba)
    bx = per_head(lru_bx)
    bg = jnp.concatenate([ba[:, :, 0], bx[:, :, 0], ba[:, :, 1], bx[:, :, 1]], axis=-1)[:, :, None, :]
    lam = per_head(lru_lambda).reshape(DEPTH, LRU_HEADS, 1, 2 * LRU_BLOCK)
    return {
        "norm1": row(norm1), "norm2": row(norm2), "b_gate": row(b_gate),
        "lru": {"conv_w": lru_conv, "conv_b": row(lru_conv_b), "wg": wg, "bg": bg, "lam": lam},
        "sink": attn_sink, "pool_w": pool_w.astype(BF16), "pool_scale": row(pool_scale),
        "ffn_conv": ffn_conv, "ffn_conv_b": row(ffn_conv_b),
    }


def kernel(x_prompt, x_sample, cache_k, cache_v, state_lru, c, c_ctx, w_ada, b_ada, norm1, norm2, w_in,
           b_gate, lru_conv, lru_conv_b, lru_wa, lru_ba, lru_wx, lru_bx, lru_lambda, attn_sink, pool_w,
           pool_scale, w_branch, w_out, ffn_up, ffn_conv, ffn_conv_b, ffn_down, final_norm):
    batch, seq, _ = x_prompt.shape
    dec_batch, dec_seq, _ = x_sample.shape
    past = cache_k.shape[2]
    assert seq == SCAN_CHUNK and dec_seq % SCAN_ROWS == 0 and (batch * seq) % SCAN_ROWS == 0

    c_rows = jnp.concatenate(
        [c_ctx[None], c, jnp.zeros((SUBLANES - 1 - dec_batch, D_MODEL), F32)], axis=0)
    mods = _ada(c_rows, w_ada, b_ada)

    xp = x_prompt.reshape(batch * seq, D_MODEL)
    xs = x_sample.reshape(dec_batch * dec_seq, D_MODEL)
    fn = final_norm[None]
    p = _stacked_params(norm1, norm2, b_gate, lru_conv, lru_conv_b, lru_wa, lru_ba, lru_wx, lru_bx,
                        lru_lambda, attn_sink, pool_w, pool_scale, ffn_conv, ffn_conv_b)
    raw = {"w_in": w_in, "w_branch": w_branch, "w_out": w_out, "ffn_up": ffn_up, "ffn_down": ffn_down}
    w_in_l = w_in[0].astype(BF16)
    mod_rows = mods.reshape(DEPTH * SUBLANES, 1, 6 * D_MODEL)
    ck = cache_k.reshape(dec_batch, DEPTH, past, D_KV).astype(BF16)
    cv = cache_v.reshape(dec_batch, DEPTH, past, D_KV).astype(BF16)
    h0 = state_lru.transpose(1, 2, 0, 3)[:, :, :, None, :]
    caches, hs = None, []
    for l in range(DEPTH):
        last = fn if l == DEPTH - 1 else None
        wts = {"w_in": w_in_l}
        xp, k_all, v_all, h_fin, w_in_l = _trunk_layer(xp, mod_rows, l * SUBLANES, batch * seq, p, wts, l, seq,
                                                       None, last, caches, raw)
        caches = (k_all, v_all)
        xs = _trunk_layer(xs, mod_rows, l * SUBLANES + 1, dec_seq, p, wts, l, dec_seq, (ck, cv, h0), last)[0]
        hs.append(h_fin.transpose(1, 0, 2))
    y_prompt = xp.reshape(batch, seq, D_MODEL)
    y_sample = xs.reshape(dec_batch, dec_seq, D_MODEL)
    cache_dims = (batch, DEPTH, seq, N_KV_HEADS, HEAD_DIM)
    return (y_prompt, y_sample, k_all.reshape(cache_dims), v_all.reshape(cache_dims), jnp.stack(hs, axis=1))
```

```python
import functools

import numpy as np
import jax
import jax.numpy as jnp
from jax import lax
from jax.experimental import pallas as pl
from jax.experimental.pallas import tpu as pltpu

F32 = jnp.float32
BF16 = jnp.bfloat16

D_MODEL = 1024
DEPTH = 2
GRID_W = 64
EPS = 1e-6
N_BRANCH = 3
D_RNN = 1024
LRU_HEADS = 8
LRU_BLOCK = D_RNN // LRU_HEADS
LRU_C = 8.0
N_HEADS = 8
N_KV_HEADS = 2
KV_GROUPS = N_HEADS // N_KV_HEADS
HEAD_DIM = 128
D_KV = N_KV_HEADS * HEAD_DIM
WINDOW = 128
BLOCK_Q = 128
ROPE_BASE = 10000.0
NEG_INF = -1e30
D_POOL = 1024
POOL_WINDOWS = (2, 4, 8, 16)
POOL_GROUP = D_POOL // len(POOL_WINDOWS)
D_FF = 2816
D_IN = D_RNN + N_HEADS * HEAD_DIM + 2 * D_KV + D_POOL + N_BRANCH * D_MODEL

COL_XA = 0
COL_Q = COL_XA + D_RNN
COL_K = COL_Q + N_HEADS * HEAD_DIM
COL_V = COL_K + D_KV
COL_XC = COL_V + D_KV
COL_G = COL_XC + D_POOL
LOG2E = float(np.log2(np.e))

VMEM_LIMIT_BYTES = 52 * 1024 * 1024
SUBLANES = 8
LANES = 128
BF16_ROWS = 16

SCAN_CHUNK = 256
SCAN_PITCH = 260
SCAN_ROWS = 2048


def _params(*sem):
    return pltpu.CompilerParams(dimension_semantics=sem, vmem_limit_bytes=VMEM_LIMIT_BYTES)


def _dot(a, b):
    return jnp.dot(a, b, preferred_element_type=F32)


def _dot_nt(a, b):
    return lax.dot_general(a, b, (((1,), (1,)), ((), ())), preferred_element_type=F32)


def _sigmoid(z):
    return 0.5 * (1.0 + jnp.tanh(0.5 * z))


def _rms(x):
    return x * lax.rsqrt(jnp.mean(x * x, axis=-1, keepdims=True) + EPS)


def _ada_kernel(c_ref, w_ref, b_ref, o_ref):
    c = c_ref[...]
    s = c * _sigmoid(c)
    o_ref[0] = _dot(s.astype(BF16), w_ref[0].astype(BF16)) + b_ref[0]


def _ada(c_rows, w_ada, b_ada):
    tn = 1536
    return pl.pallas_call(
        _ada_kernel,
        grid=(DEPTH, 6 * D_MODEL // tn),
        in_specs=[
            pl.BlockSpec((SUBLANES, D_MODEL), lambda l, j: (0, 0)),
            pl.BlockSpec((1, D_MODEL, tn), lambda l, j: (l, 0, j)),
            pl.BlockSpec((1, 1, tn), lambda l, j: (l, 0, j)),
        ],
        out_specs=pl.BlockSpec((1, SUBLANES, tn), lambda l, j: (l, 0, j)),
        out_shape=jax.ShapeDtypeStruct((DEPTH, SUBLANES, 6 * D_MODEL), F32),
        compiler_params=_params("parallel", "parallel"),
        name="ada_mod",
    )(c_rows, w_ada, b_ada.reshape(DEPTH, 1, 6 * D_MODEL))


INPROJ_TM = 512
INPROJ_CK = 2 * D_KV


def _resident(shape, layer):
    ndim = len(shape)
    return pl.BlockSpec((None,) + tuple(shape), lambda *_: (layer,) + (0,) * ndim,
                        pipeline_mode=pl.Buffered(1))


def _whole(shape):
    return pl.BlockSpec(tuple(shape), lambda *_: (0,) * len(shape), pipeline_mode=pl.Buffered(1))


def _side_cast_specs(side, steps):
    in_specs = [pl.BlockSpec((rows, arr.shape[1]), lambda i, first=first: (first + i, 0))
                for arr, rows, first in side]
    out_specs = [pl.BlockSpec((rows, arr.shape[1]), lambda i: (i, 0)) for arr, rows, _ in side]
    out_shapes = [jax.ShapeDtypeStruct((rows * steps, arr.shape[1]), BF16) for arr, rows, _ in side]
    return in_specs, out_specs, out_shapes


def _run_side_casts(refs, n_in, n_out, n_side):
    refs = list(refs)
    side_in = refs[n_in:n_in + n_side]
    side_out = refs[n_in + n_side + n_out:n_in + 2 * n_side + n_out]
    for src, dst in zip(side_in, side_out):
        dst[...] = src[...].astype(BF16)
    return refs[:n_in] + refs[n_in + n_side:n_in + n_side + n_out] + refs[n_in + 2 * n_side + n_out:]


def _layer_slabs(stacked, layer, steps):
    cols = stacked.shape[-1]
    rows = int(np.prod(stacked.shape[1:-1]))
    assert rows % (steps * BF16_ROWS) == 0
    return (stacked.reshape(stacked.shape[0] * rows, cols), rows // steps, layer * steps)


def _mod_spec(row0, per):
    return pl.BlockSpec((1, 1, 6 * D_MODEL), lambda i: (row0 + i // per, 0, 0))


def _inproj_kernel(*refs, seq_len, want_kv, layer, owns_cache, n_in, n_side):
    refs = _run_side_casts(refs, n_in, 5 if want_kv else 1, n_side)
    _inproj_body(*refs, seq_len=seq_len, want_kv=want_kv, layer=layer, owns_cache=owns_cache)


def _inproj_body(*refs, seq_len, want_kv, layer, owns_cache):
    if want_kv:
        x_ref, mod_ref, g_ref, w_ref, sink_ref, band_ref, pw_ref, ps_ref = refs[:8]
        o_ref, k_ref, v_ref, y_ref, yc_ref = refs[-5:]
        q_chunks = []
        if owns_cache:
            for other in range(DEPTH):
                if other != layer:
                    k_ref[:, other] = jnp.zeros(k_ref.shape[:1] + k_ref.shape[2:], F32)
                    v_ref[:, other] = jnp.zeros(v_ref.shape[:1] + v_ref.shape[2:], F32)
    else:
        x_ref, mod_ref, g_ref, w_ref, cos_ref, sa_ref, sb_ref, o_ref = refs
    mod = mod_ref[0]
    shift = mod[:, 0:D_MODEL]
    scale = mod[:, D_MODEL:2 * D_MODEL]
    h = (_rms(x_ref[...]) * g_ref[...] * (1.0 + scale) + shift).astype(BF16)
    nf = HEAD_DIM // 4

    def rope(x):
        return (x * cos_ref[...] + pltpu.roll(x, HEAD_DIM - nf, 1) * sa_ref[...]
                + pltpu.roll(x, nf, 1) * sb_ref[...])

    for c in range(D_IN // INPROJ_CK):
        acc = _dot(h, w_ref[:, c * INPROJ_CK:(c + 1) * INPROJ_CK])
        if not want_kv:
            lo = c * INPROJ_CK
            heads = [acc[:, j * HEAD_DIM:(j + 1) * HEAD_DIM] for j in range(INPROJ_CK // HEAD_DIM)]
            heads = [rope(hd) if COL_Q <= lo + j * HEAD_DIM < COL_V else hd for j, hd in enumerate(heads)]
            heads = [hd * SOFTMAX_SCALE if COL_Q <= lo + j * HEAD_DIM < COL_K else hd
                     for j, hd in enumerate(heads)]
            acc = jnp.concatenate(heads, axis=1)
        o_ref[:, c * INPROJ_CK:(c + 1) * INPROJ_CK] = acc.astype(o_ref.dtype)
        if want_kv and COL_Q <= c * INPROJ_CK < COL_K:
            q_chunks.append((acc * SOFTMAX_SCALE).astype(BF16))
        if want_kv and COL_XC <= c * INPROJ_CK < COL_G:
            xc = acc.astype(BF16)
            for j in range(INPROJ_CK // POOL_GROUP):
                gi = (c * INPROJ_CK - COL_XC) // POOL_GROUP + j
                gc = slice(gi * POOL_GROUP, (gi + 1) * POOL_GROUP)
                edges = _pool_edge_weights(POOL_WINDOWS[gi])
                for b in range(INPROJ_TM // seq_len):
                    rows = slice(b * seq_len, (b + 1) * seq_len)
                    x = xc[rows, j * POOL_GROUP:(j + 1) * POOL_GROUP]
                    yc = _pool_tile(_dot(band_ref[gi], x), x, POOL_WINDOWS[gi], edges, True, True,
                                    pw_ref[gi], ps_ref[:, gc])
                    yc_ref[rows, gc] = yc.astype(yc_ref.dtype)
        if want_kv and c == COL_K // INPROJ_CK:
            for b in range(INPROJ_TM // seq_len):
                rows = slice(b * seq_len, (b + 1) * seq_len)
                for hd in range(N_KV_HEADS):
                    dst = pl.ds(hd, seq_len, stride=N_KV_HEADS)
                    at = (b, layer, dst, slice(None)) if owns_cache else (b, dst, slice(None))
                    k_ref[at] = acc[rows, hd * HEAD_DIM:(hd + 1) * HEAD_DIM]
                    v_ref[at] = acc[rows, D_KV + hd * HEAD_DIM:D_KV + (hd + 1) * HEAD_DIM]
            kv = acc.astype(BF16)

            def scores(b, kh):
                rows = slice(b * seq_len, (b + 1) * seq_len)
                q = jnp.concatenate([q_chunks[kh][rows, g * HEAD_DIM:(g + 1) * HEAD_DIM]
                                     for g in range(KV_GROUPS)], axis=0)
                return (b, kh, _dot_nt(kv[rows, kh * HEAD_DIM:(kh + 1) * HEAD_DIM], q),
                        kv[rows, D_KV + kh * HEAD_DIM:D_KV + (kh + 1) * HEAD_DIM])

            order = [(b, kh) for b in range(INPROJ_TM // seq_len) for kh in range(N_KV_HEADS)]
            pending = [scores(*ch) for ch in order[:ATTN_LOOKAHEAD]]
            for n in range(len(order)):
                if n + ATTN_LOOKAHEAD < len(order):
                    pending.append(scores(*order[n + ATTN_LOOKAHEAD]))
                b, kh, t, vals = pending.pop(0)
                o_t = _softmax_pv_t(t, _sink_row(sink_ref, layer, kh, seq_len), vals)
                _store_heads_t(y_ref, o_t, kh, b * seq_len, seq_len)


def _inproj(x, mod, mod_row0, rows_per_mod, g, w, layer, seq_len, want_kv, sink=None, pool=None, caches=None,
            side=()):
    m = x.shape[0]
    tm = INPROJ_TM
    per = rows_per_mod // tm
    proj_spec = pl.BlockSpec((tm, D_IN), lambda i: (i, 0))
    proj_shape = jax.ShapeDtypeStruct((m, D_IN), BF16)
    aliases = {}
    owns_cache = want_kv and caches is None
    if want_kv:
        nb = tm // seq_len
        if owns_cache:
            cache_spec = pl.BlockSpec((nb, DEPTH, seq_len * N_KV_HEADS, HEAD_DIM), lambda i: (i, 0, 0, 0))
        else:
            cache_spec = pl.BlockSpec((nb, None, seq_len * N_KV_HEADS, HEAD_DIM), lambda i: (i, layer, 0, 0))
        cache_shape = jax.ShapeDtypeStruct((m // seq_len, DEPTH, seq_len * N_KV_HEADS, HEAD_DIM), F32)
        assert tm % seq_len == 0 and INPROJ_CK == KV_GROUPS * HEAD_DIM and COL_Q % INPROJ_CK == 0
        assert seq_len == POOL_TILE and COL_XC % INPROJ_CK == 0 and INPROJ_CK % POOL_GROUP == 0
        mix_spec = pl.BlockSpec((tm, D_MODEL), lambda i: (i, 0))
        mix_shape = jax.ShapeDtypeStruct((m, D_MODEL), BF16)
        out_specs = (proj_spec, cache_spec, cache_spec, mix_spec, mix_spec)
        out_shape = (proj_shape, cache_shape, cache_shape, mix_shape, mix_shape)
        nwin = len(POOL_WINDOWS)
        extra_specs = [pl.BlockSpec(memory_space=pltpu.SMEM), _whole((nwin, POOL_TILE, POOL_TILE)),
                       _resident((nwin, POOL_GROUP, POOL_GROUP), layer), _resident((1, D_POOL), layer)]
        extra_args = [sink, _pool_bands(seq_len), pool[0], pool[1]]
        if caches is not None:
            aliases = {4 + len(extra_args): 1, 5 + len(extra_args): 2}
            extra_specs += [pl.BlockSpec(memory_space=pl.ANY)] * 2
            extra_args += list(caches)
    else:
        out_specs, out_shape = (proj_spec,), (proj_shape,)
        tab = pl.BlockSpec((tm, HEAD_DIM), lambda i: (i % (seq_len // tm), 0))
        extra_specs, extra_args = [tab, tab, tab], list(_rope_tables(seq_len))
    side_in, side_out, side_shapes = _side_cast_specs(side, m // tm)
    out = pl.pallas_call(
        functools.partial(_inproj_kernel, seq_len=seq_len, want_kv=want_kv, layer=layer, owns_cache=owns_cache,
                          n_in=4 + len(extra_specs), n_side=len(side)),
        grid=(m // tm,),
        in_specs=[
            pl.BlockSpec((tm, D_MODEL), lambda i: (i, 0)),
            _mod_spec(mod_row0, per),
            _resident((1, D_MODEL), layer),
            _whole((D_MODEL, D_IN)),
        ] + extra_specs + side_in,
        out_specs=tuple(out_specs) + tuple(side_out),
        out_shape=tuple(out_shape) + tuple(side_shapes),
        input_output_aliases=aliases,
        compiler_params=_params("parallel"),
        name="in_proj",
    )(x, mod, g, w, *extra_args, *[item[0] for item in side])
    n_own = len(out_specs)
    own = tuple(out[:n_own]) if want_kv else (out[0], None, None, None, None)
    return own + (list(out[n_own:]),)


LRU_HEADS_PER_STEP = 2


def _chunk_neighbour(v, towards_later):
    sub = lax.broadcasted_iota(jnp.int32, v.shape, 0)
    if towards_later:
        return jnp.where(sub >= 1, pltpu.roll(v, 1, 0), 0.0)
    return jnp.where(sub <= SUBLANES - 2, pltpu.roll(v, SUBLANES - 1, 0), 0.0)


def _shift_time(x_tm, k, chained):
    n = x_tm.shape[0]
    steps = abs(k)
    edge = []
    for s in range(steps):
        if not chained:
            edge.append(jnp.zeros((SUBLANES, x_tm.shape[1]), F32))
        elif k > 0:
            src = n - (steps - s) * SUBLANES
            edge.append(_chunk_neighbour(x_tm[src:src + SUBLANES], True))
        else:
            edge.append(_chunk_neighbour(x_tm[s * SUBLANES:(s + 1) * SUBLANES], False))
    if k > 0:
        return jnp.concatenate(edge + [x_tm[:n - steps * SUBLANES]], axis=0)
    return jnp.concatenate([x_tm[steps * SUBLANES:]] + edge, axis=0)


def _lru_tm_kernel(*refs, seq_len, latent):
    if latent:
        (x_ref, cw_ref, cb_ref, wg_ref, bg_ref, lam_ref, h0_ref, y_ref,
         io_ref, af_ref, bf_ref, ab_ref, bb_ref, hf_ref, hb_ref) = refs
    else:
        (x_ref, cw_ref, cb_ref, wg_ref, bg_ref, lam_ref, y_ref, fin_ref,
         io_ref, af_ref, bf_ref, ab_ref, bb_ref, hf_ref, hb_ref) = refs
    rows = SCAN_ROWS
    nchunk = rows // SCAN_CHUNK
    nhead = LRU_HEADS_PER_STEP
    chained = seq_len > SCAN_CHUNK
    for hd in range(nhead):
        lanes = slice(hd * LRU_BLOCK, (hd + 1) * LRU_BLOCK)
        x = x_ref[:, lanes].astype(F32)
        for c in range(nchunk):
            io_ref[hd, pl.ds(c * SCAN_PITCH, SCAN_CHUNK), :] = x[c * SCAN_CHUNK:(c + 1) * SCAN_CHUNK]
        x = jnp.concatenate(
            [io_ref[hd, pl.ds(t, nchunk, stride=SCAN_PITCH), :] for t in range(SCAN_CHUNK)], axis=0)
        cw = 0.5 * cw_ref[:, lanes]
        xh = (_shift_time(x, 2, chained) * cw[0:1] + _shift_time(x, 1, chained) * cw[1:2]
              + x * cw[2:3] + _shift_time(x, -1, chained) * cw[3:4] + 0.5 * cb_ref[:, lanes])
        half_bias = 0.5 * bg_ref[hd]
        bias_hi = half_bias.astype(BF16).astype(F32)
        bias_rows = jnp.concatenate(
            [bias_hi, half_bias - bias_hi, jnp.zeros((BF16_ROWS - 2, 4 * LRU_BLOCK), F32)], axis=0)
        lhs = jnp.concatenate([xh.astype(BF16), jnp.ones((rows, BF16_ROWS), BF16)], axis=1)
        rhs = jnp.concatenate([wg_ref[hd], bias_rows.astype(BF16)], axis=0)
        th = jnp.tanh(_dot(lhs, rhs))
        lam = lam_ref[hd]
        for d, (a_ref, b_ref) in enumerate(((af_ref, bf_ref), (ab_ref, bb_ref))):
            th_r = th[:, 2 * d * LRU_BLOCK:(2 * d + 1) * LRU_BLOCK]
            th_i = th[:, (2 * d + 1) * LRU_BLOCK:(2 * d + 2) * LRU_BLOCK]
            nl = -lam[:, d * LRU_BLOCK:(d + 1) * LRU_BLOCK]
            softplus = jnp.maximum(nl, 0.0) + jnp.log(1.0 + jnp.exp(-jnp.abs(nl)))
            ch = (-0.5 * LRU_C * LOG2E) * softplus
            a = jnp.exp2(ch + ch * th_r)
            y = 1.0 - a * a
            a_ref[hd] = a
            b_ref[hd] = (y * lax.rsqrt(jnp.maximum(y, 1e-30))) * ((1.0 + th_i) * xh)

    def step(t, carry):
        rf = pl.ds(pl.multiple_of(t * nchunk, nchunk), nchunk)
        rb = pl.ds(pl.multiple_of((SCAN_CHUNK - 1 - t) * nchunk, nchunk), nchunk)
        out = []
        for hd in range(nhead):
            hf, hb, pf, pb = carry[4 * hd:4 * hd + 4]
            a_f = af_ref[hd, rf, :]
            a_b = ab_ref[hd, rb, :]
            hf = a_f * hf + bf_ref[hd, rf, :]
            hb = a_b * hb + bb_ref[hd, rb, :]
            hf_ref[hd, rf, :] = hf
            hb_ref[hd, rb, :] = hb
            if chained:
                pf = pf * a_f
                pb = pb * a_b
                af_ref[hd, rf, :] = pf
                ab_ref[hd, rb, :] = pb
            out += [hf, hb, pf, pb]
        return tuple(out)

    zero = jnp.zeros((nchunk, LRU_BLOCK), F32)
    one = jnp.ones((nchunk, LRU_BLOCK), F32)
    lax.fori_loop(0, SCAN_CHUNK, step, (zero, zero, one, one) * nhead, unroll=8)

    sub = lax.broadcasted_iota(jnp.int32, (nchunk, LRU_BLOCK), 0)
    last = slice(rows - nchunk, rows)
    first = slice(0, nchunk)
    for hd in range(nhead):
        lanes = slice(hd * LRU_BLOCK, (hd + 1) * LRU_BLOCK)
        hf = hf_ref[hd]
        hb = hb_ref[hd]
        if chained:
            pf = af_ref[hd]
            pb = ab_ref[hd]
            ef = jnp.where(sub == 0, h0_ref[0, 0][:, lanes], 0.0)
            eb = jnp.where(sub == nchunk - 1, h0_ref[1, 0][:, lanes], 0.0)
            for c in range(1, nchunk):
                ef = jnp.where(sub == c, pltpu.roll(hf[last] + pf[last] * ef, 1, 0), ef)
                eb = jnp.where(sub == nchunk - 1 - c,
                               pltpu.roll(hb[first] + pb[first] * eb, nchunk - 1, 0), eb)
            hf = (hf.reshape(SCAN_CHUNK, nchunk, LRU_BLOCK)
                  + pf.reshape(SCAN_CHUNK, nchunk, LRU_BLOCK) * ef[None]).reshape(rows, LRU_BLOCK)
            hb = (hb.reshape(SCAN_CHUNK, nchunk, LRU_BLOCK)
                  + pb.reshape(SCAN_CHUNK, nchunk, LRU_BLOCK) * eb[None]).reshape(rows, LRU_BLOCK)
        else:
            fin_ref[0, :, lanes] = hf[last]
            fin_ref[1, :, lanes] = hb[first]
        y = hf + hb
        for t in range(SCAN_CHUNK):
            io_ref[hd, pl.ds(t, nchunk, stride=SCAN_PITCH), :] = y[t * nchunk:(t + 1) * nchunk]
        for c in range(nchunk):
            y_ref[pl.ds(c * SCAN_CHUNK, SCAN_CHUNK), lanes] = (
                io_ref[hd, pl.ds(c * SCAN_PITCH, SCAN_CHUNK), :].astype(y_ref.dtype))


def _lru(proj, lp, layer, seq_len, h0):
    m = proj.shape[0]
    latent = h0 is not None
    rows = SCAN_ROWS
    nhead = LRU_HEADS_PER_STEP
    width = nhead * LRU_BLOCK
    in_specs = [
        pl.BlockSpec((rows, width), lambda r, h: (r, COL_XA // width + h)),
        pl.BlockSpec((None, 4, width), lambda r, h: (layer, 0, h)),
        pl.BlockSpec((None, 1, width), lambda r, h: (layer, 0, h)),
        pl.BlockSpec((None, nhead, LRU_BLOCK, 4 * LRU_BLOCK), lambda r, h: (layer, h, 0, 0)),
        pl.BlockSpec((None, nhead, 1, 4 * LRU_BLOCK), lambda r, h: (layer, h, 0, 0)),
        pl.BlockSpec((None, nhead, 1, 2 * LRU_BLOCK), lambda r, h: (layer, h, 0, 0)),
    ]
    args = [proj, lp["conv_w"], lp["conv_b"], lp["wg"], lp["bg"], lp["lam"]]
    y_spec = pl.BlockSpec((rows, width), lambda r, h: (r, h))
    y_shape = jax.ShapeDtypeStruct((m, D_RNN), BF16)
    if latent:
        in_specs.append(pl.BlockSpec((None, 2, 1, 1, width), lambda r, h: (layer, 0, r, 0, h)))
        args.append(h0)
        out_specs, out_shape = y_spec, y_shape
    else:
        nseq = m // seq_len
        out_specs = (y_spec, pl.BlockSpec((2, rows // seq_len, width), lambda r, h: (0, r, h)))
        out_shape = (y_shape, jax.ShapeDtypeStruct((2, nseq, D_RNN), F32))
    strided_buf = pltpu.VMEM((nhead, rows // SCAN_CHUNK * SCAN_PITCH, LRU_BLOCK), F32)
    scan_buf = pltpu.VMEM((nhead, rows, LRU_BLOCK), F32)
    return pl.pallas_call(
        functools.partial(_lru_tm_kernel, seq_len=seq_len, latent=latent),
        grid=(m // rows, LRU_HEADS // nhead),
        in_specs=in_specs,
        out_specs=out_specs,
        out_shape=out_shape,
        scratch_shapes=[strided_buf] + [scan_buf] * 6,
        compiler_params=_params("parallel", "parallel"),
        name="rglru_latent" if latent else "rglru_context",
    )(*args)


SOFTMAX_SCALE = HEAD_DIM ** -0.5 * LOG2E


def _stack_heads(q_ref, kh, r0, rows):
    parts = [q_ref[r0:r0 + rows, (kh * KV_GROUPS + g) * HEAD_DIM:(kh * KV_GROUPS + g + 1) * HEAD_DIM]
             for g in range(KV_GROUPS)]
    return jnp.concatenate(parts, axis=0)


def _dot_tn(a, b):
    return lax.dot_general(a, b, (((0,), (0,)), ((), ())), preferred_element_type=F32)


def _sink_row(sink_ref, layer, kh, cols):
    parts = [jnp.full((1, cols), sink_ref[layer, kh * KV_GROUPS + g] * LOG2E, F32) for g in range(KV_GROUPS)]
    return jnp.concatenate(parts, axis=1)


def _softmax_pv_t(t, sink, v):
    m = jnp.maximum(jnp.max(t, axis=0, keepdims=True), sink)
    e = jnp.exp2(t - m).astype(BF16)
    d = v.shape[1]
    v_ones = jnp.concatenate([v, jnp.ones((v.shape[0], SUBLANES), BF16)], axis=1)
    o_sum = _dot_tn(v_ones, e)
    denom = o_sum[d:d + 1] + jnp.exp2(sink - m)
    return o_sum[:d] * (1.0 / denom)


def _store_heads_t(y_ref, o_t, kh, r0, rows):
    for g in range(KV_GROUPS):
        h = kh * KV_GROUPS + g
        y_ref[r0:r0 + rows, h * HEAD_DIM:(h + 1) * HEAD_DIM] = (
            o_t[:, g * rows:(g + 1) * rows].T.astype(y_ref.dtype))


def _rope_tables(seq_len):
    nf = HEAD_DIM // 4
    freqs = ROPE_BASE ** (-np.arange(nf, dtype=np.float64) / nf)
    t = np.arange(seq_len)
    ang_row = (t // GRID_W)[:, None] * freqs[None, :]
    ang_col = (t % GRID_W)[:, None] * freqs[None, :]
    ang = np.concatenate([ang_row, ang_row, ang_col, ang_col], axis=1)
    first = (np.arange(HEAD_DIM) % (2 * nf)) < nf
    cos = np.cos(ang)
    sin = np.sin(ang)
    sin_a = np.where(first[None, :], -sin, 0.0)
    sin_b = np.where(first[None, :], 0.0, sin)
    return tuple(jnp.asarray(a, F32) for a in (cos, sin_a, sin_b))


ATTN_LAT_BLOCKS = 4
ATTN_LOOKAHEAD = 3


def _attn_lat_kernel(sink_ref, q_ref, kvp_ref, kvc_ref, kvn_ref, ck_ref, cv_ref, y_ref, *, nblk, layer):
    step = pl.program_id(1)
    cols = KV_GROUPS * BLOCK_Q
    span = 3 * BLOCK_Q
    key = lax.broadcasted_iota(jnp.int32, (span, cols), 0)
    qry = lax.broadcasted_iota(jnp.int32, (span, cols), 1) & (BLOCK_Q - 1)
    kv = jnp.concatenate([kvp_ref[...], kvc_ref[...], kvn_ref[...]], axis=0)
    def scores(i, kh):
        win = kv[i * BLOCK_Q:i * BLOCK_Q + span]
        sl = slice(kh * HEAD_DIM, (kh + 1) * HEAD_DIM)
        vl = slice(D_KV + kh * HEAD_DIM, D_KV + (kh + 1) * HEAD_DIM)
        q = _stack_heads(q_ref, kh, i * BLOCK_Q, BLOCK_Q)
        keys = jnp.concatenate([win[:, sl], ck_ref[0, :, sl]], axis=0)
        vals = jnp.concatenate([win[:, vl], cv_ref[0, :, sl]], axis=0)
        return i, kh, _dot_nt(keys, q), vals

    order = [(i, kh) for i in range(ATTN_LAT_BLOCKS) for kh in range(N_KV_HEADS)]
    pending = [scores(*c) for c in order[:ATTN_LOOKAHEAD]]
    for n in range(len(order)):
        if n + ATTN_LOOKAHEAD < len(order):
            pending.append(scores(*order[n + ATTN_LOOKAHEAD]))
        i, kh, t, vals = pending.pop(0)
        j = step * ATTN_LAT_BLOCKS + i
        lo = jnp.where(j > 0, qry, BLOCK_Q)
        hi = jnp.where(j < nblk - 1, qry + 2 * BLOCK_Q, 2 * BLOCK_Q - 1)
        bias = jnp.where(jnp.logical_and(key >= lo, key <= hi), 0.0, NEG_INF)
        t = jnp.concatenate([t[:span] + bias, t[span:]], axis=0)
        o_t = _softmax_pv_t(t, _sink_row(sink_ref, layer, kh, BLOCK_Q), vals)
        _store_heads_t(y_ref, o_t, kh, i * BLOCK_Q, BLOCK_Q)


def _attn_lat(proj, ck, cv, sink, layer, seq_len):
    m = proj.shape[0]
    nblk = seq_len // BLOCK_Q
    nstep = nblk // ATTN_LAT_BLOCKS
    rows = ATTN_LAT_BLOCKS * BLOCK_Q
    nb = m // seq_len
    past = ck.shape[2]
    kv_col = COL_K // (2 * D_KV)
    assert COL_V == COL_K + D_KV and COL_K % (2 * D_KV) == 0

    def halo(shift):
        def index(b, s):
            return (b * nblk + jnp.clip(s * ATTN_LAT_BLOCKS + shift, 0, nblk - 1), kv_col)
        return index

    return pl.pallas_call(
        functools.partial(_attn_lat_kernel, nblk=nblk, layer=layer),
        grid=(nb, nstep),
        in_specs=[
            pl.BlockSpec(memory_space=pltpu.SMEM),
            pl.BlockSpec((rows, N_HEADS * HEAD_DIM), lambda b, s: (b * nstep + s, COL_Q // (N_HEADS * HEAD_DIM))),
            pl.BlockSpec((BLOCK_Q, 2 * D_KV), halo(-1)),
            pl.BlockSpec((rows, 2 * D_KV), lambda b, s: (b * nstep + s, kv_col)),
            pl.BlockSpec((BLOCK_Q, 2 * D_KV), halo(ATTN_LAT_BLOCKS)),
            pl.BlockSpec((1, None, past, D_KV), lambda b, s: (b, layer, 0, 0)),
            pl.BlockSpec((1, None, past, D_KV), lambda b, s: (b, layer, 0, 0)),
        ],
        out_specs=pl.BlockSpec((rows, N_HEADS * HEAD_DIM), lambda b, s: (b * nstep + s, 0)),
        out_shape=jax.ShapeDtypeStruct((m, N_HEADS * HEAD_DIM), BF16),
        compiler_params=_params("parallel", "parallel"),
        name="attn_latent",
    )(sink, proj, proj, proj, proj, ck, cv)


POOL_TILE = 256
POOL_LEAD = BF16_ROWS
POOL_ROWS = 2048


def _pool_plan(seq_len):
    lead = 0 if seq_len == POOL_TILE else POOL_LEAD
    return lead, POOL_TILE - 2 * lead


def _pool_bands(seq_len):
    lead, nout = _pool_plan(seq_len)
    r = np.arange(nout)[:, None]
    c = np.arange(POOL_TILE)[None, :] - lead
    bands = [(c >= r - win // 2) & (c < r + win // 2) for win in POOL_WINDOWS]
    return jnp.asarray(np.stack(bands), BF16)


def _pool_edge_weights(win):
    half = win // 2
    edge = lax.broadcasted_iota(jnp.int32, (SUBLANES, POOL_GROUP), 0)
    inv_head = 1.0 / ((edge + half) - jnp.maximum(edge - half, 0)).astype(F32)
    inv_tail = 1.0 / (jnp.minimum(SUBLANES - edge, half) + half).astype(F32)
    return inv_head, inv_tail


def _pool_tile(sums, x, win, edges, at_start, at_end, w, scale):
    n = sums.shape[0]
    head = sums[:SUBLANES] * (edges[0] if at_start else 1.0 / win)
    tail = sums[n - SUBLANES:] * (edges[1] if at_end else 1.0 / win)
    mean = jnp.concatenate([head, sums[SUBLANES:n - SUBLANES] * (1.0 / win), tail], axis=0)
    pooled = (mean - x.astype(F32)).astype(BF16)
    return _dot(pooled, w) * scale


def _pool_kernel(x0_ref, x1_ref, x2_ref, x3_ref, band_ref, w_ref, s_ref, y_ref, pad_ref, *, seq_len):
    lead, nout = _pool_plan(seq_len)
    if lead:
        pad_ref[0:lead, :] = jnp.zeros((lead, POOL_GROUP), BF16)
        pad_ref[lead + seq_len:, :] = jnp.zeros((pad_ref.shape[0] - lead - seq_len, POOL_GROUP), BF16)
    for gi, (win, x_ref) in enumerate(zip(POOL_WINDOWS, (x0_ref, x1_ref, x2_ref, x3_ref))):
        cs = slice(gi * POOL_GROUP, (gi + 1) * POOL_GROUP)
        edges = _pool_edge_weights(win)
        if lead:
            pad_ref[lead:lead + seq_len, :] = x_ref[...]
        for base in range(0, POOL_ROWS, seq_len):
            for p0 in range(0, seq_len, nout):
                n = min(nout, seq_len - p0)
                x = x_ref[base + p0:base + p0 + n, :]
                src = pad_ref[p0:p0 + POOL_TILE, :] if lead else x
                sums = _dot(band_ref[gi, :n, :], src)
                y = _pool_tile(sums, x, win, edges, p0 == 0, p0 + n == seq_len, w_ref[gi], s_ref[:, cs])
                y_ref[base + p0:base + p0 + n, cs] = y.astype(y_ref.dtype)


def _pool(proj, w, s, layer, seq_len):
    m = proj.shape[0]
    rows = POOL_ROWS
    lead, nout = _pool_plan(seq_len)
    assert seq_len in (POOL_TILE, rows) and max(POOL_WINDOWS) // 2 <= min(SUBLANES, lead or SUBLANES)
    pad_rows = (pl.cdiv(seq_len, nout) - 1) * nout + POOL_TILE
    group = lambda gi: pl.BlockSpec((rows, POOL_GROUP), lambda r: (r, COL_XC // POOL_GROUP + gi))
    whole = lambda shape: pl.BlockSpec(shape, lambda r: (0,) * len(shape))
    nwin = len(POOL_WINDOWS)
    return pl.pallas_call(
        functools.partial(_pool_kernel, seq_len=seq_len),
        grid=(m // rows,),
        in_specs=[
            group(0), group(1), group(2), group(3),
            whole((nwin, nout, POOL_TILE)),
            _resident((nwin, POOL_GROUP, POOL_GROUP), layer),
            _resident((1, D_POOL), layer),
        ],
        out_specs=pl.BlockSpec((rows, D_POOL), lambda r: (r, 0)),
        out_shape=jax.ShapeDtypeStruct((m, D_POOL), BF16),
        scratch_shapes=[pltpu.VMEM((pad_rows, POOL_GROUP), BF16)],
        compiler_params=_params("parallel"),
        name="pool_mix",
    )(proj, proj, proj, proj, _pool_bands(seq_len), w, s)


MERGE_N_IN = 15
MERGE_TM = 512


def _merge_kernel(*refs, n_side):
    (ya_ref, yb_ref, yc_ref, g0_ref, g1_ref, g2_ref, g3_ref, g4_ref, g5_ref, x_ref, mod_ref,
     bg_ref, wb_ref, wo_ref, n2_ref, x1_ref, h2_ref) = _run_side_casts(refs, MERGE_N_IN, 2, n_side)
    mod = mod_ref[0]
    g_refs = (g0_ref, g1_ref, g2_ref, g3_ref, g4_ref, g5_ref)
    half = D_MODEL // 2
    merged = None
    for k, y_ref in enumerate((ya_ref, yb_ref, yc_ref)):
        y = _dot(y_ref[...], wb_ref[k])
        parts = []
        for p in range(2):
            z = g_refs[2 * k + p][...].astype(F32) + bg_ref[:, k * D_MODEL + p * half:k * D_MODEL + (p + 1) * half]
            parts.append((1.0 + jnp.tanh(0.5 * z)) * y[:, p * half:(p + 1) * half])
        term = jnp.concatenate(parts, axis=1)
        merged = term if merged is None else merged + term
    merged = 0.5 * merged
    gate1 = mod[:, 2 * D_MODEL:3 * D_MODEL]
    x1 = x_ref[...] + gate1 * _dot(merged.astype(BF16), wo_ref[...])
    x1_ref[...] = x1
    shift2 = mod[:, 3 * D_MODEL:4 * D_MODEL]
    scale2 = mod[:, 4 * D_MODEL:5 * D_MODEL]
    h2_ref[...] = (_rms(x1) * n2_ref[...] * (1.0 + scale2) + shift2).astype(BF16)


def _merge(ya, yb, yc, proj, x, mod, mod_row0, rows_per_mod, bg, wb, wo, n2, layer, side=()):
    m = x.shape[0]
    tm = MERGE_TM
    per = rows_per_mod // tm
    half = D_MODEL // 2
    row = pl.BlockSpec((tm, D_MODEL), lambda i: (i, 0))
    gate = lambda c: pl.BlockSpec((tm, half), lambda i: (i, COL_G // half + c))
    side_in, side_out, side_shapes = _side_cast_specs(side, m // tm)
    in_specs = [
        row, row, row,
        gate(0), gate(1), gate(2), gate(3), gate(4), gate(5),
        row,
        _mod_spec(mod_row0, per),
        _resident((1, N_BRANCH * D_MODEL), layer),
        _whole((N_BRANCH, D_MODEL, D_MODEL)),
        _whole((D_MODEL, D_MODEL)),
        _resident((1, D_MODEL), layer),
    ]
    assert len(in_specs) == MERGE_N_IN
    out = pl.pallas_call(
        functools.partial(_merge_kernel, n_side=len(side)),
        grid=(m // tm,),
        in_specs=in_specs + side_in,
        out_specs=(row, row) + tuple(side_out),
        out_shape=(jax.ShapeDtypeStruct((m, D_MODEL), F32), jax.ShapeDtypeStruct((m, D_MODEL), BF16))
        + tuple(side_shapes),
        compiler_params=_params("parallel"),
        name="merge_out",
    )(ya, yb, yc, proj, proj, proj, proj, proj, proj, x, mod, bg, wb, wo, n2, *[item[0] for item in side])
    return out[0], out[1], list(out[2:])


FFN_CK = 256
FFN_TM = 512
FFN_GAP = SUBLANES
GELU_C = float(np.sqrt(2.0 / np.pi))


def _ffn_kernel(*refs, seq_len, final, n_side):
    refs = _run_side_casts(refs, 9 + int(final), 1, n_side)
    h_ref, hp_ref, hn_ref, x_ref, mod_ref, wup_ref, cw_ref, cb_ref, wd_ref = refs[:9]
    fn_ref = refs[9] if final else None
    o_ref, hx_ref, u_ref, act_ref = refs[-4:]
    tm = FFN_TM
    halo = seq_len > tm
    i = pl.program_id(0)

    if halo:
        per_seq = seq_len // tm
        at_start = i % per_seq == 0
        at_end = i % per_seq == per_seq - 1
        zeros = jnp.zeros((BF16_ROWS, D_MODEL), BF16)

        @pl.when(at_start)
        def _():
            hx_ref[0:BF16_ROWS, :] = zeros

        @pl.when(jnp.logical_not(at_start))
        def _():
            hx_ref[0:BF16_ROWS, :] = hp_ref[...]

        @pl.when(at_end)
        def _():
            hx_ref[BF16_ROWS + tm:, :] = zeros

        @pl.when(jnp.logical_not(at_end))
        def _():
            hx_ref[BF16_ROWS + tm:, :] = hn_ref[...]

        hx_ref[BF16_ROWS:BF16_ROWS + tm, :] = h_ref[...]
        bases = (BF16_ROWS,)
        seg = tm
    else:
        nseg = tm // seq_len
        seg = seq_len
        bases = tuple(FFN_GAP + s * (seg + FFN_GAP) for s in range(nseg))
        for s in range(nseg + 1):
            u_ref[s * (seg + FFN_GAP):s * (seg + FFN_GAP) + FFN_GAP, :] = jnp.zeros((FFN_GAP, FFN_CK), F32)

    def taps(offset):
        return jnp.concatenate([u_ref[b + offset:b + offset + seg, :] for b in bases], axis=0)

    for c in range(D_FF // FFN_CK):
        cs = slice(c * FFN_CK, (c + 1) * FFN_CK)
        vs = slice(D_FF + c * FFN_CK, D_FF + (c + 1) * FFN_CK)
        if halo:
            u_ext = _dot(hx_ref[...], wup_ref[:, cs])
            u_ref[...] = u_ext
            u0 = u_ext[BF16_ROWS:BF16_ROWS + tm]
        else:
            u0 = _dot(h_ref[...], wup_ref[:, cs])
            for s, b in enumerate(bases):
                u_ref[b:b + seg, :] = u0[s * seg:(s + 1) * seg]
        uv = _dot(h_ref[...], wup_ref[:, vs])
        gff = taps(-1) * cw_ref[0:1, cs] + u0 * cw_ref[1:2, cs] + taps(1) * cw_ref[2:3, cs] + cb_ref[:, cs]
        inner = gff * (GELU_C + (GELU_C * 0.044715) * (gff * gff))
        act_ref[:, cs] = (0.5 * (gff * uv) * (1.0 + jnp.tanh(inner))).astype(BF16)

    gate2 = mod_ref[0][:, 5 * D_MODEL:6 * D_MODEL]
    out = x_ref[...] + gate2 * _dot(act_ref[...], wd_ref[...])
    if final:
        out = _rms(out) * fn_ref[...]
    o_ref[...] = out


def _ffn(h2, x1, mod, mod_row0, rows_per_mod, wup, cw, cb, wd, layer, seq_len, final_norm, side=()):
    m = x1.shape[0]
    tm = FFN_TM
    per = rows_per_mod // tm
    hb = tm // BF16_ROWS
    last_halo = m // BF16_ROWS - 1
    final = final_norm is not None
    halo = seq_len > tm
    in_specs = [
        pl.BlockSpec((tm, D_MODEL), lambda i: (i, 0)),
        pl.BlockSpec((BF16_ROWS, D_MODEL), lambda i: (jnp.maximum(i * hb - 1, 0), 0)),
        pl.BlockSpec((BF16_ROWS, D_MODEL), lambda i: (jnp.minimum((i + 1) * hb, last_halo), 0)),
        pl.BlockSpec((tm, D_MODEL), lambda i: (i, 0)),
        _mod_spec(mod_row0, per),
        _whole((D_MODEL, 2 * D_FF)),
        _resident((3, D_FF), layer),
        _resident((1, D_FF), layer),
        _whole((D_FF, D_MODEL)),
    ]
    args = [h2, h2, h2, x1, mod, wup, cw, cb, wd]
    if final:
        in_specs.append(_whole((1, D_MODEL)))
        args.append(final_norm)
    if halo:
        u_rows = tm + 2 * BF16_ROWS
    else:
        u_rows = FFN_GAP + (tm // seq_len) * (seq_len + FFN_GAP)
    side_in, side_out, side_shapes = _side_cast_specs(side, m // tm)
    out = pl.pallas_call(
        functools.partial(_ffn_kernel, seq_len=seq_len, final=final, n_side=len(side)),
        grid=(m // tm,),
        in_specs=in_specs + side_in,
        out_specs=(pl.BlockSpec((tm, D_MODEL), lambda i: (i, 0)),) + tuple(side_out),
        out_shape=(jax.ShapeDtypeStruct((m, D_MODEL), F32),) + tuple(side_shapes),
        scratch_shapes=[
            pltpu.VMEM((tm + 2 * BF16_ROWS, D_MODEL), BF16),
            pltpu.VMEM((u_rows, FFN_CK), F32),
            pltpu.VMEM((tm, D_FF), BF16),
        ],
        compiler_params=_params("parallel"),
        name="conv_glu_ffn",
    )(*args, *[item[0] for item in side])
    return out[0], list(out[1:])


def _trunk_layer(x, mod, mod_row0, rows_per_mod, p, wts, layer, seq_len, ctx, final_norm, caches=None,
                 raw=None):
    m = x.shape[0]
    cast = raw is not None
    side = [_layer_slabs(raw[k], layer, m // INPROJ_TM) for k in ("w_branch", "w_out")] if cast else ()
    proj, k_new, v_new, yb, yc, done = _inproj(
        x, mod, mod_row0, rows_per_mod, p["norm1"], wts["w_in"], layer, seq_len, want_kv=ctx is None,
        sink=p["sink"], pool=(p["pool_w"], p["pool_scale"]), caches=caches, side=side)
    if cast:
        wts["w_branch"] = done[0].reshape(N_BRANCH, D_MODEL, D_MODEL)
        wts["w_out"] = done[1]
    if ctx is None:
        ya, h_fin = _lru(proj, p["lru"], layer, seq_len, None)
    else:
        ck, cv, h0 = ctx
        ya = _lru(proj, p["lru"], layer, seq_len, h0)
        h_fin = None
        yb = _attn_lat(proj, ck, cv, p["sink"], layer, seq_len)
        yc = _pool(proj, p["pool_w"], p["pool_scale"], layer, seq_len)
    side = [_layer_slabs(raw[k], layer, m // MERGE_TM) for k in ("ffn_up", "ffn_down")] if cast else ()
    x1, h2, done = _merge(ya, yb, yc, proj, x, mod, mod_row0, rows_per_mod, p["b_gate"], wts["w_branch"],
                          wts["w_out"], p["norm2"], layer, side=side)
    if cast:
        wts["ffn_up"], wts["ffn_down"] = done
    side = [_layer_slabs(raw["w_in"], layer + 1, m // FFN_TM)] if cast and layer + 1 < DEPTH else ()
    out, done = _ffn(h2, x1, mod, mod_row0, rows_per_mod, wts["ffn_up"], p["ffn_conv"], p["ffn_conv_b"],
                     wts["ffn_down"], layer, seq_len, final_norm, side=side)
    return out, k_new, v_new, h_fin, (done[0] if side else None)


def _stacked_params(norm1, norm2, b_gate, lru_conv, lru_conv_b, lru_wa, lru_ba, lru_wx, lru_bx,
                    lru_lambda, attn_sink, pool_w, pool_scale, ffn_conv, ffn_conv_b):
    def per_head(v):
        return v.reshape(DEPTH, 2, LRU_HEADS, LRU_BLOCK).transpose(0, 2, 1, 3)

    row = lambda v: v[:, None, :]
    wg = jnp.concatenate([lru_wa[:, 0], lru_wx[:, 0], lru_wa[:, 1], lru_wx[:, 1]], axis=-1).astype(BF16)
    ba = per_head(lru_ba)
    bx = per_head(lru_bx)
    bg = jnp.concatenate([ba[:, :, 0], bx[:, :, 0], ba[:, :, 1], bx[:, :, 1]], axis=-1)[:, :, None, :]
    lam = per_head(lru_lambda).reshape(DEPTH, LRU_HEADS, 1, 2 * LRU_BLOCK)
    return {
        "norm1": row(norm1), "norm2": row(norm2), "b_gate": row(b_gate),
        "lru": {"conv_w": lru_conv, "conv_b": row(lru_conv_b), "wg": wg, "bg": bg, "lam": lam},
        "sink": attn_sink, "pool_w": pool_w.astype(BF16), "pool_scale": row(pool_scale),
        "ffn_conv": ffn_conv, "ffn_conv_b": row(ffn_conv_b),
    }


def kernel(x_prompt, x_sample, cache_k, cache_v, state_lru, c, c_ctx, w_ada, b_ada, norm1, norm2, w_in,
           b_gate, lru_conv, lru_conv_b, lru_wa, lru_ba, lru_wx, lru_bx, lru_lambda, attn_sink, pool_w,
           pool_scale, w_branch, w_out, ffn_up, ffn_conv, ffn_conv_b, ffn_down, final_norm):
    batch, seq, _ = x_prompt.shape
    dec_batch, dec_seq, _ = x_sample.shape
    past = cache_k.shape[2]
    assert seq == SCAN_CHUNK and dec_seq % SCAN_ROWS == 0 and (batch * seq) % SCAN_ROWS == 0

    c_rows = jnp.concatenate(
        [c_ctx[None], c, jnp.zeros((SUBLANES - 1 - dec_batch, D_MODEL), F32)], axis=0)
    mods = _ada(c_rows, w_ada, b_ada)

    xp = x_prompt.reshape(batch * seq, D_MODEL)
    xs = x_sample.reshape(dec_batch * dec_seq, D_MODEL)
    fn = final_norm[None]
    p = _stacked_params(norm1, norm2, b_gate, lru_conv, lru_conv_b, lru_wa, lru_ba, lru_wx, lru_bx,
                        lru_lambda, attn_sink, pool_w, pool_scale, ffn_conv, ffn_conv_b)
    raw = {"w_in": w_in, "w_branch": w_branch, "w_out": w_out, "ffn_up": ffn_up, "ffn_down": ffn_down}
    w_in_l = w_in[0].astype(BF16)
    mod_rows = mods.reshape(DEPTH * SUBLANES, 1, 6 * D_MODEL)
    ck = cache_k.reshape(dec_batch, DEPTH, past, D_KV).astype(BF16)
    cv = cache_v.reshape(dec_batch, DEPTH, past, D_KV).astype(BF16)
    h0 = state_lru.transpose(1, 2, 0, 3)[:, :, :, None, :]
    caches, hs = None, []
    for l in range(DEPTH):
        last = fn if l == DEPTH - 1 else None
        wts = {"w_in": w_in_l}
        xp, k_all, v_all, h_fin, w_in_l = _trunk_layer(xp, mod_rows, l * SUBLANES, batch * seq, p, wts, l, seq,
                                                       None, last, caches, raw)
        caches = (k_all, v_all)
        xs = _trunk_layer(xs, mod_rows, l * SUBLANES + 1, dec_seq, p, wts, l, dec_seq, (ck, cv, h0), last)[0]
        hs.append(h_fin.transpose(1, 0, 2))
    y_prompt = xp.reshape(batch, seq, D_MODEL)
    y_sample = xs.reshape(dec_batch, dec_seq, D_MODEL)
    cache_dims = (batch, DEPTH, seq, N_KV_HEADS, HEAD_DIM)
    return (y_prompt, y_sample, k_all.reshape(cache_dims), v_all.reshape(cache_dims), jnp.stack(hs, axis=1))
```

```python
import functools

import numpy as np
import jax
import jax.numpy as jnp
from jax import lax
from jax.experimental import pallas as pl
from jax.experimental.pallas import tpu as pltpu

F32 = jnp.float32
BF16 = jnp.bfloat16

D_MODEL = 1024
DEPTH = 2
GRID_W = 64
EPS = 1e-6
N_BRANCH = 3
D_RNN = 1024
LRU_HEADS = 8
LRU_BLOCK = D_RNN // LRU_HEADS
LRU_C = 8.0
N_HEADS = 8
N_KV_HEADS = 2
KV_GROUPS = N_HEADS // N_KV_HEADS
HEAD_DIM = 128
D_KV = N_KV_HEADS * HEAD_DIM
WINDOW = 128
BLOCK_Q = 128
ROPE_BASE = 10000.0
NEG_INF = -1e30
D_POOL = 1024
POOL_WINDOWS = (2, 4, 8, 16)
POOL_GROUP = D_POOL // len(POOL_WINDOWS)
D_FF = 2816
D_IN = D_RNN + N_HEADS * HEAD_DIM + 2 * D_KV + D_POOL + N_BRANCH * D_MODEL

COL_XA = 0
COL_Q = COL_XA + D_RNN
COL_K = COL_Q + N_HEADS * HEAD_DIM
COL_V = COL_K + D_KV
COL_XC = COL_V + D_KV
COL_G = COL_XC + D_POOL
LOG2E = float(np.log2(np.e))

VMEM_LIMIT_BYTES = 52 * 1024 * 1024
SUBLANES = 8
LANES = 128
BF16_ROWS = 16

SCAN_CHUNK = 256
SCAN_PITCH = 260
SCAN_ROWS = 2048


def _params(*sem):
    return pltpu.CompilerParams(dimension_semantics=sem, vmem_limit_bytes=VMEM_LIMIT_BYTES)


def _dot(a, b):
    return jnp.dot(a, b, preferred_element_type=F32)


def _dot_nt(a, b):
    return lax.dot_general(a, b, (((1,), (1,)), ((), ())), preferred_element_type=F32)


def _sigmoid(z):
    return 0.5 * (1.0 + jnp.tanh(0.5 * z))


def _rms(x):
    return x * lax.rsqrt(jnp.mean(x * x, axis=-1, keepdims=True) + EPS)


def _ada_kernel(c_ref, w_ref, b_ref, o_ref):
    c = c_ref[...]
    s = c * _sigmoid(c)
    o_ref[0] = _dot(s.astype(BF16), w_ref[0].astype(BF16)) + b_ref[0]


def _ada(c_rows, w_ada, b_ada):
    tn = 1536
    return pl.pallas_call(
        _ada_kernel,
        grid=(DEPTH, 6 * D_MODEL // tn),
        in_specs=[
            pl.BlockSpec((SUBLANES, D_MODEL), lambda l, j: (0, 0)),
            pl.BlockSpec((1, D_MODEL, tn), lambda l, j: (l, 0, j)),
            pl.BlockSpec((1, 1, tn), lambda l, j: (l, 0, j)),
        ],
        out_specs=pl.BlockSpec((1, SUBLANES, tn), lambda l, j: (l, 0, j)),
        out_shape=jax.ShapeDtypeStruct((DEPTH, SUBLANES, 6 * D_MODEL), F32),
        compiler_params=_params("parallel", "parallel"),
        name="ada_mod",
    )(c_rows, w_ada, b_ada.reshape(DEPTH, 1, 6 * D_MODEL))


INPROJ_TM = 512
INPROJ_CK = 2 * D_KV


def _resident(shape, layer):
    ndim = len(shape)
    return pl.BlockSpec((None,) + tuple(shape), lambda *_: (layer,) + (0,) * ndim,
                        pipeline_mode=pl.Buffered(1))


def _whole(shape):
    return pl.BlockSpec(tuple(shape), lambda *_: (0,) * len(shape), pipeline_mode=pl.Buffered(1))


def _side_cast_specs(side, steps):
    in_specs = [pl.BlockSpec((rows, arr.shape[1]), lambda i, first=first: (first + i, 0))
                for arr, rows, first in side]
    out_specs = [pl.BlockSpec((rows, arr.shape[1]), lambda i: (i, 0)) for arr, rows, _ in side]
    out_shapes = [jax.ShapeDtypeStruct((rows * steps, arr.shape[1]), BF16) for arr, rows, _ in side]
    return in_specs, out_specs, out_shapes


def _run_side_casts(refs, n_in, n_out, n_side):
    refs = list(refs)
    side_in = refs[n_in:n_in + n_side]
    side_out = refs[n_in + n_side + n_out:n_in + 2 * n_side + n_out]
    for src, dst in zip(side_in, side_out):
        dst[...] = src[...].astype(BF16)
    return refs[:n_in] + refs[n_in + n_side:n_in + n_side + n_out] + refs[n_in + 2 * n_side + n_out:]


def _layer_slabs(stacked, layer, steps):
    cols = stacked.shape[-1]
    rows = int(np.prod(stacked.shape[1:-1]))
    assert rows % (steps * BF16_ROWS) == 0
    return (stacked.reshape(stacked.shape[0] * rows, cols), rows // steps, layer * steps)


def _mod_spec(row0, per):
    return pl.BlockSpec((1, 1, 6 * D_MODEL), lambda i: (row0 + i // per, 0, 0))


def _inproj_kernel(*refs, seq_len, want_kv, layer, owns_cache, n_in, n_side):
    refs = _run_side_casts(refs, n_in, 5 if want_kv else 1, n_side)
    _inproj_body(*refs, seq_len=seq_len, want_kv=want_kv, layer=layer, owns_cache=owns_cache)


def _inproj_body(*refs, seq_len, want_kv, layer, owns_cache):
    if want_kv:
        x_ref, mod_ref, g_ref, w_ref, sink_ref, band_ref, pw_ref, ps_ref = refs[:8]
        o_ref, k_ref, v_ref, y_ref, yc_ref = refs[-5:]
        q_chunks = []
        if owns_cache:
            for other in range(DEPTH):
                if other != layer:
                    k_ref[:, other] = jnp.zeros(k_ref.shape[:1] + k_ref.shape[2:], F32)
                    v_ref[:, other] = jnp.zeros(v_ref.shape[:1] + v_ref.shape[2:], F32)
    else:
        x_ref, mod_ref, g_ref, w_ref, cos_ref, sa_ref, sb_ref, o_ref = refs
    mod = mod_ref[0]
    shift = mod[:, 0:D_MODEL]
    scale = mod[:, D_MODEL:2 * D_MODEL]
    h = (_rms(x_ref[...]) * g_ref[...] * (1.0 + scale) + shift).astype(BF16)
    nf = HEAD_DIM // 4

    def rope(x):
        return (x * cos_ref[...] + pltpu.roll(x, HEAD_DIM - nf, 1) * sa_ref[...]
                + pltpu.roll(x, nf, 1) * sb_ref[...])

    for c in range(D_IN // INPROJ_CK):
        acc = _dot(h, w_ref[:, c * INPROJ_CK:(c + 1) * INPROJ_CK])
        if not want_kv:
            lo = c * INPROJ_CK
            heads = [acc[:, j * HEAD_DIM:(j + 1) * HEAD_DIM] for j in range(INPROJ_CK // HEAD_DIM)]
            heads = [rope(hd) if COL_Q <= lo + j * HEAD_DIM < COL_V else hd for j, hd in enumerate(heads)]
            heads = [hd * SOFTMAX_SCALE if COL_Q <= lo + j * HEAD_DIM < COL_K else hd
                     for j, hd in enumerate(heads)]
            acc = jnp.concatenate(heads, axis=1)
        o_ref[:, c * INPROJ_CK:(c + 1) * INPROJ_CK] = acc.astype(o_ref.dtype)
        if want_kv and COL_Q <= c * INPROJ_CK < COL_K:
            q_chunks.append((acc * SOFTMAX_SCALE).astype(BF16))
        if want_kv and COL_XC <= c * INPROJ_CK < COL_G:
            xc = acc.astype(BF16)
            for j in range(INPROJ_CK // POOL_GROUP):
                gi = (c * INPROJ_CK - COL_XC) // POOL_GROUP + j
                gc = slice(gi * POOL_GROUP, (gi + 1) * POOL_GROUP)
                edges = _pool_edge_weights(POOL_WINDOWS[gi])
                for b in range(INPROJ_TM // seq_len):
                    rows = slice(b * seq_len, (b + 1) * seq_len)
                    x = xc[rows, j * POOL_GROUP:(j + 1) * POOL_GROUP]
                    yc = _pool_tile(_dot(band_ref[gi], x), x, POOL_WINDOWS[gi], edges, True, True,
                                    pw_ref[gi], ps_ref[:, gc])
                    yc_ref[rows, gc] = yc.astype(yc_ref.dtype)
        if want_kv and c == COL_K // INPROJ_CK:
            for b in range(INPROJ_TM // seq_len):
                rows = slice(b * seq_len, (b + 1) * seq_len)
                for hd in range(N_KV_HEADS):
                    dst = pl.ds(hd, seq_len, stride=N_KV_HEADS)
                    at = (b, layer, dst, slice(None)) if owns_cache else (b, dst, slice(None))
                    k_ref[at] = acc[rows, hd * HEAD_DIM:(hd + 1) * HEAD_DIM]
                    v_ref[at] = acc[rows, D_KV + hd * HEAD_DIM:D_KV + (hd + 1) * HEAD_DIM]
            kv = acc.astype(BF16)

            def scores(b, kh):
                rows = slice(b * seq_len, (b + 1) * seq_len)
                q = jnp.concatenate([q_chunks[kh][rows, g * HEAD_DIM:(g + 1) * HEAD_DIM]
                                     for g in range(KV_GROUPS)], axis=0)
                return (b, kh, _dot_nt(kv[rows, kh * HEAD_DIM:(kh + 1) * HEAD_DIM], q),
                        kv[rows, D_KV + kh * HEAD_DIM:D_KV + (kh + 1) * HEAD_DIM])

            order = [(b, kh) for b in range(INPROJ_TM // seq_len) for kh in range(N_KV_HEADS)]
            pending = [scores(*ch) for ch in order[:ATTN_LOOKAHEAD]]
            for n in range(len(order)):
                if n + ATTN_LOOKAHEAD < len(order):
                    pending.append(scores(*order[n + ATTN_LOOKAHEAD]))
                b, kh, t, vals = pending.pop(0)
                o_t = _softmax_pv_t(t, _sink_row(sink_ref, layer, kh, seq_len), vals)
                _store_heads_t(y_ref, o_t, kh, b * seq_len, seq_len)


def _inproj(x, mod, mod_row0, rows_per_mod, g, w, layer, seq_len, want_kv, sink=None, pool=None, caches=None,
            side=()):
    m = x.shape[0]
    tm = INPROJ_TM
    per = rows_per_mod // tm
    proj_spec = pl.BlockSpec((tm, D_IN), lambda i: (i, 0))
    proj_shape = jax.ShapeDtypeStruct((m, D_IN), BF16)
    aliases = {}
    owns_cache = want_kv and caches is None
    if want_kv:
        nb = tm // seq_len
        if owns_cache:
            cache_spec = pl.BlockSpec((nb, DEPTH, seq_len * N_KV_HEADS, HEAD_DIM), lambda i: (i, 0, 0, 0))
        else:
            cache_spec = pl.BlockSpec((nb, None, seq_len * N_KV_HEADS, HEAD_DIM), lambda i: (i, layer, 0, 0))
        cache_shape = jax.ShapeDtypeStruct((m // seq_len, DEPTH, seq_len * N_KV_HEADS, HEAD_DIM), F32)
        assert tm % seq_len == 0 and INPROJ_CK == KV_GROUPS * HEAD_DIM and COL_Q % INPROJ_CK == 0
        assert seq_len == POOL_TILE and COL_XC % INPROJ_CK == 0 and INPROJ_CK % POOL_GROUP == 0
        mix_spec = pl.BlockSpec((tm, D_MODEL), lambda i: (i, 0))
        mix_shape = jax.ShapeDtypeStruct((m, D_MODEL), BF16)
        out_specs = (proj_spec, cache_spec, cache_spec, mix_spec, mix_spec)
        out_shape = (proj_shape, cache_shape, cache_shape, mix_shape, mix_shape)
        nwin = len(POOL_WINDOWS)
        extra_specs = [pl.BlockSpec(memory_space=pltpu.SMEM), _whole((nwin, POOL_TILE, POOL_TILE)),
                       _resident((nwin, POOL_GROUP, POOL_GROUP), layer), _resident((1, D_POOL), layer)]
        extra_args = [sink, _pool_bands(seq_len), pool[0], pool[1]]
        if caches is not None:
            aliases = {4 + len(extra_args): 1, 5 + len(extra_args): 2}
            extra_specs += [pl.BlockSpec(memory_space=pl.ANY)] * 2
            extra_args += list(caches)
    else:
        out_specs, out_shape = (proj_spec,), (proj_shape,)
        tab = pl.BlockSpec((tm, HEAD_DIM), lambda i: (i % (seq_len // tm), 0))
        extra_specs, extra_args = [tab, tab, tab], list(_rope_tables(seq_len))
    side_in, side_out, side_shapes = _side_cast_specs(side, m // tm)
    out = pl.pallas_call(
        functools.partial(_inproj_kernel, seq_len=seq_len, want_kv=want_kv, layer=layer, owns_cache=owns_cache,
                          n_in=4 + len(extra_specs), n_side=len(side)),
        grid=(m // tm,),
        in_specs=[
            pl.BlockSpec((tm, D_MODEL), lambda i: (i, 0)),
            _mod_spec(mod_row0, per),
            _resident((1, D_MODEL), layer),
            _whole((D_MODEL, D_IN)),
        ] + extra_specs + side_in,
        out_specs=tuple(out_specs) + tuple(side_out),
        out_shape=tuple(out_shape) + tuple(side_shapes),
        input_output_aliases=aliases,
        compiler_params=_params("parallel"),
        name="in_proj",
    )(x, mod, g, w, *extra_args, *[item[0] for item in side])
    n_own = len(out_specs)
    own = tuple(out[:n_own]) if want_kv else (out[0], None, None, None, None)
    return own + (list(out[n_own:]),)


LRU_HEADS_PER_STEP = 2


def _chunk_neighbour(v, towards_later):
    sub = lax.broadcasted_iota(jnp.int32, v.shape, 0)
    if towards_later:
        return jnp.where(sub >= 1, pltpu.roll(v, 1, 0), 0.0)
    return jnp.where(sub <= SUBLANES - 2, pltpu.roll(v, SUBLANES - 1, 0), 0.0)


def _shift_time(x_tm, k, chained):
    n = x_tm.shape[0]
    steps = abs(k)
    edge = []
    for s in range(steps):
        if not chained:
            edge.append(jnp.zeros((SUBLANES, x_tm.shape[1]), F32))
        elif k > 0:
            src = n - (steps - s) * SUBLANES
            edge.append(_chunk_neighbour(x_tm[src:src + SUBLANES], True))
        else:
            edge.append(_chunk_neighbour(x_tm[s * SUBLANES:(s + 1) * SUBLANES], False))
    if k > 0:
        return jnp.concatenate(edge + [x_tm[:n - steps * SUBLANES]], axis=0)
    return jnp.concatenate([x_tm[steps * SUBLANES:]] + edge, axis=0)


def _lru_tm_kernel(*refs, seq_len, latent):
    if latent:
        (x_ref, cw_ref, cb_ref, wg_ref, bg_ref, lam_ref, h0_ref, y_ref,
         io_ref, af_ref, bf_ref, ab_ref, bb_ref, hf_ref, hb_ref) = refs
    else:
        (x_ref, cw_ref, cb_ref, wg_ref, bg_ref, lam_ref, y_ref, fin_ref,
         io_ref, af_ref, bf_ref, ab_ref, bb_ref, hf_ref, hb_ref) = refs
    rows = SCAN_ROWS
    nchunk = rows // SCAN_CHUNK
    nhead = LRU_HEADS_PER_STEP
    chained = seq_len > SCAN_CHUNK
    for hd in range(nhead):
        lanes = slice(hd * LRU_BLOCK, (hd + 1) * LRU_BLOCK)
        x = x_ref[:, lanes].astype(F32)
        for c in range(nchunk):
            io_ref[hd, pl.ds(c * SCAN_PITCH, SCAN_CHUNK), :] = x[c * SCAN_CHUNK:(c + 1) * SCAN_CHUNK]
        x = jnp.concatenate(
            [io_ref[hd, pl.ds(t, nchunk, stride=SCAN_PITCH), :] for t in range(SCAN_CHUNK)], axis=0)
        cw = 0.5 * cw_ref[:, lanes]
        xh = (_shift_time(x, 2, chained) * cw[0:1] + _shift_time(x, 1, chained) * cw[1:2]
              + x * cw[2:3] + _shift_time(x, -1, chained) * cw[3:4] + 0.5 * cb_ref[:, lanes])
        half_bias = 0.5 * bg_ref[hd]
        bias_hi = half_bias.astype(BF16).astype(F32)
        bias_rows = jnp.concatenate(
            [bias_hi, half_bias - bias_hi, jnp.zeros((BF16_ROWS - 2, 4 * LRU_BLOCK), F32)], axis=0)
        lhs = jnp.concatenate([xh.astype(BF16), jnp.ones((rows, BF16_ROWS), BF16)], axis=1)
        rhs = jnp.concatenate([wg_ref[hd], bias_rows.astype(BF16)], axis=0)
        th = jnp.tanh(_dot(lhs, rhs))
        lam = lam_ref[hd]
        for d, (a_ref, b_ref) in enumerate(((af_ref, bf_ref), (ab_ref, bb_ref))):
            th_r = th[:, 2 * d * LRU_BLOCK:(2 * d + 1) * LRU_BLOCK]
            th_i = th[:, (2 * d + 1) * LRU_BLOCK:(2 * d + 2) * LRU_BLOCK]
            nl = -lam[:, d * LRU_BLOCK:(d + 1) * LRU_BLOCK]
            softplus = jnp.maximum(nl, 0.0) + jnp.log(1.0 + jnp.exp(-jnp.abs(nl)))
            ch = (-0.5 * LRU_C * LOG2E) * softplus
            a = jnp.exp2(ch + ch * th_r)
            y = 1.0 - a * a
            a_ref[hd] = a
            b_ref[hd] = (y * lax.rsqrt(jnp.maximum(y, 1e-30))) * ((1.0 + th_i) * xh)

    def step(t, carry):
        rf = pl.ds(t * nchunk, nchunk)
        rb = pl.ds((SCAN_CHUNK - 1 - t) * nchunk, nchunk)
        out = []
        for hd in range(nhead):
            hf, hb, pf, pb = carry[4 * hd:4 * hd + 4]
            a_f = af_ref[hd, rf, :]
            a_b = ab_ref[hd, rb, :]
            hf = a_f * hf + bf_ref[hd, rf, :]
            hb = a_b * hb + bb_ref[hd, rb, :]
            hf_ref[hd, rf, :] = hf
            hb_ref[hd, rb, :] = hb
            if chained:
                pf = pf * a_f
                pb = pb * a_b
                af_ref[hd, rf, :] = pf
                ab_ref[hd, rb, :] = pb
            out += [hf, hb, pf, pb]
        return tuple(out)

    zero = jnp.zeros((nchunk, LRU_BLOCK), F32)
    one = jnp.ones((nchunk, LRU_BLOCK), F32)
    carry = (zero, zero, one, one) * nhead
    for t in range(SCAN_CHUNK):
        carry = step(t, carry)

    sub = lax.broadcasted_iota(jnp.int32, (nchunk, LRU_BLOCK), 0)
    last = slice(rows - nchunk, rows)
    first = slice(0, nchunk)
    for hd in range(nhead):
        lanes = slice(hd * LRU_BLOCK, (hd + 1) * LRU_BLOCK)
        hf = hf_ref[hd]
        hb = hb_ref[hd]
        if chained:
            pf = af_ref[hd]
            pb = ab_ref[hd]
            ef = jnp.where(sub == 0, h0_ref[0, 0][:, lanes], 0.0)
            eb = jnp.where(sub == nchunk - 1, h0_ref[1, 0][:, lanes], 0.0)
            for c in range(1, nchunk):
                ef = jnp.where(sub == c, pltpu.roll(hf[last] + pf[last] * ef, 1, 0), ef)
                eb = jnp.where(sub == nchunk - 1 - c,
                               pltpu.roll(hb[first] + pb[first] * eb, nchunk - 1, 0), eb)
            hf = (hf.reshape(SCAN_CHUNK, nchunk, LRU_BLOCK)
                  + pf.reshape(SCAN_CHUNK, nchunk, LRU_BLOCK) * ef[None]).reshape(rows, LRU_BLOCK)
            hb = (hb.reshape(SCAN_CHUNK, nchunk, LRU_BLOCK)
                  + pb.reshape(SCAN_CHUNK, nchunk, LRU_BLOCK) * eb[None]).reshape(rows, LRU_BLOCK)
        else:
            fin_ref[0, :, lanes] = hf[last]
            fin_ref[1, :, lanes] = hb[first]
        y = hf + hb
        for t in range(SCAN_CHUNK):
            io_ref[hd, pl.ds(t, nchunk, stride=SCAN_PITCH), :] = y[t * nchunk:(t + 1) * nchunk]
        for c in range(nchunk):
            y_ref[pl.ds(c * SCAN_CHUNK, SCAN_CHUNK), lanes] = (
                io_ref[hd, pl.ds(c * SCAN_PITCH, SCAN_CHUNK), :].astype(y_ref.dtype))


def _lru(proj, lp, layer, seq_len, h0):
    m = proj.shape[0]
    latent = h0 is not None
    rows = SCAN_ROWS
    nhead = LRU_HEADS_PER_STEP
    width = nhead * LRU_BLOCK
    in_specs = [
        pl.BlockSpec((rows, width), lambda r, h: (r, COL_XA // width + h)),
        pl.BlockSpec((None, 4, width), lambda r, h: (layer, 0, h)),
        pl.BlockSpec((None, 1, width), lambda r, h: (layer, 0, h)),
        pl.BlockSpec((None, nhead, LRU_BLOCK, 4 * LRU_BLOCK), lambda r, h: (layer, h, 0, 0)),
        pl.BlockSpec((None, nhead, 1, 4 * LRU_BLOCK), lambda r, h: (layer, h, 0, 0)),
        pl.BlockSpec((None, nhead, 1, 2 * LRU_BLOCK), lambda r, h: (layer, h, 0, 0)),
    ]
    args = [proj, lp["conv_w"], lp["conv_b"], lp["wg"], lp["bg"], lp["lam"]]
    y_spec = pl.BlockSpec((rows, width), lambda r, h: (r, h))
    y_shape = jax.ShapeDtypeStruct((m, D_RNN), BF16)
    if latent:
        in_specs.append(pl.BlockSpec((None, 2, 1, 1, width), lambda r, h: (layer, 0, r, 0, h)))
        args.append(h0)
        out_specs, out_shape = y_spec, y_shape
    else:
        nseq = m // seq_len
        out_specs = (y_spec, pl.BlockSpec((2, rows // seq_len, width), lambda r, h: (0, r, h)))
        out_shape = (y_shape, jax.ShapeDtypeStruct((2, nseq, D_RNN), F32))
    strided_buf = pltpu.VMEM((nhead, rows // SCAN_CHUNK * SCAN_PITCH, LRU_BLOCK), F32)
    scan_buf = pltpu.VMEM((nhead, rows, LRU_BLOCK), F32)
    return pl.pallas_call(
        functools.partial(_lru_tm_kernel, seq_len=seq_len, latent=latent),
        grid=(m // rows, LRU_HEADS // nhead),
        in_specs=in_specs,
        out_specs=out_specs,
        out_shape=out_shape,
        scratch_shapes=[strided_buf] + [scan_buf] * 6,
        compiler_params=_params("parallel", "parallel"),
        name="rglru_latent" if latent else "rglru_context",
    )(*args)


SOFTMAX_SCALE = HEAD_DIM ** -0.5 * LOG2E


def _stack_heads(q_ref, kh, r0, rows):
    parts = [q_ref[r0:r0 + rows, (kh * KV_GROUPS + g) * HEAD_DIM:(kh * KV_GROUPS + g + 1) * HEAD_DIM]
             for g in range(KV_GROUPS)]
    return jnp.concatenate(parts, axis=0)


def _dot_tn(a, b):
    return lax.dot_general(a, b, (((0,), (0,)), ((), ())), preferred_element_type=F32)


def _sink_row(sink_ref, layer, kh, cols):
    parts = [jnp.full((1, cols), sink_ref[layer, kh * KV_GROUPS + g] * LOG2E, F32) for g in range(KV_GROUPS)]
    return jnp.concatenate(parts, axis=1)


def _softmax_pv_t(t, sink, v):
    m = jnp.maximum(jnp.max(t, axis=0, keepdims=True), sink)
    e = jnp.exp2(t - m).astype(BF16)
    d = v.shape[1]
    v_ones = jnp.concatenate([v, jnp.ones((v.shape[0], SUBLANES), BF16)], axis=1)
    o_sum = _dot_tn(v_ones, e)
    denom = o_sum[d:d + 1] + jnp.exp2(sink - m)
    return o_sum[:d] * (1.0 / denom)


def _store_heads_t(y_ref, o_t, kh, r0, rows):
    for g in range(KV_GROUPS):
        h = kh * KV_GROUPS + g
        y_ref[r0:r0 + rows, h * HEAD_DIM:(h + 1) * HEAD_DIM] = (
            o_t[:, g * rows:(g + 1) * rows].T.astype(y_ref.dtype))


def _rope_tables(seq_len):
    nf = HEAD_DIM // 4
    freqs = ROPE_BASE ** (-np.arange(nf, dtype=np.float64) / nf)
    t = np.arange(seq_len)
    ang_row = (t // GRID_W)[:, None] * freqs[None, :]
    ang_col = (t % GRID_W)[:, None] * freqs[None, :]
    ang = np.concatenate([ang_row, ang_row, ang_col, ang_col], axis=1)
    first = (np.arange(HEAD_DIM) % (2 * nf)) < nf
    cos = np.cos(ang)
    sin = np.sin(ang)
    sin_a = np.where(first[None, :], -sin, 0.0)
    sin_b = np.where(first[None, :], 0.0, sin)
    return tuple(jnp.asarray(a, F32) for a in (cos, sin_a, sin_b))


ATTN_LAT_BLOCKS = 4
ATTN_LOOKAHEAD = 3


def _attn_lat_kernel(sink_ref, q_ref, kvp_ref, kvc_ref, kvn_ref, ck_ref, cv_ref, y_ref, *, nblk, layer):
    step = pl.program_id(1)
    cols = KV_GROUPS * BLOCK_Q
    span = 3 * BLOCK_Q
    key = lax.broadcasted_iota(jnp.int32, (span, cols), 0)
    qry = lax.broadcasted_iota(jnp.int32, (span, cols), 1) & (BLOCK_Q - 1)
    kv = jnp.concatenate([kvp_ref[...], kvc_ref[...], kvn_ref[...]], axis=0)
    def scores(i, kh):
        win = kv[i * BLOCK_Q:i * BLOCK_Q + span]
        sl = slice(kh * HEAD_DIM, (kh + 1) * HEAD_DIM)
        vl = slice(D_KV + kh * HEAD_DIM, D_KV + (kh + 1) * HEAD_DIM)
        q = _stack_heads(q_ref, kh, i * BLOCK_Q, BLOCK_Q)
        keys = jnp.concatenate([win[:, sl], ck_ref[0, :, sl]], axis=0)
        vals = jnp.concatenate([win[:, vl], cv_ref[0, :, sl]], axis=0)
        return i, kh, _dot_nt(keys, q), vals

    order = [(i, kh) for i in range(ATTN_LAT_BLOCKS) for kh in range(N_KV_HEADS)]
    pending = [scores(*c) for c in order[:ATTN_LOOKAHEAD]]
    for n in range(len(order)):
        if n + ATTN_LOOKAHEAD < len(order):
            pending.append(scores(*order[n + ATTN_LOOKAHEAD]))
        i, kh, t, vals = pending.pop(0)
        j = step * ATTN_LAT_BLOCKS + i
        lo = jnp.where(j > 0, qry, BLOCK_Q)
        hi = jnp.where(j < nblk - 1, qry + 2 * BLOCK_Q, 2 * BLOCK_Q - 1)
        bias = jnp.where(jnp.logical_and(key >= lo, key <= hi), 0.0, NEG_INF)
        t = jnp.concatenate([t[:span] + bias, t[span:]], axis=0)
        o_t = _softmax_pv_t(t, _sink_row(sink_ref, layer, kh, BLOCK_Q), vals)
        _store_heads_t(y_ref, o_t, kh, i * BLOCK_Q, BLOCK_Q)


def _attn_lat(proj, ck, cv, sink, layer, seq_len):
    m = proj.shape[0]
    nblk = seq_len // BLOCK_Q
    nstep = nblk // ATTN_LAT_BLOCKS
    rows = ATTN_LAT_BLOCKS * BLOCK_Q
    nb = m // seq_len
    past = ck.shape[2]
    kv_col = COL_K // (2 * D_KV)
    assert COL_V == COL_K + D_KV and COL_K % (2 * D_KV) == 0

    def halo(shift):
        def index(b, s):
            return (b * nblk + jnp.clip(s * ATTN_LAT_BLOCKS + shift, 0, nblk - 1), kv_col)
        return index

    return pl.pallas_call(
        functools.partial(_attn_lat_kernel, nblk=nblk, layer=layer),
        grid=(nb, nstep),
        in_specs=[
            pl.BlockSpec(memory_space=pltpu.SMEM),
            pl.BlockSpec((rows, N_HEADS * HEAD_DIM), lambda b, s: (b * nstep + s, COL_Q // (N_HEADS * HEAD_DIM))),
            pl.BlockSpec((BLOCK_Q, 2 * D_KV), halo(-1)),
            pl.BlockSpec((rows, 2 * D_KV), lambda b, s: (b * nstep + s, kv_col)),
            pl.BlockSpec((BLOCK_Q, 2 * D_KV), halo(ATTN_LAT_BLOCKS)),
            pl.BlockSpec((1, None, past, D_KV), lambda b, s: (b, layer, 0, 0)),
            pl.BlockSpec((1, None, past, D_KV), lambda b, s: (b, layer, 0, 0)),
        ],
        out_specs=pl.BlockSpec((rows, N_HEADS * HEAD_DIM), lambda b, s: (b * nstep + s, 0)),
        out_shape=jax.ShapeDtypeStruct((m, N_HEADS * HEAD_DIM), BF16),
        compiler_params=_params("parallel", "parallel"),
        name="attn_latent",
    )(sink, proj, proj, proj, proj, ck, cv)


POOL_TILE = 256
POOL_LEAD = BF16_ROWS
POOL_ROWS = 2048


def _pool_plan(seq_len):
    lead = 0 if seq_len == POOL_TILE else POOL_LEAD
    return lead, POOL_TILE - 2 * lead


def _pool_bands(seq_len):
    lead, nout = _pool_plan(seq_len)
    r = np.arange(nout)[:, None]
    c = np.arange(POOL_TILE)[None, :] - lead
    bands = [(c >= r - win // 2) & (c < r + win // 2) for win in POOL_WINDOWS]
    return jnp.asarray(np.stack(bands), BF16)


def _pool_edge_weights(win):
    half = win // 2
    edge = lax.broadcasted_iota(jnp.int32, (SUBLANES, POOL_GROUP), 0)
    inv_head = 1.0 / ((edge + half) - jnp.maximum(edge - half, 0)).astype(F32)
    inv_tail = 1.0 / (jnp.minimum(SUBLANES - edge, half) + half).astype(F32)
    return inv_head, inv_tail


def _pool_tile(sums, x, win, edges, at_start, at_end, w, scale):
    n = sums.shape[0]
    head = sums[:SUBLANES] * (edges[0] if at_start else 1.0 / win)
    tail = sums[n - SUBLANES:] * (edges[1] if at_end else 1.0 / win)
    mean = jnp.concatenate([head, sums[SUBLANES:n - SUBLANES] * (1.0 / win), tail], axis=0)
    pooled = (mean - x.astype(F32)).astype(BF16)
    return _dot(pooled, w) * scale


def _pool_kernel(x0_ref, x1_ref, x2_ref, x3_ref, band_ref, w_ref, s_ref, y_ref, pad_ref, *, seq_len):
    lead, nout = _pool_plan(seq_len)
    if lead:
        pad_ref[0:lead, :] = jnp.zeros((lead, POOL_GROUP), BF16)
        pad_ref[lead + seq_len:, :] = jnp.zeros((pad_ref.shape[0] - lead - seq_len, POOL_GROUP), BF16)
    for gi, (win, x_ref) in enumerate(zip(POOL_WINDOWS, (x0_ref, x1_ref, x2_ref, x3_ref))):
        cs = slice(gi * POOL_GROUP, (gi + 1) * POOL_GROUP)
        edges = _pool_edge_weights(win)
        if lead:
            pad_ref[lead:lead + seq_len, :] = x_ref[...]
        for base in range(0, POOL_ROWS, seq_len):
            for p0 in range(0, seq_len, nout):
                n = min(nout, seq_len - p0)
                x = x_ref[base + p0:base + p0 + n, :]
                src = pad_ref[p0:p0 + POOL_TILE, :] if lead else x
                sums = _dot(band_ref[gi, :n, :], src)
                y = _pool_tile(sums, x, win, edges, p0 == 0, p0 + n == seq_len, w_ref[gi], s_ref[:, cs])
                y_ref[base + p0:base + p0 + n, cs] = y.astype(y_ref.dtype)


def _pool(proj, w, s, layer, seq_len):
    m = proj.shape[0]
    rows = POOL_ROWS
    lead, nout = _pool_plan(seq_len)
    assert seq_len in (POOL_TILE, rows) and max(POOL_WINDOWS) // 2 <= min(SUBLANES, lead or SUBLANES)
    pad_rows = (pl.cdiv(seq_len, nout) - 1) * nout + POOL_TILE
    group = lambda gi: pl.BlockSpec((rows, POOL_GROUP), lambda r: (r, COL_XC // POOL_GROUP + gi))
    whole = lambda shape: pl.BlockSpec(shape, lambda r: (0,) * len(shape))
    nwin = len(POOL_WINDOWS)
    return pl.pallas_call(
        functools.partial(_pool_kernel, seq_len=seq_len),
        grid=(m // rows,),
        in_specs=[
            group(0), group(1), group(2), group(3),
            whole((nwin, nout, POOL_TILE)),
            _resident((nwin, POOL_GROUP, POOL_GROUP), layer),
            _resident((1, D_POOL), layer),
        ],
        out_specs=pl.BlockSpec((rows, D_POOL), lambda r: (r, 0)),
        out_shape=jax.ShapeDtypeStruct((m, D_POOL), BF16),
        scratch_shapes=[pltpu.VMEM((pad_rows, POOL_GROUP), BF16)],
        compiler_params=_params("parallel"),
        name="pool_mix",
    )(proj, proj, proj, proj, _pool_bands(seq_len), w, s)


MERGE_N_IN = 15
MERGE_TM = 512


def _merge_kernel(*refs, n_side):
    (ya_ref, yb_ref, yc_ref, g0_ref, g1_ref, g2_ref, g3_ref, g4_ref, g5_ref, x_ref, mod_ref,
     bg_ref, wb_ref, wo_ref, n2_ref, x1_ref, h2_ref) = _run_side_casts(refs, MERGE_N_IN, 2, n_side)
    mod = mod_ref[0]
    g_refs = (g0_ref, g1_ref, g2_ref, g3_ref, g4_ref, g5_ref)
    half = D_MODEL // 2
    merged = None
    for k, y_ref in enumerate((ya_ref, yb_ref, yc_ref)):
        y = _dot(y_ref[...], wb_ref[k])
        parts = []
        for p in range(2):
            z = g_refs[2 * k + p][...].astype(F32) + bg_ref[:, k * D_MODEL + p * half:k * D_MODEL + (p + 1) * half]
            parts.append((1.0 + jnp.tanh(0.5 * z)) * y[:, p * half:(p + 1) * half])
        term = jnp.concatenate(parts, axis=1)
        merged = term if merged is None else merged + term
    merged = 0.5 * merged
    gate1 = mod[:, 2 * D_MODEL:3 * D_MODEL]
    x1 = x_ref[...] + gate1 * _dot(merged.astype(BF16), wo_ref[...])
    x1_ref[...] = x1
    shift2 = mod[:, 3 * D_MODEL:4 * D_MODEL]
    scale2 = mod[:, 4 * D_MODEL:5 * D_MODEL]
    h2_ref[...] = (_rms(x1) * n2_ref[...] * (1.0 + scale2) + shift2).astype(BF16)


def _merge(ya, yb, yc, proj, x, mod, mod_row0, rows_per_mod, bg, wb, wo, n2, layer, side=()):
    m = x.shape[0]
    tm = MERGE_TM
    per = rows_per_mod // tm
    half = D_MODEL // 2
    row = pl.BlockSpec((tm, D_MODEL), lambda i: (i, 0))
    gate = lambda c: pl.BlockSpec((tm, half), lambda i: (i, COL_G // half + c))
    side_in, side_out, side_shapes = _side_cast_specs(side, m // tm)
    in_specs = [
        row, row, row,
        gate(0), gate(1), gate(2), gate(3), gate(4), gate(5),
        row,
        _mod_spec(mod_row0, per),
        _resident((1, N_BRANCH * D_MODEL), layer),
        _whole((N_BRANCH, D_MODEL, D_MODEL)),
        _whole((D_MODEL, D_MODEL)),
        _resident((1, D_MODEL), layer),
    ]
    assert len(in_specs) == MERGE_N_IN
    out = pl.pallas_call(
        functools.partial(_merge_kernel, n_side=len(side)),
        grid=(m // tm,),
        in_specs=in_specs + side_in,
        out_specs=(row, row) + tuple(side_out),
        out_shape=(jax.ShapeDtypeStruct((m, D_MODEL), F32), jax.ShapeDtypeStruct((m, D_MODEL), BF16))
        + tuple(side_shapes),
        compiler_params=_params("parallel"),
        name="merge_out",
    )(ya, yb, yc, proj, proj, proj, proj, proj, proj, x, mod, bg, wb, wo, n2, *[item[0] for item in side])
    return out[0], out[1], list(out[2:])


FFN_CK = 256
FFN_TM = 512
FFN_GAP = SUBLANES
GELU_C = float(np.sqrt(2.0 / np.pi))


def _ffn_kernel(*refs, seq_len, final, n_side):
    refs = _run_side_casts(refs, 9 + int(final), 1, n_side)
    h_ref, hp_ref, hn_ref, x_ref, mod_ref, wup_ref, cw_ref, cb_ref, wd_ref = refs[:9]
    fn_ref = refs[9] if final else None
    o_ref, hx_ref, u_ref, act_ref = refs[-4:]
    tm = FFN_TM
    halo = seq_len > tm
    i = pl.program_id(0)

    if halo:
        per_seq = seq_len // tm
        at_start = i % per_seq == 0
        at_end = i % per_seq == per_seq - 1
        zeros = jnp.zeros((BF16_ROWS, D_MODEL), BF16)

        @pl.when(at_start)
        def _():
            hx_ref[0:BF16_ROWS, :] = zeros

        @pl.when(jnp.logical_not(at_start))
        def _():
            hx_ref[0:BF16_ROWS, :] = hp_ref[...]

        @pl.when(at_end)
        def _():
            hx_ref[BF16_ROWS + tm:, :] = zeros

        @pl.when(jnp.logical_not(at_end))
        def _():
            hx_ref[BF16_ROWS + tm:, :] = hn_ref[...]

        hx_ref[BF16_ROWS:BF16_ROWS + tm, :] = h_ref[...]
        bases = (BF16_ROWS,)
        seg = tm
    else:
        nseg = tm // seq_len
        seg = seq_len
        bases = tuple(FFN_GAP + s * (seg + FFN_GAP) for s in range(nseg))
        for s in range(nseg + 1):
            u_ref[s * (seg + FFN_GAP):s * (seg + FFN_GAP) + FFN_GAP, :] = jnp.zeros((FFN_GAP, FFN_CK), F32)

    def taps(offset):
        return jnp.concatenate([u_ref[b + offset:b + offset + seg, :] for b in bases], axis=0)

    for c in range(D_FF // FFN_CK):
        cs = slice(c * FFN_CK, (c + 1) * FFN_CK)
        vs = slice(D_FF + c * FFN_CK, D_FF + (c + 1) * FFN_CK)
        if halo:
            u_ext = _dot(hx_ref[...], wup_ref[:, cs])
            u_ref[...] = u_ext
            u0 = u_ext[BF16_ROWS:BF16_ROWS + tm]
        else:
            u0 = _dot(h_ref[...], wup_ref[:, cs])
            for s, b in enumerate(bases):
                u_ref[b:b + seg, :] = u0[s * seg:(s + 1) * seg]
        uv = _dot(h_ref[...], wup_ref[:, vs])
        gff = taps(-1) * cw_ref[0:1, cs] + u0 * cw_ref[1:2, cs] + taps(1) * cw_ref[2:3, cs] + cb_ref[:, cs]
        inner = gff * (GELU_C + (GELU_C * 0.044715) * (gff * gff))
        act_ref[:, cs] = (0.5 * (gff * uv) * (1.0 + jnp.tanh(inner))).astype(BF16)

    gate2 = mod_ref[0][:, 5 * D_MODEL:6 * D_MODEL]
    out = x_ref[...] + gate2 * _dot(act_ref[...], wd_ref[...])
    if final:
        out = _rms(out) * fn_ref[...]
    o_ref[...] = out


def _ffn(h2, x1, mod, mod_row0, rows_per_mod, wup, cw, cb, wd, layer, seq_len, final_norm, side=()):
    m = x1.shape[0]
    tm = FFN_TM
    per = rows_per_mod // tm
    hb = tm // BF16_ROWS
    last_halo = m // BF16_ROWS - 1
    final = final_norm is not None
    halo = seq_len > tm
    in_specs = [
        pl.BlockSpec((tm, D_MODEL), lambda i: (i, 0)),
        pl.BlockSpec((BF16_ROWS, D_MODEL), lambda i: (jnp.maximum(i * hb - 1, 0), 0)),
        pl.BlockSpec((BF16_ROWS, D_MODEL), lambda i: (jnp.minimum((i + 1) * hb, last_halo), 0)),
        pl.BlockSpec((tm, D_MODEL), lambda i: (i, 0)),
        _mod_spec(mod_row0, per),
        _whole((D_MODEL, 2 * D_FF)),
        _resident((3, D_FF), layer),
        _resident((1, D_FF), layer),
        _whole((D_FF, D_MODEL)),
    ]
    args = [h2, h2, h2, x1, mod, wup, cw, cb, wd]
    if final:
        in_specs.append(_whole((1, D_MODEL)))
        args.append(final_norm)
    if halo:
        u_rows = tm + 2 * BF16_ROWS
    else:
        u_rows = FFN_GAP + (tm // seq_len) * (seq_len + FFN_GAP)
    side_in, side_out, side_shapes = _side_cast_specs(side, m // tm)
    out = pl.pallas_call(
        functools.partial(_ffn_kernel, seq_len=seq_len, final=final, n_side=len(side)),
        grid=(m // tm,),
        in_specs=in_specs + side_in,
        out_specs=(pl.BlockSpec((tm, D_MODEL), lambda i: (i, 0)),) + tuple(side_out),
        out_shape=(jax.ShapeDtypeStruct((m, D_MODEL), F32),) + tuple(side_shapes),
        scratch_shapes=[
            pltpu.VMEM((tm + 2 * BF16_ROWS, D_MODEL), BF16),
            pltpu.VMEM((u_rows, FFN_CK), F32),
            pltpu.VMEM((tm, D_FF), BF16),
        ],
        compiler_params=_params("parallel"),
        name="conv_glu_ffn",
    )(*args, *[item[0] for item in side])
    return out[0], list(out[1:])


def _trunk_layer(x, mod, mod_row0, rows_per_mod, p, wts, layer, seq_len, ctx, final_norm, caches=None,
                 raw=None):
    m = x.shape[0]
    cast = raw is not None
    side = [_layer_slabs(raw[k], layer, m // INPROJ_TM) for k in ("w_branch", "w_out")] if cast else ()
    proj, k_new, v_new, yb, yc, done = _inproj(
        x, mod, mod_row0, rows_per_mod, p["norm1"], wts["w_in"], layer, seq_len, want_kv=ctx is None,
        sink=p["sink"], pool=(p["pool_w"], p["pool_scale"]), caches=caches, side=side)
    if cast:
        wts["w_branch"] = done[0].reshape(N_BRANCH, D_MODEL, D_MODEL)
        wts["w_out"] = done[1]
    if ctx is None:
        ya, h_fin = _lru(proj, p["lru"], layer, seq_len, None)
    else:
        ck, cv, h0 = ctx
        ya = _lru(proj, p["lru"], layer, seq_len, h0)
        h_fin = None
        yb = _attn_lat(proj, ck, cv, p["sink"], layer, seq_len)
        yc = _pool(proj, p["pool_w"], p["pool_scale"], layer, seq_len)
    side = [_layer_slabs(raw[k], layer, m // MERGE_TM) for k in ("ffn_up", "ffn_down")] if cast else ()
    x1, h2, done = _merge(ya, yb, yc, proj, x, mod, mod_row0, rows_per_mod, p["b_gate"], wts["w_branch"],
                          wts["w_out"], p["norm2"], layer, side=side)
    if cast:
        wts["ffn_up"], wts["ffn_down"] = done
    side = [_layer_slabs(raw["w_in"], layer + 1, m // FFN_TM)] if cast and layer + 1 < DEPTH else ()
    out, done = _ffn(h2, x1, mod, mod_row0, rows_per_mod, wts["ffn_up"], p["ffn_conv"], p["ffn_conv_b"],
                     wts["ffn_down"], layer, seq_len, final_norm, side=side)
    return out, k_new, v_new, h_fin, (done[0] if side else None)


def _stacked_params(norm1, norm2, b_gate, lru_conv, lru_conv_b, lru_wa, lru_ba, lru_wx, lru_bx,
                    lru_lambda, attn_sink, pool_w, pool_scale, ffn_conv, ffn_conv_b):
    def per_head(v):
        return v.reshape(DEPTH, 2, LRU_HEADS, LRU_BLOCK).transpose(0, 2, 1, 3)

    row = lambda v: v[:, None, :]
    wg = jnp.concatenate([lru_wa[:, 0], lru_wx[:, 0], lru_wa[:, 1], lru_wx[:, 1]], axis=-1).astype(BF16)
    ba = per_head(lru_ba)
    bx = per_head(lru_bx)
    bg = jnp.concatenate([ba[:, :, 0], bx[:, :, 0], ba[:, :, 1], bx[:, :, 1]], axis=-1)[:, :, None, :]
    lam = per_head(lru_lambda).reshape(DEPTH, LRU_HEADS, 1, 2 * LRU_BLOCK)
    return {
        "norm1": row(norm1), "norm2": row(norm2), "b_gate": row(b_gate),
        "lru": {"conv_w": lru_conv, "conv_b": row(lru_conv_b), "wg": wg, "bg": bg, "lam": lam},
        "sink": attn_sink, "pool_w": pool_w.astype(BF16), "pool_scale": row(pool_scale),
        "ffn_conv": ffn_conv, "ffn_conv_b": row(ffn_conv_b),
    }


def kernel(x_prompt, x_sample, cache_k, cache_v, state_lru, c, c_ctx, w_ada, b_ada, norm1, norm2, w_in,
           b_gate, lru_conv, lru_conv_b, lru_wa, lru_ba, lru_wx, lru_bx, lru_lambda, attn_sink, pool_w,
           pool_scale, w_branch, w_out, ffn_up, ffn_conv, ffn_conv_b, ffn_down, final_norm):
    batch, seq, _ = x_prompt.shape
    dec_batch, dec_seq, _ = x_sample.shape
    past = cache_k.shape[2]
    assert seq == SCAN_CHUNK and dec_seq % SCAN_ROWS == 0 and (batch * seq) % SCAN_ROWS == 0

    c_rows = jnp.concatenate(
        [c_ctx[None], c, jnp.zeros((SUBLANES - 1 - dec_batch, D_MODEL), F32)], axis=0)
    mods = _ada(c_rows, w_ada, b_ada)

    xp = x_prompt.reshape(batch * seq, D_MODEL)
    xs = x_sample.reshape(dec_batch * dec_seq, D_MODEL)
    fn = final_norm[None]
    p = _stacked_params(norm1, norm2, b_gate, lru_conv, lru_conv_b, lru_wa, lru_ba, lru_wx, lru_bx,
                        lru_lambda, attn_sink, pool_w, pool_scale, ffn_conv, ffn_conv_b)
    raw = {"w_in": w_in, "w_branch": w_branch, "w_out": w_out, "ffn_up": ffn_up, "ffn_down": ffn_down}
    w_in_l = w_in[0].astype(BF16)
    mod_rows = mods.reshape(DEPTH * SUBLANES, 1, 6 * D_MODEL)
    ck = cache_k.reshape(dec_batch, DEPTH, past, D_KV).astype(BF16)
    cv = cache_v.reshape(dec_batch, DEPTH, past, D_KV).astype(BF16)
    h0 = state_lru.transpose(1, 2, 0, 3)[:, :, :, None, :]
    caches, hs = None, []
    for l in range(DEPTH):
        last = fn if l == DEPTH - 1 else None
        wts = {"w_in": w_in_l}
        xp, k_all, v_all, h_fin, w_in_l = _trunk_layer(xp, mod_rows, l * SUBLANES, batch * seq, p, wts, l, seq,
                                                       None, last, caches, raw)
        caches = (k_all, v_all)
        xs = _trunk_layer(xs, mod_rows, l * SUBLANES + 1, dec_seq, p, wts, l, dec_seq, (ck, cv, h0), last)[0]
        hs.append(h_fin.transpose(1, 0, 2))
    y_prompt = xp.reshape(batch, seq, D_MODEL)
    y_sample = xs.reshape(dec_batch, dec_seq, D_MODEL)
    cache_dims = (batch, DEPTH, seq, N_KV_HEADS, HEAD_DIM)
    return (y_prompt, y_sample, k_all.reshape(cache_dims), v_all.reshape(cache_dims), jnp.stack(hs, axis=1))
```

```python
import functools

import numpy as np
import jax
import jax.numpy as jnp
from jax import lax
from jax.experimental import pallas as pl
from jax.experimental.pallas import tpu as pltpu

F32 = jnp.float32
BF16 = jnp.bfloat16

D_MODEL = 1024
DEPTH = 2
GRID_W = 64
EPS = 1e-6
N_BRANCH = 3
D_RNN = 1024
LRU_HEADS = 8
LRU_BLOCK = D_RNN // LRU_HEADS
LRU_C = 8.0
N_HEADS = 8
N_KV_HEADS = 2
KV_GROUPS = N_HEADS // N_KV_HEADS
HEAD_DIM = 128
D_KV = N_KV_HEADS * HEAD_DIM
WINDOW = 128
BLOCK_Q = 128
ROPE_BASE = 10000.0
NEG_INF = -1e30
D_POOL = 1024
POOL_WINDOWS = (2, 4, 8, 16)
POOL_GROUP = D_POOL // len(POOL_WINDOWS)
D_FF = 2816
D_IN = D_RNN + N_HEADS * HEAD_DIM + 2 * D_KV + D_POOL + N_BRANCH * D_MODEL

COL_XA = 0
COL_Q = COL_XA + D_RNN
COL_K = COL_Q + N_HEADS * HEAD_DIM
COL_V = COL_K + D_KV
COL_XC = COL_V + D_KV
COL_G = COL_XC + D_POOL
LOG2E = float(np.log2(np.e))

VMEM_LIMIT_BYTES = 52 * 1024 * 1024
SUBLANES = 8
LANES = 128
BF16_ROWS = 16

SCAN_CHUNK = 256
SCAN_PITCH = 260
SCAN_ROWS = 2048


def _params(*sem):
    return pltpu.CompilerParams(dimension_semantics=sem, vmem_limit_bytes=VMEM_LIMIT_BYTES)


def _dot(a, b):
    return jnp.dot(a, b, preferred_element_type=F32)


def _dot_nt(a, b):
    return lax.dot_general(a, b, (((1,), (1,)), ((), ())), preferred_element_type=F32)


def _sigmoid(z):
    return 0.5 * (1.0 + jnp.tanh(0.5 * z))


def _rms(x):
    return x * lax.rsqrt(jnp.mean(x * x, axis=-1, keepdims=True) + EPS)


def _ada_kernel(c_ref, w_ref, b_ref, o_ref):
    c = c_ref[...]
    s = c * _sigmoid(c)
    o_ref[0] = _dot(s.astype(BF16), w_ref[0].astype(BF16)) + b_ref[0]


def _ada(c_rows, w_ada, b_ada):
    tn = 1536
    return pl.pallas_call(
        _ada_kernel,
        grid=(DEPTH, 6 * D_MODEL // tn),
        in_specs=[
            pl.BlockSpec((SUBLANES, D_MODEL), lambda l, j: (0, 0)),
            pl.BlockSpec((1, D_MODEL, tn), lambda l, j: (l, 0, j)),
            pl.BlockSpec((1, 1, tn), lambda l, j: (l, 0, j)),
        ],
        out_specs=pl.BlockSpec((1, SUBLANES, tn), lambda l, j: (l, 0, j)),
        out_shape=jax.ShapeDtypeStruct((DEPTH, SUBLANES, 6 * D_MODEL), F32),
        compiler_params=_params("parallel", "parallel"),
        name="ada_mod",
    )(c_rows, w_ada, b_ada.reshape(DEPTH, 1, 6 * D_MODEL))


INPROJ_TM = 512
INPROJ_CK = 2 * D_KV


def _resident(shape, layer):
    ndim = len(shape)
    return pl.BlockSpec((None,) + tuple(shape), lambda *_: (layer,) + (0,) * ndim,
                        pipeline_mode=pl.Buffered(1))


def _whole(shape):
    return pl.BlockSpec(tuple(shape), lambda *_: (0,) * len(shape), pipeline_mode=pl.Buffered(1))


def _side_cast_specs(side, steps):
    in_specs = [pl.BlockSpec((rows, arr.shape[1]), lambda i, first=first: (first + i, 0))
                for arr, rows, first in side]
    out_specs = [pl.BlockSpec((rows, arr.shape[1]), lambda i: (i, 0)) for arr, rows, _ in side]
    out_shapes = [jax.ShapeDtypeStruct((rows * steps, arr.shape[1]), BF16) for arr, rows, _ in side]
    return in_specs, out_specs, out_shapes


def _run_side_casts(refs, n_in, n_out, n_side):
    refs = list(refs)
    side_in = refs[n_in:n_in + n_side]
    side_out = refs[n_in + n_side + n_out:n_in + 2 * n_side + n_out]
    for src, dst in zip(side_in, side_out):
        dst[...] = src[...].astype(BF16)
    return refs[:n_in] + refs[n_in + n_side:n_in + n_side + n_out] + refs[n_in + 2 * n_side + n_out:]


def _layer_slabs(stacked, layer, steps):
    cols = stacked.shape[-1]
    rows = int(np.prod(stacked.shape[1:-1]))
    assert rows % (steps * BF16_ROWS) == 0
    return (stacked.reshape(stacked.shape[0] * rows, cols), rows // steps, layer * steps)


def _mod_spec(row0, per):
    return pl.BlockSpec((1, 1, 6 * D_MODEL), lambda i: (row0 + i // per, 0, 0))


def _inproj_kernel(*refs, seq_len, want_kv, layer, owns_cache, n_in, n_side):
    refs = _run_side_casts(refs, n_in, 5 if want_kv else 1, n_side)
    _inproj_body(*refs, seq_len=seq_len, want_kv=want_kv, layer=layer, owns_cache=owns_cache)


def _inproj_body(*refs, seq_len, want_kv, layer, owns_cache):
    if want_kv:
        x_ref, mod_ref, g_ref, w_ref, sink_ref, band_ref, pw_ref, ps_ref = refs[:8]
        o_ref, k_ref, v_ref, y_ref, yc_ref = refs[-5:]
        q_chunks = []
        if owns_cache:
            for other in range(DEPTH):
                if other != layer:
                    k_ref[:, other] = jnp.zeros(k_ref.shape[:1] + k_ref.shape[2:], F32)
                    v_ref[:, other] = jnp.zeros(v_ref.shape[:1] + v_ref.shape[2:], F32)
    else:
        x_ref, mod_ref, g_ref, w_ref, cos_ref, sa_ref, sb_ref, o_ref = refs
    mod = mod_ref[0]
    shift = mod[:, 0:D_MODEL]
    scale = mod[:, D_MODEL:2 * D_MODEL]
    h = (_rms(x_ref[...]) * g_ref[...] * (1.0 + scale) + shift).astype(BF16)
    nf = HEAD_DIM // 4

    def rope(x):
        return (x * cos_ref[...] + pltpu.roll(x, HEAD_DIM - nf, 1) * sa_ref[...]
                + pltpu.roll(x, nf, 1) * sb_ref[...])

    for c in range(D_IN // INPROJ_CK):
        acc = _dot(h, w_ref[:, c * INPROJ_CK:(c + 1) * INPROJ_CK])
        if not want_kv:
            lo = c * INPROJ_CK
            heads = [acc[:, j * HEAD_DIM:(j + 1) * HEAD_DIM] for j in range(INPROJ_CK // HEAD_DIM)]
            heads = [rope(hd) if COL_Q <= lo + j * HEAD_DIM < COL_V else hd for j, hd in enumerate(heads)]
            heads = [hd * SOFTMAX_SCALE if COL_Q <= lo + j * HEAD_DIM < COL_K else hd
                     for j, hd in enumerate(heads)]
            acc = jnp.concatenate(heads, axis=1)
        o_ref[:, c * INPROJ_CK:(c + 1) * INPROJ_CK] = acc.astype(o_ref.dtype)
        if want_kv and COL_Q <= c * INPROJ_CK < COL_K:
            q_chunks.append((acc * SOFTMAX_SCALE).astype(BF16))
        if want_kv and COL_XC <= c * INPROJ_CK < COL_G:
            xc = acc.astype(BF16)
            for j in range(INPROJ_CK // POOL_GROUP):
                gi = (c * INPROJ_CK - COL_XC) // POOL_GROUP + j
                gc = slice(gi * POOL_GROUP, (gi + 1) * POOL_GROUP)
                edges = _pool_edge_weights(POOL_WINDOWS[gi])
                for b in range(INPROJ_TM // seq_len):
                    rows = slice(b * seq_len, (b + 1) * seq_len)
                    x = xc[rows, j * POOL_GROUP:(j + 1) * POOL_GROUP]
                    yc = _pool_tile(_dot(band_ref[gi], x), x, POOL_WINDOWS[gi], edges, True, True,
                                    pw_ref[gi], ps_ref[:, gc])
                    yc_ref[rows, gc] = yc.astype(yc_ref.dtype)
        if want_kv and c == COL_K // INPROJ_CK:
            for b in range(INPROJ_TM // seq_len):
                rows = slice(b * seq_len, (b + 1) * seq_len)
                for hd in range(N_KV_HEADS):
                    dst = pl.ds(hd, seq_len, stride=N_KV_HEADS)
                    at = (b, layer, dst, slice(None)) if owns_cache else (b, dst, slice(None))
                    k_ref[at] = acc[rows, hd * HEAD_DIM:(hd + 1) * HEAD_DIM]
                    v_ref[at] = acc[rows, D_KV + hd * HEAD_DIM:D_KV + (hd + 1) * HEAD_DIM]
            kv = acc.astype(BF16)

            def scores(b, kh):
                rows = slice(b * seq_len, (b + 1) * seq_len)
                q = jnp.concatenate([q_chunks[kh][rows, g * HEAD_DIM:(g + 1) * HEAD_DIM]
                                     for g in range(KV_GROUPS)], axis=0)
                return (b, kh, _dot_nt(kv[rows, kh * HEAD_DIM:(kh + 1) * HEAD_DIM], q),
                        kv[rows, D_KV + kh * HEAD_DIM:D_KV + (kh + 1) * HEAD_DIM])

            order = [(b, kh) for b in range(INPROJ_TM // seq_len) for kh in range(N_KV_HEADS)]
            pending = [scores(*ch) for ch in order[:ATTN_LOOKAHEAD]]
            for n in range(len(order)):
                if n + ATTN_LOOKAHEAD < len(order):
                    pending.append(scores(*order[n + ATTN_LOOKAHEAD]))
                b, kh, t, vals = pending.pop(0)
                o_t = _softmax_pv_t(t, _sink_row(sink_ref, layer, kh, seq_len), vals)
                _store_heads_t(y_ref, o_t, kh, b * seq_len, seq_len)


def _inproj(x, mod, mod_row0, rows_per_mod, g, w, layer, seq_len, want_kv, sink=None, pool=None, caches=None,
            side=()):
    m = x.shape[0]
    tm = INPROJ_TM
    per = rows_per_mod // tm
    proj_spec = pl.BlockSpec((tm, D_IN), lambda i: (i, 0))
    proj_shape = jax.ShapeDtypeStruct((m, D_IN), BF16)
    aliases = {}
    owns_cache = want_kv and caches is None
    if want_kv:
        nb = tm // seq_len
        if owns_cache:
            cache_spec = pl.BlockSpec((nb, DEPTH, seq_len * N_KV_HEADS, HEAD_DIM), lambda i: (i, 0, 0, 0))
        else:
            cache_spec = pl.BlockSpec((nb, None, seq_len * N_KV_HEADS, HEAD_DIM), lambda i: (i, layer, 0, 0))
        cache_shape = jax.ShapeDtypeStruct((m // seq_len, DEPTH, seq_len * N_KV_HEADS, HEAD_DIM), F32)
        assert tm % seq_len == 0 and INPROJ_CK == KV_GROUPS * HEAD_DIM and COL_Q % INPROJ_CK == 0
        assert seq_len == POOL_TILE and COL_XC % INPROJ_CK == 0 and INPROJ_CK % POOL_GROUP == 0
        mix_spec = pl.BlockSpec((tm, D_MODEL), lambda i: (i, 0))
        mix_shape = jax.ShapeDtypeStruct((m, D_MODEL), BF16)
        out_specs = (proj_spec, cache_spec, cache_spec, mix_spec, mix_spec)
        out_shape = (proj_shape, cache_shape, cache_shape, mix_shape, mix_shape)
        nwin = len(POOL_WINDOWS)
        extra_specs = [pl.BlockSpec(memory_space=pltpu.SMEM), _whole((nwin, POOL_TILE, POOL_TILE)),
                       _resident((nwin, POOL_GROUP, POOL_GROUP), layer), _resident((1, D_POOL), layer)]
        extra_args = [sink, _pool_bands(seq_len), pool[0], pool[1]]
        if caches is not None:
            aliases = {4 + len(extra_args): 1, 5 + len(extra_args): 2}
            extra_specs += [pl.BlockSpec(memory_space=pl.ANY)] * 2
            extra_args += list(caches)
    else:
        out_specs, out_shape = (proj_spec,), (proj_shape,)
        tab = pl.BlockSpec((tm, HEAD_DIM), lambda i: (i % (seq_len // tm), 0))
        extra_specs, extra_args = [tab, tab, tab], list(_rope_tables(seq_len))
    side_in, side_out, side_shapes = _side_cast_specs(side, m // tm)
    out = pl.pallas_call(
        functools.partial(_inproj_kernel, seq_len=seq_len, want_kv=want_kv, layer=layer, owns_cache=owns_cache,
                          n_in=4 + len(extra_specs), n_side=len(side)),
        grid=(m // tm,),
        in_specs=[
            pl.BlockSpec((tm, D_MODEL), lambda i: (i, 0)),
            _mod_spec(mod_row0, per),
            _resident((1, D_MODEL), layer),
            _whole((D_MODEL, D_IN)),
        ] + extra_specs + side_in,
        out_specs=tuple(out_specs) + tuple(side_out),
        out_shape=tuple(out_shape) + tuple(side_shapes),
        input_output_aliases=aliases,
        compiler_params=_params("parallel"),
        name="in_proj",
    )(x, mod, g, w, *extra_args, *[item[0] for item in side])
    n_own = len(out_specs)
    own = tuple(out[:n_own]) if want_kv else (out[0], None, None, None, None)
    return own + (list(out[n_own:]),)


LRU_HEADS_PER_STEP = 4


def _chunk_neighbour(v, towards_later):
    sub = lax.broadcasted_iota(jnp.int32, v.shape, 0)
    if towards_later:
        return jnp.where(sub >= 1, pltpu.roll(v, 1, 0), 0.0)
    return jnp.where(sub <= SUBLANES - 2, pltpu.roll(v, SUBLANES - 1, 0), 0.0)


def _shift_time(x_tm, k, chained):
    n = x_tm.shape[0]
    steps = abs(k)
    edge = []
    for s in range(steps):
        if not chained:
            edge.append(jnp.zeros((SUBLANES, x_tm.shape[1]), F32))
        elif k > 0:
            src = n - (steps - s) * SUBLANES
            edge.append(_chunk_neighbour(x_tm[src:src + SUBLANES], True))
        else:
            edge.append(_chunk_neighbour(x_tm[s * SUBLANES:(s + 1) * SUBLANES], False))
    if k > 0:
        return jnp.concatenate(edge + [x_tm[:n - steps * SUBLANES]], axis=0)
    return jnp.concatenate([x_tm[steps * SUBLANES:]] + edge, axis=0)


def _lru_tm_kernel(*refs, seq_len, latent):
    if latent:
        (x_ref, cw_ref, cb_ref, wg_ref, bg_ref, lam_ref, h0_ref, y_ref,
         io_ref, af_ref, bf_ref, ab_ref, bb_ref, hf_ref, hb_ref) = refs
    else:
        (x_ref, cw_ref, cb_ref, wg_ref, bg_ref, lam_ref, y_ref, fin_ref,
         io_ref, af_ref, bf_ref, ab_ref, bb_ref, hf_ref, hb_ref) = refs
    rows = SCAN_ROWS
    nchunk = rows // SCAN_CHUNK
    nhead = LRU_HEADS_PER_STEP
    chained = seq_len > SCAN_CHUNK
    for hd in range(nhead):
        lanes = slice(hd * LRU_BLOCK, (hd + 1) * LRU_BLOCK)
        x = x_ref[:, lanes].astype(F32)
        for c in range(nchunk):
            io_ref[hd, pl.ds(c * SCAN_PITCH, SCAN_CHUNK), :] = x[c * SCAN_CHUNK:(c + 1) * SCAN_CHUNK]
        x = jnp.concatenate(
            [io_ref[hd, pl.ds(t, nchunk, stride=SCAN_PITCH), :] for t in range(SCAN_CHUNK)], axis=0)
        cw = 0.5 * cw_ref[:, lanes]
        xh = (_shift_time(x, 2, chained) * cw[0:1] + _shift_time(x, 1, chained) * cw[1:2]
              + x * cw[2:3] + _shift_time(x, -1, chained) * cw[3:4] + 0.5 * cb_ref[:, lanes])
        half_bias = 0.5 * bg_ref[hd]
        bias_hi = half_bias.astype(BF16).astype(F32)
        bias_rows = jnp.concatenate(
            [bias_hi, half_bias - bias_hi, jnp.zeros((BF16_ROWS - 2, 4 * LRU_BLOCK), F32)], axis=0)
        lhs = jnp.concatenate([xh.astype(BF16), jnp.ones((rows, BF16_ROWS), BF16)], axis=1)
        rhs = jnp.concatenate([wg_ref[hd], bias_rows.astype(BF16)], axis=0)
        th = jnp.tanh(_dot(lhs, rhs))
        lam = lam_ref[hd]
        for d, (a_ref, b_ref) in enumerate(((af_ref, bf_ref), (ab_ref, bb_ref))):
            th_r = th[:, 2 * d * LRU_BLOCK:(2 * d + 1) * LRU_BLOCK]
            th_i = th[:, (2 * d + 1) * LRU_BLOCK:(2 * d + 2) * LRU_BLOCK]
            nl = -lam[:, d * LRU_BLOCK:(d + 1) * LRU_BLOCK]
            softplus = jnp.maximum(nl, 0.0) + jnp.log(1.0 + jnp.exp(-jnp.abs(nl)))
            ch = (-0.5 * LRU_C * LOG2E) * softplus
            a = jnp.exp2(ch + ch * th_r)
            y = 1.0 - a * a
            a_ref[hd] = a
            b_ref[hd] = (y * lax.rsqrt(jnp.maximum(y, 1e-30))) * ((1.0 + th_i) * xh)

    def step(t, carry):
        rf = pl.ds(t * nchunk, nchunk)
        rb = pl.ds((SCAN_CHUNK - 1 - t) * nchunk, nchunk)
        out = []
        for hd in range(nhead):
            hf, hb, pf, pb = carry[4 * hd:4 * hd + 4]
            a_f = af_ref[hd, rf, :]
            a_b = ab_ref[hd, rb, :]
            hf = a_f * hf + bf_ref[hd, rf, :]
            hb = a_b * hb + bb_ref[hd, rb, :]
            hf_ref[hd, rf, :] = hf
            hb_ref[hd, rb, :] = hb
            if chained:
                pf = pf * a_f
                pb = pb * a_b
                af_ref[hd, rf, :] = pf
                ab_ref[hd, rb, :] = pb
            out += [hf, hb, pf, pb]
        return tuple(out)

    zero = jnp.zeros((nchunk, LRU_BLOCK), F32)
    one = jnp.ones((nchunk, LRU_BLOCK), F32)
    carry = (zero, zero, one, one) * nhead
    for t in range(SCAN_CHUNK):
        carry = step(t, carry)

    sub = lax.broadcasted_iota(jnp.int32, (nchunk, LRU_BLOCK), 0)
    last = slice(rows - nchunk, rows)
    first = slice(0, nchunk)
    for hd in range(nhead):
        lanes = slice(hd * LRU_BLOCK, (hd + 1) * LRU_BLOCK)
        hf = hf_ref[hd]
        hb = hb_ref[hd]
        if chained:
            pf = af_ref[hd]
            pb = ab_ref[hd]
            ef = jnp.where(sub == 0, h0_ref[0, 0][:, lanes], 0.0)
            eb = jnp.where(sub == nchunk - 1, h0_ref[1, 0][:, lanes], 0.0)
            for c in range(1, nchunk):
                ef = jnp.where(sub == c, pltpu.roll(hf[last] + pf[last] * ef, 1, 0), ef)
                eb = jnp.where(sub == nchunk - 1 - c,
                               pltpu.roll(hb[first] + pb[first] * eb, nchunk - 1, 0), eb)
            hf = (hf.reshape(SCAN_CHUNK, nchunk, LRU_BLOCK)
                  + pf.reshape(SCAN_CHUNK, nchunk, LRU_BLOCK) * ef[None]).reshape(rows, LRU_BLOCK)
            hb = (hb.reshape(SCAN_CHUNK, nchunk, LRU_BLOCK)
                  + pb.reshape(SCAN_CHUNK, nchunk, LRU_BLOCK) * eb[None]).reshape(rows, LRU_BLOCK)
        else:
            fin_ref[0, :, lanes] = hf[last]
            fin_ref[1, :, lanes] = hb[first]
        y = hf + hb
        for t in range(SCAN_CHUNK):
            io_ref[hd, pl.ds(t, nchunk, stride=SCAN_PITCH), :] = y[t * nchunk:(t + 1) * nchunk]
        for c in range(nchunk):
            y_ref[pl.ds(c * SCAN_CHUNK, SCAN_CHUNK), lanes] = (
                io_ref[hd, pl.ds(c * SCAN_PITCH, SCAN_CHUNK), :].astype(y_ref.dtype))


def _lru(proj, lp, layer, seq_len, h0):
    m = proj.shape[0]
    latent = h0 is not None
    rows = SCAN_ROWS
    nhead = LRU_HEADS_PER_STEP
    width = nhead * LRU_BLOCK
    in_specs = [
        pl.BlockSpec((rows, width), lambda r, h: (r, COL_XA // width + h)),
        pl.BlockSpec((None, 4, width), lambda r, h: (layer, 0, h)),
        pl.BlockSpec((None, 1, width), lambda r, h: (layer, 0, h)),
        pl.BlockSpec((None, nhead, LRU_BLOCK, 4 * LRU_BLOCK), lambda r, h: (layer, h, 0, 0)),
        pl.BlockSpec((None, nhead, 1, 4 * LRU_BLOCK), lambda r, h: (layer, h, 0, 0)),
        pl.BlockSpec((None, nhead, 1, 2 * LRU_BLOCK), lambda r, h: (layer, h, 0, 0)),
    ]
    args = [proj, lp["conv_w"], lp["conv_b"], lp["wg"], lp["bg"], lp["lam"]]
    y_spec = pl.BlockSpec((rows, width), lambda r, h: (r, h))
    y_shape = jax.ShapeDtypeStruct((m, D_RNN), BF16)
    if latent:
        in_specs.append(pl.BlockSpec((None, 2, 1, 1, width), lambda r, h: (layer, 0, r, 0, h)))
        args.append(h0)
        out_specs, out_shape = y_spec, y_shape
    else:
        nseq = m // seq_len
        out_specs = (y_spec, pl.BlockSpec((2, rows // seq_len, width), lambda r, h: (0, r, h)))
        out_shape = (y_shape, jax.ShapeDtypeStruct((2, nseq, D_RNN), F32))
    strided_buf = pltpu.VMEM((nhead, rows // SCAN_CHUNK * SCAN_PITCH, LRU_BLOCK), F32)
    scan_buf = pltpu.VMEM((nhead, rows, LRU_BLOCK), F32)
    return pl.pallas_call(
        functools.partial(_lru_tm_kernel, seq_len=seq_len, latent=latent),
        grid=(m // rows, LRU_HEADS // nhead),
        in_specs=in_specs,
        out_specs=out_specs,
        out_shape=out_shape,
        scratch_shapes=[strided_buf] + [scan_buf] * 6,
        compiler_params=_params("parallel", "parallel"),
        name="rglru_latent" if latent else "rglru_context",
    )(*args)


SOFTMAX_SCALE = HEAD_DIM ** -0.5 * LOG2E


def _stack_heads(q_ref, kh, r0, rows):
    parts = [q_ref[r0:r0 + rows, (kh * KV_GROUPS + g) * HEAD_DIM:(kh * KV_GROUPS + g + 1) * HEAD_DIM]
             for g in range(KV_GROUPS)]
    return jnp.concatenate(parts, axis=0)


def _dot_tn(a, b):
    return lax.dot_general(a, b, (((0,), (0,)), ((), ())), preferred_element_type=F32)


def _sink_row(sink_ref, layer, kh, cols):
    parts = [jnp.full((1, cols), sink_ref[layer, kh * KV_GROUPS + g] * LOG2E, F32) for g in range(KV_GROUPS)]
    return jnp.concatenate(parts, axis=1)


def _softmax_pv_t(t, sink, v):
    m = jnp.maximum(jnp.max(t, axis=0, keepdims=True), sink)
    e = jnp.exp2(t - m).astype(BF16)
    d = v.shape[1]
    v_ones = jnp.concatenate([v, jnp.ones((v.shape[0], SUBLANES), BF16)], axis=1)
    o_sum = _dot_tn(v_ones, e)
    denom = o_sum[d:d + 1] + jnp.exp2(sink - m)
    return o_sum[:d] * (1.0 / denom)


def _store_heads_t(y_ref, o_t, kh, r0, rows):
    for g in range(KV_GROUPS):
        h = kh * KV_GROUPS + g
        y_ref[r0:r0 + rows, h * HEAD_DIM:(h + 1) * HEAD_DIM] = (
            o_t[:, g * rows:(g + 1) * rows].T.astype(y_ref.dtype))


def _rope_tables(seq_len):
    nf = HEAD_DIM // 4
    freqs = ROPE_BASE ** (-np.arange(nf, dtype=np.float64) / nf)
    t = np.arange(seq_len)
    ang_row = (t // GRID_W)[:, None] * freqs[None, :]
    ang_col = (t % GRID_W)[:, None] * freqs[None, :]
    ang = np.concatenate([ang_row, ang_row, ang_col, ang_col], axis=1)
    first = (np.arange(HEAD_DIM) % (2 * nf)) < nf
    cos = np.cos(ang)
    sin = np.sin(ang)
    sin_a = np.where(first[None, :], -sin, 0.0)
    sin_b = np.where(first[None, :], 0.0, sin)
    return tuple(jnp.asarray(a, F32) for a in (cos, sin_a, sin_b))


ATTN_LAT_BLOCKS = 4
ATTN_LOOKAHEAD = 3


def _attn_lat_kernel(sink_ref, q_ref, kvp_ref, kvc_ref, kvn_ref, ck_ref, cv_ref, y_ref, *, nblk, layer):
    step = pl.program_id(1)
    cols = KV_GROUPS * BLOCK_Q
    span = 3 * BLOCK_Q
    key = lax.broadcasted_iota(jnp.int32, (span, cols), 0)
    qry = lax.broadcasted_iota(jnp.int32, (span, cols), 1) & (BLOCK_Q - 1)
    kv = jnp.concatenate([kvp_ref[...], kvc_ref[...], kvn_ref[...]], axis=0)
    def scores(i, kh):
        win = kv[i * BLOCK_Q:i * BLOCK_Q + span]
        sl = slice(kh * HEAD_DIM, (kh + 1) * HEAD_DIM)
        vl = slice(D_KV + kh * HEAD_DIM, D_KV + (kh + 1) * HEAD_DIM)
        q = _stack_heads(q_ref, kh, i * BLOCK_Q, BLOCK_Q)
        keys = jnp.concatenate([win[:, sl], ck_ref[0, :, sl]], axis=0)
        vals = jnp.concatenate([win[:, vl], cv_ref[0, :, sl]], axis=0)
        return i, kh, _dot_nt(keys, q), vals

    order = [(i, kh) for i in range(ATTN_LAT_BLOCKS) for kh in range(N_KV_HEADS)]
    pending = [scores(*c) for c in order[:ATTN_LOOKAHEAD]]
    for n in range(len(order)):
        if n + ATTN_LOOKAHEAD < len(order):
            pending.append(scores(*order[n + ATTN_LOOKAHEAD]))
        i, kh, t, vals = pending.pop(0)
        j = step * ATTN_LAT_BLOCKS + i
        lo = jnp.where(j > 0, qry, BLOCK_Q)
        hi = jnp.where(j < nblk - 1, qry + 2 * BLOCK_Q, 2 * BLOCK_Q - 1)
        bias = jnp.where(jnp.logical_and(key >= lo, key <= hi), 0.0, NEG_INF)
        t = jnp.concatenate([t[:span] + bias, t[span:]], axis=0)
        o_t = _softmax_pv_t(t, _sink_row(sink_ref, layer, kh, BLOCK_Q), vals)
        _store_heads_t(y_ref, o_t, kh, i * BLOCK_Q, BLOCK_Q)


def _attn_lat(proj, ck, cv, sink, layer, seq_len):
    m = proj.shape[0]
    nblk = seq_len // BLOCK_Q
    nstep = nblk // ATTN_LAT_BLOCKS
    rows = ATTN_LAT_BLOCKS * BLOCK_Q
    nb = m // seq_len
    past = ck.shape[2]
    kv_col = COL_K // (2 * D_KV)
    assert COL_V == COL_K + D_KV and COL_K % (2 * D_KV) == 0

    def halo(shift):
        def index(b, s):
            return (b * nblk + jnp.clip(s * ATTN_LAT_BLOCKS + shift, 0, nblk - 1), kv_col)
        return index

    return pl.pallas_call(
        functools.partial(_attn_lat_kernel, nblk=nblk, layer=layer),
        grid=(nb, nstep),
        in_specs=[
            pl.BlockSpec(memory_space=pltpu.SMEM),
            pl.BlockSpec((rows, N_HEADS * HEAD_DIM), lambda b, s: (b * nstep + s, COL_Q // (N_HEADS * HEAD_DIM))),
            pl.BlockSpec((BLOCK_Q, 2 * D_KV), halo(-1)),
            pl.BlockSpec((rows, 2 * D_KV), lambda b, s: (b * nstep + s, kv_col)),
            pl.BlockSpec((BLOCK_Q, 2 * D_KV), halo(ATTN_LAT_BLOCKS)),
            pl.BlockSpec((1, None, past, D_KV), lambda b, s: (b, layer, 0, 0)),
            pl.BlockSpec((1, None, past, D_KV), lambda b, s: (b, layer, 0, 0)),
        ],
        out_specs=pl.BlockSpec((rows, N_HEADS * HEAD_DIM), lambda b, s: (b * nstep + s, 0)),
        out_shape=jax.ShapeDtypeStruct((m, N_HEADS * HEAD_DIM), BF16),
        compiler_params=_params("parallel", "parallel"),
        name="attn_latent",
    )(sink, proj, proj, proj, proj, ck, cv)


POOL_TILE = 256
POOL_LEAD = BF16_ROWS
POOL_ROWS = 2048


def _pool_plan(seq_len):
    lead = 0 if seq_len == POOL_TILE else POOL_LEAD
    return lead, POOL_TILE - 2 * lead


def _pool_bands(seq_len):
    lead, nout = _pool_plan(seq_len)
    r = np.arange(nout)[:, None]
    c = np.arange(POOL_TILE)[None, :] - lead
    bands = [(c >= r - win // 2) & (c < r + win // 2) for win in POOL_WINDOWS]
    return jnp.asarray(np.stack(bands), BF16)


def _pool_edge_weights(win):
    half = win // 2
    edge = lax.broadcasted_iota(jnp.int32, (SUBLANES, POOL_GROUP), 0)
    inv_head = 1.0 / ((edge + half) - jnp.maximum(edge - half, 0)).astype(F32)
    inv_tail = 1.0 / (jnp.minimum(SUBLANES - edge, half) + half).astype(F32)
    return inv_head, inv_tail


def _pool_tile(sums, x, win, edges, at_start, at_end, w, scale):
    n = sums.shape[0]
    head = sums[:SUBLANES] * (edges[0] if at_start else 1.0 / win)
    tail = sums[n - SUBLANES:] * (edges[1] if at_end else 1.0 / win)
    mean = jnp.concatenate([head, sums[SUBLANES:n - SUBLANES] * (1.0 / win), tail], axis=0)
    pooled = (mean - x.astype(F32)).astype(BF16)
    return _dot(pooled, w) * scale


def _pool_kernel(x0_ref, x1_ref, x2_ref, x3_ref, band_ref, w_ref, s_ref, y_ref, pad_ref, *, seq_len):
    lead, nout = _pool_plan(seq_len)
    if lead:
        pad_ref[0:lead, :] = jnp.zeros((lead, POOL_GROUP), BF16)
        pad_ref[lead + seq_len:, :] = jnp.zeros((pad_ref.shape[0] - lead - seq_len, POOL_GROUP), BF16)
    for gi, (win, x_ref) in enumerate(zip(POOL_WINDOWS, (x0_ref, x1_ref, x2_ref, x3_ref))):
        cs = slice(gi * POOL_GROUP, (gi + 1) * POOL_GROUP)
        edges = _pool_edge_weights(win)
        if lead:
            pad_ref[lead:lead + seq_len, :] = x_ref[...]
        for base in range(0, POOL_ROWS, seq_len):
            for p0 in range(0, seq_len, nout):
                n = min(nout, seq_len - p0)
                x = x_ref[base + p0:base + p0 + n, :]
                src = pad_ref[p0:p0 + POOL_TILE, :] if lead else x
                sums = _dot(band_ref[gi, :n, :], src)
                y = _pool_tile(sums, x, win, edges, p0 == 0, p0 + n == seq_len, w_ref[gi], s_ref[:, cs])
                y_ref[base + p0:base + p0 + n, cs] = y.astype(y_ref.dtype)


def _pool(proj, w, s, layer, seq_len):
    m = proj.shape[0]
    rows = POOL_ROWS
    lead, nout = _pool_plan(seq_len)
    assert seq_len in (POOL_TILE, rows) and max(POOL_WINDOWS) // 2 <= min(SUBLANES, lead or SUBLANES)
    pad_rows = (pl.cdiv(seq_len, nout) - 1) * nout + POOL_TILE
    group = lambda gi: pl.BlockSpec((rows, POOL_GROUP), lambda r: (r, COL_XC // POOL_GROUP + gi))
    whole = lambda shape: pl.BlockSpec(shape, lambda r: (0,) * len(shape))
    nwin = len(POOL_WINDOWS)
    return pl.pallas_call(
        functools.partial(_pool_kernel, seq_len=seq_len),
        grid=(m // rows,),
        in_specs=[
            group(0), group(1), group(2), group(3),
            whole((nwin, nout, POOL_TILE)),
            _resident((nwin, POOL_GROUP, POOL_GROUP), layer),
            _resident((1, D_POOL), layer),
        ],
        out_specs=pl.BlockSpec((rows, D_POOL), lambda r: (r, 0)),
        out_shape=jax.ShapeDtypeStruct((m, D_POOL), BF16),
        scratch_shapes=[pltpu.VMEM((pad_rows, POOL_GROUP), BF16)],
        compiler_params=_params("parallel"),
        name="pool_mix",
    )(proj, proj, proj, proj, _pool_bands(seq_len), w, s)


MERGE_N_IN = 15
MERGE_TM = 512


def _merge_kernel(*refs, n_side):
    (ya_ref, yb_ref, yc_ref, g0_ref, g1_ref, g2_ref, g3_ref, g4_ref, g5_ref, x_ref, mod_ref,
     bg_ref, wb_ref, wo_ref, n2_ref, x1_ref, h2_ref) = _run_side_casts(refs, MERGE_N_IN, 2, n_side)
    mod = mod_ref[0]
    g_refs = (g0_ref, g1_ref, g2_ref, g3_ref, g4_ref, g5_ref)
    half = D_MODEL // 2
    merged = None
    for k, y_ref in enumerate((ya_ref, yb_ref, yc_ref)):
        y = _dot(y_ref[...], wb_ref[k])
        parts = []
        for p in range(2):
            z = g_refs[2 * k + p][...].astype(F32) + bg_ref[:, k * D_MODEL + p * half:k * D_MODEL + (p + 1) * half]
            parts.append((1.0 + jnp.tanh(0.5 * z)) * y[:, p * half:(p + 1) * half])
        term = jnp.concatenate(parts, axis=1)
        merged = term if merged is None else merged + term
    merged = 0.5 * merged
    gate1 = mod[:, 2 * D_MODEL:3 * D_MODEL]
    x1 = x_ref[...] + gate1 * _dot(merged.astype(BF16), wo_ref[...])
    x1_ref[...] = x1
    shift2 = mod[:, 3 * D_MODEL:4 * D_MODEL]
    scale2 = mod[:, 4 * D_MODEL:5 * D_MODEL]
    h2_ref[...] = (_rms(x1) * n2_ref[...] * (1.0 + scale2) + shift2).astype(BF16)


def _merge(ya, yb, yc, proj, x, mod, mod_row0, rows_per_mod, bg, wb, wo, n2, layer, side=()):
    m = x.shape[0]
    tm = MERGE_TM
    per = rows_per_mod // tm
    half = D_MODEL // 2
    row = pl.BlockSpec((tm, D_MODEL), lambda i: (i, 0))
    gate = lambda c: pl.BlockSpec((tm, half), lambda i: (i, COL_G // half + c))
    side_in, side_out, side_shapes = _side_cast_specs(side, m // tm)
    in_specs = [
        row, row, row,
        gate(0), gate(1), gate(2), gate(3), gate(4), gate(5),
        row,
        _mod_spec(mod_row0, per),
        _resident((1, N_BRANCH * D_MODEL), layer),
        _whole((N_BRANCH, D_MODEL, D_MODEL)),
        _whole((D_MODEL, D_MODEL)),
        _resident((1, D_MODEL), layer),
    ]
    assert len(in_specs) == MERGE_N_IN
    out = pl.pallas_call(
        functools.partial(_merge_kernel, n_side=len(side)),
        grid=(m // tm,),
        in_specs=in_specs + side_in,
        out_specs=(row, row) + tuple(side_out),
        out_shape=(jax.ShapeDtypeStruct((m, D_MODEL), F32), jax.ShapeDtypeStruct((m, D_MODEL), BF16))
        + tuple(side_shapes),
        compiler_params=_params("parallel"),
        name="merge_out",
    )(ya, yb, yc, proj, proj, proj, proj, proj, proj, x, mod, bg, wb, wo, n2, *[item[0] for item in side])
    return out[0], out[1], list(out[2:])


FFN_CK = 256
FFN_TM = 512
FFN_GAP = SUBLANES
GELU_C = float(np.sqrt(2.0 / np.pi))


def _ffn_kernel(*refs, seq_len, final, n_side):
    refs = _run_side_casts(refs, 9 + int(final), 1, n_side)
    h_ref, hp_ref, hn_ref, x_ref, mod_ref, wup_ref, cw_ref, cb_ref, wd_ref = refs[:9]
    fn_ref = refs[9] if final else None
    o_ref, hx_ref, u_ref, act_ref = refs[-4:]
    tm = FFN_TM
    halo = seq_len > tm
    i = pl.program_id(0)

    if halo:
        per_seq = seq_len // tm
        at_start = i % per_seq == 0
        at_end = i % per_seq == per_seq - 1
        zeros = jnp.zeros((BF16_ROWS, D_MODEL), BF16)

        @pl.when(at_start)
        def _():
            hx_ref[0:BF16_ROWS, :] = zeros

        @pl.when(jnp.logical_not(at_start))
        def _():
            hx_ref[0:BF16_ROWS, :] = hp_ref[...]

        @pl.when(at_end)
        def _():
            hx_ref[BF16_ROWS + tm:, :] = zeros

        @pl.when(jnp.logical_not(at_end))
        def _():
            hx_ref[BF16_ROWS + tm:, :] = hn_ref[...]

        hx_ref[BF16_ROWS:BF16_ROWS + tm, :] = h_ref[...]
        bases = (BF16_ROWS,)
        seg = tm
    else:
        nseg = tm // seq_len
        seg = seq_len
        bases = tuple(FFN_GAP + s * (seg + FFN_GAP) for s in range(nseg))
        for s in range(nseg + 1):
            u_ref[s * (seg + FFN_GAP):s * (seg + FFN_GAP) + FFN_GAP, :] = jnp.zeros((FFN_GAP, FFN_CK), F32)

    def taps(offset):
        return jnp.concatenate([u_ref[b + offset:b + offset + seg, :] for b in bases], axis=0)

    for c in range(D_FF // FFN_CK):
        cs = slice(c * FFN_CK, (c + 1) * FFN_CK)
        vs = slice(D_FF + c * FFN_CK, D_FF + (c + 1) * FFN_CK)
        if halo:
            u_ext = _dot(hx_ref[...], wup_ref[:, cs])
            u_ref[...] = u_ext
            u0 = u_ext[BF16_ROWS:BF16_ROWS + tm]
        else:
            u0 = _dot(h_ref[...], wup_ref[:, cs])
            for s, b in enumerate(bases):
                u_ref[b:b + seg, :] = u0[s * seg:(s + 1) * seg]
        uv = _dot(h_ref[...], wup_ref[:, vs])
        gff = taps(-1) * cw_ref[0:1, cs] + u0 * cw_ref[1:2, cs] + taps(1) * cw_ref[2:3, cs] + cb_ref[:, cs]
        inner = gff * (GELU_C + (GELU_C * 0.044715) * (gff * gff))
        act_ref[:, cs] = (0.5 * (gff * uv) * (1.0 + jnp.tanh(inner))).astype(BF16)

    gate2 = mod_ref[0][:, 5 * D_MODEL:6 * D_MODEL]
    out = x_ref[...] + gate2 * _dot(act_ref[...], wd_ref[...])
    if final:
        out = _rms(out) * fn_ref[...]
    o_ref[...] = out


def _ffn(h2, x1, mod, mod_row0, rows_per_mod, wup, cw, cb, wd, layer, seq_len, final_norm, side=()):
    m = x1.shape[0]
    tm = FFN_TM
    per = rows_per_mod // tm
    hb = tm // BF16_ROWS
    last_halo = m // BF16_ROWS - 1
    final = final_norm is not None
    halo = seq_len > tm
    in_specs = [
        pl.BlockSpec((tm, D_MODEL), lambda i: (i, 0)),
        pl.BlockSpec((BF16_ROWS, D_MODEL), lambda i: (jnp.maximum(i * hb - 1, 0), 0)),
        pl.BlockSpec((BF16_ROWS, D_MODEL), lambda i: (jnp.minimum((i + 1) * hb, last_halo), 0)),
        pl.BlockSpec((tm, D_MODEL), lambda i: (i, 0)),
        _mod_spec(mod_row0, per),
        _whole((D_MODEL, 2 * D_FF)),
        _resident((3, D_FF), layer),
        _resident((1, D_FF), layer),
        _whole((D_FF, D_MODEL)),
    ]
    args = [h2, h2, h2, x1, mod, wup, cw, cb, wd]
    if final:
        in_specs.append(_whole((1, D_MODEL)))
        args.append(final_norm)
    if halo:
        u_rows = tm + 2 * BF16_ROWS
    else:
        u_rows = FFN_GAP + (tm // seq_len) * (seq_len + FFN_GAP)
    side_in, side_out, side_shapes = _side_cast_specs(side, m // tm)
    out = pl.pallas_call(
        functools.partial(_ffn_kernel, seq_len=seq_len, final=final, n_side=len(side)),
        grid=(m // tm,),
        in_specs=in_specs + side_in,
        out_specs=(pl.BlockSpec((tm, D_MODEL), lambda i: (i, 0)),) + tuple(side_out),
        out_shape=(jax.ShapeDtypeStruct((m, D_MODEL), F32),) + tuple(side_shapes),
        scratch_shapes=[
            pltpu.VMEM((tm + 2 * BF16_ROWS, D_MODEL), BF16),
            pltpu.VMEM((u_rows, FFN_CK), F32),
            pltpu.VMEM((tm, D_FF), BF16),
        ],
        compiler_params=_params("parallel"),
        name="conv_glu_ffn",
    )(*args, *[item[0] for item in side])
    return out[0], list(out[1:])


def _trunk_layer(x, mod, mod_row0, rows_per_mod, p, wts, layer, seq_len, ctx, final_norm, caches=None,
                 raw=None):
    m = x.shape[0]
    cast = raw is not None
    side = [_layer_slabs(raw[k], layer, m // INPROJ_TM) for k in ("w_branch", "w_out")] if cast else ()
    proj, k_new, v_new, yb, yc, done = _inproj(
        x, mod, mod_row0, rows_per_mod, p["norm1"], wts["w_in"], layer, seq_len, want_kv=ctx is None,
        sink=p["sink"], pool=(p["pool_w"], p["pool_scale"]), caches=caches, side=side)
    if cast:
        wts["w_branch"] = done[0].reshape(N_BRANCH, D_MODEL, D_MODEL)
        wts["w_out"] = done[1]
    if ctx is None:
        ya, h_fin = _lru(proj, p["lru"], layer, seq_len, None)
    else:
        ck, cv, h0 = ctx
        ya = _lru(proj, p["lru"], layer, seq_len, h0)
        h_fin = None
        yb = _attn_lat(proj, ck, cv, p["sink"], layer, seq_len)
        yc = _pool(proj, p["pool_w"], p["pool_scale"], layer, seq_len)
    side = [_layer_slabs(raw[k], layer, m // MERGE_TM) for k in ("ffn_up", "ffn_down")] if cast else ()
    x1, h2, done = _merge(ya, yb, yc, proj, x, mod, mod_row0, rows_per_mod, p["b_gate"], wts["w_branch"],
                          wts["w_out"], p["norm2"], layer, side=side)
    if cast:
        wts["ffn_up"], wts["ffn_down"] = done
    side = [_layer_slabs(raw["w_in"], layer + 1, m // FFN_TM)] if cast and layer + 1 < DEPTH else ()
    out, done = _ffn(h2, x1, mod, mod_row0, rows_per_mod, wts["ffn_up"], p["ffn_conv"], p["ffn_conv_b"],
                     wts["ffn_down"], layer, seq_len, final_norm, side=side)
    return out, k_new, v_new, h_fin, (done[0] if side else None)


def _stacked_params(norm1, norm2, b_gate, lru_conv, lru_conv_b, lru_wa, lru_ba, lru_wx, lru_bx,
                    lru_lambda, attn_sink, pool_w, pool_scale, ffn_conv, ffn_conv_b):
    def per_head(v):
        return v.reshape(DEPTH, 2, LRU_HEADS, LRU_BLOCK).transpose(0, 2, 1, 3)

    row = lambda v: v[:, None, :]
    wg = jnp.concatenate([lru_wa[:, 0], lru_wx[:, 0], lru_wa[:, 1], lru_wx[:, 1]], axis=-1).astype(BF16)
    ba = per_head(lru_ba)
    bx = per_head(lru_bx)
    bg = jnp.concatenate([ba[:, :, 0], bx[:, :, 0], ba[:, :, 1], bx[:, :, 1]], axis=-1)[:, :, None, :]
    lam = per_head(lru_lambda).reshape(DEPTH, LRU_HEADS, 1, 2 * LRU_BLOCK)
    return {
        "norm1": row(norm1), "norm2": row(norm2), "b_gate": row(b_gate),
        "lru": {"conv_w": lru_conv, "conv_b": row(lru_conv_b), "wg": wg, "bg": bg, "lam": lam},
        "sink": attn_sink, "pool_w": pool_w.astype(BF16), "pool_scale": row(pool_scale),
        "ffn_conv": ffn_conv, "ffn_conv_b": row(ffn_conv_b),
    }


def kernel(x_prompt, x_sample, cache_k, cache_v, state_lru, c, c_ctx, w_ada, b_ada, norm1, norm2, w_in,
           b_gate, lru_conv, lru_conv_b, lru_wa, lru_ba, lru_wx, lru_bx, lru_lambda, attn_sink, pool_w,
           pool_scale, w_branch, w_out, ffn_up, ffn_conv, ffn_conv_b, ffn_down, final_norm):
    batch, seq, _ = x_prompt.shape
    dec_batch, dec_seq, _ = x_sample.shape
    past = cache_k.shape[2]
    assert seq == SCAN_CHUNK and dec_seq % SCAN_ROWS == 0 and (batch * seq) % SCAN_ROWS == 0

    c_rows = jnp.concatenate(
        [c_ctx[None], c, jnp.zeros((SUBLANES - 1 - dec_batch, D_MODEL), F32)], axis=0)
    mods = _ada(c_rows, w_ada, b_ada)

    xp = x_prompt.reshape(batch * seq, D_MODEL)
    xs = x_sample.reshape(dec_batch * dec_seq, D_MODEL)
    fn = final_norm[None]
    p = _stacked_params(norm1, norm2, b_gate, lru_conv, lru_conv_b, lru_wa, lru_ba, lru_wx, lru_bx,
                        lru_lambda, attn_sink, pool_w, pool_scale, ffn_conv, ffn_conv_b)
    raw = {"w_in": w_in, "w_branch": w_branch, "w_out": w_out, "ffn_up": ffn_up, "ffn_down": ffn_down}
    w_in_l = w_in[0].astype(BF16)
    mod_rows = mods.reshape(DEPTH * SUBLANES, 1, 6 * D_MODEL)
    ck = cache_k.reshape(dec_batch, DEPTH, past, D_KV).astype(BF16)
    cv = cache_v.reshape(dec_batch, DEPTH, past, D_KV).astype(BF16)
    h0 = state_lru.transpose(1, 2, 0, 3)[:, :, :, None, :]
    caches, hs = None, []
    for l in range(DEPTH):
        last = fn if l == DEPTH - 1 else None
        wts = {"w_in": w_in_l}
        xp, k_all, v_all, h_fin, w_in_l = _trunk_layer(xp, mod_rows, l * SUBLANES, batch * seq, p, wts, l, seq,
                                                       None, last, caches, raw)
        caches = (k_all, v_all)
        xs = _trunk_layer(xs, mod_rows, l * SUBLANES + 1, dec_seq, p, wts, l, dec_seq, (ck, cv, h0), last)[0]
        hs.append(h_fin.transpose(1, 0, 2))
    y_prompt = xp.reshape(batch, seq, D_MODEL)
    y_sample = xs.reshape(dec_batch, dec_seq, D_MODEL)
    cache_dims = (batch, DEPTH, seq, N_KV_HEADS, HEAD_DIM)
    return (y_prompt, y_sample, k_all.reshape(cache_dims), v_all.reshape(cache_dims), jnp.stack(hs, axis=1))
```

```python
import functools

import numpy as np
import jax
import jax.numpy as jnp
from jax import lax
from jax.experimental import pallas as pl
from jax.experimental.pallas import tpu as pltpu

F32 = jnp.float32
BF16 = jnp.bfloat16

D_MODEL = 1024
DEPTH = 2
GRID_W = 64
EPS = 1e-6
N_BRANCH = 3
D_RNN = 1024
LRU_HEADS = 8
LRU_BLOCK = D_RNN // LRU_HEADS
LRU_C = 8.0
N_HEADS = 8
N_KV_HEADS = 2
KV_GROUPS = N_HEADS // N_KV_HEADS
HEAD_DIM = 128
D_KV = N_KV_HEADS * HEAD_DIM
WINDOW = 128
BLOCK_Q = 128
ROPE_BASE = 10000.0
NEG_INF = -1e30
D_POOL = 1024
POOL_WINDOWS = (2, 4, 8, 16)
POOL_GROUP = D_POOL // len(POOL_WINDOWS)
D_FF = 2816
D_IN = D_RNN + N_HEADS * HEAD_DIM + 2 * D_KV + D_POOL + N_BRANCH * D_MODEL

COL_XA = 0
COL_Q = COL_XA + D_RNN
COL_K = COL_Q + N_HEADS * HEAD_DIM
COL_V = COL_K + D_KV
COL_XC = COL_V + D_KV
COL_G = COL_XC + D_POOL
LOG2E = float(np.log2(np.e))

VMEM_LIMIT_BYTES = 52 * 1024 * 1024
SUBLANES = 8
LANES = 128
BF16_ROWS = 16

SCAN_CHUNK = 256
SCAN_PITCH = 260
SCAN_ROWS = 2048


def _params(*sem):
    return pltpu.CompilerParams(dimension_semantics=sem, vmem_limit_bytes=VMEM_LIMIT_BYTES)


def _dot(a, b):
    return jnp.dot(a, b, preferred_element_type=F32)


def _dot_nt(a, b):
    return lax.dot_general(a, b, (((1,), (1,)), ((), ())), preferred_element_type=F32)


def _sigmoid(z):
    return 0.5 * (1.0 + jnp.tanh(0.5 * z))


def _rms(x):
    return x * lax.rsqrt(jnp.mean(x * x, axis=-1, keepdims=True) + EPS)


def _ada_kernel(c_ref, w_ref, b_ref, o_ref):
    c = c_ref[...]
    s = c * _sigmoid(c)
    o_ref[0] = _dot(s.astype(BF16), w_ref[0].astype(BF16)) + b_ref[0]


def _ada(c_rows, w_ada, b_ada):
    tn = 1536
    return pl.pallas_call(
        _ada_kernel,
        grid=(DEPTH, 6 * D_MODEL // tn),
        in_specs=[
            pl.BlockSpec((SUBLANES, D_MODEL), lambda l, j: (0, 0)),
            pl.BlockSpec((1, D_MODEL, tn), lambda l, j: (l, 0, j)),
            pl.BlockSpec((1, 1, tn), lambda l, j: (l, 0, j)),
        ],
        out_specs=pl.BlockSpec((1, SUBLANES, tn), lambda l, j: (l, 0, j)),
        out_shape=jax.ShapeDtypeStruct((DEPTH, SUBLANES, 6 * D_MODEL), F32),
        compiler_params=_params("parallel", "parallel"),
        name="ada_mod",
    )(c_rows, w_ada, b_ada.reshape(DEPTH, 1, 6 * D_MODEL))


INPROJ_TM = 512
INPROJ_CK = 2 * D_KV


def _resident(shape, layer):
    ndim = len(shape)
    return pl.BlockSpec((None,) + tuple(shape), lambda *_: (layer,) + (0,) * ndim,
                        pipeline_mode=pl.Buffered(1))


def _whole(shape):
    return pl.BlockSpec(tuple(shape), lambda *_: (0,) * len(shape), pipeline_mode=pl.Buffered(1))


def _side_cast_specs(side, steps):
    in_specs = [pl.BlockSpec((rows, arr.shape[1]), lambda i, first=first: (first + i, 0))
                for arr, rows, first in side]
    out_specs = [pl.BlockSpec((rows, arr.shape[1]), lambda i: (i, 0)) for arr, rows, _ in side]
    out_shapes = [jax.ShapeDtypeStruct((rows * steps, arr.shape[1]), BF16) for arr, rows, _ in side]
    return in_specs, out_specs, out_shapes


def _run_side_casts(refs, n_in, n_out, n_side):
    refs = list(refs)
    side_in = refs[n_in:n_in + n_side]
    side_out = refs[n_in + n_side + n_out:n_in + 2 * n_side + n_out]
    for src, dst in zip(side_in, side_out):
        dst[...] = src[...].astype(BF16)
    return refs[:n_in] + refs[n_in + n_side:n_in + n_side + n_out] + refs[n_in + 2 * n_side + n_out:]


def _layer_slabs(stacked, layer, steps):
    cols = stacked.shape[-1]
    rows = int(np.prod(stacked.shape[1:-1]))
    assert rows % (steps * BF16_ROWS) == 0
    return (stacked.reshape(stacked.shape[0] * rows, cols), rows // steps, layer * steps)


def _mod_spec(row0, per):
    return pl.BlockSpec((1, 1, 6 * D_MODEL), lambda i: (row0 + i // per, 0, 0))


def _inproj_kernel(*refs, seq_len, want_kv, layer, owns_cache, n_in, n_side):
    refs = _run_side_casts(refs, n_in, 5 if want_kv else 1, n_side)
    _inproj_body(*refs, seq_len=seq_len, want_kv=want_kv, layer=layer, owns_cache=owns_cache)


def _inproj_body(*refs, seq_len, want_kv, layer, owns_cache):
    if want_kv:
        x_ref, mod_ref, g_ref, w_ref, sink_ref, band_ref, pw_ref, ps_ref = refs[:8]
        o_ref, k_ref, v_ref, y_ref, yc_ref = refs[-5:]
        q_chunks = []
        if owns_cache:
            for other in range(DEPTH):
                if other != layer:
                    k_ref[:, other] = jnp.zeros(k_ref.shape[:1] + k_ref.shape[2:], F32)
                    v_ref[:, other] = jnp.zeros(v_ref.shape[:1] + v_ref.shape[2:], F32)
    else:
        x_ref, mod_ref, g_ref, w_ref, cos_ref, sa_ref, sb_ref, o_ref = refs
    mod = mod_ref[0]
    shift = mod[:, 0:D_MODEL]
    scale = mod[:, D_MODEL:2 * D_MODEL]
    h = (_rms(x_ref[...]) * g_ref[...] * (1.0 + scale) + shift).astype(BF16)
    nf = HEAD_DIM // 4

    def rope(x):
        return (x * cos_ref[...] + pltpu.roll(x, HEAD_DIM - nf, 1) * sa_ref[...]
                + pltpu.roll(x, nf, 1) * sb_ref[...])

    for c in range(D_IN // INPROJ_CK):
        acc = _dot(h, w_ref[:, c * INPROJ_CK:(c + 1) * INPROJ_CK])
        if not want_kv:
            lo = c * INPROJ_CK
            heads = [acc[:, j * HEAD_DIM:(j + 1) * HEAD_DIM] for j in range(INPROJ_CK // HEAD_DIM)]
            heads = [rope(hd) if COL_Q <= lo + j * HEAD_DIM < COL_V else hd for j, hd in enumerate(heads)]
            heads = [hd * SOFTMAX_SCALE if COL_Q <= lo + j * HEAD_DIM < COL_K else hd
                     for j, hd in enumerate(heads)]
            acc = jnp.concatenate(heads, axis=1)
        o_ref[:, c * INPROJ_CK:(c + 1) * INPROJ_CK] = acc.astype(o_ref.dtype)
        if want_kv and COL_Q <= c * INPROJ_CK < COL_K:
            q_chunks.append((acc * SOFTMAX_SCALE).astype(BF16))
        if want_kv and COL_XC <= c * INPROJ_CK < COL_G:
            xc = acc.astype(BF16)
            for j in range(INPROJ_CK // POOL_GROUP):
                gi = (c * INPROJ_CK - COL_XC) // POOL_GROUP + j
                gc = slice(gi * POOL_GROUP, (gi + 1) * POOL_GROUP)
                edges = _pool_edge_weights(POOL_WINDOWS[gi])
                for b in range(INPROJ_TM // seq_len):
                    rows = slice(b * seq_len, (b + 1) * seq_len)
                    x = xc[rows, j * POOL_GROUP:(j + 1) * POOL_GROUP]
                    yc = _pool_tile(_dot(band_ref[gi], x), x, POOL_WINDOWS[gi], edges, True, True,
                                    pw_ref[gi], ps_ref[:, gc])
                    yc_ref[rows, gc] = yc.astype(yc_ref.dtype)
        if want_kv and c == COL_K // INPROJ_CK:
            for b in range(INPROJ_TM // seq_len):
                rows = slice(b * seq_len, (b + 1) * seq_len)
                for hd in range(N_KV_HEADS):
                    dst = pl.ds(hd, seq_len, stride=N_KV_HEADS)
                    at = (b, layer, dst, slice(None)) if owns_cache else (b, dst, slice(None))
                    k_ref[at] = acc[rows, hd * HEAD_DIM:(hd + 1) * HEAD_DIM]
                    v_ref[at] = acc[rows, D_KV + hd * HEAD_DIM:D_KV + (hd + 1) * HEAD_DIM]
            kv = acc.astype(BF16)

            def scores(b, kh):
                rows = slice(b * seq_len, (b + 1) * seq_len)
                q = jnp.concatenate([q_chunks[kh][rows, g * HEAD_DIM:(g + 1) * HEAD_DIM]
                                     for g in range(KV_GROUPS)], axis=0)
                return (b, kh, _dot_nt(kv[rows, kh * HEAD_DIM:(kh + 1) * HEAD_DIM], q),
                        kv[rows, D_KV + kh * HEAD_DIM:D_KV + (kh + 1) * HEAD_DIM])

            order = [(b, kh) for b in range(INPROJ_TM // seq_len) for kh in range(N_KV_HEADS)]
            pending = [scores(*ch) for ch in order[:ATTN_LOOKAHEAD]]
            for n in range(len(order)):
                if n + ATTN_LOOKAHEAD < len(order):
                    pending.append(scores(*order[n + ATTN_LOOKAHEAD]))
                b, kh, t, vals = pending.pop(0)
                o_t = _softmax_pv_t(t, _sink_row(sink_ref, layer, kh, seq_len), vals)
                _store_heads_t(y_ref, o_t, kh, b * seq_len, seq_len)


def _inproj(x, mod, mod_row0, rows_per_mod, g, w, layer, seq_len, want_kv, sink=None, pool=None, caches=None,
            side=()):
    m = x.shape[0]
    tm = INPROJ_TM
    per = rows_per_mod // tm
    proj_spec = pl.BlockSpec((tm, D_IN), lambda i: (i, 0))
    proj_shape = jax.ShapeDtypeStruct((m, D_IN), BF16)
    aliases = {}
    owns_cache = want_kv and caches is None
    if want_kv:
        nb = tm // seq_len
        if owns_cache:
            cache_spec = pl.BlockSpec((nb, DEPTH, seq_len * N_KV_HEADS, HEAD_DIM), lambda i: (i, 0, 0, 0))
        else:
            cache_spec = pl.BlockSpec((nb, None, seq_len * N_KV_HEADS, HEAD_DIM), lambda i: (i, layer, 0, 0))
        cache_shape = jax.ShapeDtypeStruct((m // seq_len, DEPTH, seq_len * N_KV_HEADS, HEAD_DIM), F32)
        assert tm % seq_len == 0 and INPROJ_CK == KV_GROUPS * HEAD_DIM and COL_Q % INPROJ_CK == 0
        assert seq_len == POOL_TILE and COL_XC % INPROJ_CK == 0 and INPROJ_CK % POOL_GROUP == 0
        mix_spec = pl.BlockSpec((tm, D_MODEL), lambda i: (i, 0))
        mix_shape = jax.ShapeDtypeStruct((m, D_MODEL), BF16)
        out_specs = (proj_spec, cache_spec, cache_spec, mix_spec, mix_spec)
        out_shape = (proj_shape, cache_shape, cache_shape, mix_shape, mix_shape)
        nwin = len(POOL_WINDOWS)
        extra_specs = [pl.BlockSpec(memory_space=pltpu.SMEM), _whole((nwin, POOL_TILE, POOL_TILE)),
                       _resident((nwin, POOL_GROUP, POOL_GROUP), layer), _resident((1, D_POOL), layer)]
        extra_args = [sink, _pool_bands(seq_len), pool[0], pool[1]]
        if caches is not None:
            aliases = {4 + len(extra_args): 1, 5 + len(extra_args): 2}
            extra_specs += [pl.BlockSpec(memory_space=pl.ANY)] * 2
            extra_args += list(caches)
    else:
        out_specs, out_shape = (proj_spec,), (proj_shape,)
        tab = pl.BlockSpec((tm, HEAD_DIM), lambda i: (i % (seq_len // tm), 0))
        extra_specs, extra_args = [tab, tab, tab], list(_rope_tables(seq_len))
    side_in, side_out, side_shapes = _side_cast_specs(side, m // tm)
    out = pl.pallas_call(
        functools.partial(_inproj_kernel, seq_len=seq_len, want_kv=want_kv, layer=layer, owns_cache=owns_cache,
                          n_in=4 + len(extra_specs), n_side=len(side)),
        grid=(m // tm,),
        in_specs=[
            pl.BlockSpec((tm, D_MODEL), lambda i: (i, 0)),
            _mod_spec(mod_row0, per),
            _resident((1, D_MODEL), layer),
            _whole((D_MODEL, D_IN)),
        ] + extra_specs + side_in,
        out_specs=tuple(out_specs) + tuple(side_out),
        out_shape=tuple(out_shape) + tuple(side_shapes),
        input_output_aliases=aliases,
        compiler_params=_params("parallel"),
        name="in_proj",
    )(x, mod, g, w, *extra_args, *[item[0] for item in side])
    n_own = len(out_specs)
    own = tuple(out[:n_own]) if want_kv else (out[0], None, None, None, None)
    return own + (list(out[n_own:]),)


LRU_HEADS_PER_STEP = 4


def _chunk_neighbour(v, towards_later):
    sub = lax.broadcasted_iota(jnp.int32, v.shape, 0)
    if towards_later:
        return jnp.where(sub >= 1, pltpu.roll(v, 1, 0), 0.0)
    return jnp.where(sub <= SUBLANES - 2, pltpu.roll(v, SUBLANES - 1, 0), 0.0)


def _shift_time(x_tm, k, chained):
    n = x_tm.shape[0]
    steps = abs(k)
    edge = []
    for s in range(steps):
        if not chained:
            edge.append(jnp.zeros((SUBLANES, x_tm.shape[1]), F32))
        elif k > 0:
            src = n - (steps - s) * SUBLANES
            edge.append(_chunk_neighbour(x_tm[src:src + SUBLANES], True))
        else:
            edge.append(_chunk_neighbour(x_tm[s * SUBLANES:(s + 1) * SUBLANES], False))
    if k > 0:
        return jnp.concatenate(edge + [x_tm[:n - steps * SUBLANES]], axis=0)
    return jnp.concatenate([x_tm[steps * SUBLANES:]] + edge, axis=0)


def _lru_tm_kernel(*refs, seq_len, latent):
    if latent:
        (x_ref, cw_ref, cb_ref, wg_ref, bg_ref, lam_ref, h0_ref, y_ref,
         io_ref, af_ref, bf_ref, ab_ref, bb_ref, hf_ref, hb_ref) = refs
    else:
        (x_ref, cw_ref, cb_ref, wg_ref, bg_ref, lam_ref, y_ref, fin_ref,
         io_ref, af_ref, bf_ref, ab_ref, bb_ref, hf_ref, hb_ref) = refs
    rows = SCAN_ROWS
    nchunk = rows // SCAN_CHUNK
    nhead = LRU_HEADS_PER_STEP
    chained = seq_len > SCAN_CHUNK
    for hd in range(nhead):
        lanes = slice(hd * LRU_BLOCK, (hd + 1) * LRU_BLOCK)
        x = x_ref[:, lanes].astype(F32)
        for c in range(nchunk):
            io_ref[hd, pl.ds(c * SCAN_PITCH, SCAN_CHUNK), :] = x[c * SCAN_CHUNK:(c + 1) * SCAN_CHUNK]
        x = jnp.concatenate(
            [io_ref[hd, pl.ds(t, nchunk, stride=SCAN_PITCH), :] for t in range(SCAN_CHUNK)], axis=0)
        cw = 0.5 * cw_ref[:, lanes]
        xh = (_shift_time(x, 2, chained) * cw[0:1] + _shift_time(x, 1, chained) * cw[1:2]
              + x * cw[2:3] + _shift_time(x, -1, chained) * cw[3:4] + 0.5 * cb_ref[:, lanes])
        half_bias = 0.5 * bg_ref[hd]
        bias_hi = half_bias.astype(BF16).astype(F32)
        bias_rows = jnp.concatenate(
            [bias_hi, half_bias - bias_hi, jnp.zeros((BF16_ROWS - 2, 4 * LRU_BLOCK), F32)], axis=0)
        lhs = jnp.concatenate([xh.astype(BF16), jnp.ones((rows, BF16_ROWS), BF16)], axis=1)
        rhs = jnp.concatenate([wg_ref[hd], bias_rows.astype(BF16)], axis=0)
        th = jnp.tanh(_dot(lhs, rhs))
        lam = lam_ref[hd]
        for d, (a_ref, b_ref) in enumerate(((af_ref, bf_ref), (ab_ref, bb_ref))):
            th_r = th[:, 2 * d * LRU_BLOCK:(2 * d + 1) * LRU_BLOCK]
            th_i = th[:, (2 * d + 1) * LRU_BLOCK:(2 * d + 2) * LRU_BLOCK]
            nl = -lam[:, d * LRU_BLOCK:(d + 1) * LRU_BLOCK]
            softplus = jnp.maximum(nl, 0.0) + jnp.log(1.0 + jnp.exp(-jnp.abs(nl)))
            ch = (-0.5 * LRU_C * LOG2E) * softplus
            a = jnp.exp2(ch + ch * th_r)
            y = 1.0 - a * a
            a_ref[hd] = a
            b_ref[hd] = (y * lax.rsqrt(jnp.maximum(y, 1e-30))) * ((1.0 + th_i) * xh)

    def step(t, carry):
        rf = pl.ds(t * nchunk, nchunk)
        rb = pl.ds((SCAN_CHUNK - 1 - t) * nchunk, nchunk)
        out = []
        for hd in range(nhead):
            hf, hb, pf, pb = carry[4 * hd:4 * hd + 4]
            a_f = af_ref[hd, rf, :]
            a_b = ab_ref[hd, rb, :]
            hf = a_f * hf + bf_ref[hd, rf, :]
            hb = a_b * hb + bb_ref[hd, rb, :]
            hf_ref[hd, rf, :] = hf
            hb_ref[hd, rb, :] = hb
            if chained:
                pf = pf * a_f
                pb = pb * a_b
                af_ref[hd, rf, :] = pf
                ab_ref[hd, rb, :] = pb
            out += [hf, hb, pf, pb]
        return tuple(out)

    zero = jnp.zeros((nchunk, LRU_BLOCK), F32)
    one = jnp.ones((nchunk, LRU_BLOCK), F32)
    carry = (zero, zero, one, one) * nhead
    for t in range(SCAN_CHUNK):
        carry = step(t, carry)

    sub = lax.broadcasted_iota(jnp.int32, (nchunk, LRU_BLOCK), 0)
    last = slice(rows - nchunk, rows)
    first = slice(0, nchunk)
    for hd in range(nhead):
        lanes = slice(hd * LRU_BLOCK, (hd + 1) * LRU_BLOCK)
        hf = hf_ref[hd]
        hb = hb_ref[hd]
        if chained:
            pf = af_ref[hd]
            pb = ab_ref[hd]
            ef = jnp.where(sub == 0, h0_ref[0, 0][:, lanes], 0.0)
            eb = jnp.where(sub == nchunk - 1, h0_ref[1, 0][:, lanes], 0.0)
            for c in range(1, nchunk):
                ef = jnp.where(sub == c, pltpu.roll(hf[last] + pf[last] * ef, 1, 0), ef)
                eb = jnp.where(sub == nchunk - 1 - c,
                               pltpu.roll(hb[first] + pb[first] * eb, nchunk - 1, 0), eb)
            hf = (hf.reshape(SCAN_CHUNK, nchunk, LRU_BLOCK)
                  + pf.reshape(SCAN_CHUNK, nchunk, LRU_BLOCK) * ef[None]).reshape(rows, LRU_BLOCK)
            hb = (hb.reshape(SCAN_CHUNK, nchunk, LRU_BLOCK)
                  + pb.reshape(SCAN_CHUNK, nchunk, LRU_BLOCK) * eb[None]).reshape(rows, LRU_BLOCK)
        else:
            fin_ref[0, :, lanes] = hf[last]
            fin_ref[1, :, lanes] = hb[first]
        y = hf + hb
        for t in range(SCAN_CHUNK):
            io_ref[hd, pl.ds(t, nchunk, stride=SCAN_PITCH), :] = y[t * nchunk:(t + 1) * nchunk]
        for c in range(nchunk):
            y_ref[pl.ds(c * SCAN_CHUNK, SCAN_CHUNK), lanes] = (
                io_ref[hd, pl.ds(c * SCAN_PITCH, SCAN_CHUNK), :].astype(y_ref.dtype))


def _lru(proj, lp, layer, seq_len, h0):
    m = proj.shape[0]
    latent = h0 is not None
    rows = SCAN_ROWS
    nhead = LRU_HEADS_PER_STEP
    width = nhead * LRU_BLOCK
    in_specs = [
        pl.BlockSpec((rows, width), lambda r, h: (r, COL_XA // width + h)),
        pl.BlockSpec((None, 4, width), lambda r, h: (layer, 0, h)),
        pl.BlockSpec((None, 1, width), lambda r, h: (layer, 0, h)),
        pl.BlockSpec((None, nhead, LRU_BLOCK, 4 * LRU_BLOCK), lambda r, h: (layer, h, 0, 0)),
        pl.BlockSpec((None, nhead, 1, 4 * LRU_BLOCK), lambda r, h: (layer, h, 0, 0)),
        pl.BlockSpec((None, nhead, 1, 2 * LRU_BLOCK), lambda r, h: (layer, h, 0, 0)),
    ]
    args = [proj, lp["conv_w"], lp["conv_b"], lp["wg"], lp["bg"], lp["lam"]]
    y_spec = pl.BlockSpec((rows, width), lambda r, h: (r, h))
    y_shape = jax.ShapeDtypeStruct((m, D_RNN), BF16)
    if latent:
        in_specs.append(pl.BlockSpec((None, 2, 1, 1, width), lambda r, h: (layer, 0, r, 0, h)))
        args.append(h0)
        out_specs, out_shape = y_spec, y_shape
    else:
        nseq = m // seq_len
        out_specs = (y_spec, pl.BlockSpec((2, rows // seq_len, width), lambda r, h: (0, r, h)))
        out_shape = (y_shape, jax.ShapeDtypeStruct((2, nseq, D_RNN), F32))
    strided_buf = pltpu.VMEM((nhead, rows // SCAN_CHUNK * SCAN_PITCH, LRU_BLOCK), F32)
    scan_buf = pltpu.VMEM((nhead, rows, LRU_BLOCK), F32)
    return pl.pallas_call(
        functools.partial(_lru_tm_kernel, seq_len=seq_len, latent=latent),
        grid=(m // rows, LRU_HEADS // nhead),
        in_specs=in_specs,
        out_specs=out_specs,
        out_shape=out_shape,
        scratch_shapes=[strided_buf] + [scan_buf] * 6,
        compiler_params=_params("parallel", "parallel"),
        name="rglru_latent" if latent else "rglru_context",
    )(*args)


SOFTMAX_SCALE = HEAD_DIM ** -0.5 * LOG2E


def _stack_heads(q_ref, kh, r0, rows):
    parts = [q_ref[r0:r0 + rows, (kh * KV_GROUPS + g) * HEAD_DIM:(kh * KV_GROUPS + g + 1) * HEAD_DIM]
             for g in range(KV_GROUPS)]
    return jnp.concatenate(parts, axis=0)


def _dot_tn(a, b):
    return lax.dot_general(a, b, (((0,), (0,)), ((), ())), preferred_element_type=F32)


def _sink_row(sink_ref, layer, kh, cols):
    parts = [jnp.full((1, cols), sink_ref[layer, kh * KV_GROUPS + g] * LOG2E, F32) for g in range(KV_GROUPS)]
    return jnp.concatenate(parts, axis=1)


def _softmax_pv_t(t, sink, v):
    m = jnp.maximum(jnp.max(t, axis=0, keepdims=True), sink)
    e = jnp.exp2(t - m).astype(BF16)
    d = v.shape[1]
    v_ones = jnp.concatenate([v, jnp.ones((v.shape[0], SUBLANES), BF16)], axis=1)
    o_sum = _dot_tn(v_ones, e)
    denom = o_sum[d:d + 1] + jnp.exp2(sink - m)
    return o_sum[:d] * (1.0 / denom)


def _store_heads_t(y_ref, o_t, kh, r0, rows):
    for g in range(KV_GROUPS):
        h = kh * KV_GROUPS + g
        y_ref[r0:r0 + rows, h * HEAD_DIM:(h + 1) * HEAD_DIM] = (
            o_t[:, g * rows:(g + 1) * rows].T.astype(y_ref.dtype))


def _rope_tables(seq_len):
    nf = HEAD_DIM // 4
    freqs = ROPE_BASE ** (-np.arange(nf, dtype=np.float64) / nf)
    t = np.arange(seq_len)
    ang_row = (t // GRID_W)[:, None] * freqs[None, :]
    ang_col = (t % GRID_W)[:, None] * freqs[None, :]
    ang = np.concatenate([ang_row, ang_row, ang_col, ang_col], axis=1)
    first = (np.arange(HEAD_DIM) % (2 * nf)) < nf
    cos = np.cos(ang)
    sin = np.sin(ang)
    sin_a = np.where(first[None, :], -sin, 0.0)
    sin_b = np.where(first[None, :], 0.0, sin)
    return tuple(jnp.asarray(a, F32) for a in (cos, sin_a, sin_b))


ATTN_LAT_BLOCKS = 8
ATTN_LOOKAHEAD = 3


def _attn_lat_kernel(sink_ref, q_ref, kvp_ref, kvc_ref, kvn_ref, ck_ref, cv_ref, y_ref, *, nblk, layer):
    step = pl.program_id(1)
    cols = KV_GROUPS * BLOCK_Q
    span = 3 * BLOCK_Q
    key = lax.broadcasted_iota(jnp.int32, (span, cols), 0)
    qry = lax.broadcasted_iota(jnp.int32, (span, cols), 1) & (BLOCK_Q - 1)
    kv = jnp.concatenate([kvp_ref[...], kvc_ref[...], kvn_ref[...]], axis=0)
    def scores(i, kh):
        win = kv[i * BLOCK_Q:i * BLOCK_Q + span]
        sl = slice(kh * HEAD_DIM, (kh + 1) * HEAD_DIM)
        vl = slice(D_KV + kh * HEAD_DIM, D_KV + (kh + 1) * HEAD_DIM)
        q = _stack_heads(q_ref, kh, i * BLOCK_Q, BLOCK_Q)
        keys = jnp.concatenate([win[:, sl], ck_ref[0, :, sl]], axis=0)
        vals = jnp.concatenate([win[:, vl], cv_ref[0, :, sl]], axis=0)
        return i, kh, _dot_nt(keys, q), vals

    order = [(i, kh) for i in range(ATTN_LAT_BLOCKS) for kh in range(N_KV_HEADS)]
    pending = [scores(*c) for c in order[:ATTN_LOOKAHEAD]]
    for n in range(len(order)):
        if n + ATTN_LOOKAHEAD < len(order):
            pending.append(scores(*order[n + ATTN_LOOKAHEAD]))
        i, kh, t, vals = pending.pop(0)
        j = step * ATTN_LAT_BLOCKS + i
        lo = jnp.where(j > 0, qry, BLOCK_Q)
        hi = jnp.where(j < nblk - 1, qry + 2 * BLOCK_Q, 2 * BLOCK_Q - 1)
        bias = jnp.where(jnp.logical_and(key >= lo, key <= hi), 0.0, NEG_INF)
        t = jnp.concatenate([t[:span] + bias, t[span:]], axis=0)
        o_t = _softmax_pv_t(t, _sink_row(sink_ref, layer, kh, BLOCK_Q), vals)
        _store_heads_t(y_ref, o_t, kh, i * BLOCK_Q, BLOCK_Q)


def _attn_lat(proj, ck, cv, sink, layer, seq_len):
    m = proj.shape[0]
    nblk = seq_len // BLOCK_Q
    nstep = nblk // ATTN_LAT_BLOCKS
    rows = ATTN_LAT_BLOCKS * BLOCK_Q
    nb = m // seq_len
    past = ck.shape[2]
    kv_col = COL_K // (2 * D_KV)
    assert COL_V == COL_K + D_KV and COL_K % (2 * D_KV) == 0

    def halo(shift):
        def index(b, s):
            return (b * nblk + jnp.clip(s * ATTN_LAT_BLOCKS + shift, 0, nblk - 1), kv_col)
        return index

    return pl.pallas_call(
        functools.partial(_attn_lat_kernel, nblk=nblk, layer=layer),
        grid=(nb, nstep),
        in_specs=[
            pl.BlockSpec(memory_space=pltpu.SMEM),
            pl.BlockSpec((rows, N_HEADS * HEAD_DIM), lambda b, s: (b * nstep + s, COL_Q // (N_HEADS * HEAD_DIM))),
            pl.BlockSpec((BLOCK_Q, 2 * D_KV), halo(-1)),
            pl.BlockSpec((rows, 2 * D_KV), lambda b, s: (b * nstep + s, kv_col)),
            pl.BlockSpec((BLOCK_Q, 2 * D_KV), halo(ATTN_LAT_BLOCKS)),
            pl.BlockSpec((1, None, past, D_KV), lambda b, s: (b, layer, 0, 0)),
            pl.BlockSpec((1, None, past, D_KV), lambda b, s: (b, layer, 0, 0)),
        ],
        out_specs=pl.BlockSpec((rows, N_HEADS * HEAD_DIM), lambda b, s: (b * nstep + s, 0)),
        out_shape=jax.ShapeDtypeStruct((m, N_HEADS * HEAD_DIM), BF16),
        compiler_params=_params("parallel", "parallel"),
        name="attn_latent",
    )(sink, proj, proj, proj, proj, ck, cv)


POOL_TILE = 256
POOL_LEAD = BF16_ROWS
POOL_ROWS = 2048


def _pool_plan(seq_len):
    lead = 0 if seq_len == POOL_TILE else POOL_LEAD
    return lead, POOL_TILE - 2 * lead


def _pool_bands(seq_len):
    lead, nout = _pool_plan(seq_len)
    r = np.arange(nout)[:, None]
    c = np.arange(POOL_TILE)[None, :] - lead
    bands = [(c >= r - win // 2) & (c < r + win // 2) for win in POOL_WINDOWS]
    return jnp.asarray(np.stack(bands), BF16)


def _pool_edge_weights(win):
    half = win // 2
    edge = lax.broadcasted_iota(jnp.int32, (SUBLANES, POOL_GROUP), 0)
    inv_head = 1.0 / ((edge + half) - jnp.maximum(edge - half, 0)).astype(F32)
    inv_tail = 1.0 / (jnp.minimum(SUBLANES - edge, half) + half).astype(F32)
    return inv_head, inv_tail


def _pool_tile(sums, x, win, edges, at_start, at_end, w, scale):
    n = sums.shape[0]
    head = sums[:SUBLANES] * (edges[0] if at_start else 1.0 / win)
    tail = sums[n - SUBLANES:] * (edges[1] if at_end else 1.0 / win)
    mean = jnp.concatenate([head, sums[SUBLANES:n - SUBLANES] * (1.0 / win), tail], axis=0)
    pooled = (mean - x.astype(F32)).astype(BF16)
    return _dot(pooled, w) * scale


def _pool_kernel(x0_ref, x1_ref, x2_ref, x3_ref, band_ref, w_ref, s_ref, y_ref, pad_ref, *, seq_len):
    lead, nout = _pool_plan(seq_len)
    if lead:
        pad_ref[0:lead, :] = jnp.zeros((lead, POOL_GROUP), BF16)
        pad_ref[lead + seq_len:, :] = jnp.zeros((pad_ref.shape[0] - lead - seq_len, POOL_GROUP), BF16)
    for gi, (win, x_ref) in enumerate(zip(POOL_WINDOWS, (x0_ref, x1_ref, x2_ref, x3_ref))):
        cs = slice(gi * POOL_GROUP, (gi + 1) * POOL_GROUP)
        edges = _pool_edge_weights(win)
        if lead:
            pad_ref[lead:lead + seq_len, :] = x_ref[...]
        for base in range(0, POOL_ROWS, seq_len):
            for p0 in range(0, seq_len, nout):
                n = min(nout, seq_len - p0)
                x = x_ref[base + p0:base + p0 + n, :]
                src = pad_ref[p0:p0 + POOL_TILE, :] if lead else x
                sums = _dot(band_ref[gi, :n, :], src)
                y = _pool_tile(sums, x, win, edges, p0 == 0, p0 + n == seq_len, w_ref[gi], s_ref[:, cs])
                y_ref[base + p0:base + p0 + n, cs] = y.astype(y_ref.dtype)


def _pool(proj, w, s, layer, seq_len):
    m = proj.shape[0]
    rows = POOL_ROWS
    lead, nout = _pool_plan(seq_len)
    assert seq_len in (POOL_TILE, rows) and max(POOL_WINDOWS) // 2 <= min(SUBLANES, lead or SUBLANES)
    pad_rows = (pl.cdiv(seq_len, nout) - 1) * nout + POOL_TILE
    group = lambda gi: pl.BlockSpec((rows, POOL_GROUP), lambda r: (r, COL_XC // POOL_GROUP + gi))
    whole = lambda shape: pl.BlockSpec(shape, lambda r: (0,) * len(shape))
    nwin = len(POOL_WINDOWS)
    return pl.pallas_call(
        functools.partial(_pool_kernel, seq_len=seq_len),
        grid=(m // rows,),
        in_specs=[
            group(0), group(1), group(2), group(3),
            whole((nwin, nout, POOL_TILE)),
            _resident((nwin, POOL_GROUP, POOL_GROUP), layer),
            _resident((1, D_POOL), layer),
        ],
        out_specs=pl.BlockSpec((rows, D_POOL), lambda r: (r, 0)),
        out_shape=jax.ShapeDtypeStruct((m, D_POOL), BF16),
        scratch_shapes=[pltpu.VMEM((pad_rows, POOL_GROUP), BF16)],
        compiler_params=_params("parallel"),
        name="pool_mix",
    )(proj, proj, proj, proj, _pool_bands(seq_len), w, s)


MERGE_N_IN = 15
MERGE_TM = 512


def _merge_kernel(*refs, n_side):
    (ya_ref, yb_ref, yc_ref, g0_ref, g1_ref, g2_ref, g3_ref, g4_ref, g5_ref, x_ref, mod_ref,
     bg_ref, wb_ref, wo_ref, n2_ref, x1_ref, h2_ref) = _run_side_casts(refs, MERGE_N_IN, 2, n_side)
    mod = mod_ref[0]
    g_refs = (g0_ref, g1_ref, g2_ref, g3_ref, g4_ref, g5_ref)
    half = D_MODEL // 2
    merged = None
    for k, y_ref in enumerate((ya_ref, yb_ref, yc_ref)):
        y = _dot(y_ref[...], wb_ref[k])
        parts = []
        for p in range(2):
            z = g_refs[2 * k + p][...].astype(F32) + bg_ref[:, k * D_MODEL + p * half:k * D_MODEL + (p + 1) * half]
            parts.append((1.0 + jnp.tanh(0.5 * z)) * y[:, p * half:(p + 1) * half])
        term = jnp.concatenate(parts, axis=1)
        merged = term if merged is None else merged + term
    merged = 0.5 * merged
    gate1 = mod[:, 2 * D_MODEL:3 * D_MODEL]
    x1 = x_ref[...] + gate1 * _dot(merged.astype(BF16), wo_ref[...])
    x1_ref[...] = x1
    shift2 = mod[:, 3 * D_MODEL:4 * D_MODEL]
    scale2 = mod[:, 4 * D_MODEL:5 * D_MODEL]
    h2_ref[...] = (_rms(x1) * n2_ref[...] * (1.0 + scale2) + shift2).astype(BF16)


def _merge(ya, yb, yc, proj, x, mod, mod_row0, rows_per_mod, bg, wb, wo, n2, layer, side=()):
    m = x.shape[0]
    tm = MERGE_TM
    per = rows_per_mod // tm
    half = D_MODEL // 2
    row = pl.BlockSpec((tm, D_MODEL), lambda i: (i, 0))
    gate = lambda c: pl.BlockSpec((tm, half), lambda i: (i, COL_G // half + c))
    side_in, side_out, side_shapes = _side_cast_specs(side, m // tm)
    in_specs = [
        row, row, row,
        gate(0), gate(1), gate(2), gate(3), gate(4), gate(5),
        row,
        _mod_spec(mod_row0, per),
        _resident((1, N_BRANCH * D_MODEL), layer),
        _whole((N_BRANCH, D_MODEL, D_MODEL)),
        _whole((D_MODEL, D_MODEL)),
        _resident((1, D_MODEL), layer),
    ]
    assert len(in_specs) == MERGE_N_IN
    out = pl.pallas_call(
        functools.partial(_merge_kernel, n_side=len(side)),
        grid=(m // tm,),
        in_specs=in_specs + side_in,
        out_specs=(row, row) + tuple(side_out),
        out_shape=(jax.ShapeDtypeStruct((m, D_MODEL), F32), jax.ShapeDtypeStruct((m, D_MODEL), BF16))
        + tuple(side_shapes),
        compiler_params=_params("parallel"),
        name="merge_out",
    )(ya, yb, yc, proj, proj, proj, proj, proj, proj, x, mod, bg, wb, wo, n2, *[item[0] for item in side])
    return out[0], out[1], list(out[2:])


FFN_CK = 256
FFN_TM = 512
FFN_GAP = SUBLANES
GELU_C = float(np.sqrt(2.0 / np.pi))


def _ffn_kernel(*refs, seq_len, final, n_side):
    refs = _run_side_casts(refs, 9 + int(final), 1, n_side)
    h_ref, hp_ref, hn_ref, x_ref, mod_ref, wup_ref, cw_ref, cb_ref, wd_ref = refs[:9]
    fn_ref = refs[9] if final else None
    o_ref, hx_ref, u_ref, act_ref = refs[-4:]
    tm = FFN_TM
    halo = seq_len > tm
    i = pl.program_id(0)

    if halo:
        per_seq = seq_len // tm
        at_start = i % per_seq == 0
        at_end = i % per_seq == per_seq - 1
        zeros = jnp.zeros((BF16_ROWS, D_MODEL), BF16)

        @pl.when(at_start)
        def _():
            hx_ref[0:BF16_ROWS, :] = zeros

        @pl.when(jnp.logical_not(at_start))
        def _():
            hx_ref[0:BF16_ROWS, :] = hp_ref[...]

        @pl.when(at_end)
        def _():
            hx_ref[BF16_ROWS + tm:, :] = zeros

        @pl.when(jnp.logical_not(at_end))
        def _():
            hx_ref[BF16_ROWS + tm:, :] = hn_ref[...]

        hx_ref[BF16_ROWS:BF16_ROWS + tm, :] = h_ref[...]
        bases = (BF16_ROWS,)
        seg = tm
    else:
        nseg = tm // seq_len
        seg = seq_len
        bases = tuple(FFN_GAP + s * (seg + FFN_GAP) for s in range(nseg))
        for s in range(nseg + 1):
            u_ref[s * (seg + FFN_GAP):s * (seg + FFN_GAP) + FFN_GAP, :] = jnp.zeros((FFN_GAP, FFN_CK), F32)

    def taps(offset):
        return jnp.concatenate([u_ref[b + offset:b + offset + seg, :] for b in bases], axis=0)

    for c in range(D_FF // FFN_CK):
        cs = slice(c * FFN_CK, (c + 1) * FFN_CK)
        vs = slice(D_FF + c * FFN_CK, D_FF + (c + 1) * FFN_CK)
        if halo:
            u_ext = _dot(hx_ref[...], wup_ref[:, cs])
            u_ref[...] = u_ext
            u0 = u_ext[BF16_ROWS:BF16_ROWS + tm]
        else:
            u0 = _dot(h_ref[...], wup_ref[:, cs])
            for s, b in enumerate(bases):
                u_ref[b:b + seg, :] = u0[s * seg:(s + 1) * seg]
        uv = _dot(h_ref[...], wup_ref[:, vs])
        gff = taps(-1) * cw_ref[0:1, cs] + u0 * cw_ref[1:2, cs] + taps(1) * cw_ref[2:3, cs] + cb_ref[:, cs]
        inner = gff * (GELU_C + (GELU_C * 0.044715) * (gff * gff))
        act_ref[:, cs] = (0.5 * (gff * uv) * (1.0 + jnp.tanh(inner))).astype(BF16)

    gate2 = mod_ref[0][:, 5 * D_MODEL:6 * D_MODEL]
    out = x_ref[...] + gate2 * _dot(act_ref[...], wd_ref[...])
    if final:
        out = _rms(out) * fn_ref[...]
    o_ref[...] = out


def _ffn(h2, x1, mod, mod_row0, rows_per_mod, wup, cw, cb, wd, layer, seq_len, final_norm, side=()):
    m = x1.shape[0]
    tm = FFN_TM
    per = rows_per_mod // tm
    hb = tm // BF16_ROWS
    last_halo = m // BF16_ROWS - 1
    final = final_norm is not None
    halo = seq_len > tm
    in_specs = [
        pl.BlockSpec((tm, D_MODEL), lambda i: (i, 0)),
        pl.BlockSpec((BF16_ROWS, D_MODEL), lambda i: (jnp.maximum(i * hb - 1, 0), 0)),
        pl.BlockSpec((BF16_ROWS, D_MODEL), lambda i: (jnp.minimum((i + 1) * hb, last_halo), 0)),
        pl.BlockSpec((tm, D_MODEL), lambda i: (i, 0)),
        _mod_spec(mod_row0, per),
        _whole((D_MODEL, 2 * D_FF)),
        _resident((3, D_FF), layer),
        _resident((1, D_FF), layer),
        _whole((D_FF, D_MODEL)),
    ]
    args = [h2, h2, h2, x1, mod, wup, cw, cb, wd]
    if final:
        in_specs.append(_whole((1, D_MODEL)))
        args.append(final_norm)
    if halo:
        u_rows = tm + 2 * BF16_ROWS
    else:
        u_rows = FFN_GAP + (tm // seq_len) * (seq_len + FFN_GAP)
    side_in, side_out, side_shapes = _side_cast_specs(side, m // tm)
    out = pl.pallas_call(
        functools.partial(_ffn_kernel, seq_len=seq_len, final=final, n_side=len(side)),
        grid=(m // tm,),
        in_specs=in_specs + side_in,
        out_specs=(pl.BlockSpec((tm, D_MODEL), lambda i: (i, 0)),) + tuple(side_out),
        out_shape=(jax.ShapeDtypeStruct((m, D_MODEL), F32),) + tuple(side_shapes),
        scratch_shapes=[
            pltpu.VMEM((tm + 2 * BF16_ROWS, D_MODEL), BF16),
            pltpu.VMEM((u_rows, FFN_CK), F32),
            pltpu.VMEM((tm, D_FF), BF16),
        ],
        compiler_params=_params("parallel"),
        name="conv_glu_ffn",
    )(*args, *[item[0] for item in side])
    return out[0], list(out[1:])


def _trunk_layer(x, mod, mod_row0, rows_per_mod, p, wts, layer, seq_len, ctx, final_norm, caches=None,
                 raw=None):
    m = x.shape[0]
    cast = raw is not None
    side = [_layer_slabs(raw[k], layer, m // INPROJ_TM) for k in ("w_branch", "w_out")] if cast else ()
    proj, k_new, v_new, yb, yc, done = _inproj(
        x, mod, mod_row0, rows_per_mod, p["norm1"], wts["w_in"], layer, seq_len, want_kv=ctx is None,
        sink=p["sink"], pool=(p["pool_w"], p["pool_scale"]), caches=caches, side=side)
    if cast:
        wts["w_branch"] = done[0].reshape(N_BRANCH, D_MODEL, D_MODEL)
        wts["w_out"] = done[1]
    if ctx is None:
        ya, h_fin = _lru(proj, p["lru"], layer, seq_len, None)
    else:
        ck, cv, h0 = ctx
        ya = _lru(proj, p["lru"], layer, seq_len, h0)
        h_fin = None
        yb = _attn_lat(proj, ck, cv, p["sink"], layer, seq_len)
        yc = _pool(proj, p["pool_w"], p["pool_scale"], layer, seq_len)
    side = [_layer_slabs(raw[k], layer, m // MERGE_TM) for k in ("ffn_up", "ffn_down")] if cast else ()
    x1, h2, done = _merge(ya, yb, yc, proj, x, mod, mod_row0, rows_per_mod, p["b_gate"], wts["w_branch"],
                          wts["w_out"], p["norm2"], layer, side=side)
    if cast:
        wts["ffn_up"], wts["ffn_down"] = done
    side = [_layer_slabs(raw["w_in"], layer + 1, m // FFN_TM)] if cast and layer + 1 < DEPTH else ()
    out, done = _ffn(h2, x1, mod, mod_row0, rows_per_mod, wts["ffn_up"], p["ffn_conv"], p["ffn_conv_b"],
                     wts["ffn_down"], layer, seq_len, final_norm, side=side)
    return out, k_new, v_new, h_fin, (done[0] if side else None)


def _stacked_params(norm1, norm2, b_gate, lru_conv, lru_conv_b, lru_wa, lru_ba, lru_wx, lru_bx,
                    lru_lambda, attn_sink, pool_w, pool_scale, ffn_conv, ffn_conv_b):
    def per_head(v):
        return v.reshape(DEPTH, 2, LRU_HEADS, LRU_BLOCK).transpose(0, 2, 1, 3)

    row = lambda v: v[:, None, :]
    wg = jnp.concatenate([lru_wa[:, 0], lru_wx[:, 0], lru_wa[:, 1], lru_wx[:, 1]], axis=-1).astype(BF16)
    ba = per_head(lru_ba)
    bx = per_head(lru_bx)
    bg = jnp.concatenate([ba[:, :, 0], bx[:, :, 0], ba[:, :, 1], bx[:, :, 1]], axis=-1)[:, :, None, :]
    lam = per_head(lru_lambda).reshape(DEPTH, LRU_HEADS, 1, 2 * LRU_BLOCK)
    return {
        "norm1": row(norm1), "norm2": row(norm2), "b_gate": row(b_gate),
        "lru": {"conv_w": lru_conv, "conv_b": row(lru_conv_b), "wg": wg, "bg": bg, "lam": lam},
        "sink": attn_sink, "pool_w": pool_w.astype(BF16), "pool_scale": row(pool_scale),
        "ffn_conv": ffn_conv, "ffn_conv_b": row(ffn_conv_b),
    }


def kernel(x_prompt, x_sample, cache_k, cache_v, state_lru, c, c_ctx, w_ada, b_ada, norm1, norm2, w_in,
           b_gate, lru_conv, lru_conv_b, lru_wa, lru_ba, lru_wx, lru_bx, lru_lambda, attn_sink, pool_w,
           pool_scale, w_branch, w_out, ffn_up, ffn_conv, ffn_conv_b, ffn_down, final_norm):
    batch, seq, _ = x_prompt.shape
    dec_batch, dec_seq, _ = x_sample.shape
    past = cache_k.shape[2]
    assert seq == SCAN_CHUNK and dec_seq % SCAN_ROWS == 0 and (batch * seq) % SCAN_ROWS == 0

    c_rows = jnp.concatenate(
        [c_ctx[None], c, jnp.zeros((SUBLANES - 1 - dec_batch, D_MODEL), F32)], axis=0)
    mods = _ada(c_rows, w_ada, b_ada)

    xp = x_prompt.reshape(batch * seq, D_MODEL)
    xs = x_sample.reshape(dec_batch * dec_seq, D_MODEL)
    fn = final_norm[None]
    p = _stacked_params(norm1, norm2, b_gate, lru_conv, lru_conv_b, lru_wa, lru_ba, lru_wx, lru_bx,
                        lru_lambda, attn_sink, pool_w, pool_scale, ffn_conv, ffn_conv_b)
    raw = {"w_in": w_in, "w_branch": w_branch, "w_out": w_out, "ffn_up": ffn_up, "ffn_down": ffn_down}
    w_in_l = w_in[0].astype(BF16)
    mod_rows = mods.reshape(DEPTH * SUBLANES, 1, 6 * D_MODEL)
    ck = cache_k.reshape(dec_batch, DEPTH, past, D_KV).astype(BF16)
    cv = cache_v.reshape(dec_batch, DEPTH, past, D_KV).astype(BF16)
    h0 = state_lru.transpose(1, 2, 0, 3)[:, :, :, None, :]
    caches, hs = None, []
    for l in range(DEPTH):
        last = fn if l == DEPTH - 1 else None
        wts = {"w_in": w_in_l}
        xp, k_all, v_all, h_fin, w_in_l = _trunk_layer(xp, mod_rows, l * SUBLANES, batch * seq, p, wts, l, seq,
                                                       None, last, caches, raw)
        caches = (k_all, v_all)
        xs = _trunk_layer(xs, mod_rows, l * SUBLANES + 1, dec_seq, p, wts, l, dec_seq, (ck, cv, h0), last)[0]
        hs.append(h_fin.transpose(1, 0, 2))
    y_prompt = xp.reshape(batch, seq, D_MODEL)
    y_sample = xs.reshape(dec_batch, dec_seq, D_MODEL)
    cache_dims = (batch, DEPTH, seq, N_KV_HEADS, HEAD_DIM)
    return (y_prompt, y_sample, k_all.reshape(cache_dims), v_all.reshape(cache_dims), jnp.stack(hs, axis=1))
```
